```python
import math
import jax, jax.numpy as jnp
from jax import lax
import numpy as np

D_MODEL = 2048
BATCH = 2
SEQ = 8192
DEPTH = 2

GRID_W = 64
CTX_LEN = 256
N_BRANCH = 4
BRANCH_W = D_MODEL // 4

HY_CH = BRANCH_W
HY_BANDS = 16
HY_EMB = 1 + 2 * HY_BANDS
HY_HID = 64
SHORT_CONV = 3

MLA_HEADS = 4
MLA_NOPE = 128
MLA_ROPE = 64
MLA_V = BRANCH_W // MLA_HEADS
MLA_QRANK = D_MODEL // 4
MLA_KVRANK = D_MODEL // 8
MLA_SCALE = (MLA_NOPE + MLA_ROPE) ** -0.5

DIF_HEADS = 4
DIF_QK = 64
DIF_V = BRANCH_W // DIF_HEADS

ML_HEADS = 4
ML_DH = BRANCH_W // ML_HEADS
ML_CHUNK = 64

N_EXPERTS = 32
TOP_K = 4
D_EXPERT = D_MODEL // 2
SWIGLU_LIMIT = 7.0
SWIGLU_ALPHA = 1.702
MOE_BLOCK = 256

ROPE_THETA = 10000.0
NORM_EPS = 1e-6
Q_BLOCK = 128

MEM_SPLITS = (MLA_KVRANK + MLA_ROPE, DIF_HEADS * 2 * DIF_QK, DIF_HEADS * DIF_V, BRANCH_W, BRANCH_W, 4 * ML_HEADS)
QRY_SPLITS = (MLA_QRANK, DIF_HEADS * 2 * DIF_QK, BRANCH_W, BRANCH_W, 3 * HY_CH, N_BRANCH * D_MODEL)
MEM_COLS = sum(MEM_SPLITS)
IN_COLS = MEM_COLS + sum(QRY_SPLITS)

kernel_name = 'hybrid_diffusion_prefix_trunk'


def rms_norm(x, g):
    xf = x.astype(jnp.float32)
    y = xf * lax.rsqrt(jnp.mean(xf * xf, axis=-1, keepdims=True) + NORM_EPS)
    return (y * g.astype(jnp.float32)).astype(x.dtype)


def split_cols(u, sizes):
    parts, off = [], 0
    for s in sizes:
        parts.append(u[..., off:off + s])
        off += s
    return parts


def to_heads(x, n_heads):
    b, t, _ = x.shape
    return x.reshape(b, t, n_heads, -1).transpose(0, 2, 1, 3)


def from_heads(x):
    b, h, t, d = x.shape
    return x.transpose(0, 2, 1, 3).reshape(b, t, h * d)


def centred_conv(x, w):
    width = w.shape[0]
    r = width // 2
    n = x.shape[1]
    xp = jnp.pad(x, ((0, 0), (r, r), (0, 0)))
    y = xp[:, 0:n] * w[0]
    for i in range(1, width):
        y = y + xp[:, i:i + n] * w[i]
    return y


def rotate(x, p, dim):
    inv = ROPE_THETA ** (-jnp.arange(0, dim, 2, dtype=jnp.float32) / dim)
    ang = p.astype(jnp.float32)[:, None] * inv[None, :]
    cos, sin = jnp.cos(ang).astype(x.dtype), jnp.sin(ang).astype(x.dtype)
    x1, x2 = jnp.split(x, 2, axis=-1)
    return jnp.concatenate([x1 * cos - x2 * sin, x2 * cos + x1 * sin], axis=-1)


def rope2d(x, pos):
    if pos is None:
        return x
    row, col = pos
    half = x.shape[-1] // 2
    return jnp.concatenate([rotate(x[..., :half], row, half), rotate(x[..., half:], col, half)], axis=-1)


def multi_map_attention(q, k, v, map_w):
    b, h, m, s, dq = q.shape
    blk = min(Q_BLOCK, s)
    nb = s // blk
    qb = jnp.moveaxis(q.reshape(b, h, m, nb, blk, dq), 3, 0)
    w = map_w.astype(jnp.float32)

    def one_block(qblk):
        scores = jnp.einsum('bhmqd,bhmkd->bhmqk', qblk, k).astype(jnp.float32)
        probs = jax.nn.softmax(scores, axis=-1)
        attn = jnp.einsum('bhmqk,hm->bhqk', probs, w)
        return jnp.einsum('bhqk,bhkd->bhqd', attn.astype(v.dtype), v)

    out = lax.map(one_block, qb)
    return jnp.moveaxis(out, 0, 2).reshape(b, h, s, v.shape[-1])


def zero_mlstm_state(b):
    return (jnp.zeros((b, ML_HEADS, ML_DH, ML_DH), jnp.float32),
            jnp.zeros((b, ML_HEADS, ML_DH), jnp.float32),
            jnp.zeros((b, ML_HEADS), jnp.float32))


def mlstm_scan(q, k, v, log_i, log_f, state, with_out):
    b, nh, n, d = k.shape
    nc = n // ML_CHUNK
    causal = jnp.tril(jnp.ones((ML_CHUNK, ML_CHUNK), dtype=bool))

    def to_chunks(a):
        return jnp.moveaxis(a.reshape((b, nh, nc, ML_CHUNK) + a.shape[3:]), 2, 0)

    def body(carry, xs):
        c_mat, n_vec, m = carry
        kc, vc, ic, fc = xs[0], xs[1], xs[2], xs[3]
        bc = jnp.cumsum(fc, axis=-1)
        g_end = bc[..., -1:] - bc + ic
        m_new = jnp.maximum(bc[..., -1] + m, jnp.max(g_end, axis=-1))
        decay = jnp.exp(bc[..., -1] + m - m_new)
        wk = jnp.exp(g_end - m_new[..., None])
        c_new = decay[..., None, None] * c_mat + jnp.einsum('bhs,bhsd,bhse->bhde', wk, kc, vc)
        n_new = decay[..., None] * n_vec + jnp.einsum('bhs,bhsd->bhd', wk, kc)
        if not with_out:
            return (c_new, n_new, m_new), None
        qc = xs[4]
        inter = bc + m[..., None]
        log_d = jnp.where(causal, bc[..., :, None] - bc[..., None, :] + ic[..., None, :], -jnp.inf)
        m_t = jnp.maximum(inter, jnp.max(log_d, axis=-1))
        s = jnp.einsum('bhtd,bhsd->bhts', qc, kc) * jnp.exp(log_d - m_t[..., None])
        w_inter = jnp.exp(inter - m_t)
        num = jnp.einsum('bhts,bhse->bhte', s, vc) + w_inter[..., None] * jnp.einsum('bhtd,bhde->bhte', qc, c_mat)
        den = jnp.sum(s, axis=-1) + w_inter * jnp.einsum('bhtd,bhd->bht', qc, n_vec)
        h = num / jnp.maximum(jnp.abs(den), jnp.exp(-m_t))[..., None]
        return (c_new, n_new, m_new), h

    xs = (to_chunks(k), to_chunks(v), to_chunks(log_i), to_chunks(log_f))
    if with_out:
        xs = xs + (to_chunks(q),)
    state, hs = lax.scan(body, state, xs)
    if not with_out:
        return state, None
    return state, jnp.moveaxis(hs, 0, 2).reshape(b, nh, n, d)


def mlstm_bidir(q, k, v, log_i, log_f, init, with_out):
    def flip(a):
        return None if a is None else jnp.flip(a, axis=2)
    st_f, h_f = mlstm_scan(q, k, v, log_i[0], log_f[0], init[0], with_out)
    st_b, h_b = mlstm_scan(flip(q), flip(k), flip(v), flip(log_i[1]), flip(log_f[1]), init[1], with_out)
    h = h_f + flip(h_b) if with_out else None
    return h, (st_f, st_b)


def hyena_filter(n, lp):
    f32 = jnp.float32
    t = jnp.arange(n, dtype=f32) / n
    bands = jnp.linspace(1e-4, HY_BANDS - 1, HY_BANDS, dtype=f32)
    ang = 2.0 * math.pi * t[:, None] * bands[None, :]
    feat = jnp.concatenate([t[:, None], jnp.cos(ang), jnp.sin(ang)], axis=-1)
    freq = lp['hy_freq'].astype(f32)
    hid = jnp.sin(freq[0] * (feat @ lp['hy_w1'].astype(f32) + lp['hy_b1'].astype(f32)))
    hid = jnp.sin(freq[1] * (hid @ lp['hy_w2'].astype(f32) + lp['hy_b2'].astype(f32)))
    window = jnp.exp(-t[:, None, None] * jnp.abs(lp['hy_decay'].astype(f32))[None])
    h = (hid @ lp['hy_w3'].astype(f32)).reshape(n, 2, HY_CH) * window
    filt = jnp.concatenate([h[:, 0], jnp.zeros((1, HY_CH), f32), jnp.flip(h[1:, 1], axis=0)], axis=0)
    return filt / jnp.sum(jnp.abs(filt), axis=0, keepdims=True)


def hyena_mixer(u, lp):
    n = u.shape[1]
    u = centred_conv(u, lp['hy_conv']) + lp['hy_conv_b']
    x0, x1, v = jnp.split(u, 3, axis=-1)
    z = (v * x1).astype(jnp.float32)
    filt = hyena_filter(n, lp)
    y = jnp.fft.irfft(jnp.fft.rfft(z, n=2 * n, axis=1) * jnp.fft.rfft(filt, axis=0)[None], n=2 * n, axis=1)[:, :n]
    y = y + z * lp['hy_skip'].astype(jnp.float32)
    return (x0.astype(jnp.float32) * y).astype(u.dtype)


def seq_memory(u, pos, lp):
    b, t, _ = u.shape
    f32 = jnp.float32
    kv_a, dk, dv, mk, mv, mg = split_cols(u, MEM_SPLITS)
    c_kv = rms_norm(kv_a[..., :MLA_KVRANK], lp['mla_kv_norm'])
    k_pe = rope2d(kv_a[..., MLA_KVRANK:], pos)
    kv = (c_kv @ lp['mla_w_kvb']).reshape(b, t, MLA_HEADS, MLA_NOPE + MLA_V).transpose(0, 2, 1, 3)
    k_pe_h = jnp.broadcast_to(k_pe[:, None], (b, MLA_HEADS, t, MLA_ROPE))
    mla_k = jnp.concatenate([kv[..., :MLA_NOPE], k_pe_h], axis=-1)[:, :, None]
    mla_v = kv[..., MLA_NOPE:]
    dif_k = rope2d(dk.reshape(b, t, DIF_HEADS, 2, DIF_QK).transpose(0, 2, 3, 1, 4), pos)
    dif_v = to_heads(dv, DIF_HEADS)
    ml_k = to_heads(jax.nn.silu(centred_conv(mk, lp['ml_conv_k'])), ML_HEADS).astype(f32) * ML_DH ** -0.5
    ml_v = to_heads(mv, ML_HEADS).astype(f32)
    g = (mg + lp['ml_gate_b']).astype(f32).reshape(b, t, 2, 2, ML_HEADS)
    log_i = g[:, :, :, 0].transpose(2, 0, 3, 1)
    log_f = jax.nn.log_sigmoid(g[:, :, :, 1]).transpose(2, 0, 3, 1)
    return (mla_k, mla_v, dif_k, dif_v, ml_k, ml_v, log_i, log_f)


def token_mixer(h, pos, lp, layer, prefix, want_out):
    b, n, _ = h.shape
    f32 = jnp.float32
    mem = seq_memory(h @ lp['w_in'][:, :MEM_COLS], pos, lp)
    mla_k, mla_v, dif_k, dif_v, ml_k, ml_v, ml_li, ml_lf = mem
    if prefix is None:
        init = (zero_mlstm_state(b), zero_mlstm_state(b))
    else:
        ctx_mem, init = prefix
        mla_k = jnp.concatenate([ctx_mem[0], mla_k], axis=3)
        mla_v = jnp.concatenate([ctx_mem[1], mla_v], axis=2)
        dif_k = jnp.concatenate([ctx_mem[2], dif_k], axis=3)
        dif_v = jnp.concatenate([ctx_mem[3], dif_v], axis=2)
    if not want_out:
        _, states = mlstm_bidir(None, ml_k, ml_v, ml_li, ml_lf, init, False)
        return None, mem, states
    q_a, dq, mq, mo, hy_u, gate_u = split_cols(h @ lp['w_in'][:, MEM_COLS:], QRY_SPLITS)

    q = (rms_norm(q_a, lp['mla_q_norm']) @ lp['mla_w_qb']).reshape(b, n, MLA_HEADS, MLA_NOPE + MLA_ROPE)
    q = q.transpose(0, 2, 1, 3)
    q = jnp.concatenate([q[..., :MLA_NOPE], rope2d(q[..., MLA_NOPE:], pos)], axis=-1) * MLA_SCALE
    y_mla = from_heads(multi_map_attention(q[:, :, None], mla_k, mla_v, jnp.ones((MLA_HEADS, 1), f32)))

    qd = rope2d(dq.reshape(b, n, DIF_HEADS, 2, DIF_QK).transpose(0, 2, 3, 1, 4), pos) * DIF_QK ** -0.5
    lam_p = lp['dif_lambda'].astype(f32)
    lam_init = 0.8 - 0.6 * math.exp(-0.3 * layer)
    lam = jnp.exp(jnp.sum(lam_p[0] * lam_p[1])) - jnp.exp(jnp.sum(lam_p[2] * lam_p[3])) + lam_init
    map_w = jnp.stack([jnp.ones((DIF_HEADS,), f32), jnp.full((DIF_HEADS,), -1.0, f32) * lam], axis=-1)
    o_dif = multi_map_attention(qd, dif_k, dif_v, map_w)
    y_dif = from_heads(rms_norm(o_dif, lp['dif_norm']) * (1.0 - lam_init))

    ml_q = to_heads(jax.nn.silu(centred_conv(mq, lp['ml_conv_q'])), ML_HEADS).astype(f32)
    h_ml, states = mlstm_bidir(ml_q, ml_k, ml_v, ml_li, ml_lf, init, True)
    h_ml = rms_norm(h_ml, lp['ml_norm'].reshape(ML_HEADS, 1, ML_DH).astype(f32))
    y_ml = jax.nn.sigmoid(mo) * from_heads(h_ml).astype(h.dtype)

    y_hy = hyena_mixer(hy_u, lp)

    ys = (y_hy, y_mla, y_dif, y_ml)
    gates = jnp.split(gate_u, N_BRANCH, axis=-1)
    merged = jnp.zeros((b, n, D_MODEL), h.dtype)
    for i in range(N_BRANCH):
        merged = merged + jax.nn.sigmoid(gates[i]) * (ys[i] @ lp['w_branch'][i])
    return merged @ lp['w_out'], mem, states


def moe_ffn(h, router_w, router_b, w_gu, b_gu, w_down, b_down):
    n, d = h.shape
    logits = (h @ router_w + router_b).astype(jnp.float32)
    top_val, top_idx = lax.top_k(logits, TOP_K)
    gates = jax.nn.softmax(top_val, axis=-1)
    a = n * TOP_K
    flat_e = top_idx.reshape(-1)
    flat_t = jnp.arange(a, dtype=jnp.int32) // TOP_K
    flat_g = gates.reshape(-1)
    order = jnp.argsort(flat_e)
    se = flat_e[order]
    counts = jnp.bincount(flat_e, length=N_EXPERTS)
    starts = jnp.cumsum(counts) - counts
    padded = (counts + MOE_BLOCK - 1) // MOE_BLOCK * MOE_BLOCK
    pend = jnp.cumsum(padded)
    dest = (pend - padded)[se] + jnp.arange(a, dtype=jnp.int32) - starts[se]
    n_blocks = (a + N_EXPERTS * (MOE_BLOCK - 1)) // MOE_BLOCK + 1
    slots = n_blocks * MOE_BLOCK
    slot_tok = jnp.full((slots,), n, jnp.int32).at[dest].set(flat_t[order])
    slot_gate = jnp.zeros((slots,), jnp.float32).at[dest].set(flat_g[order])
    blk_e = jnp.minimum(jnp.searchsorted(pend, jnp.arange(n_blocks, dtype=jnp.int32) * MOE_BLOCK, side='right'),
                        N_EXPERTS - 1)
    h_pad = jnp.concatenate([h, jnp.zeros((1, d), h.dtype)], axis=0)

    def body(acc, inp):
        tok, gate, e = inp
        gu = h_pad[tok] @ w_gu[e] + b_gu[e]
        g_, u_ = jnp.split(gu, 2, axis=-1)
        g_ = jnp.minimum(g_, SWIGLU_LIMIT)
        u_ = jnp.clip(u_, -SWIGLU_LIMIT, SWIGLU_LIMIT)
        act = g_ * jax.nn.sigmoid(SWIGLU_ALPHA * g_) * (u_ + 1.0)
        y = act @ w_down[e] + b_down[e]
        return acc.at[tok].add(y * gate[:, None].astype(y.dtype)), None

    acc, _ = lax.scan(body, jnp.zeros((n + 1, d), h.dtype),
                      (slot_tok.reshape(n_blocks, MOE_BLOCK), slot_gate.reshape(n_blocks, MOE_BLOCK), blk_e))
    return acc[:n]


def setup_inputs(seed: int = 0) -> dict:
    key = jax.random.key(seed)
    keys = iter(jax.random.split(key, 48))
    f32 = jnp.float32

    def nrm(shape, scale):
        return jax.random.normal(next(keys), shape, f32) * scale

    def gain(shape):
        return 1.0 + nrm(shape, 0.02)

    L, D, E, F = DEPTH, D_MODEL, N_EXPERTS, D_EXPERT
    decay_base = jnp.linspace(-math.log(1e-2) / 1.5, -math.log(1e-2) / 0.3, HY_CH, dtype=f32)
    gate_i = nrm((L, 2, 1, ML_HEADS), 0.1)
    gate_f = jnp.linspace(3.0, 6.0, ML_HEADS, dtype=f32) + nrm((L, 2, 1, ML_HEADS), 0.1)
    ml_gate_b = jnp.concatenate([gate_i, gate_f], axis=2).reshape(L, 4 * ML_HEADS)
    return {
        'x': nrm((BATCH, SEQ, D), 1.0),
        'c': nrm((BATCH, D), 1.0),
        'ctx': nrm((BATCH, CTX_LEN, D), 1.0),
        'c_ctx': nrm((D,), 1.0),
        'ada_w': nrm((L, D, 6 * D), 0.5 * D ** -0.5),
        'ada_b': nrm((L, 6 * D), 0.02),
        'norm_mix': gain((L, D)),
        'norm_ffn': gain((L, D)),
        'w_in': nrm((L, D, IN_COLS), D ** -0.5),
        'hy_conv': nrm((L, SHORT_CONV, 3 * HY_CH), SHORT_CONV ** -0.5),
        'hy_conv_b': nrm((L, 3 * HY_CH), 0.02),
        'hy_w1': nrm((L, HY_EMB, HY_HID), HY_EMB ** -0.5),
        'hy_b1': nrm((L, HY_HID), 0.02),
        'hy_w2': nrm((L, HY_HID, HY_HID), HY_HID ** -0.5),
        'hy_b2': nrm((L, HY_HID), 0.02),
        'hy_w3': nrm((L, HY_HID, 2 * HY_CH), HY_HID ** -0.5),
        'hy_freq': 1.0 + nrm((L, 2, HY_HID), 0.1),
        'hy_decay': decay_base * (1.0 + nrm((L, 2, HY_CH), 0.05)),
        'hy_skip': nrm((L, HY_CH), 1.0),
        'mla_q_norm': gain((L, MLA_QRANK)),
        'mla_w_qb': nrm((L, MLA_QRANK, MLA_HEADS * (MLA_NOPE + MLA_ROPE)), MLA_QRANK ** -0.5),
        'mla_kv_norm': gain((L, MLA_KVRANK)),
        'mla_w_kvb': nrm((L, MLA_KVRANK, MLA_HEADS * (MLA_NOPE + MLA_V)), MLA_KVRANK ** -0.5),
        'dif_lambda': nrm((L, 4, DIF_QK), 0.1),
        'dif_norm': gain((L, DIF_V)),
        'ml_conv_q': nrm((L, SHORT_CONV, BRANCH_W), SHORT_CONV ** -0.5),
        'ml_conv_k': nrm((L, SHORT_CONV, BRANCH_W), SHORT_CONV ** -0.5),
        'ml_gate_b': ml_gate_b,
        'ml_norm': gain((L, BRANCH_W)),
        'w_branch': nrm((L, N_BRANCH, BRANCH_W, D), BRANCH_W ** -0.5),
        'w_out': nrm((L, D, D), D ** -0.5),
        'router_w': nrm((L, D, E), D ** -0.5),
        'router_b': nrm((L, E), 0.01),
        'exp_w_gu': nrm((L, E, D, 2 * F), D ** -0.5),
        'exp_b_gu': nrm((L, E, 2 * F), 0.02),
        'exp_w_down': nrm((L, E, F, D), F ** -0.5),
        'exp_b_down': nrm((L, E, D), 0.02),
        'final_norm': gain((D,)),
    }


def reference(x, c, ctx, c_ctx, ada_w, ada_b, norm_mix, norm_ffn, w_in, hy_conv, hy_conv_b, hy_w1, hy_b1,
              hy_w2, hy_b2, hy_w3, hy_freq, hy_decay, hy_skip, mla_q_norm, mla_w_qb, mla_kv_norm, mla_w_kvb,
              dif_lambda, dif_norm, ml_conv_q, ml_conv_k, ml_gate_b, ml_norm, w_branch, w_out, router_w,
              router_b, exp_w_gu, exp_b_gu, exp_w_down, exp_b_down, final_norm):
    b, n, _ = x.shape
    n_ctx = ctx.shape[1]
    n_rows = n // GRID_W
    pos = (jnp.repeat(jnp.arange(n_rows, dtype=jnp.int32), GRID_W),
           jnp.tile(jnp.arange(GRID_W, dtype=jnp.int32), n_rows))
    x_lat, x_ctx = x, ctx
    for l in range(DEPTH):
        last = l == DEPTH - 1
        lp = dict(w_in=w_in[l], hy_conv=hy_conv[l], hy_conv_b=hy_conv_b[l], hy_w1=hy_w1[l], hy_b1=hy_b1[l],
                  hy_w2=hy_w2[l], hy_b2=hy_b2[l], hy_w3=hy_w3[l], hy_freq=hy_freq[l], hy_decay=hy_decay[l],
                  hy_skip=hy_skip[l], mla_q_norm=mla_q_norm[l], mla_w_qb=mla_w_qb[l], mla_kv_norm=mla_kv_norm[l],
                  mla_w_kvb=mla_w_kvb[l], dif_lambda=dif_lambda[l], dif_norm=dif_norm[l], ml_conv_q=ml_conv_q[l],
                  ml_conv_k=ml_conv_k[l], ml_gate_b=ml_gate_b[l], ml_norm=ml_norm[l], w_branch=w_branch[l],
                  w_out=w_out[l])
        mod_lat = (jax.nn.silu(c) @ ada_w[l] + ada_b[l])[:, None, :]
        mod_ctx = jax.nn.silu(c_ctx) @ ada_w[l] + ada_b[l]
        sh_a, sc_a, g_a, sh_f, sc_f, g_f = jnp.split(mod_lat, 6, axis=-1)
        csh_a, csc_a, cg_a, csh_f, csc_f, cg_f = jnp.split(mod_ctx, 6, axis=-1)
        h_ctx = rms_norm(x_ctx, norm_mix[l]) * (1.0 + csc_a) + csh_a
        h_lat = rms_norm(x_lat, norm_mix[l]) * (1.0 + sc_a) + sh_a
        y_ctx, mem_ctx, st_ctx = token_mixer(h_ctx, None, lp, l, None, not last)
        y_lat, _, _ = token_mixer(h_lat, pos, lp, l, (mem_ctx, st_ctx), True)
        x_lat = x_lat + g_a * y_lat
        f_lat = (rms_norm(x_lat, norm_ffn[l]) * (1.0 + sc_f) + sh_f).reshape(b * n, D_MODEL)
        moe_w = (router_w[l], router_b[l], exp_w_gu[l], exp_b_gu[l], exp_w_down[l], exp_b_down[l])
        if last:
            x_lat = x_lat + g_f * moe_ffn(f_lat, *moe_w).reshape(b, n, D_MODEL)
        else:
            x_ctx = x_ctx + cg_a * y_ctx
            f_ctx = (rms_norm(x_ctx, norm_ffn[l]) * (1.0 + csc_f) + csh_f).reshape(b * n_ctx, D_MODEL)
            out = moe_ffn(jnp.concatenate([f_ctx, f_lat], axis=0), *moe_w)
            x_ctx = x_ctx + cg_f * out[:b * n_ctx].reshape(b, n_ctx, D_MODEL)
            x_lat = x_lat + g_f * out[b * n_ctx:].reshape(b, n, D_MODEL)
    return rms_norm(x_lat, final_norm)
```

```python
import functools
import math

import jax
import jax.numpy as jnp
from jax import lax
from jax.experimental import pallas as pl
from jax.experimental.pallas import tpu as pltpu

F32 = jnp.float32
BF16 = jnp.bfloat16
HIGHEST = lax.Precision.HIGHEST

D_MODEL = 2048
GRID_W = 64
BRANCH_W = 512
N_HEADS = 4
HEAD_W = 128
MLA_NOPE = 128
MLA_ROPE = 64
MLA_QK_PAD = 256
MLA_SCALE = (MLA_NOPE + MLA_ROPE) ** -0.5
DIF_QK = 64
ML_CHUNK = 64
N_EXPERTS = 32
TOP_K = 4
D_EXPERT = 1024
SWIGLU_LIMIT = 7.0
SWIGLU_ALPHA = 1.702
MOE_BLOCK = 256
ROPE_THETA = 10000.0
NORM_EPS = 1e-6
HY_BANDS = 16
HY_EMB = 1 + 2 * HY_BANDS
HY_HID = 64
LANE = 128
SUBLANE = 8
SMALL_COLS = 6144
GATE_COLS = 4 * D_MODEL

CB_DK, CB_DV, CB_MK, CB_MV, CB_QA, CB_DQ, CB_MQ, CB_MO, CB_HX0, CB_HX1, CB_HV = range(11)
CB_CKV_256 = 22
CB_KPE_128 = 46


def _cparams(sem, vmem_mb=48):
    return pltpu.CompilerParams(dimension_semantics=sem, vmem_limit_bytes=vmem_mb * 1024 * 1024)


def _rms(x):
    return x * lax.rsqrt(jnp.mean(x * x, axis=-1, keepdims=True) + NORM_EPS)


def _sigmoid(x):
    return 1.0 / (1.0 + jnp.exp(-x))


def _silu(x):
    return x * _sigmoid(x)


def _log_sigmoid(x):
    return jnp.minimum(x, 0.0) - jnp.log(1.0 + jnp.exp(-jnp.abs(x)))


def _bdot(a, b):
    return jnp.dot(a.astype(BF16), b.astype(BF16), preferred_element_type=F32)


def _bdot_nt(a, b):
    return lax.dot_general(a.astype(BF16), b.astype(BF16), (((1,), (1,)), ((), ())),
                           preferred_element_type=F32)


def _hdot(a, b):
    return jnp.dot(a, b, preferred_element_type=F32, precision=HIGHEST)


def _mod_kernel(c_ref, w_ref, b_ref, o_ref):
    o_ref[0] = _bdot(_silu(c_ref[...]), w_ref[0]) + b_ref[0]


def _modulation(cvec, ada_w, ada_b):
    n_layers, d, six_d = ada_w.shape
    tn = 1024
    return pl.pallas_call(
        _mod_kernel,
        grid=(n_layers, six_d // tn),
        in_specs=[pl.BlockSpec((SUBLANE, d), lambda l, j: (0, 0)),
                  pl.BlockSpec((1, d, tn), lambda l, j: (l, 0, j)),
                  pl.BlockSpec((1, 1, tn), lambda l, j: (l, 0, j))],
        out_specs=pl.BlockSpec((1, SUBLANE, tn), lambda l, j: (l, 0, j)),
        out_shape=jax.ShapeDtypeStruct((n_layers, SUBLANE, six_d), F32),
        compiler_params=_cparams(("parallel", "parallel")),
        name="adaln_modulation",
    )(cvec, ada_w, ada_b.reshape(n_layers, 1, six_d))


def _group_of_tile(i, tm, lat_rows, seq_rows):
    return jnp.minimum(i // (seq_rows // tm), lat_rows // seq_rows)


def _in_proj_kernel(x_ref, g_ref, sc_ref, sh_ref, w_ref, o_ref, h_scr):
    @pl.when(pl.program_id(1) == 0)
    def _():
        h = _rms(x_ref[...]) * g_ref[...]
        h_scr[...] = (h * (1.0 + sc_ref[0]) + sh_ref[0]).astype(BF16)

    o_ref[...] = jnp.dot(h_scr[...], w_ref[...], preferred_element_type=F32).astype(o_ref.dtype)


def _in_proj(x_all, gain, sc, sh, w, out_dtype, nrows, lat_rows, seq_rows, tm=512, tn=1024):
    d = x_all.shape[1]
    ncol = w.shape[1]
    grp = functools.partial(_group_of_tile, tm=tm, lat_rows=lat_rows, seq_rows=seq_rows)
    return pl.pallas_call(
        _in_proj_kernel,
        grid=(nrows // tm, ncol // tn),
        in_specs=[pl.BlockSpec((tm, d), lambda i, j: (i, 0)),
                  pl.BlockSpec((1, d), lambda i, j: (0, 0)),
                  pl.BlockSpec((1, 1, d), lambda i, j: (grp(i), 0, 0)),
                  pl.BlockSpec((1, 1, d), lambda i, j: (grp(i), 0, 0)),
                  pl.BlockSpec((d, tn), lambda i, j: (0, j))],
        out_specs=pl.BlockSpec((tm, tn), lambda i, j: (i, j)),
        out_shape=jax.ShapeDtypeStruct((nrows, ncol), out_dtype),
        scratch_shapes=[pltpu.VMEM((tm, d), BF16)],
        compiler_params=_cparams(("parallel", "arbitrary")),
        name="in_proj",
    )(x_all, gain, sc, sh, w)


def _rope(x, c, s):
    w = x.shape[-1]
    lane = lax.broadcasted_iota(jnp.int32, x.shape, 1)
    up = pltpu.roll(x, w - 16, 1)
    dn = pltpu.roll(x, 16, 1)
    return x * c + jnp.where((lane % 32) < 16, up, dn) * s


def _attn_prep_kernel(dk_ref, dv_ref, qa_ref, dq_ref, ckv_ref, kpe_ref, cd_ref, sd_ref, cm_ref, sm_ref,
                      wqb_ref, wkvb_ref, qn_ref, kvn_ref,
                      mq_ref, mk_ref, mv_ref, dqo_ref, dko_ref, dvo_ref):
    cd4 = jnp.concatenate([cd_ref[...]] * N_HEADS, axis=1)
    sd4 = jnp.concatenate([sd_ref[...]] * N_HEADS, axis=1)
    cm, sm = cm_ref[...], sm_ref[...]
    dqo_ref[...] = (_rope(dq_ref[...], cd4, sd4) * DIF_QK ** -0.5).astype(BF16)
    dko_ref[...] = _rope(dk_ref[...], cd4, sd4).astype(BF16)
    dvo_ref[...] = dv_ref[...].astype(BF16)

    q = _bdot(_rms(qa_ref[...]) * qn_ref[...], wqb_ref[...])
    kv = _bdot(_rms(ckv_ref[...]) * kvn_ref[...], wkvb_ref[...])
    kpe = kpe_ref[...]
    lane = lax.broadcasted_iota(jnp.int32, kpe.shape, 1)
    kpe = _rope(jnp.where(lane < MLA_ROPE, kpe, 0.0), cm, sm).astype(BF16)
    for h in range(N_HEADS):
        o = h * MLA_QK_PAD
        mq_ref[:, o:o + MLA_NOPE] = (q[:, o:o + MLA_NOPE] * MLA_SCALE).astype(BF16)
        mq_ref[:, o + MLA_NOPE:o + MLA_QK_PAD] = (
            _rope(q[:, o + MLA_NOPE:o + MLA_QK_PAD], cm, sm) * MLA_SCALE).astype(BF16)
        mk_ref[:, o:o + MLA_NOPE] = kv[:, h * MLA_NOPE:(h + 1) * MLA_NOPE].astype(BF16)
        mk_ref[:, o + MLA_NOPE:o + MLA_QK_PAD] = kpe
    mv_ref[...] = kv[:, N_HEADS * MLA_NOPE:].astype(BF16)


def _attn_prep(u, tabs, wqb, wkvb, qn, kvn, tm=256):
    t_all = u.shape[0]
    cd, sd, cm, sm = tabs
    col = lambda w, cb: pl.BlockSpec((tm, w), lambda i, cb=cb: (i, cb))
    tab = pl.BlockSpec((tm, LANE), lambda i: (i, 0))
    full = lambda a: pl.BlockSpec(a.shape, lambda i: (0,) * a.ndim)
    out = lambda w: pl.BlockSpec((tm, w), lambda i: (i, 0))
    shp = lambda w: jax.ShapeDtypeStruct((t_all, w), BF16)
    return pl.pallas_call(
        _attn_prep_kernel,
        grid=(t_all // tm,),
        in_specs=[col(512, CB_DK), col(512, CB_DV), col(512, CB_QA), col(512, CB_DQ),
                  col(256, CB_CKV_256), col(128, CB_KPE_128), tab, tab, tab, tab,
                  full(wqb), full(wkvb), full(qn), full(kvn)],
        out_specs=[out(1024), out(1024), out(512), out(512), out(512), out(512)],
        out_shape=[shp(1024), shp(1024), shp(512), shp(512), shp(512), shp(512)],
        compiler_params=_cparams(("parallel",)),
        name="attn_prep",
    )(u, u, u, u, u, u, cd, sd, cm, sm, wqb, wkvb, qn, kvn)


def _conv3(x, prev_blk, next_blk, w, first, last):
    tm = x.shape[0]
    row = lax.broadcasted_iota(jnp.int32, x.shape, 0)
    prev_row = jnp.where(first, 0.0, prev_blk[SUBLANE - 1:SUBLANE, :])
    next_row = jnp.where(last, 0.0, next_blk[0:1, :])
    xm = jnp.where(row == 0, prev_row, pltpu.roll(x, 1, 0))
    xp = jnp.where(row == tm - 1, next_row, pltpu.roll(x, tm - 1, 0))
    return xm * w[0:1, :] + x * w[1:2, :] + xp * w[2:3, :]


def _seq_edges(tile, tm, seq_rows, lat_rows, ctx_len):
    r = tile * tm
    in_lat = r < lat_rows
    pos = jnp.where(in_lat, r % seq_rows, (r - lat_rows) % ctx_len)
    length = jnp.where(in_lat, seq_rows, ctx_len)
    return pos == 0, pos + tm == length


def _ml_prep_kernel(seq_rows, lat_rows, ctx_len,
                    q_ref, qp_ref, qn_ref, k_ref, kp_ref, kn_ref, gblk_ref, wq_ref, wk_ref, gb_ref,
                    qo_ref, ko_ref, go_ref):
    tm = q_ref.shape[0]
    first, last = _seq_edges(pl.program_id(0), tm, seq_rows, lat_rows, ctx_len)
    qo_ref[...] = _silu(_conv3(q_ref[...], qp_ref[...], qn_ref[...], wq_ref[...], first, last))
    ko_ref[...] = _silu(_conv3(k_ref[...], kp_ref[...], kn_ref[...], wk_ref[...], first, last)) * HEAD_W ** -0.5
    go_ref[...] = gblk_ref[:, MLA_ROPE:MLA_ROPE + 16] + gb_ref[...]


def _halo_specs(tm, w, cb, t_all, r0=0):
    nb8 = t_all // SUBLANE
    r8 = tm // SUBLANE
    return [pl.BlockSpec((tm, w), lambda i: (i + r0, cb)),
            pl.BlockSpec((SUBLANE, w), lambda i: (jnp.maximum((i + r0) * r8 - 1, 0), cb)),
            pl.BlockSpec((SUBLANE, w), lambda i: (jnp.minimum((i + r0 + 1) * r8, nb8 - 1), cb))]


def _ml_prep(u, wq, wk, gb, seq_rows, lat_rows, ctx_len, tm=256):
    t_all = u.shape[0]
    full = lambda a: pl.BlockSpec(a.shape, lambda i: (0,) * a.ndim)
    return pl.pallas_call(
        functools.partial(_ml_prep_kernel, seq_rows, lat_rows, ctx_len),
        grid=(t_all // tm,),
        in_specs=_halo_specs(tm, 512, CB_MQ, t_all) + _halo_specs(tm, 512, CB_MK, t_all)
        + [pl.BlockSpec((tm, LANE), lambda i: (i, CB_KPE_128)), full(wq), full(wk), full(gb)],
        out_specs=[pl.BlockSpec((tm, 512), lambda i: (i, 0)), pl.BlockSpec((tm, 512), lambda i: (i, 0)),
                   pl.BlockSpec((tm, 16), lambda i: (i, 0))],
        out_shape=[jax.ShapeDtypeStruct((t_all, 512), F32), jax.ShapeDtypeStruct((t_all, 512), F32),
                   jax.ShapeDtypeStruct((t_all, 16), F32)],
        compiler_params=_cparams(("parallel",)),
        name="mlstm_prep",
    )(u, u, u, u, u, u, u, wq, wk, gb)


def _hy_prep_kernel(r0, seq_rows, lat_rows, ctx_len,
                    a_ref, ap_ref, an_ref, b_ref, bp_ref, bn_ref, c_ref, cp_ref, cn_ref, w_ref, bias_ref,
                    z_ref, x0_ref):
    tm = a_ref.shape[0]
    first, last = _seq_edges(pl.program_id(0) + r0, tm, seq_rows, lat_rows, ctx_len)
    w, bias = w_ref[...], bias_ref[...]
    outs = []
    for s, (m, p, n) in enumerate(((a_ref, ap_ref, an_ref), (b_ref, bp_ref, bn_ref), (c_ref, cp_ref, cn_ref))):
        sl = slice(s * BRANCH_W, (s + 1) * BRANCH_W)
        outs.append(_conv3(m[...], p[...], n[...], w[:, sl], first, last) + bias[:, sl])
    x0_ref[...] = outs[0]
    z_ref[...] = outs[2] * outs[1]


def _hy_prep(u, w, bias, row0, nrows, seq_rows, lat_rows, ctx_len, tm=256):
    t_all = u.shape[0]
    r0 = row0 // tm
    full = lambda a: pl.BlockSpec(a.shape, lambda i: (0,) * a.ndim)
    halo = lambda cb: _halo_specs(tm, 512, cb, t_all, r0)
    return pl.pallas_call(
        functools.partial(_hy_prep_kernel, r0, seq_rows, lat_rows, ctx_len),
        grid=(nrows // tm,),
        in_specs=halo(CB_HX0) + halo(CB_HX1) + halo(CB_HV) + [full(w), full(bias)],
        out_specs=[pl.BlockSpec((tm, 512), lambda i: (i, 0)), pl.BlockSpec((tm, 512), lambda i: (i, 0))],
        out_shape=[jax.ShapeDtypeStruct((nrows, 512), F32), jax.ShapeDtypeStruct((nrows, 512), F32)],
        compiler_params=_cparams(("parallel",)),
        name="hyena_prep",
    )(u, u, u, u, u, u, u, u, u, w, bias)


def _attn_kernel(n_maps, seg_lens, tk, lam_init, *refs):
    q_ref = refs[0]
    kv_refs = refs[1:1 + 2 * len(seg_lens)]
    pos = 1 + 2 * len(seg_lens)
    if n_maps == 2:
        lam_ref, gain_ref = refs[pos], refs[pos + 1]
        pos += 2
    o_ref = refs[pos]

    q = q_ref[...]
    tq = q.shape[0]
    if n_maps == 1:
        qs = [q]
    else:
        lane = lax.broadcasted_iota(jnp.int32, q.shape, 1)
        zero = jnp.zeros_like(q)
        qs = [jnp.where(lane < DIF_QK, q, zero), jnp.where(lane >= DIF_QK, q, zero)]

    def step(k, v, carry):
        new = []
        for qm, (m, l, acc) in zip(qs, carry):
            s = lax.dot_general(qm, k, (((1,), (1,)), ((), ())), preferred_element_type=F32)
            m_new = jnp.maximum(m, jnp.max(s, axis=-1, keepdims=True))
            p = jnp.exp(s - m_new)
            alpha = jnp.exp(m - m_new)
            l_new = alpha * l + jnp.sum(p, axis=-1, keepdims=True)
            acc_new = alpha * acc + jnp.dot(p.astype(BF16), v, preferred_element_type=F32)
            new.append((m_new, l_new, acc_new))
        return tuple(new)

    carry = tuple((jnp.full((tq, 1), -jnp.inf, F32), jnp.zeros((tq, 1), F32), jnp.zeros((tq, HEAD_W), F32))
                  for _ in range(n_maps))
    for si, slen in enumerate(seg_lens):
        k_ref, v_ref = kv_refs[2 * si], kv_refs[2 * si + 1]
        chunk = min(tk, slen)

        def body(c, carry, k_ref=k_ref, v_ref=v_ref, chunk=chunk):
            start = pl.multiple_of(c * chunk, chunk)
            return step(k_ref[pl.ds(start, chunk), :], v_ref[pl.ds(start, chunk), :], carry)

        carry = lax.fori_loop(0, slen // chunk, body, carry)

    outs = [acc / l for (_, l, acc) in carry]
    if n_maps == 1:
        o_ref[...] = outs[0].astype(o_ref.dtype)
    else:
        lp = lam_ref[...]
        lam = (jnp.exp(jnp.sum(lp[0:1] * lp[1:2], axis=-1, keepdims=True))
               - jnp.exp(jnp.sum(lp[2:3] * lp[3:4], axis=-1, keepdims=True)) + lam_init)
        o = outs[0] - lam * outs[1]
        o_ref[...] = (_rms(o) * gain_ref[...] * (1.0 - lam_init)).astype(o_ref.dtype)


def _attention(q, k, v, n_maps, q_row0, q_len, segs, batch, qk_w, lam=None, gain=None, lam_init=0.0,
               tq=256, tk=512):
    nq = q_len // tq
    qb0 = q_row0 // tq
    in_specs = [pl.BlockSpec((tq, qk_w), lambda b, h, i: (qb0 + b * nq + i, h))]
    args = [q]
    for (r0, slen) in segs:
        kb0 = r0 // slen
        in_specs.append(pl.BlockSpec((slen, qk_w), lambda b, h, i, kb0=kb0: (kb0 + b, h)))
        in_specs.append(pl.BlockSpec((slen, HEAD_W), lambda b, h, i, kb0=kb0: (kb0 + b, h)))
        args += [k, v]
    if n_maps == 2:
        in_specs += [pl.BlockSpec(lam.shape, lambda b, h, i: (0, 0)),
                     pl.BlockSpec(gain.shape, lambda b, h, i: (0, 0))]
        args += [lam, gain]
    return pl.pallas_call(
        functools.partial(_attn_kernel, n_maps, tuple(s for _, s in segs), tk, lam_init),
        grid=(batch, N_HEADS, nq),
        in_specs=in_specs,
        out_specs=pl.BlockSpec((tq, HEAD_W), lambda b, h, i: (b * nq + i, h)),
        out_shape=jax.ShapeDtypeStruct((batch * q_len, N_HEADS * HEAD_W), BF16),
        compiler_params=_cparams(("parallel", "parallel", "arbitrary")),
        name="attention_maps%d" % n_maps,
    )(*args)


ML_STEP = 256


def _mlstm_kernel(rev, with_out, n_steps, *refs):
    q_ref, k_ref, v_ref, g_ref, gt_ref, c0_ref, n0_ref, m0_ref = refs[:8]
    pos = 8
    if with_out:
        h_ref = refs[pos]
        pos += 1
    cf_ref, nf_ref, mf_ref, c_scr, n_scr, m_scr = refs[pos:pos + 6]
    j = pl.program_id(1)

    @pl.when(j == 0)
    def _():
        c_scr[...] = c0_ref[0]
        n_scr[...] = n0_ref[0]
        m_scr[...] = m0_ref[0]

    d = 1 if rev else 0
    ti = lax.broadcasted_iota(jnp.int32, (ML_CHUNK, ML_CHUNK), 0)
    si = lax.broadcasted_iota(jnp.int32, (ML_CHUNK, ML_CHUNK), 1)
    tri = (si >= ti) if rev else (si <= ti)
    tri_f = tri.astype(F32)
    tri_t = ((ti >= si) if rev else (ti <= si)).astype(F32)

    n_chunks = ML_STEP // ML_CHUNK
    order = range(n_chunks - 1, -1, -1) if rev else range(n_chunks)
    for c in order:
        o = c * ML_CHUNK
        gcol = g_ref[o:o + ML_CHUNK, :]
        grow = gt_ref[0, :, o:o + ML_CHUNK]
        li_col = gcol[:, d * 8:d * 8 + 4]
        lf_col = _log_sigmoid(gcol[:, d * 8 + 4:d * 8 + 8])
        li_row = grow[d * 8:d * 8 + 4, :]
        lf_row = _log_sigmoid(grow[d * 8 + 4:d * 8 + 8, :])
        bc_col = _hdot(tri_f, lf_col)
        bc_row = _hdot(lf_row, tri_t)
        f_tot = jnp.sum(lf_col, axis=0, keepdims=True)
        for h in range(N_HEADS):
            hs = slice(h * HEAD_W, (h + 1) * HEAD_W)
            qc = q_ref[o:o + ML_CHUNK, hs]
            kc = k_ref[o:o + ML_CHUNK, hs]
            vc = v_ref[o:o + ML_CHUNK, hs]
            bcc, bcr = bc_col[:, h:h + 1], bc_row[h:h + 1, :]
            icol, irow = li_col[:, h:h + 1], li_row[h:h + 1, :]
            fh = f_tot[:, h:h + 1]
            c_mat = c_scr[h]
            n_vec = n_scr[h]
            m = m_scr[h][:, 0:1]
            g_end_r = fh - bcr + irow
            g_end_c = fh - bcc + icol
            m_new = jnp.maximum(fh + m, jnp.max(g_end_r, axis=-1, keepdims=True))
            decay = jnp.exp(fh + m - m_new)
            wk = jnp.exp(g_end_c - m_new)
            kw = kc * wk
            upd = lax.dot_general(kw.astype(BF16), vc.astype(BF16), (((0,), (0,)), ((), ())),
                                  preferred_element_type=F32)
            if with_out:
                inter = bcc + m
                log_d = jnp.where(tri, bcc - bcr + irow, -jnp.inf)
                m_t = jnp.maximum(inter, jnp.max(log_d, axis=-1, keepdims=True))
                s = _bdot_nt(qc, kc) * jnp.exp(log_d - m_t)
                w_inter = jnp.exp(inter - m_t)
                num = _bdot(s, vc) + w_inter * _bdot(qc, c_mat)
                den = jnp.sum(s, axis=-1, keepdims=True) + w_inter * jnp.sum(qc * n_vec, axis=-1, keepdims=True)
                h_ref[0, o:o + ML_CHUNK, hs] = num / jnp.maximum(jnp.abs(den), jnp.exp(-m_t))
            c_scr[h] = decay * c_mat + upd
            n_scr[h] = decay * n_vec + jnp.sum(kw, axis=0, keepdims=True)
            m_scr[h] = jnp.broadcast_to(m_new, (1, HEAD_W))

    @pl.when(j == n_steps - 1)
    def _():
        cf_ref[0] = c_scr[...]
        nf_ref[0] = n_scr[...]
        mf_ref[0] = m_scr[...]


def _mlstm_dir(q, k, u, g, gt, init, rev, with_out, row0, seq_len, batch):
    n_steps = seq_len // ML_STEP
    b0 = row0 // ML_STEP

    def blk(b, j):
        jj = (n_steps - 1 - j) if rev else j
        return b0 + b * n_steps + jj

    def gt_idx(b, j):
        jj = (n_steps - 1 - j) if rev else j
        return (b, 0, jj)

    st_specs = [pl.BlockSpec((1, N_HEADS, HEAD_W, HEAD_W), lambda b, j: (b, 0, 0, 0)),
                pl.BlockSpec((1, N_HEADS, 1, HEAD_W), lambda b, j: (b, 0, 0, 0)),
                pl.BlockSpec((1, N_HEADS, 1, HEAD_W), lambda b, j: (b, 0, 0, 0))]
    st_shapes = [jax.ShapeDtypeStruct((batch, N_HEADS, HEAD_W, HEAD_W), F32),
                 jax.ShapeDtypeStruct((batch, N_HEADS, 1, HEAD_W), F32),
                 jax.ShapeDtypeStruct((batch, N_HEADS, 1, HEAD_W), F32)]
    in_specs = [pl.BlockSpec((ML_STEP, 512), lambda b, j: (blk(b, j), 0)),
                pl.BlockSpec((ML_STEP, 512), lambda b, j: (blk(b, j), 0)),
                pl.BlockSpec((ML_STEP, 512), lambda b, j: (blk(b, j), CB_MV)),
                pl.BlockSpec((ML_STEP, 16), lambda b, j: (blk(b, j), 0)),
                pl.BlockSpec((1, 16, ML_STEP), gt_idx)] + st_specs
    out_specs, out_shapes = list(st_specs), list(st_shapes)
    if with_out:
        out_specs = [pl.BlockSpec((1, ML_STEP, 512), lambda b, j: (b, (n_steps - 1 - j) if rev else j, 0))] + out_specs
        out_shapes = [jax.ShapeDtypeStruct((batch, seq_len, 512), F32)] + out_shapes
    res = pl.pallas_call(
        functools.partial(_mlstm_kernel, rev, with_out, n_steps),
        grid=(batch, n_steps),
        in_specs=in_specs,
        out_specs=out_specs,
        out_shape=out_shapes,
        scratch_shapes=[pltpu.VMEM((N_HEADS, HEAD_W, HEAD_W), F32), pltpu.VMEM((N_HEADS, 1, HEAD_W), F32),
                        pltpu.VMEM((N_HEADS, 1, HEAD_W), F32)],
        compiler_params=_cparams(("parallel", "arbitrary")),
        name="mlstm_%s" % ("bwd" if rev else "fwd"),
    )(q, k, u, g, gt, *init)
    if with_out:
        return res[0], tuple(res[1:])
    return None, tuple(res)


def _hy_filter_kernel(n, feat_ref, w1_ref, b1_ref, w2_ref, b2_ref, w3_ref, freq_ref, decay_ref,
                      h_ref, norm_ref):
    i = pl.program_id(0)
    tm = feat_ref.shape[0]
    freq = freq_ref[...]
    hid = jnp.sin(freq[0:1, :] * (_hdot(feat_ref[...], w1_ref[...]) + b1_ref[...]))
    hid = jnp.sin(freq[1:2, :] * (_hdot(hid, w2_ref[...]) + b2_ref[...]))
    row = lax.broadcasted_iota(jnp.int32, (tm, 2 * BRANCH_W), 0) + i * tm
    col = lax.broadcasted_iota(jnp.int32, (tm, 2 * BRANCH_W), 1)
    t = row.astype(F32) / n
    h = _hdot(hid, w3_ref[...]) * jnp.exp(-t * jnp.abs(decay_ref[...]))
    h = jnp.where((row == 0) & (col >= BRANCH_W), 0.0, h)
    h_ref[0] = h[:, :BRANCH_W]
    h_ref[1] = h[:, BRANCH_W:]

    @pl.when(i == 0)
    def _():
        norm_ref[...] = jnp.zeros_like(norm_ref)

    norm_ref[...] += jnp.sum(jnp.abs(h), axis=0, keepdims=True)


def _hy_filter(n, feat, w1p, b1, w2, b2, w3, freq, decay_flat):
    tm = min(n, 512)
    full = lambda a: pl.BlockSpec(a.shape, lambda i: (0,) * a.ndim)
    return pl.pallas_call(
        functools.partial(_hy_filter_kernel, n),
        grid=(n // tm,),
        in_specs=[pl.BlockSpec((tm, LANE), lambda i: (i, 0)), full(w1p), full(b1), full(w2), full(b2),
                  full(w3), full(freq), full(decay_flat)],
        out_specs=[pl.BlockSpec((2, tm, BRANCH_W), lambda i: (0, i, 0)),
                   pl.BlockSpec((1, 2 * BRANCH_W), lambda i: (0, 0))],
        out_shape=[jax.ShapeDtypeStruct((2, n, BRANCH_W), F32), jax.ShapeDtypeStruct((1, 2 * BRANCH_W), F32)],
        compiler_params=_cparams(("arbitrary",)),
        name="hyena_filter",
    )(feat, w1p, b1, w2, b2, w3, freq, decay_flat)


def _dft_a_kernel(f_ref, x_ref, o_ref):
    o_ref[0] = _hdot(f_ref[...], x_ref[0])


def _dft_stage_a(fa, x, tc=2048):
    g, n1h, cols = x.shape
    rows = fa.shape[0]
    return pl.pallas_call(
        _dft_a_kernel,
        grid=(g, cols // tc),
        in_specs=[pl.BlockSpec(fa.shape, lambda a, j: (0, 0)),
                  pl.BlockSpec((1, n1h, tc), lambda a, j: (a, 0, j))],
        out_specs=pl.BlockSpec((1, rows, tc), lambda a, j: (a, 0, j)),
        out_shape=jax.ShapeDtypeStruct((g, rows, cols), F32),
        compiler_params=_cparams(("parallel", "parallel")),
        name="hyena_dft_a",
    )(fa, x)


def _spectral_product(s, batch):
    kk = s.shape[0] // 2
    c = BRANCH_W
    x0, x1 = s[:, batch * c:(batch + 1) * c], s[:, (batch + 1) * c:(batch + 2) * c]
    hr = x0[:kk] + x1[:kk]
    hi = x0[kk:] - x1[kk:]
    outs = []
    for b in range(batch):
        z = s[:, b * c:(b + 1) * c]
        zr, zi = z[:kk], z[kk:]
        outs.append(jnp.concatenate([zr * hr - zi * hi, zr * hi + zi * hr], axis=0))
    return jnp.concatenate(outs, axis=1)


def _dft_c_kernel(batch, az_ref, ah_ref, g_ref, gt_ref, o_ref):
    parts = [az_ref[b, :, 0].reshape(2 * LANE, BRANCH_W) for b in range(batch)]
    parts += [ah_ref[f, :, 0].reshape(2 * LANE, BRANCH_W) for f in range(2)]
    s = _hdot(g_ref[0], jnp.concatenate(parts, axis=1))
    p = _hdot(gt_ref[0], _spectral_product(s, batch))
    for b in range(batch):
        o_ref[b, :, 0] = p[:, b * BRANCH_W:(b + 1) * BRANCH_W].reshape(2, LANE, BRANCH_W)


def _dft_stage_c(az, ah, g, gt):
    batch, _, n1, _, c = az.shape
    return pl.pallas_call(
        functools.partial(_dft_c_kernel, batch),
        grid=(n1,),
        in_specs=[pl.BlockSpec((batch, 2, 1, LANE, c), lambda k: (0, 0, k, 0, 0)),
                  pl.BlockSpec((2, 2, 1, LANE, c), lambda k: (0, 0, k, 0, 0)),
                  pl.BlockSpec((1, 2 * LANE, 2 * LANE), lambda k: (k, 0, 0)),
                  pl.BlockSpec((1, 2 * LANE, 2 * LANE), lambda k: (k, 0, 0))],
        out_specs=pl.BlockSpec((batch, 2, 1, LANE, c), lambda k: (0, 0, k, 0, 0)),
        out_shape=jax.ShapeDtypeStruct(az.shape, F32),
        compiler_params=_cparams(("parallel",)),
        name="hyena_dft_c",
    )(az, ah, g, gt)


def _hy_epilogue(y, z, x0, norm, skip):
    reps = y.shape[1] // BRANCH_W
    nsum = norm[:, :BRANCH_W] + norm[:, BRANCH_W:]
    inv = jnp.concatenate([1.0 / nsum] * reps, axis=1)
    sk = jnp.concatenate([skip] * reps, axis=1)
    return (x0 * (y * inv + z * sk)).astype(BF16)


def _dft_a_inv_kernel(f_ref, p_ref, z_ref, x0_ref, norm_ref, skip_ref, o_ref):
    y = _hdot(f_ref[...], p_ref[0])
    o_ref[0] = _hy_epilogue(y, z_ref[0], x0_ref[0], norm_ref[...], skip_ref[...])


def _dft_stage_a_inv(fai, p, z, x0, norm, skip, tc=2048):
    batch, rows, cols = p.shape
    n1h = z.shape[1]
    return pl.pallas_call(
        _dft_a_inv_kernel,
        grid=(batch, cols // tc),
        in_specs=[pl.BlockSpec(fai.shape, lambda b, j: (0, 0)),
                  pl.BlockSpec((1, rows, tc), lambda b, j: (b, 0, j)),
                  pl.BlockSpec((1, n1h, tc), lambda b, j: (b, 0, j)),
                  pl.BlockSpec((1, n1h, tc), lambda b, j: (b, 0, j)),
                  pl.BlockSpec(norm.shape, lambda b, j: (0, 0)),
                  pl.BlockSpec(skip.shape, lambda b, j: (0, 0))],
        out_specs=pl.BlockSpec((1, n1h, tc), lambda b, j: (b, 0, j)),
        out_shape=jax.ShapeDtypeStruct((batch, n1h, cols), BF16),
        compiler_params=_cparams(("parallel", "parallel")),
        name="hyena_dft_a_inv",
    )(fai, p, z, x0, norm, skip)


def _dft_tables(n):
    big_l = 2 * n
    n1 = big_l // LANE
    n1h = n1 // 2
    two_pi = 2.0 * math.pi
    k1 = jnp.arange(n1, dtype=jnp.int32)
    th = two_pi * ((k1[:, None] * jnp.arange(n1h, dtype=jnp.int32)[None, :]) % n1).astype(F32) / n1
    fa = jnp.concatenate([jnp.cos(th), -jnp.sin(th)], axis=0)
    fai = jnp.concatenate([jnp.cos(th).T, -jnp.sin(th).T], axis=1) / big_l
    s2 = jnp.arange(LANE, dtype=jnp.int32)
    k = k1[:, None, None] + n1 * s2[None, :, None]
    ph = two_pi * ((k * s2[None, None, :]) % big_l).astype(F32) / big_l
    gr, gi = jnp.cos(ph), -jnp.sin(ph)
    g = jnp.concatenate([jnp.concatenate([gr, -gi], axis=2), jnp.concatenate([gi, gr], axis=2)], axis=1)
    return fa, fai, g, jnp.swapaxes(g, 1, 2)


def _hy_small_kernel(batch, n, z_ref, x0_ref, h_ref, f_ref, fi_ref, norm_ref, skip_ref, o_ref):
    parts = [z_ref[b] for b in range(batch)] + [h_ref[0], h_ref[1]]
    s = _hdot(f_ref[...], jnp.concatenate(parts, axis=1))
    y = _hdot(fi_ref[...], _spectral_product(s, batch))
    for b in range(batch):
        o_ref[b] = _hy_epilogue(y[:, b * BRANCH_W:(b + 1) * BRANCH_W], z_ref[b], x0_ref[b],
                                norm_ref[...], skip_ref[...])


def _hy_small(z, x0, hfilt, norm, skip):
    batch, n, c = z.shape
    big_l = 2 * n
    kk = jnp.arange(big_l, dtype=jnp.int32)
    th = 2.0 * math.pi * ((kk[:, None] * jnp.arange(n, dtype=jnp.int32)[None, :]) % big_l).astype(F32) / big_l
    f = jnp.concatenate([jnp.cos(th), -jnp.sin(th)], axis=0)
    fi = jnp.concatenate([jnp.cos(th).T, -jnp.sin(th).T], axis=1) / big_l
    full = lambda a: pl.BlockSpec(a.shape, lambda i: (0,) * a.ndim)
    return pl.pallas_call(
        functools.partial(_hy_small_kernel, batch, n),
        grid=(1,),
        in_specs=[full(z), full(x0), full(hfilt), full(f), full(fi), full(norm), full(skip)],
        out_specs=full(z),
        out_shape=jax.ShapeDtypeStruct(z.shape, BF16),
        compiler_params=_cparams(("arbitrary",)),
        name="hyena_small",
    )(z, x0, hfilt, f, fi, norm, skip)


def _hy_features(n):
    t = jnp.arange(n, dtype=F32) / n
    bands = jnp.linspace(1e-4, HY_BANDS - 1, HY_BANDS, dtype=F32)
    ang = 2.0 * math.pi * t[:, None] * bands[None, :]
    feat = jnp.concatenate([t[:, None], jnp.cos(ang), jnp.sin(ang)], axis=-1)
    return jnp.pad(feat, ((0, 0), (0, LANE - HY_EMB)))


def _merge_kernel(hy_ref, mla_ref, dif_ref, hf_ref, hb_ref, mo_ref, gate_ref, wb_ref, mln_ref, o_ref):
    hsum = hf_ref[...] + hb_ref[...]
    mln = mln_ref[...]
    y_ml = jnp.concatenate(
        [_rms(hsum[:, h * HEAD_W:(h + 1) * HEAD_W]) * mln[:, h * HEAD_W:(h + 1) * HEAD_W]
         for h in range(N_HEADS)], axis=1)
    y_ml = _sigmoid(mo_ref[...]) * y_ml
    ys = (hy_ref[...], mla_ref[...], dif_ref[...], y_ml.astype(BF16))
    acc = None
    for i in range(4):
        gi = _sigmoid(gate_ref[:, i * D_MODEL:(i + 1) * D_MODEL].astype(F32))
        term = gi * jnp.dot(ys[i], wb_ref[i], preferred_element_type=F32)
        acc = term if acc is None else acc + term
    o_ref[...] = acc.astype(BF16)


def _merge(y_hy, y_mla, y_dif, h_f, h_b, u, gate, wb, mln, nrows, tm=256):
    row = pl.BlockSpec((tm, 512), lambda i: (i, 0))
    full = lambda a: pl.BlockSpec(a.shape, lambda i: (0,) * a.ndim)
    return pl.pallas_call(
        _merge_kernel,
        grid=(nrows // tm,),
        in_specs=[row, row, row, row, row,
                  pl.BlockSpec((tm, 512), lambda i: (i, CB_MO)),
                  pl.BlockSpec((tm, GATE_COLS), lambda i: (i, 0)),
                  full(wb), full(mln)],
        out_specs=pl.BlockSpec((tm, D_MODEL), lambda i: (i, 0)),
        out_shape=jax.ShapeDtypeStruct((nrows, D_MODEL), BF16),
        compiler_params=_cparams(("parallel",), 56),
        name="branch_merge",
    )(y_hy, y_mla, y_dif, h_f, h_b, u, gate, wb, mln)


def _out_kernel(m_ref, w_ref, x_ref, ga_ref, gain_ref, sc_ref, sh_ref, rw_ref, rb_ref,
                xo_ref, f_ref, idx_ref, gt_ref):
    y = jnp.dot(m_ref[...], w_ref[...], preferred_element_type=F32)
    x = x_ref[...] + ga_ref[0] * y
    xo_ref[...] = x
    f = _rms(x) * gain_ref[...] * (1.0 + sc_ref[0]) + sh_ref[0]
    f_ref[...] = f
    logits = _bdot(f, rw_ref[...]) + rb_ref[...]
    lane = lax.broadcasted_iota(jnp.int32, logits.shape, 1)
    idx_out = jnp.zeros(logits.shape, jnp.int32)
    val_out = jnp.full(logits.shape, -jnp.inf, F32)
    work = logits
    for k in range(TOP_K):
        mx = jnp.max(work, axis=-1, keepdims=True)
        am = jnp.min(jnp.where(work == mx, lane, LANE), axis=-1, keepdims=True)
        idx_out = jnp.where(lane == k, am, idx_out)
        val_out = jnp.where(lane == k, mx, val_out)
        work = jnp.where(lane == am, -jnp.inf, work)
    e = jnp.exp(val_out - jnp.max(val_out, axis=-1, keepdims=True))
    gt_ref[...] = e / jnp.sum(e, axis=-1, keepdims=True)
    idx_ref[...] = idx_out


def _out_proj(merged, w_out, x_all, ga, gain, sc, sh, rw, rb, nrows, lat_rows, seq_rows, tm=256):
    grp = functools.partial(_group_of_tile, tm=tm, lat_rows=lat_rows, seq_rows=seq_rows)
    row = lambda w: pl.BlockSpec((tm, w), lambda i: (i, 0))
    full = lambda a: pl.BlockSpec(a.shape, lambda i: (0,) * a.ndim)
    mod = pl.BlockSpec((1, 1, D_MODEL), lambda i: (grp(i), 0, 0))
    return pl.pallas_call(
        _out_kernel,
        grid=(nrows // tm,),
        in_specs=[row(D_MODEL), full(w_out), row(D_MODEL), mod, full(gain), mod, mod, full(rw), full(rb)],
        out_specs=[row(D_MODEL), row(D_MODEL), row(LANE), row(LANE)],
        out_shape=[jax.ShapeDtypeStruct((nrows, D_MODEL), F32), jax.ShapeDtypeStruct((nrows, D_MODEL), F32),
                   jax.ShapeDtypeStruct((nrows, LANE), jnp.int32), jax.ShapeDtypeStruct((nrows, LANE), F32)],
        compiler_params=_cparams(("parallel",)),
        name="out_proj_router",
    )(merged, w_out, x_all, ga, gain, sc, sh, rw, rb)


def _moe_kernel(blk_e_ref, tok_ref, gate_ref, f_hbm, wgu_ref, bgu_ref, wd_ref, bd_ref, o_ref, xbuf, sem):
    def row_copy(r):
        return pltpu.make_async_copy(f_hbm.at[pl.ds(tok_ref[0, 0, r], 1), :], xbuf.at[pl.ds(r, 1), :], sem)

    def issue(r, carry):
        row_copy(r).start()
        return carry

    def drain(r, carry):
        row_copy(r).wait()
        return carry

    lax.fori_loop(0, MOE_BLOCK, issue, 0)
    lax.fori_loop(0, MOE_BLOCK, drain, 0)
    gu = _bdot(xbuf[...], wgu_ref[0]) + bgu_ref[0]
    g = jnp.minimum(gu[:, :D_EXPERT], SWIGLU_LIMIT)
    u = jnp.clip(gu[:, D_EXPERT:], -SWIGLU_LIMIT, SWIGLU_LIMIT)
    act = g * _sigmoid(SWIGLU_ALPHA * g) * (u + 1.0)
    y = _bdot(act, wd_ref[0]) + bd_ref[0]
    o_ref[...] = y * gate_ref[...]


def _moe_experts(f_all, blk_e, slot_tok, slot_gate, wgu, bgu, wd, bd):
    n_blocks = blk_e.shape[0]
    slots = n_blocks * MOE_BLOCK
    grid_spec = pltpu.PrefetchScalarGridSpec(
        num_scalar_prefetch=1,
        grid=(n_blocks,),
        in_specs=[pl.BlockSpec((1, 1, MOE_BLOCK), lambda i, be: (i, 0, 0), memory_space=pltpu.SMEM),
                  pl.BlockSpec((MOE_BLOCK, 1), lambda i, be: (i, 0)),
                  pl.BlockSpec(memory_space=pl.ANY),
                  pl.BlockSpec((1, D_MODEL, 2 * D_EXPERT), lambda i, be: (be[i], 0, 0)),
                  pl.BlockSpec((1, 1, 2 * D_EXPERT), lambda i, be: (be[i], 0, 0)),
                  pl.BlockSpec((1, D_EXPERT, D_MODEL), lambda i, be: (be[i], 0, 0)),
                  pl.BlockSpec((1, 1, D_MODEL), lambda i, be: (be[i], 0, 0))],
        out_specs=pl.BlockSpec((MOE_BLOCK, D_MODEL), lambda i, be: (i, 0)),
        scratch_shapes=[pltpu.VMEM((MOE_BLOCK, D_MODEL), F32), pltpu.SemaphoreType.DMA(())],
    )
    return pl.pallas_call(
        _moe_kernel,
        grid_spec=grid_spec,
        out_shape=jax.ShapeDtypeStruct((slots, D_MODEL), F32),
        compiler_params=_cparams(("arbitrary",), 56),
        name="moe_experts",
    )(blk_e, slot_tok.reshape(n_blocks, 1, MOE_BLOCK), slot_gate.reshape(slots, 1), f_all,
      wgu, bgu.reshape(N_EXPERTS, 1, 2 * D_EXPERT), wd, bd.reshape(N_EXPERTS, 1, D_MODEL))


def _combine_kernel(final, dest_ref, y_hbm, x_ref, gf_ref, fn_ref, o_ref, ybuf, sem):
    tm = x_ref.shape[0]

    def row_copy(r, k):
        return pltpu.make_async_copy(y_hbm.at[pl.ds(dest_ref[0, 0, r * TOP_K + k], 1), :],
                                     ybuf.at[k, pl.ds(r, 1), :], sem)

    def issue(r, carry):
        for k in range(TOP_K):
            row_copy(r, k).start()
        return carry

    def drain(r, carry):
        for k in range(TOP_K):
            row_copy(r, k).wait()
        return carry

    lax.fori_loop(0, tm, issue, 0)
    lax.fori_loop(0, tm, drain, 0)
    moe = (ybuf[0] + ybuf[1]) + (ybuf[2] + ybuf[3])
    x = x_ref[...] + gf_ref[0] * moe
    if final:
        x = _rms(x) * fn_ref[...]
    o_ref[...] = x


def _moe_combine(y_slots, dest, x_all, gf, final_gain, final, nrows, lat_rows, seq_rows, tm=128):
    grp = functools.partial(_group_of_tile, tm=tm, lat_rows=lat_rows, seq_rows=seq_rows)
    grid_spec = pl.GridSpec(
        grid=(nrows // tm,),
        in_specs=[pl.BlockSpec((1, 1, tm * TOP_K), lambda i: (i, 0, 0), memory_space=pltpu.SMEM),
                  pl.BlockSpec(memory_space=pl.ANY),
                  pl.BlockSpec((tm, D_MODEL), lambda i: (i, 0)),
                  pl.BlockSpec((1, 1, D_MODEL), lambda i: (grp(i), 0, 0)),
                  pl.BlockSpec((1, D_MODEL), lambda i: (0, 0))],
        out_specs=pl.BlockSpec((tm, D_MODEL), lambda i: (i, 0)),
        scratch_shapes=[pltpu.VMEM((TOP_K, tm, D_MODEL), F32), pltpu.SemaphoreType.DMA(())],
    )
    return pl.pallas_call(
        functools.partial(_combine_kernel, final),
        grid_spec=grid_spec,
        out_shape=jax.ShapeDtypeStruct((nrows, D_MODEL), F32),
        compiler_params=_cparams(("arbitrary",)),
        name="moe_combine",
    )(dest.reshape(nrows // tm, 1, tm * TOP_K), y_slots, x_all, gf, final_gain)


def _route(top_idx, gates):
    n = top_idx.shape[0]
    a = n * TOP_K
    onehot = jnp.sum((top_idx[:, :, None] == jnp.arange(N_EXPERTS, dtype=jnp.int32)).astype(jnp.int32), axis=1)
    csum = jnp.cumsum(onehot, axis=0)
    rank = csum - onehot
    counts = csum[-1]
    padded = (counts + MOE_BLOCK - 1) // MOE_BLOCK * MOE_BLOCK
    pend = jnp.cumsum(padded)
    pstart = pend - padded
    dest = pstart[top_idx] + jnp.take_along_axis(rank, top_idx, axis=1)
    n_blocks = (a + N_EXPERTS * (MOE_BLOCK - 1)) // MOE_BLOCK + 1
    slots = n_blocks * MOE_BLOCK
    tok = jnp.broadcast_to(jnp.arange(n, dtype=jnp.int32)[:, None], (n, TOP_K))
    slot_tok = jnp.zeros((slots,), jnp.int32).at[dest.reshape(-1)].set(tok.reshape(-1))
    slot_gate = jnp.zeros((slots,), F32).at[dest.reshape(-1)].set(gates.reshape(-1))
    blk_e = jnp.minimum(jnp.searchsorted(pend, jnp.arange(n_blocks, dtype=jnp.int32) * MOE_BLOCK, side='right'),
                        N_EXPERTS - 1).astype(jnp.int32)
    return blk_e, slot_tok, slot_gate, dest.astype(jnp.int32)


def _rope_tables(n_lat_rows, seq_len, ctx_rows):
    t = jnp.arange(seq_len, dtype=jnp.int32)
    inv = ROPE_THETA ** (-jnp.arange(0, 32, 2, dtype=F32) / 32)

    def cs(p):
        ang = p.astype(F32)[:, None] * inv[None, :]
        return jnp.cos(ang), jnp.sin(ang)

    cr, sr = cs(t // GRID_W)
    cc, sc = cs(t % GRID_W)
    c64 = jnp.concatenate([cr, cr, cc, cc], axis=1)
    s64 = jnp.concatenate([-sr, sr, -sc, sc], axis=1)
    reps = n_lat_rows // seq_len

    def build(c_half2, s_half2):
        c = jnp.concatenate([c64, c_half2], axis=1)
        s = jnp.concatenate([s64, s_half2], axis=1)
        c = jnp.concatenate([c] * reps + [jnp.ones((ctx_rows, LANE), F32)], axis=0)
        s = jnp.concatenate([s] * reps + [jnp.zeros((ctx_rows, LANE), F32)], axis=0)
        return c, s

    cd, sd = build(c64, s64)
    cm, sm = build(jnp.ones_like(c64), jnp.zeros_like(s64))
    return cd, sd, cm, sm


def _layout_w_in(w):
    d = w.shape[0]
    kv_a, dk, dv, mk, mv, mg = (w[:, 0:320], w[:, 320:832], w[:, 832:1344], w[:, 1344:1856],
                                w[:, 1856:2368], w[:, 2368:2384])
    o = 2384
    q_a, dq, mq, mo, hy, gate = (w[:, o:o + 512], w[:, o + 512:o + 1024], w[:, o + 1024:o + 1536],
                                 w[:, o + 1536:o + 2048], w[:, o + 2048:o + 3584], w[:, o + 3584:])
    small = jnp.concatenate([dk, dv, mk, mv, q_a, dq, mq, mo, hy, kv_a[:, :256], kv_a[:, 256:], mg,
                             jnp.zeros((d, SMALL_COLS - 5968), w.dtype)], axis=1)
    return small.astype(BF16), gate.astype(BF16)


def _layout_w_qb(w):
    w = w.reshape(w.shape[0], N_HEADS, MLA_NOPE + MLA_ROPE)
    w = jnp.pad(w, ((0, 0), (0, 0), (0, MLA_QK_PAD - MLA_NOPE - MLA_ROPE)))
    return w.reshape(w.shape[0], N_HEADS * MLA_QK_PAD).astype(BF16)


def _layout_w_kvb(w):
    w = w.reshape(w.shape[0], N_HEADS, MLA_NOPE + HEAD_W)
    return jnp.concatenate([w[:, :, :MLA_NOPE].reshape(w.shape[0], -1),
                            w[:, :, MLA_NOPE:].reshape(w.shape[0], -1)], axis=1).astype(BF16)


def kernel(x, c, ctx, c_ctx, ada_w, ada_b, norm_mix, norm_ffn, w_in, hy_conv, hy_conv_b, hy_w1, hy_b1, hy_w2, hy_b2, hy_w3, hy_freq, hy_decay, hy_skip, mla_q_norm, mla_w_qb, mla_kv_norm, mla_w_kvb, dif_lambda, dif_norm, ml_conv_q, ml_conv_k, ml_gate_b, ml_norm, w_branch, w_out, router_w, router_b, exp_w_gu, exp_b_gu, exp_w_down, exp_b_down, final_norm):
    batch, n, d = x.shape
    n_ctx = ctx.shape[1]
    depth = w_in.shape[0]
    ctx_rows = batch * n_ctx
    lat_rows = batch * n
    t_all = ctx_rows + lat_rows

    assert batch + 1 <= SUBLANE
    cvec = jnp.zeros((SUBLANE, d), F32).at[:batch].set(c).at[batch].set(c_ctx)
    mod = _modulation(cvec, ada_w, ada_b)
    x_all = jnp.concatenate([x.reshape(lat_rows, d), ctx.reshape(ctx_rows, d)], axis=0)
    tabs = _rope_tables(lat_rows, n, ctx_rows)
    fa, fai, g_tab, gt_tab = _dft_tables(n)
    feat_lat, feat_ctx = _hy_features(n), _hy_features(n_ctx)
    zero_state = (jnp.zeros((batch, N_HEADS, HEAD_W, HEAD_W), F32), jnp.zeros((batch, N_HEADS, 1, HEAD_W), F32),
                  jnp.zeros((batch, N_HEADS, 1, HEAD_W), F32))
    n1h = n // LANE

    for l in range(depth):
        last = l == depth - 1
        lam_init = 0.8 - 0.6 * math.exp(-0.3 * l)
        m6 = mod[l].reshape(SUBLANE, 6, 1, d)
        sh_a, sc_a, g_a, sh_f, sc_f, g_f = (m6[:, i] for i in range(6))
        w_small, w_gate = _layout_w_in(w_in[l])
        gain_mix = norm_mix[l].reshape(1, d)

        n_tok = lat_rows if last else t_all
        u = _in_proj(x_all, gain_mix, sc_a, sh_a, w_small, F32, t_all, lat_rows, n)
        gate = _in_proj(x_all, gain_mix, sc_a, sh_a, w_gate, BF16, n_tok, lat_rows, n)

        mq, mk, mv, dq, dk, dv = _attn_prep(
            u, tabs, _layout_w_qb(mla_w_qb[l]), _layout_w_kvb(mla_w_kvb[l]),
            mla_q_norm[l].reshape(1, -1), mla_kv_norm[l].reshape(1, -1))
        ml_q, ml_k, ml_g = _ml_prep(u, ml_conv_q[l], ml_conv_k[l], ml_gate_b[l].reshape(1, 16), n, lat_rows, n_ctx)
        gt_lat = jnp.swapaxes(ml_g[:lat_rows].reshape(batch, n, 16), 1, 2)
        gt_ctx = jnp.swapaxes(ml_g[lat_rows:].reshape(batch, n_ctx, 16), 1, 2)

        lam_p = dif_lambda[l]
        dgain = dif_norm[l].reshape(1, HEAD_W)
        segs = [(lat_rows, n_ctx), (0, n)]
        y_mla = _attention(mq, mk, mv, 1, 0, n, segs, batch, MLA_QK_PAD)
        y_dif = _attention(dq, dk, dv, 2, 0, n, segs, batch, 2 * DIF_QK, lam_p, dgain, lam_init)

        hcf, st_f = _mlstm_dir(ml_q, ml_k, u, ml_g, gt_ctx, zero_state, False, not last, lat_rows, n_ctx, batch)
        hcb, st_b = _mlstm_dir(ml_q, ml_k, u, ml_g, gt_ctx, zero_state, True, not last, lat_rows, n_ctx, batch)
        hlf, _ = _mlstm_dir(ml_q, ml_k, u, ml_g, gt_lat, st_f, False, True, 0, n, batch)
        hlb, _ = _mlstm_dir(ml_q, ml_k, u, ml_g, gt_lat, st_b, True, True, 0, n, batch)

        hw1 = jnp.pad(hy_w1[l], ((0, LANE - HY_EMB), (0, 0)))
        filt_args = (hw1, hy_b1[l].reshape(1, -1), hy_w2[l], hy_b2[l].reshape(1, -1), hy_w3[l], hy_freq[l],
                     hy_decay[l].reshape(1, -1))
        skip = hy_skip[l].reshape(1, -1)
        hconv_b = hy_conv_b[l].reshape(1, -1)
        z, x0 = _hy_prep(u, hy_conv[l], hconv_b, 0, lat_rows, n, lat_rows, n_ctx)
        hfilt, hnorm = _hy_filter(n, feat_lat, *filt_args)
        zr = z.reshape(batch, n1h, LANE * BRANCH_W)
        az = _dft_stage_a(fa, zr)
        ah = _dft_stage_a(fa, hfilt.reshape(2, n1h, LANE * BRANCH_W))
        n1 = 2 * n1h
        pz = _dft_stage_c(az.reshape(batch, 2, n1, LANE, BRANCH_W), ah.reshape(2, 2, n1, LANE, BRANCH_W),
                          g_tab, gt_tab)
        y_hy = _dft_stage_a_inv(fai, pz.reshape(batch, 2 * n1, LANE * BRANCH_W), zr,
                                x0.reshape(batch, n1h, LANE * BRANCH_W), hnorm, skip).reshape(lat_rows, BRANCH_W)

        h_f, h_b = hlf.reshape(lat_rows, -1), hlb.reshape(lat_rows, -1)
        w_o = w_out[l].astype(BF16)
        gain_ffn = norm_ffn[l].reshape(1, d)
        rw = jnp.pad(router_w[l], ((0, 0), (0, LANE - N_EXPERTS))).astype(BF16)
        rb = jnp.concatenate([router_b[l], jnp.full((LANE - N_EXPERTS,), -jnp.inf, F32)]).reshape(1, LANE)

        if not last:
            cseg = [(lat_rows, n_ctx)]
            yc_mla = _attention(mq, mk, mv, 1, lat_rows, n_ctx, cseg, batch, MLA_QK_PAD)
            yc_dif = _attention(dq, dk, dv, 2, lat_rows, n_ctx, cseg, batch, 2 * DIF_QK, lam_p, dgain, lam_init)
            zc, x0c = _hy_prep(u, hy_conv[l], hconv_b, lat_rows, ctx_rows, n, lat_rows, n_ctx)
            hfc, hnc = _hy_filter(n_ctx, feat_ctx, *filt_args)
            yc_hy = _hy_small(zc.reshape(batch, n_ctx, BRANCH_W), x0c.reshape(batch, n_ctx, BRANCH_W), hfc, hnc,
                              skip).reshape(ctx_rows, BRANCH_W)
            cat = lambda a, b: jnp.concatenate([a, b], axis=0)
            y_hy, y_mla, y_dif = cat(y_hy, yc_hy), cat(y_mla, yc_mla), cat(y_dif, yc_dif)
            h_f, h_b = cat(h_f, hcf.reshape(ctx_rows, -1)), cat(h_b, hcb.reshape(ctx_rows, -1))

        merged = _merge(y_hy, y_mla, y_dif, h_f, h_b, u, gate, w_branch[l].astype(BF16), ml_norm[l].reshape(1, -1),
                        n_tok)
        xo, f_all, top_idx, top_gate = _out_proj(merged, w_o, x_all, g_a, gain_ffn, sc_f, sh_f, rw, rb,
                                                 n_tok, lat_rows, n)
        blk_e, slot_tok, slot_gate, dest = _route(top_idx[:, :TOP_K], top_gate[:, :TOP_K])
        y_slots = _moe_experts(f_all, blk_e, slot_tok, slot_gate, exp_w_gu[l].astype(BF16), exp_b_gu[l],
                               exp_w_down[l].astype(BF16), exp_b_down[l])
        fgain = final_norm.reshape(1, d)
        x_all = _moe_combine(y_slots, dest, xo, g_f, fgain, last, n_tok, lat_rows, n)

    return x_all.reshape(batch, n, d)
```

```python
import functools
import math

import jax
import jax.numpy as jnp
from jax import lax
from jax.experimental import pallas as pl
from jax.experimental.pallas import tpu as pltpu

F32 = jnp.float32
BF16 = jnp.bfloat16
HIGHEST = lax.Precision.HIGHEST

D_MODEL = 2048
GRID_W = 64
BRANCH_W = 512
N_HEADS = 4
HEAD_W = 128
MLA_NOPE = 128
MLA_ROPE = 64
MLA_QK_PAD = 256
MLA_SCALE = (MLA_NOPE + MLA_ROPE) ** -0.5
LOG2E = math.log2(math.e)
DIF_QK = 64
ML_CHUNK = 64
N_EXPERTS = 32
TOP_K = 4
D_EXPERT = 1024
SWIGLU_LIMIT = 7.0
SWIGLU_ALPHA = 1.702
MOE_BLOCK = 256
ROPE_THETA = 10000.0
NORM_EPS = 1e-6
HY_BANDS = 16
HY_EMB = 1 + 2 * HY_BANDS
HY_HID = 64
LANE = 128
SUBLANE = 8
SMALL_COLS = 6144
GATE_COLS = 4 * D_MODEL

CB_DK, CB_DV, CB_MK, CB_MV, CB_QA, CB_DQ, CB_MQ, CB_MO, CB_HX0, CB_HX1, CB_HV = range(11)
CB_CKV_256 = 22
CB_KPE_128 = 46


def _cparams(sem, vmem_mb=48):
    return pltpu.CompilerParams(dimension_semantics=sem, vmem_limit_bytes=vmem_mb * 1024 * 1024)


def _rms(x):
    return x * lax.rsqrt(jnp.mean(x * x, axis=-1, keepdims=True) + NORM_EPS)


def _sigmoid(x):
    return 1.0 / (1.0 + jnp.exp(-x))


def _silu(x):
    return x * _sigmoid(x)


def _log_sigmoid(x):
    return jnp.minimum(x, 0.0) - jnp.log(1.0 + jnp.exp(-jnp.abs(x)))


def _bdot(a, b):
    return jnp.dot(a.astype(BF16), b.astype(BF16), preferred_element_type=F32)


def _bdot_nt(a, b):
    return lax.dot_general(a.astype(BF16), b.astype(BF16), (((1,), (1,)), ((), ())),
                           preferred_element_type=F32)


def _hdot(a, b):
    return jnp.dot(a, b, preferred_element_type=F32, precision=HIGHEST)


def _mod_kernel(c_ref, w_ref, b_ref, o_ref):
    o_ref[0] = _bdot(_silu(c_ref[...]), w_ref[0]) + b_ref[0]


def _modulation(cvec, ada_w, ada_b):
    n_layers, d, six_d = ada_w.shape
    tn = 1024
    return pl.pallas_call(
        _mod_kernel,
        grid=(n_layers, six_d // tn),
        in_specs=[pl.BlockSpec((SUBLANE, d), lambda l, j: (0, 0)),
                  pl.BlockSpec((1, d, tn), lambda l, j: (l, 0, j)),
                  pl.BlockSpec((1, 1, tn), lambda l, j: (l, 0, j))],
        out_specs=pl.BlockSpec((1, SUBLANE, tn), lambda l, j: (l, 0, j)),
        out_shape=jax.ShapeDtypeStruct((n_layers, SUBLANE, six_d), F32),
        compiler_params=_cparams(("parallel", "parallel")),
        name="adaln_modulation",
    )(cvec, ada_w, ada_b.reshape(n_layers, 1, six_d))


def _group_of_tile(i, tm, lat_rows, seq_rows):
    return jnp.minimum(i // (seq_rows // tm), lat_rows // seq_rows)


def _in_proj_kernel(x_ref, g_ref, sc_ref, sh_ref, w_ref, o_ref, h_scr):
    @pl.when(pl.program_id(1) == 0)
    def _():
        h = _rms(x_ref[...]) * g_ref[...]
        h_scr[...] = (h * (1.0 + sc_ref[0]) + sh_ref[0]).astype(BF16)

    o_ref[...] = jnp.dot(h_scr[...], w_ref[...], preferred_element_type=F32).astype(o_ref.dtype)


def _in_proj(x_all, gain, sc, sh, w, out_dtype, nrows, lat_rows, seq_rows, tm=512, tn=1024):
    d = x_all.shape[1]
    ncol = w.shape[1]
    grp = functools.partial(_group_of_tile, tm=tm, lat_rows=lat_rows, seq_rows=seq_rows)
    return pl.pallas_call(
        _in_proj_kernel,
        grid=(nrows // tm, ncol // tn),
        in_specs=[pl.BlockSpec((tm, d), lambda i, j: (i, 0)),
                  pl.BlockSpec((1, d), lambda i, j: (0, 0)),
                  pl.BlockSpec((1, 1, d), lambda i, j: (grp(i), 0, 0)),
                  pl.BlockSpec((1, 1, d), lambda i, j: (grp(i), 0, 0)),
                  pl.BlockSpec((d, tn), lambda i, j: (0, j))],
        out_specs=pl.BlockSpec((tm, tn), lambda i, j: (i, j)),
        out_shape=jax.ShapeDtypeStruct((nrows, ncol), out_dtype),
        scratch_shapes=[pltpu.VMEM((tm, d), BF16)],
        compiler_params=_cparams(("parallel", "arbitrary")),
        name="in_proj",
    )(x_all, gain, sc, sh, w)


def _rope(x, c, s):
    w = x.shape[-1]
    lane = lax.broadcasted_iota(jnp.int32, x.shape, 1)
    up = pltpu.roll(x, w - 16, 1)
    dn = pltpu.roll(x, 16, 1)
    return x * c + jnp.where((lane % 32) < 16, up, dn) * s


def _attn_prep_kernel(dk_ref, dv_ref, qa_ref, dq_ref, ckv_ref, kpe_ref, cd_ref, sd_ref, cm_ref, sm_ref,
                      wqb_ref, wkvb_ref, qn_ref, kvn_ref,
                      mq_ref, mk_ref, mv_ref, dqo_ref, dko_ref, dvo_ref):
    cd4 = jnp.concatenate([cd_ref[...]] * N_HEADS, axis=1)
    sd4 = jnp.concatenate([sd_ref[...]] * N_HEADS, axis=1)
    cm, sm = cm_ref[...], sm_ref[...]
    dqo_ref[...] = (_rope(dq_ref[...], cd4, sd4) * (DIF_QK ** -0.5 * LOG2E)).astype(BF16)
    dko_ref[...] = _rope(dk_ref[...], cd4, sd4).astype(BF16)
    dvo_ref[...] = dv_ref[...].astype(BF16)

    q = _bdot(_rms(qa_ref[...]) * qn_ref[...], wqb_ref[...])
    kv = _bdot(_rms(ckv_ref[...]) * kvn_ref[...], wkvb_ref[...])
    kpe = kpe_ref[...]
    lane = lax.broadcasted_iota(jnp.int32, kpe.shape, 1)
    kpe = _rope(jnp.where(lane < MLA_ROPE, kpe, 0.0), cm, sm).astype(BF16)
    for h in range(N_HEADS):
        o = h * MLA_QK_PAD
        mq_ref[:, o:o + MLA_NOPE] = (q[:, o:o + MLA_NOPE] * (MLA_SCALE * LOG2E)).astype(BF16)
        mq_ref[:, o + MLA_NOPE:o + MLA_QK_PAD] = (
            _rope(q[:, o + MLA_NOPE:o + MLA_QK_PAD], cm, sm) * (MLA_SCALE * LOG2E)).astype(BF16)
        mk_ref[:, o:o + MLA_NOPE] = kv[:, h * MLA_NOPE:(h + 1) * MLA_NOPE].astype(BF16)
        mk_ref[:, o + MLA_NOPE:o + MLA_QK_PAD] = kpe
    mv_ref[...] = kv[:, N_HEADS * MLA_NOPE:].astype(BF16)


def _attn_prep(u, tabs, wqb, wkvb, qn, kvn, tm=256):
    t_all = u.shape[0]
    cd, sd, cm, sm = tabs
    col = lambda w, cb: pl.BlockSpec((tm, w), lambda i, cb=cb: (i, cb))
    tab = pl.BlockSpec((tm, LANE), lambda i: (i, 0))
    full = lambda a: pl.BlockSpec(a.shape, lambda i: (0,) * a.ndim)
    out = lambda w: pl.BlockSpec((tm, w), lambda i: (i, 0))
    shp = lambda w: jax.ShapeDtypeStruct((t_all, w), BF16)
    return pl.pallas_call(
        _attn_prep_kernel,
        grid=(t_all // tm,),
        in_specs=[col(512, CB_DK), col(512, CB_DV), col(512, CB_QA), col(512, CB_DQ),
                  col(256, CB_CKV_256), col(128, CB_KPE_128), tab, tab, tab, tab,
                  full(wqb), full(wkvb), full(qn), full(kvn)],
        out_specs=[out(1024), out(1024), out(512), out(512), out(512), out(512)],
        out_shape=[shp(1024), shp(1024), shp(512), shp(512), shp(512), shp(512)],
        compiler_params=_cparams(("parallel",)),
        name="attn_prep",
    )(u, u, u, u, u, u, cd, sd, cm, sm, wqb, wkvb, qn, kvn)


def _conv3(x, prev_blk, next_blk, w, first, last):
    tm = x.shape[0]
    row = lax.broadcasted_iota(jnp.int32, x.shape, 0)
    prev_row = jnp.where(first, 0.0, prev_blk[SUBLANE - 1:SUBLANE, :])
    next_row = jnp.where(last, 0.0, next_blk[0:1, :])
    xm = jnp.where(row == 0, prev_row, pltpu.roll(x, 1, 0))
    xp = jnp.where(row == tm - 1, next_row, pltpu.roll(x, tm - 1, 0))
    return xm * w[0:1, :] + x * w[1:2, :] + xp * w[2:3, :]


def _seq_edges(tile, tm, seq_rows, lat_rows, ctx_len):
    r = tile * tm
    in_lat = r < lat_rows
    pos = jnp.where(in_lat, r % seq_rows, (r - lat_rows) % ctx_len)
    length = jnp.where(in_lat, seq_rows, ctx_len)
    return pos == 0, pos + tm == length


def _ml_prep_kernel(seq_rows, lat_rows, ctx_len,
                    q_ref, qp_ref, qn_ref, k_ref, kp_ref, kn_ref, gblk_ref, wq_ref, wk_ref, gb_ref,
                    qo_ref, ko_ref, go_ref):
    tm = q_ref.shape[0]
    first, last = _seq_edges(pl.program_id(0), tm, seq_rows, lat_rows, ctx_len)
    qo_ref[...] = _silu(_conv3(q_ref[...], qp_ref[...], qn_ref[...], wq_ref[...], first, last))
    ko_ref[...] = _silu(_conv3(k_ref[...], kp_ref[...], kn_ref[...], wk_ref[...], first, last)) * HEAD_W ** -0.5
    go_ref[...] = gblk_ref[:, MLA_ROPE:MLA_ROPE + 16] + gb_ref[...]


def _halo_specs(tm, w, cb, t_all, r0=0):
    nb8 = t_all // SUBLANE
    r8 = tm // SUBLANE
    return [pl.BlockSpec((tm, w), lambda i: (i + r0, cb)),
            pl.BlockSpec((SUBLANE, w), lambda i: (jnp.maximum((i + r0) * r8 - 1, 0), cb)),
            pl.BlockSpec((SUBLANE, w), lambda i: (jnp.minimum((i + r0 + 1) * r8, nb8 - 1), cb))]


def _ml_prep(u, wq, wk, gb, seq_rows, lat_rows, ctx_len, tm=256):
    t_all = u.shape[0]
    full = lambda a: pl.BlockSpec(a.shape, lambda i: (0,) * a.ndim)
    return pl.pallas_call(
        functools.partial(_ml_prep_kernel, seq_rows, lat_rows, ctx_len),
        grid=(t_all // tm,),
        in_specs=_halo_specs(tm, 512, CB_MQ, t_all) + _halo_specs(tm, 512, CB_MK, t_all)
        + [pl.BlockSpec((tm, LANE), lambda i: (i, CB_KPE_128)), full(wq), full(wk), full(gb)],
        out_specs=[pl.BlockSpec((tm, 512), lambda i: (i, 0)), pl.BlockSpec((tm, 512), lambda i: (i, 0)),
                   pl.BlockSpec((tm, 16), lambda i: (i, 0))],
        out_shape=[jax.ShapeDtypeStruct((t_all, 512), F32), jax.ShapeDtypeStruct((t_all, 512), F32),
                   jax.ShapeDtypeStruct((t_all, 16), F32)],
        compiler_params=_cparams(("parallel",)),
        name="mlstm_prep",
    )(u, u, u, u, u, u, u, wq, wk, gb)


def _hy_prep_kernel(r0, seq_rows, lat_rows, ctx_len,
                    a_ref, ap_ref, an_ref, b_ref, bp_ref, bn_ref, c_ref, cp_ref, cn_ref, w_ref, bias_ref,
                    z_ref, x0_ref):
    tm = a_ref.shape[0]
    first, last = _seq_edges(pl.program_id(0) + r0, tm, seq_rows, lat_rows, ctx_len)
    w, bias = w_ref[...], bias_ref[...]
    outs = []
    for s, (m, p, n) in enumerate(((a_ref, ap_ref, an_ref), (b_ref, bp_ref, bn_ref), (c_ref, cp_ref, cn_ref))):
        sl = slice(s * BRANCH_W, (s + 1) * BRANCH_W)
        outs.append(_conv3(m[...], p[...], n[...], w[:, sl], first, last) + bias[:, sl])
    x0_ref[...] = outs[0]
    z_ref[...] = outs[2] * outs[1]


def _hy_prep(u, w, bias, row0, nrows, seq_rows, lat_rows, ctx_len, tm=256):
    t_all = u.shape[0]
    r0 = row0 // tm
    full = lambda a: pl.BlockSpec(a.shape, lambda i: (0,) * a.ndim)
    halo = lambda cb: _halo_specs(tm, 512, cb, t_all, r0)
    return pl.pallas_call(
        functools.partial(_hy_prep_kernel, r0, seq_rows, lat_rows, ctx_len),
        grid=(nrows // tm,),
        in_specs=halo(CB_HX0) + halo(CB_HX1) + halo(CB_HV) + [full(w), full(bias)],
        out_specs=[pl.BlockSpec((tm, 512), lambda i: (i, 0)), pl.BlockSpec((tm, 512), lambda i: (i, 0))],
        out_shape=[jax.ShapeDtypeStruct((nrows, 512), F32), jax.ShapeDtypeStruct((nrows, 512), F32)],
        compiler_params=_cparams(("parallel",)),
        name="hyena_prep",
    )(u, u, u, u, u, u, u, u, u, w, bias)


def _attn_kernel(n_maps, seg_lens, tk, lam_init, *refs):
    q_ref = refs[0]
    kv_refs = refs[1:1 + 2 * len(seg_lens)]
    pos = 1 + 2 * len(seg_lens)
    if n_maps == 2:
        lam_ref, gain_ref = refs[pos], refs[pos + 1]
        pos += 2
    o_ref = refs[pos]

    q = q_ref[...]
    tq = q.shape[0]
    if n_maps == 1:
        qs = [q]
    else:
        lane = lax.broadcasted_iota(jnp.int32, q.shape, 1)
        zero = jnp.zeros_like(q)
        qs = [jnp.where(lane < DIF_QK, q, zero), jnp.where(lane >= DIF_QK, q, zero)]

    def step(k, v, carry):
        new = []
        for qm, (m, l, acc) in zip(qs, carry):
            s = lax.dot_general(qm, k, (((1,), (1,)), ((), ())), preferred_element_type=F32)
            m_new = jnp.maximum(m, jnp.max(s, axis=-1, keepdims=True))
            p = jnp.exp2(s - m_new)
            alpha = jnp.exp2(m - m_new)
            l_new = alpha * l + jnp.sum(p, axis=-1, keepdims=True)
            acc_new = alpha * acc + jnp.dot(p.astype(BF16), v, preferred_element_type=F32)
            new.append((m_new, l_new, acc_new))
        return tuple(new)

    carry = tuple((jnp.full((tq, 1), -jnp.inf, F32), jnp.zeros((tq, 1), F32), jnp.zeros((tq, HEAD_W), F32))
                  for _ in range(n_maps))
    for si, slen in enumerate(seg_lens):
        k_ref, v_ref = kv_refs[2 * si], kv_refs[2 * si + 1]
        chunk = min(tk, slen)

        def body(c, carry, k_ref=k_ref, v_ref=v_ref, chunk=chunk):
            start = pl.multiple_of(c * chunk, chunk)
            return step(k_ref[pl.ds(start, chunk), :], v_ref[pl.ds(start, chunk), :], carry)

        trips = slen // chunk
        carry = lax.fori_loop(0, trips, body, carry, unroll=2 if trips % 2 == 0 else 1)

    outs = [acc / l for (_, l, acc) in carry]
    if n_maps == 1:
        o_ref[...] = outs[0].astype(o_ref.dtype)
    else:
        lp = lam_ref[...]
        lam = (jnp.exp(jnp.sum(lp[0:1] * lp[1:2], axis=-1, keepdims=True))
               - jnp.exp(jnp.sum(lp[2:3] * lp[3:4], axis=-1, keepdims=True)) + lam_init)
        o = outs[0] - lam * outs[1]
        o_ref[...] = (_rms(o) * gain_ref[...] * (1.0 - lam_init)).astype(o_ref.dtype)


def _attention(q, k, v, n_maps, q_row0, q_len, segs, batch, qk_w, lam=None, gain=None, lam_init=0.0,
               tq=1024, tk=512):
    tq = min(tq, q_len)
    nq = q_len // tq
    qb0 = q_row0 // tq
    in_specs = [pl.BlockSpec((tq, qk_w), lambda b, h, i: (qb0 + b * nq + i, h))]
    args = [q]
    for (r0, slen) in segs:
        kb0 = r0 // slen
        in_specs.append(pl.BlockSpec((slen, qk_w), lambda b, h, i, kb0=kb0: (kb0 + b, h)))
        in_specs.append(pl.BlockSpec((slen, HEAD_W), lambda b, h, i, kb0=kb0: (kb0 + b, h)))
        args += [k, v]
    if n_maps == 2:
        in_specs += [pl.BlockSpec(lam.shape, lambda b, h, i: (0, 0)),
                     pl.BlockSpec(gain.shape, lambda b, h, i: (0, 0))]
        args += [lam, gain]
    return pl.pallas_call(
        functools.partial(_attn_kernel, n_maps, tuple(s for _, s in segs), tk, lam_init),
        grid=(batch, N_HEADS, nq),
        in_specs=in_specs,
        out_specs=pl.BlockSpec((tq, HEAD_W), lambda b, h, i: (b * nq + i, h)),
        out_shape=jax.ShapeDtypeStruct((batch * q_len, N_HEADS * HEAD_W), BF16),
        compiler_params=_cparams(("parallel", "parallel", "arbitrary")),
        name="attention_maps%d" % n_maps,
    )(*args)


ML_STEP = 256


def _mlstm_kernel(rev, with_out, n_steps, *refs):
    q_ref, k_ref, v_ref, g_ref, gt_ref, c0_ref, n0_ref, m0_ref = refs[:8]
    pos = 8
    if with_out:
        h_ref = refs[pos]
        pos += 1
    cf_ref, nf_ref, mf_ref, c_scr, n_scr, m_scr = refs[pos:pos + 6]
    j = pl.program_id(1)

    @pl.when(j == 0)
    def _():
        c_scr[...] = c0_ref[0]
        n_scr[...] = n0_ref[0]
        m_scr[...] = m0_ref[0]

    d = 1 if rev else 0
    ti = lax.broadcasted_iota(jnp.int32, (ML_CHUNK, ML_CHUNK), 0)
    si = lax.broadcasted_iota(jnp.int32, (ML_CHUNK, ML_CHUNK), 1)
    tri = (si >= ti) if rev else (si <= ti)
    tri_f = tri.astype(F32)
    tri_t = ((ti >= si) if rev else (ti <= si)).astype(F32)

    n_chunks = ML_STEP // ML_CHUNK
    order = range(n_chunks - 1, -1, -1) if rev else range(n_chunks)
    for c in order:
        o = c * ML_CHUNK
        gcol = g_ref[o:o + ML_CHUNK, :]
        grow = gt_ref[0, :, o:o + ML_CHUNK]
        li_col = gcol[:, d * 8:d * 8 + 4]
        lf_col = _log_sigmoid(gcol[:, d * 8 + 4:d * 8 + 8])
        li_row = grow[d * 8:d * 8 + 4, :]
        lf_row = _log_sigmoid(grow[d * 8 + 4:d * 8 + 8, :])
        bc_col = _hdot(tri_f, lf_col)
        bc_row = _hdot(lf_row, tri_t)
        f_tot = jnp.sum(lf_col, axis=0, keepdims=True)
        for h in range(N_HEADS):
            hs = slice(h * HEAD_W, (h + 1) * HEAD_W)
            qc = q_ref[o:o + ML_CHUNK, hs]
            kc = k_ref[o:o + ML_CHUNK, hs]
            vc = v_ref[o:o + ML_CHUNK, hs]
            bcc, bcr = bc_col[:, h:h + 1], bc_row[h:h + 1, :]
            icol, irow = li_col[:, h:h + 1], li_row[h:h + 1, :]
            fh = f_tot[:, h:h + 1]
            c_mat = c_scr[h]
            n_vec = n_scr[h]
            m = m_scr[h][:, 0:1]
            g_end_r = fh - bcr + irow
            g_end_c = fh - bcc + icol
            m_new = jnp.maximum(fh + m, jnp.max(g_end_r, axis=-1, keepdims=True))
            decay = jnp.exp(fh + m - m_new)
            wk = jnp.exp(g_end_c - m_new)
            kw = kc * wk
            upd = lax.dot_general(kw.astype(BF16), vc.astype(BF16), (((0,), (0,)), ((), ())),
                                  preferred_element_type=F32)
            if with_out:
                inter = bcc + m
                log_d = jnp.where(tri, bcc - bcr + irow, -jnp.inf)
                m_t = jnp.maximum(inter, jnp.max(log_d, axis=-1, keepdims=True))
                s = _bdot_nt(qc, kc) * jnp.exp(log_d - m_t)
                w_inter = jnp.exp(inter - m_t)
                num = _bdot(s, vc) + w_inter * _bdot(qc, c_mat)
                den = jnp.sum(s, axis=-1, keepdims=True) + w_inter * jnp.sum(qc * n_vec, axis=-1, keepdims=True)
                h_ref[0, o:o + ML_CHUNK, hs] = num / jnp.maximum(jnp.abs(den), jnp.exp(-m_t))
            c_scr[h] = decay * c_mat + upd
            n_scr[h] = decay * n_vec + jnp.sum(kw, axis=0, keepdims=True)
            m_scr[h] = jnp.broadcast_to(m_new, (1, HEAD_W))

    @pl.when(j == n_steps - 1)
    def _():
        cf_ref[0] = c_scr[...]
        nf_ref[0] = n_scr[...]
        mf_ref[0] = m_scr[...]


def _mlstm_dir(q, k, u, g, gt, init, rev, with_out, row0, seq_len, batch):
    n_steps = seq_len // ML_STEP
    b0 = row0 // ML_STEP

    def blk(b, j):
        jj = (n_steps - 1 - j) if rev else j
        return b0 + b * n_steps + jj

    def gt_idx(b, j):
        jj = (n_steps - 1 - j) if rev else j
        return (b, 0, jj)

    st_specs = [pl.BlockSpec((1, N_HEADS, HEAD_W, HEAD_W), lambda b, j: (b, 0, 0, 0)),
                pl.BlockSpec((1, N_HEADS, 1, HEAD_W), lambda b, j: (b, 0, 0, 0)),
                pl.BlockSpec((1, N_HEADS, 1, HEAD_W), lambda b, j: (b, 0, 0, 0))]
    st_shapes = [jax.ShapeDtypeStruct((batch, N_HEADS, HEAD_W, HEAD_W), F32),
                 jax.ShapeDtypeStruct((batch, N_HEADS, 1, HEAD_W), F32),
                 jax.ShapeDtypeStruct((batch, N_HEADS, 1, HEAD_W), F32)]
    in_specs = [pl.BlockSpec((ML_STEP, 512), lambda b, j: (blk(b, j), 0)),
                pl.BlockSpec((ML_STEP, 512), lambda b, j: (blk(b, j), 0)),
                pl.BlockSpec((ML_STEP, 512), lambda b, j: (blk(b, j), CB_MV)),
                pl.BlockSpec((ML_STEP, 16), lambda b, j: (blk(b, j), 0)),
                pl.BlockSpec((1, 16, ML_STEP), gt_idx)] + st_specs
    out_specs, out_shapes = list(st_specs), list(st_shapes)
    if with_out:
        out_specs = [pl.BlockSpec((1, ML_STEP, 512), lambda b, j: (b, (n_steps - 1 - j) if rev else j, 0))] + out_specs
        out_shapes = [jax.ShapeDtypeStruct((batch, seq_len, 512), F32)] + out_shapes
    res = pl.pallas_call(
        functools.partial(_mlstm_kernel, rev, with_out, n_steps),
        grid=(batch, n_steps),
        in_specs=in_specs,
        out_specs=out_specs,
        out_shape=out_shapes,
        scratch_shapes=[pltpu.VMEM((N_HEADS, HEAD_W, HEAD_W), F32), pltpu.VMEM((N_HEADS, 1, HEAD_W), F32),
                        pltpu.VMEM((N_HEADS, 1, HEAD_W), F32)],
        compiler_params=_cparams(("parallel", "arbitrary")),
        name="mlstm_%s" % ("bwd" if rev else "fwd"),
    )(q, k, u, g, gt, *init)
    if with_out:
        return res[0], tuple(res[1:])
    return None, tuple(res)


def _hy_filter_kernel(n, feat_ref, w1_ref, b1_ref, w2_ref, b2_ref, w3_ref, freq_ref, decay_ref,
                      h_ref, norm_ref):
    i = pl.program_id(0)
    tm = feat_ref.shape[0]
    freq = freq_ref[...]
    hid = jnp.sin(freq[0:1, :] * (_hdot(feat_ref[...], w1_ref[...]) + b1_ref[...]))
    hid = jnp.sin(freq[1:2, :] * (_hdot(hid, w2_ref[...]) + b2_ref[...]))
    row = lax.broadcasted_iota(jnp.int32, (tm, 2 * BRANCH_W), 0) + i * tm
    col = lax.broadcasted_iota(jnp.int32, (tm, 2 * BRANCH_W), 1)
    t = row.astype(F32) / n
    h = _hdot(hid, w3_ref[...]) * jnp.exp(-t * jnp.abs(decay_ref[...]))
    h = jnp.where((row == 0) & (col >= BRANCH_W), 0.0, h)
    h_ref[0] = h[:, :BRANCH_W]
    h_ref[1] = h[:, BRANCH_W:]

    @pl.when(i == 0)
    def _():
        norm_ref[...] = jnp.zeros_like(norm_ref)

    norm_ref[...] += jnp.sum(jnp.abs(h), axis=0, keepdims=True)


def _hy_filter(n, feat, w1p, b1, w2, b2, w3, freq, decay_flat):
    tm = min(n, 512)
    full = lambda a: pl.BlockSpec(a.shape, lambda i: (0,) * a.ndim)
    return pl.pallas_call(
        functools.partial(_hy_filter_kernel, n),
        grid=(n // tm,),
        in_specs=[pl.BlockSpec((tm, LANE), lambda i: (i, 0)), full(w1p), full(b1), full(w2), full(b2),
                  full(w3), full(freq), full(decay_flat)],
        out_specs=[pl.BlockSpec((2, tm, BRANCH_W), lambda i: (0, i, 0)),
                   pl.BlockSpec((1, 2 * BRANCH_W), lambda i: (0, 0))],
        out_shape=[jax.ShapeDtypeStruct((2, n, BRANCH_W), F32), jax.ShapeDtypeStruct((1, 2 * BRANCH_W), F32)],
        compiler_params=_cparams(("arbitrary",)),
        name="hyena_filter",
    )(feat, w1p, b1, w2, b2, w3, freq, decay_flat)


def _dft_a_kernel(f_ref, x_ref, o_ref):
    o_ref[0] = _hdot(f_ref[...], x_ref[0])


def _dft_stage_a(fa, x, tc=2048):
    g, n1h, cols = x.shape
    rows = fa.shape[0]
    return pl.pallas_call(
        _dft_a_kernel,
        grid=(g, cols // tc),
        in_specs=[pl.BlockSpec(fa.shape, lambda a, j: (0, 0)),
                  pl.BlockSpec((1, n1h, tc), lambda a, j: (a, 0, j))],
        out_specs=pl.BlockSpec((1, rows, tc), lambda a, j: (a, 0, j)),
        out_shape=jax.ShapeDtypeStruct((g, rows, cols), F32),
        compiler_params=_cparams(("parallel", "parallel")),
        name="hyena_dft_a",
    )(fa, x)


def _spectral_product(s, batch):
    kk = s.shape[0] // 2
    c = BRANCH_W
    x0, x1 = s[:, batch * c:(batch + 1) * c], s[:, (batch + 1) * c:(batch + 2) * c]
    hr = x0[:kk] + x1[:kk]
    hi = x0[kk:] - x1[kk:]
    outs = []
    for b in range(batch):
        z = s[:, b * c:(b + 1) * c]
        zr, zi = z[:kk], z[kk:]
        outs.append(jnp.concatenate([zr * hr - zi * hi, zr * hi + zi * hr], axis=0))
    return jnp.concatenate(outs, axis=1)


def _dft_c_kernel(batch, az_ref, ah_ref, g_ref, gt_ref, o_ref):
    parts = [az_ref[b, :, 0].reshape(2 * LANE, BRANCH_W) for b in range(batch)]
    parts += [ah_ref[f, :, 0].reshape(2 * LANE, BRANCH_W) for f in range(2)]
    s = _hdot(g_ref[0], jnp.concatenate(parts, axis=1))
    p = _hdot(gt_ref[0], _spectral_product(s, batch))
    for b in range(batch):
        o_ref[b, :, 0] = p[:, b * BRANCH_W:(b + 1) * BRANCH_W].reshape(2, LANE, BRANCH_W)


def _dft_stage_c(az, ah, g, gt):
    batch, _, n1, _, c = az.shape
    return pl.pallas_call(
        functools.partial(_dft_c_kernel, batch),
        grid=(n1,),
        in_specs=[pl.BlockSpec((batch, 2, 1, LANE, c), lambda k: (0, 0, k, 0, 0)),
                  pl.BlockSpec((2, 2, 1, LANE, c), lambda k: (0, 0, k, 0, 0)),
                  pl.BlockSpec((1, 2 * LANE, 2 * LANE), lambda k: (k, 0, 0)),
                  pl.BlockSpec((1, 2 * LANE, 2 * LANE), lambda k: (k, 0, 0))],
        out_specs=pl.BlockSpec((batch, 2, 1, LANE, c), lambda k: (0, 0, k, 0, 0)),
        out_shape=jax.ShapeDtypeStruct(az.shape, F32),
        compiler_params=_cparams(("parallel",)),
        name="hyena_dft_c",
    )(az, ah, g, gt)


def _hy_epilogue(y, z, x0, norm, skip):
    reps = y.shape[1] // BRANCH_W
    nsum = norm[:, :BRANCH_W] + norm[:, BRANCH_W:]
    inv = jnp.concatenate([1.0 / nsum] * reps, axis=1)
    sk = jnp.concatenate([skip] * reps, axis=1)
    return (x0 * (y * inv + z * sk)).astype(BF16)


def _dft_a_inv_kernel(f_ref, p_ref, z_ref, x0_ref, norm_ref, skip_ref, o_ref):
    y = _hdot(f_ref[...], p_ref[0])
    o_ref[0] = _hy_epilogue(y, z_ref[0], x0_ref[0], norm_ref[...], skip_ref[...])


def _dft_stage_a_inv(fai, p, z, x0, norm, skip, tc=2048):
    batch, rows, cols = p.shape
    n1h = z.shape[1]
    return pl.pallas_call(
        _dft_a_inv_kernel,
        grid=(batch, cols // tc),
        in_specs=[pl.BlockSpec(fai.shape, lambda b, j: (0, 0)),
                  pl.BlockSpec((1, rows, tc), lambda b, j: (b, 0, j)),
                  pl.BlockSpec((1, n1h, tc), lambda b, j: (b, 0, j)),
                  pl.BlockSpec((1, n1h, tc), lambda b, j: (b, 0, j)),
                  pl.BlockSpec(norm.shape, lambda b, j: (0, 0)),
                  pl.BlockSpec(skip.shape, lambda b, j: (0, 0))],
        out_specs=pl.BlockSpec((1, n1h, tc), lambda b, j: (b, 0, j)),
        out_shape=jax.ShapeDtypeStruct((batch, n1h, cols), BF16),
        compiler_params=_cparams(("parallel", "parallel")),
        name="hyena_dft_a_inv",
    )(fai, p, z, x0, norm, skip)


def _dft_tables(n):
    big_l = 2 * n
    n1 = big_l // LANE
    n1h = n1 // 2
    two_pi = 2.0 * math.pi
    k1 = jnp.arange(n1, dtype=jnp.int32)
    th = two_pi * ((k1[:, None] * jnp.arange(n1h, dtype=jnp.int32)[None, :]) % n1).astype(F32) / n1
    fa = jnp.concatenate([jnp.cos(th), -jnp.sin(th)], axis=0)
    fai = jnp.concatenate([jnp.cos(th).T, -jnp.sin(th).T], axis=1) / big_l
    s2 = jnp.arange(LANE, dtype=jnp.int32)
    k = k1[:, None, None] + n1 * s2[None, :, None]
    ph = two_pi * ((k * s2[None, None, :]) % big_l).astype(F32) / big_l
    gr, gi = jnp.cos(ph), -jnp.sin(ph)
    g = jnp.concatenate([jnp.concatenate([gr, -gi], axis=2), jnp.concatenate([gi, gr], axis=2)], axis=1)
    return fa, fai, g, jnp.swapaxes(g, 1, 2)


def _hy_small_kernel(batch, n, z_ref, x0_ref, h_ref, f_ref, fi_ref, norm_ref, skip_ref, o_ref):
    parts = [z_ref[b] for b in range(batch)] + [h_ref[0], h_ref[1]]
    s = _hdot(f_ref[...], jnp.concatenate(parts, axis=1))
    y = _hdot(fi_ref[...], _spectral_product(s, batch))
    for b in range(batch):
        o_ref[b] = _hy_epilogue(y[:, b * BRANCH_W:(b + 1) * BRANCH_W], z_ref[b], x0_ref[b],
                                norm_ref[...], skip_ref[...])


def _hy_small(z, x0, hfilt, norm, skip):
    batch, n, c = z.shape
    big_l = 2 * n
    kk = jnp.arange(big_l, dtype=jnp.int32)
    th = 2.0 * math.pi * ((kk[:, None] * jnp.arange(n, dtype=jnp.int32)[None, :]) % big_l).astype(F32) / big_l
    f = jnp.concatenate([jnp.cos(th), -jnp.sin(th)], axis=0)
    fi = jnp.concatenate([jnp.cos(th).T, -jnp.sin(th).T], axis=1) / big_l
    full = lambda a: pl.BlockSpec(a.shape, lambda i: (0,) * a.ndim)
    return pl.pallas_call(
        functools.partial(_hy_small_kernel, batch, n),
        grid=(1,),
        in_specs=[full(z), full(x0), full(hfilt), full(f), full(fi), full(norm), full(skip)],
        out_specs=full(z),
        out_shape=jax.ShapeDtypeStruct(z.shape, BF16),
        compiler_params=_cparams(("arbitrary",)),
        name="hyena_small",
    )(z, x0, hfilt, f, fi, norm, skip)


def _hy_features(n):
    t = jnp.arange(n, dtype=F32) / n
    bands = jnp.linspace(1e-4, HY_BANDS - 1, HY_BANDS, dtype=F32)
    ang = 2.0 * math.pi * t[:, None] * bands[None, :]
    feat = jnp.concatenate([t[:, None], jnp.cos(ang), jnp.sin(ang)], axis=-1)
    return jnp.pad(feat, ((0, 0), (0, LANE - HY_EMB)))


def _merge_kernel(hy_ref, mla_ref, dif_ref, hf_ref, hb_ref, mo_ref, gate_ref, wb_ref, mln_ref, o_ref):
    hsum = hf_ref[...] + hb_ref[...]
    mln = mln_ref[...]
    y_ml = jnp.concatenate(
        [_rms(hsum[:, h * HEAD_W:(h + 1) * HEAD_W]) * mln[:, h * HEAD_W:(h + 1) * HEAD_W]
         for h in range(N_HEADS)], axis=1)
    y_ml = _sigmoid(mo_ref[...]) * y_ml
    ys = (hy_ref[...], mla_ref[...], dif_ref[...], y_ml.astype(BF16))
    acc = None
    for i in range(4):
        gi = _sigmoid(gate_ref[:, i * D_MODEL:(i + 1) * D_MODEL].astype(F32))
        term = gi * jnp.dot(ys[i], wb_ref[i], preferred_element_type=F32)
        acc = term if acc is None else acc + term
    o_ref[...] = acc.astype(BF16)


def _merge(y_hy, y_mla, y_dif, h_f, h_b, u, gate, wb, mln, nrows, tm=256):
    row = pl.BlockSpec((tm, 512), lambda i: (i, 0))
    full = lambda a: pl.BlockSpec(a.shape, lambda i: (0,) * a.ndim)
    return pl.pallas_call(
        _merge_kernel,
        grid=(nrows // tm,),
        in_specs=[row, row, row, row, row,
                  pl.BlockSpec((tm, 512), lambda i: (i, CB_MO)),
                  pl.BlockSpec((tm, GATE_COLS), lambda i: (i, 0)),
                  full(wb), full(mln)],
        out_specs=pl.BlockSpec((tm, D_MODEL), lambda i: (i, 0)),
        out_shape=jax.ShapeDtypeStruct((nrows, D_MODEL), BF16),
        compiler_params=_cparams(("parallel",), 56),
        name="branch_merge",
    )(y_hy, y_mla, y_dif, h_f, h_b, u, gate, wb, mln)


def _out_kernel(m_ref, w_ref, x_ref, ga_ref, gain_ref, sc_ref, sh_ref, rw_ref, rb_ref,
                xo_ref, f_ref, idx_ref, gt_ref):
    y = jnp.dot(m_ref[...], w_ref[...], preferred_element_type=F32)
    x = x_ref[...] + ga_ref[0] * y
    xo_ref[...] = x
    f = _rms(x) * gain_ref[...] * (1.0 + sc_ref[0]) + sh_ref[0]
    f_ref[...] = f
    logits = _bdot(f, rw_ref[...]) + rb_ref[...]
    lane = lax.broadcasted_iota(jnp.int32, logits.shape, 1)
    idx_out = jnp.zeros(logits.shape, jnp.int32)
    val_out = jnp.full(logits.shape, -jnp.inf, F32)
    work = logits
    for k in range(TOP_K):
        mx = jnp.max(work, axis=-1, keepdims=True)
        am = jnp.min(jnp.where(work == mx, lane, LANE), axis=-1, keepdims=True)
        idx_out = jnp.where(lane == k, am, idx_out)
        val_out = jnp.where(lane == k, mx, val_out)
        work = jnp.where(lane == am, -jnp.inf, work)
    e = jnp.exp(val_out - jnp.max(val_out, axis=-1, keepdims=True))
    gt_ref[...] = e / jnp.sum(e, axis=-1, keepdims=True)
    idx_ref[...] = idx_out


def _out_proj(merged, w_out, x_all, ga, gain, sc, sh, rw, rb, nrows, lat_rows, seq_rows, tm=256):
    grp = functools.partial(_group_of_tile, tm=tm, lat_rows=lat_rows, seq_rows=seq_rows)
    row = lambda w: pl.BlockSpec((tm, w), lambda i: (i, 0))
    full = lambda a: pl.BlockSpec(a.shape, lambda i: (0,) * a.ndim)
    mod = pl.BlockSpec((1, 1, D_MODEL), lambda i: (grp(i), 0, 0))
    return pl.pallas_call(
        _out_kernel,
        grid=(nrows // tm,),
        in_specs=[row(D_MODEL), full(w_out), row(D_MODEL), mod, full(gain), mod, mod, full(rw), full(rb)],
        out_specs=[row(D_MODEL), row(D_MODEL), row(LANE), row(LANE)],
        out_shape=[jax.ShapeDtypeStruct((nrows, D_MODEL), F32), jax.ShapeDtypeStruct((nrows, D_MODEL), F32),
                   jax.ShapeDtypeStruct((nrows, LANE), jnp.int32), jax.ShapeDtypeStruct((nrows, LANE), F32)],
        compiler_params=_cparams(("parallel",)),
        name="out_proj_router",
    )(merged, w_out, x_all, ga, gain, sc, sh, rw, rb)


def _moe_kernel(blk_e_ref, tok_ref, nxt_ref, f_hbm, wgu_ref, bgu_ref, wd_ref, bd_ref, o_ref, xbuf, sem):
    i = pl.program_id(0)
    last = pl.num_programs(0) - 1
    cur = i % 2

    def start_rows(idx_ref, s):
        for r in range(MOE_BLOCK):
            pltpu.make_async_copy(f_hbm.at[pl.ds(idx_ref[0, 0, r], 1), :], xbuf.at[s, pl.ds(r, 1), :],
                                  sem.at[s]).start()

    def wait_rows(s):
        pltpu.make_async_copy(f_hbm.at[pl.ds(0, MOE_BLOCK), :], xbuf.at[s], sem.at[s]).wait()

    @pl.when(i == 0)
    def _():
        start_rows(tok_ref, 0)

    wait_rows(cur)
    start_rows(nxt_ref, 1 - cur)
    gu = _bdot(xbuf[cur], wgu_ref[0]) + bgu_ref[0]
    g = jnp.minimum(gu[:, :D_EXPERT], SWIGLU_LIMIT)
    u = jnp.clip(gu[:, D_EXPERT:], -SWIGLU_LIMIT, SWIGLU_LIMIT)
    act = g * _sigmoid(SWIGLU_ALPHA * g) * (u + 1.0)
    o_ref[...] = _bdot(act, wd_ref[0]) + bd_ref[0]

    @pl.when(i == last)
    def _():
        wait_rows(1 - cur)


def _moe_experts(f_all, blk_e, slot_tok, wgu, bgu, wd, bd):
    n_blocks = blk_e.shape[0]
    slots = n_blocks * MOE_BLOCK
    grid_spec = pltpu.PrefetchScalarGridSpec(
        num_scalar_prefetch=1,
        grid=(n_blocks,),
        in_specs=[pl.BlockSpec((1, 1, MOE_BLOCK), lambda i, be: (i, 0, 0), memory_space=pltpu.SMEM),
                  pl.BlockSpec((1, 1, MOE_BLOCK), lambda i, be: (jnp.minimum(i + 1, n_blocks - 1), 0, 0),
                               memory_space=pltpu.SMEM),
                  pl.BlockSpec(memory_space=pl.ANY),
                  pl.BlockSpec((1, D_MODEL, 2 * D_EXPERT), lambda i, be: (be[i], 0, 0)),
                  pl.BlockSpec((1, 1, 2 * D_EXPERT), lambda i, be: (be[i], 0, 0)),
                  pl.BlockSpec((1, D_EXPERT, D_MODEL), lambda i, be: (be[i], 0, 0)),
                  pl.BlockSpec((1, 1, D_MODEL), lambda i, be: (be[i], 0, 0))],
        out_specs=pl.BlockSpec((MOE_BLOCK, D_MODEL), lambda i, be: (i, 0)),
        scratch_shapes=[pltpu.VMEM((2, MOE_BLOCK, D_MODEL), F32), pltpu.SemaphoreType.DMA((2,))],
    )
    tok3 = slot_tok.reshape(n_blocks, 1, MOE_BLOCK)
    return pl.pallas_call(
        _moe_kernel,
        grid_spec=grid_spec,
        out_shape=jax.ShapeDtypeStruct((slots, D_MODEL), F32),
        compiler_params=_cparams(("arbitrary",), 56),
        name="moe_experts",
    )(blk_e, tok3, tok3, f_all,
      wgu, bgu.reshape(N_EXPERTS, 1, 2 * D_EXPERT), wd, bd.reshape(N_EXPERTS, 1, D_MODEL))


def _combine_kernel(final, dest_ref, nxt_ref, y_hbm, x_ref, tg_ref, gf_ref, fn_ref, o_ref, ybuf, sem):
    tm = x_ref.shape[0]
    i = pl.program_id(0)
    last = pl.num_programs(0) - 1
    cur = i % 2

    def start_rows(idx_ref, s):
        for r in range(tm):
            for k in range(TOP_K):
                pltpu.make_async_copy(y_hbm.at[pl.ds(idx_ref[0, 0, r * TOP_K + k], 1), :],
                                      ybuf.at[s, k, pl.ds(r, 1), :], sem.at[s]).start()

    def wait_rows(s):
        for k in range(TOP_K):
            pltpu.make_async_copy(y_hbm.at[pl.ds(0, tm), :], ybuf.at[s, k], sem.at[s]).wait()

    @pl.when(i == 0)
    def _():
        start_rows(dest_ref, 0)

    start_rows(nxt_ref, 1 - cur)
    wait_rows(cur)
    tg = tg_ref[...]
    moe = None
    for k in range(TOP_K):
        term = tg[:, k:k + 1] * ybuf[cur, k]
        moe = term if moe is None else moe + term
    x = x_ref[...] + gf_ref[0] * moe
    if final:
        x = _rms(x) * fn_ref[...]
    o_ref[...] = x

    @pl.when(i == last)
    def _():
        wait_rows(1 - cur)


def _moe_combine(y_slots, dest, top_gate, x_all, gf, final_gain, final, nrows, lat_rows, seq_rows, tm=128):
    grp = functools.partial(_group_of_tile, tm=tm, lat_rows=lat_rows, seq_rows=seq_rows)
    nt = nrows // tm
    grid_spec = pl.GridSpec(
        grid=(nt,),
        in_specs=[pl.BlockSpec((1, 1, tm * TOP_K), lambda i: (i, 0, 0), memory_space=pltpu.SMEM),
                  pl.BlockSpec((1, 1, tm * TOP_K), lambda i: (jnp.minimum(i + 1, nt - 1), 0, 0),
                               memory_space=pltpu.SMEM),
                  pl.BlockSpec(memory_space=pl.ANY),
                  pl.BlockSpec((tm, D_MODEL), lambda i: (i, 0)),
                  pl.BlockSpec((tm, LANE), lambda i: (i, 0)),
                  pl.BlockSpec((1, 1, D_MODEL), lambda i: (grp(i), 0, 0)),
                  pl.BlockSpec((1, D_MODEL), lambda i: (0, 0))],
        out_specs=pl.BlockSpec((tm, D_MODEL), lambda i: (i, 0)),
        scratch_shapes=[pltpu.VMEM((2, TOP_K, tm, D_MODEL), F32), pltpu.SemaphoreType.DMA((2,))],
    )
    dest3 = dest.reshape(nt, 1, tm * TOP_K)
    return pl.pallas_call(
        functools.partial(_combine_kernel, final),
        grid_spec=grid_spec,
        out_shape=jax.ShapeDtypeStruct((nrows, D_MODEL), F32),
        compiler_params=_cparams(("arbitrary",)),
        name="moe_combine",
    )(dest3, dest3, y_slots, x_all, top_gate, gf, final_gain)


def _route(top_idx):
    n = top_idx.shape[0]
    a = n * TOP_K
    onehot = jnp.sum((top_idx[:, :, None] == jnp.arange(N_EXPERTS, dtype=jnp.int32)).astype(jnp.int32), axis=1)
    csum = jnp.cumsum(onehot, axis=0)
    rank = csum - onehot
    counts = csum[-1]
    padded = (counts + MOE_BLOCK - 1) // MOE_BLOCK * MOE_BLOCK
    pend = jnp.cumsum(padded)
    pstart = pend - padded
    dest = pstart[top_idx] + jnp.take_along_axis(rank, top_idx, axis=1)
    n_blocks = (a + N_EXPERTS * (MOE_BLOCK - 1)) // MOE_BLOCK + 1
    slots = n_blocks * MOE_BLOCK
    tok = jnp.broadcast_to(jnp.arange(n, dtype=jnp.int32)[:, None], (n, TOP_K))
    slot_tok = jnp.zeros((slots,), jnp.int32).at[dest.reshape(-1)].set(tok.reshape(-1))
    blk_e = jnp.minimum(jnp.searchsorted(pend, jnp.arange(n_blocks, dtype=jnp.int32) * MOE_BLOCK, side='right'),
                        N_EXPERTS - 1).astype(jnp.int32)
    return blk_e, slot_tok, dest.astype(jnp.int32)


def _rope_tables(n_lat_rows, seq_len, ctx_rows):
    t = jnp.arange(seq_len, dtype=jnp.int32)
    inv = ROPE_THETA ** (-jnp.arange(0, 32, 2, dtype=F32) / 32)

    def cs(p):
        ang = p.astype(F32)[:, None] * inv[None, :]
        return jnp.cos(ang), jnp.sin(ang)

    cr, sr = cs(t // GRID_W)
    cc, sc = cs(t % GRID_W)
    c64 = jnp.concatenate([cr, cr, cc, cc], axis=1)
    s64 = jnp.concatenate([-sr, sr, -sc, sc], axis=1)
    reps = n_lat_rows // seq_len

    def build(c_half2, s_half2):
        c = jnp.concatenate([c64, c_half2], axis=1)
        s = jnp.concatenate([s64, s_half2], axis=1)
        c = jnp.concatenate([c] * reps + [jnp.ones((ctx_rows, LANE), F32)], axis=0)
        s = jnp.concatenate([s] * reps + [jnp.zeros((ctx_rows, LANE), F32)], axis=0)
        return c, s

    cd, sd = build(c64, s64)
    cm, sm = build(jnp.ones_like(c64), jnp.zeros_like(s64))
    return cd, sd, cm, sm


def _layout_w_in(w):
    d = w.shape[0]
    kv_a, dk, dv, mk, mv, mg = (w[:, 0:320], w[:, 320:832], w[:, 832:1344], w[:, 1344:1856],
                                w[:, 1856:2368], w[:, 2368:2384])
    o = 2384
    q_a, dq, mq, mo, hy, gate = (w[:, o:o + 512], w[:, o + 512:o + 1024], w[:, o + 1024:o + 1536],
                                 w[:, o + 1536:o + 2048], w[:, o + 2048:o + 3584], w[:, o + 3584:])
    small = jnp.concatenate([dk, dv, mk, mv, q_a, dq, mq, mo, hy, kv_a[:, :256], kv_a[:, 256:], mg,
                             jnp.zeros((d, SMALL_COLS - 5968), w.dtype)], axis=1)
    return small.astype(BF16), gate.astype(BF16)


def _layout_w_qb(w):
    w = w.reshape(w.shape[0], N_HEADS, MLA_NOPE + MLA_ROPE)
    w = jnp.pad(w, ((0, 0), (0, 0), (0, MLA_QK_PAD - MLA_NOPE - MLA_ROPE)))
    return w.reshape(w.shape[0], N_HEADS * MLA_QK_PAD).astype(BF16)


def _layout_w_kvb(w):
    w = w.reshape(w.shape[0], N_HEADS, MLA_NOPE + HEAD_W)
    return jnp.concatenate([w[:, :, :MLA_NOPE].reshape(w.shape[0], -1),
                            w[:, :, MLA_NOPE:].reshape(w.shape[0], -1)], axis=1).astype(BF16)


def kernel(x, c, ctx, c_ctx, ada_w, ada_b, norm_mix, norm_ffn, w_in, hy_conv, hy_conv_b, hy_w1, hy_b1, hy_w2, hy_b2, hy_w3, hy_freq, hy_decay, hy_skip, mla_q_norm, mla_w_qb, mla_kv_norm, mla_w_kvb, dif_lambda, dif_norm, ml_conv_q, ml_conv_k, ml_gate_b, ml_norm, w_branch, w_out, router_w, router_b, exp_w_gu, exp_b_gu, exp_w_down, exp_b_down, final_norm):
    batch, n, d = x.shape
    n_ctx = ctx.shape[1]
    depth = w_in.shape[0]
    ctx_rows = batch * n_ctx
    lat_rows = batch * n
    t_all = ctx_rows + lat_rows

    assert batch + 1 <= SUBLANE
    cvec = jnp.zeros((SUBLANE, d), F32).at[:batch].set(c).at[batch].set(c_ctx)
    mod = _modulation(cvec, ada_w, ada_b)
    x_all = jnp.concatenate([x.reshape(lat_rows, d), ctx.reshape(ctx_rows, d)], axis=0)
    tabs = _rope_tables(lat_rows, n, ctx_rows)
    fa, fai, g_tab, gt_tab = _dft_tables(n)
    feat_lat, feat_ctx = _hy_features(n), _hy_features(n_ctx)
    zero_state = (jnp.zeros((batch, N_HEADS, HEAD_W, HEAD_W), F32), jnp.zeros((batch, N_HEADS, 1, HEAD_W), F32),
                  jnp.zeros((batch, N_HEADS, 1, HEAD_W), F32))
    n1h = n // LANE

    for l in range(depth):
        last = l == depth - 1
        lam_init = 0.8 - 0.6 * math.exp(-0.3 * l)
        m6 = mod[l].reshape(SUBLANE, 6, 1, d)
        sh_a, sc_a, g_a, sh_f, sc_f, g_f = (m6[:, i] for i in range(6))
        w_small, w_gate = _layout_w_in(w_in[l])
        gain_mix = norm_mix[l].reshape(1, d)

        n_tok = lat_rows if last else t_all
        u = _in_proj(x_all, gain_mix, sc_a, sh_a, w_small, F32, t_all, lat_rows, n)
        gate = _in_proj(x_all, gain_mix, sc_a, sh_a, w_gate, BF16, n_tok, lat_rows, n)

        mq, mk, mv, dq, dk, dv = _attn_prep(
            u, tabs, _layout_w_qb(mla_w_qb[l]), _layout_w_kvb(mla_w_kvb[l]),
            mla_q_norm[l].reshape(1, -1), mla_kv_norm[l].reshape(1, -1))
        ml_q, ml_k, ml_g = _ml_prep(u, ml_conv_q[l], ml_conv_k[l], ml_gate_b[l].reshape(1, 16), n, lat_rows, n_ctx)
        gt_lat = jnp.swapaxes(ml_g[:lat_rows].reshape(batch, n, 16), 1, 2)
        gt_ctx = jnp.swapaxes(ml_g[lat_rows:].reshape(batch, n_ctx, 16), 1, 2)

        lam_p = dif_lambda[l]
        dgain = dif_norm[l].reshape(1, HEAD_W)
        segs = [(lat_rows, n_ctx), (0, n)]
        y_mla = _attention(mq, mk, mv, 1, 0, n, segs, batch, MLA_QK_PAD)
        y_dif = _attention(dq, dk, dv, 2, 0, n, segs, batch, 2 * DIF_QK, lam_p, dgain, lam_init)

        hcf, st_f = _mlstm_dir(ml_q, ml_k, u, ml_g, gt_ctx, zero_state, False, not last, lat_rows, n_ctx, batch)
        hcb, st_b = _mlstm_dir(ml_q, ml_k, u, ml_g, gt_ctx, zero_state, True, not last, lat_rows, n_ctx, batch)
        hlf, _ = _mlstm_dir(ml_q, ml_k, u, ml_g, gt_lat, st_f, False, True, 0, n, batch)
        hlb, _ = _mlstm_dir(ml_q, ml_k, u, ml_g, gt_lat, st_b, True, True, 0, n, batch)

        hw1 = jnp.pad(hy_w1[l], ((0, LANE - HY_EMB), (0, 0)))
        filt_args = (hw1, hy_b1[l].reshape(1, -1), hy_w2[l], hy_b2[l].reshape(1, -1), hy_w3[l], hy_freq[l],
                     hy_decay[l].reshape(1, -1))
        skip = hy_skip[l].reshape(1, -1)
        hconv_b = hy_conv_b[l].reshape(1, -1)
        z, x0 = _hy_prep(u, hy_conv[l], hconv_b, 0, lat_rows, n, lat_rows, n_ctx)
        hfilt, hnorm = _hy_filter(n, feat_lat, *filt_args)
        zr = z.reshape(batch, n1h, LANE * BRANCH_W)
        az = _dft_stage_a(fa, zr)
        ah = _dft_stage_a(fa, hfilt.reshape(2, n1h, LANE * BRANCH_W))
        n1 = 2 * n1h
        pz = _dft_stage_c(az.reshape(batch, 2, n1, LANE, BRANCH_W), ah.reshape(2, 2, n1, LANE, BRANCH_W),
                          g_tab, gt_tab)
        y_hy = _dft_stage_a_inv(fai, pz.reshape(batch, 2 * n1, LANE * BRANCH_W), zr,
                                x0.reshape(batch, n1h, LANE * BRANCH_W), hnorm, skip).reshape(lat_rows, BRANCH_W)

        h_f, h_b = hlf.reshape(lat_rows, -1), hlb.reshape(lat_rows, -1)
        w_o = w_out[l].astype(BF16)
        gain_ffn = norm_ffn[l].reshape(1, d)
        rw = jnp.pad(router_w[l], ((0, 0), (0, LANE - N_EXPERTS))).astype(BF16)
        rb = jnp.concatenate([router_b[l], jnp.full((LANE - N_EXPERTS,), -jnp.inf, F32)]).reshape(1, LANE)

        if not last:
            cseg = [(lat_rows, n_ctx)]
            yc_mla = _attention(mq, mk, mv, 1, lat_rows, n_ctx, cseg, batch, MLA_QK_PAD)
            yc_dif = _attention(dq, dk, dv, 2, lat_rows, n_ctx, cseg, batch, 2 * DIF_QK, lam_p, dgain, lam_init)
            zc, x0c = _hy_prep(u, hy_conv[l], hconv_b, lat_rows, ctx_rows, n, lat_rows, n_ctx)
            hfc, hnc = _hy_filter(n_ctx, feat_ctx, *filt_args)
            yc_hy = _hy_small(zc.reshape(batch, n_ctx, BRANCH_W), x0c.reshape(batch, n_ctx, BRANCH_W), hfc, hnc,
                              skip).reshape(ctx_rows, BRANCH_W)
            cat = lambda a, b: jnp.concatenate([a, b], axis=0)
            y_hy, y_mla, y_dif = cat(y_hy, yc_hy), cat(y_mla, yc_mla), cat(y_dif, yc_dif)
            h_f, h_b = cat(h_f, hcf.reshape(ctx_rows, -1)), cat(h_b, hcb.reshape(ctx_rows, -1))

        merged = _merge(y_hy, y_mla, y_dif, h_f, h_b, u, gate, w_branch[l].astype(BF16), ml_norm[l].reshape(1, -1),
                        n_tok)
        xo, f_all, top_idx, top_gate = _out_proj(merged, w_o, x_all, g_a, gain_ffn, sc_f, sh_f, rw, rb,
                                                 n_tok, lat_rows, n)
        blk_e, slot_tok, dest = _route(top_idx[:, :TOP_K])
        y_slots = _moe_experts(f_all, blk_e, slot_tok, exp_w_gu[l].astype(BF16), exp_b_gu[l],
                               exp_w_down[l].astype(BF16), exp_b_down[l])
        fgain = final_norm.reshape(1, d)
        x_all = _moe_combine(y_slots, dest, top_gate, xo, g_f, fgain, last, n_tok, lat_rows, n)

    return x_all.reshape(batch, n, d)
```

```python
import functools
import math

import jax
import jax.numpy as jnp
from jax import lax
from jax.experimental import pallas as pl
from jax.experimental.pallas import tpu as pltpu

F32 = jnp.float32
BF16 = jnp.bfloat16
HIGHEST = lax.Precision.HIGHEST

D_MODEL = 2048
GRID_W = 64
BRANCH_W = 512
N_HEADS = 4
HEAD_W = 128
MLA_NOPE = 128
MLA_ROPE = 64
MLA_QK_PAD = 256
MLA_SCALE = (MLA_NOPE + MLA_ROPE) ** -0.5
LOG2E = math.log2(math.e)
DIF_QK = 64
ML_CHUNK = 64
N_EXPERTS = 32
TOP_K = 4
D_EXPERT = 1024
SWIGLU_LIMIT = 7.0
SWIGLU_ALPHA = 1.702
MOE_BLOCK = 256
ROPE_THETA = 10000.0
NORM_EPS = 1e-6
HY_BANDS = 16
HY_EMB = 1 + 2 * HY_BANDS
HY_HID = 64
LANE = 128
SUBLANE = 8
ROW_TILES = D_MODEL // LANE
SMALL_COLS = 6144
GATE_COLS = 4 * D_MODEL

CB_DK, CB_DV, CB_MK, CB_MV, CB_QA, CB_DQ, CB_MQ, CB_MO, CB_HX0, CB_HX1, CB_HV = range(11)
CB_CKV_256 = 22
CB_KPE_128 = 46


def _cparams(sem, vmem_mb=48):
    return pltpu.CompilerParams(dimension_semantics=sem, vmem_limit_bytes=vmem_mb * 1024 * 1024)


def _rms(x):
    return x * lax.rsqrt(jnp.mean(x * x, axis=-1, keepdims=True) + NORM_EPS)


def _sigmoid(x):
    return 1.0 / (1.0 + jnp.exp(-x))


def _silu(x):
    return x * _sigmoid(x)


def _log_sigmoid(x):
    return jnp.minimum(x, 0.0) - jnp.log(1.0 + jnp.exp(-jnp.abs(x)))


def _bdot(a, b):
    return jnp.dot(a.astype(BF16), b.astype(BF16), preferred_element_type=F32)


def _bdot_nt(a, b):
    return lax.dot_general(a.astype(BF16), b.astype(BF16), (((1,), (1,)), ((), ())),
                           preferred_element_type=F32)


def _hdot(a, b):
    return jnp.dot(a, b, preferred_element_type=F32, precision=HIGHEST)


def _mod_kernel(c_ref, w_ref, b_ref, o_ref):
    o_ref[0] = _bdot(_silu(c_ref[...]), w_ref[0]) + b_ref[0]


def _modulation(cvec, ada_w, ada_b):
    n_layers, d, six_d = ada_w.shape
    tn = 1024
    return pl.pallas_call(
        _mod_kernel,
        grid=(n_layers, six_d // tn),
        in_specs=[pl.BlockSpec((SUBLANE, d), lambda l, j: (0, 0)),
                  pl.BlockSpec((1, d, tn), lambda l, j: (l, 0, j)),
                  pl.BlockSpec((1, 1, tn), lambda l, j: (l, 0, j))],
        out_specs=pl.BlockSpec((1, SUBLANE, tn), lambda l, j: (l, 0, j)),
        out_shape=jax.ShapeDtypeStruct((n_layers, SUBLANE, six_d), F32),
        compiler_params=_cparams(("parallel", "parallel")),
        name="adaln_modulation",
    )(cvec, ada_w, ada_b.reshape(n_layers, 1, six_d))


def _group_of_tile(i, tm, lat_rows, seq_rows):
    return jnp.minimum(i // (seq_rows // tm), lat_rows // seq_rows)


def _norm_mod_kernel(x_ref, g_ref, sc_ref, sh_ref, o_ref):
    h = _rms(x_ref[...]) * g_ref[...]
    o_ref[...] = (h * (1.0 + sc_ref[0]) + sh_ref[0]).astype(BF16)


def _norm_mod(x_all, gain, sc, sh, lat_rows, seq_rows, tm=512):
    t_all, d = x_all.shape
    grp = functools.partial(_group_of_tile, tm=tm, lat_rows=lat_rows, seq_rows=seq_rows)
    return pl.pallas_call(
        _norm_mod_kernel,
        grid=(t_all // tm,),
        in_specs=[pl.BlockSpec((tm, d), lambda i: (i, 0)),
                  pl.BlockSpec((1, d), lambda i: (0, 0)),
                  pl.BlockSpec((1, 1, d), lambda i: (grp(i), 0, 0)),
                  pl.BlockSpec((1, 1, d), lambda i: (grp(i), 0, 0))],
        out_specs=pl.BlockSpec((tm, d), lambda i: (i, 0)),
        out_shape=jax.ShapeDtypeStruct((t_all, d), BF16),
        compiler_params=_cparams(("parallel",)),
        name="norm_mod",
    )(x_all, gain, sc, sh)


def _proj_kernel(h_ref, w_ref, o_ref):
    o_ref[...] = jnp.dot(h_ref[...], w_ref[...], preferred_element_type=F32).astype(o_ref.dtype)


def _in_proj(h, w, out_dtype, nrows, tm=512, tn=1024):
    d = h.shape[1]
    ncol = w.shape[1]
    return pl.pallas_call(
        _proj_kernel,
        grid=(ncol // tn, nrows // tm),
        in_specs=[pl.BlockSpec((tm, d), lambda j, i: (i, 0)),
                  pl.BlockSpec((d, tn), lambda j, i: (0, j))],
        out_specs=pl.BlockSpec((tm, tn), lambda j, i: (i, j)),
        out_shape=jax.ShapeDtypeStruct((nrows, ncol), out_dtype),
        compiler_params=_cparams(("parallel", "arbitrary")),
        name="in_proj",
    )(h, w)


def _rope(x, c, s):
    w = x.shape[-1]
    lane = lax.broadcasted_iota(jnp.int32, x.shape, 1)
    up = pltpu.roll(x, w - 16, 1)
    dn = pltpu.roll(x, 16, 1)
    return x * c + jnp.where((lane % 32) < 16, up, dn) * s


def _attn_prep_kernel(dk_ref, dv_ref, qa_ref, dq_ref, ckv_ref, kpe_ref, cd_ref, sd_ref, cm_ref, sm_ref,
                      wqb_ref, wkvb_ref, qn_ref, kvn_ref,
                      mq_ref, mk_ref, mv_ref, dqo_ref, dko_ref, dvo_ref):
    cd4 = jnp.concatenate([cd_ref[...]] * N_HEADS, axis=1)
    sd4 = jnp.concatenate([sd_ref[...]] * N_HEADS, axis=1)
    cm, sm = cm_ref[...], sm_ref[...]
    dqo_ref[...] = (_rope(dq_ref[...], cd4, sd4) * (DIF_QK ** -0.5 * LOG2E)).astype(BF16)
    dko_ref[...] = _rope(dk_ref[...], cd4, sd4).astype(BF16)
    dvo_ref[...] = dv_ref[...].astype(BF16)

    q = _bdot(_rms(qa_ref[...]) * qn_ref[...], wqb_ref[...])
    kv = _bdot(_rms(ckv_ref[...]) * kvn_ref[...], wkvb_ref[...])
    kpe = kpe_ref[...]
    lane = lax.broadcasted_iota(jnp.int32, kpe.shape, 1)
    kpe = _rope(jnp.where(lane < MLA_ROPE, kpe, 0.0), cm, sm).astype(BF16)
    for h in range(N_HEADS):
        o = h * MLA_QK_PAD
        mq_ref[:, o:o + MLA_NOPE] = (q[:, o:o + MLA_NOPE] * (MLA_SCALE * LOG2E)).astype(BF16)
        mq_ref[:, o + MLA_NOPE:o + MLA_QK_PAD] = (
            _rope(q[:, o + MLA_NOPE:o + MLA_QK_PAD], cm, sm) * (MLA_SCALE * LOG2E)).astype(BF16)
        mk_ref[:, o:o + MLA_NOPE] = kv[:, h * MLA_NOPE:(h + 1) * MLA_NOPE].astype(BF16)
        mk_ref[:, o + MLA_NOPE:o + MLA_QK_PAD] = kpe
    mv_ref[...] = kv[:, N_HEADS * MLA_NOPE:].astype(BF16)


def _attn_prep(u, tabs, wqb, wkvb, qn, kvn, tm=256):
    t_all = u.shape[0]
    cd, sd, cm, sm = tabs
    col = lambda w, cb: pl.BlockSpec((tm, w), lambda i, cb=cb: (i, cb))
    tab = pl.BlockSpec((tm, LANE), lambda i: (i, 0))
    full = lambda a: pl.BlockSpec(a.shape, lambda i: (0,) * a.ndim)
    out = lambda w: pl.BlockSpec((tm, w), lambda i: (i, 0))
    shp = lambda w: jax.ShapeDtypeStruct((t_all, w), BF16)
    return pl.pallas_call(
        _attn_prep_kernel,
        grid=(t_all // tm,),
        in_specs=[col(512, CB_DK), col(512, CB_DV), col(512, CB_QA), col(512, CB_DQ),
                  col(256, CB_CKV_256), col(128, CB_KPE_128), tab, tab, tab, tab,
                  full(wqb), full(wkvb), full(qn), full(kvn)],
        out_specs=[out(1024), out(1024), out(512), out(512), out(512), out(512)],
        out_shape=[shp(1024), shp(1024), shp(512), shp(512), shp(512), shp(512)],
        compiler_params=_cparams(("parallel",)),
        name="attn_prep",
    )(u, u, u, u, u, u, cd, sd, cm, sm, wqb, wkvb, qn, kvn)


def _conv3(x, prev_blk, next_blk, w, first, last):
    tm = x.shape[0]
    row = lax.broadcasted_iota(jnp.int32, x.shape, 0)
    prev_row = jnp.where(first, 0.0, prev_blk[SUBLANE - 1:SUBLANE, :])
    next_row = jnp.where(last, 0.0, next_blk[0:1, :])
    xm = jnp.where(row == 0, prev_row, pltpu.roll(x, 1, 0))
    xp = jnp.where(row == tm - 1, next_row, pltpu.roll(x, tm - 1, 0))
    return xm * w[0:1, :] + x * w[1:2, :] + xp * w[2:3, :]


def _seq_edges(tile, tm, seq_rows, lat_rows, ctx_len):
    r = tile * tm
    in_lat = r < lat_rows
    pos = jnp.where(in_lat, r % seq_rows, (r - lat_rows) % ctx_len)
    length = jnp.where(in_lat, seq_rows, ctx_len)
    return pos == 0, pos + tm == length


def _ml_prep_kernel(seq_rows, lat_rows, ctx_len,
                    q_ref, qp_ref, qn_ref, k_ref, kp_ref, kn_ref, gblk_ref, wq_ref, wk_ref, gb_ref,
                    qo_ref, ko_ref, go_ref):
    tm = q_ref.shape[0]
    first, last = _seq_edges(pl.program_id(0), tm, seq_rows, lat_rows, ctx_len)
    qo_ref[...] = _silu(_conv3(q_ref[...], qp_ref[...], qn_ref[...], wq_ref[...], first, last))
    ko_ref[...] = _silu(_conv3(k_ref[...], kp_ref[...], kn_ref[...], wk_ref[...], first, last)) * HEAD_W ** -0.5
    go_ref[...] = gblk_ref[:, MLA_ROPE:MLA_ROPE + 16] + gb_ref[...]


def _halo_specs(tm, w, cb, t_all, r0=0):
    nb8 = t_all // SUBLANE
    r8 = tm // SUBLANE
    return [pl.BlockSpec((tm, w), lambda i: (i + r0, cb)),
            pl.BlockSpec((SUBLANE, w), lambda i: (jnp.maximum((i + r0) * r8 - 1, 0), cb)),
            pl.BlockSpec((SUBLANE, w), lambda i: (jnp.minimum((i + r0 + 1) * r8, nb8 - 1), cb))]


def _ml_prep(u, wq, wk, gb, seq_rows, lat_rows, ctx_len, tm=256):
    t_all = u.shape[0]
    full = lambda a: pl.BlockSpec(a.shape, lambda i: (0,) * a.ndim)
    return pl.pallas_call(
        functools.partial(_ml_prep_kernel, seq_rows, lat_rows, ctx_len),
        grid=(t_all // tm,),
        in_specs=_halo_specs(tm, 512, CB_MQ, t_all) + _halo_specs(tm, 512, CB_MK, t_all)
        + [pl.BlockSpec((tm, LANE), lambda i: (i, CB_KPE_128)), full(wq), full(wk), full(gb)],
        out_specs=[pl.BlockSpec((tm, 512), lambda i: (i, 0)), pl.BlockSpec((tm, 512), lambda i: (i, 0)),
                   pl.BlockSpec((tm, 16), lambda i: (i, 0))],
        out_shape=[jax.ShapeDtypeStruct((t_all, 512), F32), jax.ShapeDtypeStruct((t_all, 512), F32),
                   jax.ShapeDtypeStruct((t_all, 16), F32)],
        compiler_params=_cparams(("parallel",)),
        name="mlstm_prep",
    )(u, u, u, u, u, u, u, wq, wk, gb)


def _hy_prep_kernel(r0, seq_rows, lat_rows, ctx_len,
                    a_ref, ap_ref, an_ref, b_ref, bp_ref, bn_ref, c_ref, cp_ref, cn_ref, w_ref, bias_ref,
                    z_ref, x0_ref):
    tm = a_ref.shape[0]
    first, last = _seq_edges(pl.program_id(0) + r0, tm, seq_rows, lat_rows, ctx_len)
    w, bias = w_ref[...], bias_ref[...]
    outs = []
    for s, (m, p, n) in enumerate(((a_ref, ap_ref, an_ref), (b_ref, bp_ref, bn_ref), (c_ref, cp_ref, cn_ref))):
        sl = slice(s * BRANCH_W, (s + 1) * BRANCH_W)
        outs.append(_conv3(m[...], p[...], n[...], w[:, sl], first, last) + bias[:, sl])
    x0_ref[...] = outs[0]
    z_ref[...] = outs[2] * outs[1]


def _hy_prep(u, w, bias, row0, nrows, seq_rows, lat_rows, ctx_len, tm=256):
    t_all = u.shape[0]
    r0 = row0 // tm
    full = lambda a: pl.BlockSpec(a.shape, lambda i: (0,) * a.ndim)
    halo = lambda cb: _halo_specs(tm, 512, cb, t_all, r0)
    return pl.pallas_call(
        functools.partial(_hy_prep_kernel, r0, seq_rows, lat_rows, ctx_len),
        grid=(nrows // tm,),
        in_specs=halo(CB_HX0) + halo(CB_HX1) + halo(CB_HV) + [full(w), full(bias)],
        out_specs=[pl.BlockSpec((tm, 512), lambda i: (i, 0)), pl.BlockSpec((tm, 512), lambda i: (i, 0))],
        out_shape=[jax.ShapeDtypeStruct((nrows, 512), F32), jax.ShapeDtypeStruct((nrows, 512), F32)],
        compiler_params=_cparams(("parallel",)),
        name="hyena_prep",
    )(u, u, u, u, u, u, u, u, u, w, bias)


def _attn_kernel(n_maps, seg_lens, tk, lam_init, *refs):
    q_ref = refs[0]
    kv_refs = refs[1:1 + 2 * len(seg_lens)]
    pos = 1 + 2 * len(seg_lens)
    if n_maps == 2:
        lam_ref, gain_ref = refs[pos], refs[pos + 1]
        pos += 2
    o_ref = refs[pos]

    q = q_ref[...]
    tq = q.shape[0]
    if n_maps == 1:
        qs = [q]
    else:
        lane = lax.broadcasted_iota(jnp.int32, q.shape, 1)
        zero = jnp.zeros_like(q)
        qs = [jnp.where(lane < DIF_QK, q, zero), jnp.where(lane >= DIF_QK, q, zero)]

    def step(k, v, carry):
        new = []
        for qm, (m, l, acc) in zip(qs, carry):
            s = lax.dot_general(qm, k, (((1,), (1,)), ((), ())), preferred_element_type=F32)
            m_new = jnp.maximum(m, jnp.max(s, axis=-1, keepdims=True))
            p = jnp.exp2(s - m_new)
            alpha = jnp.exp2(m - m_new)
            l_new = alpha * l + jnp.sum(p, axis=-1, keepdims=True)
            acc_new = alpha * acc + jnp.dot(p.astype(BF16), v, preferred_element_type=F32)
            new.append((m_new, l_new, acc_new))
        return tuple(new)

    carry = tuple((jnp.full((tq, 1), -jnp.inf, F32), jnp.zeros((tq, 1), F32), jnp.zeros((tq, HEAD_W), F32))
                  for _ in range(n_maps))
    for si, slen in enumerate(seg_lens):
        k_ref, v_ref = kv_refs[2 * si], kv_refs[2 * si + 1]
        chunk = min(tk, slen)

        def body(c, carry, k_ref=k_ref, v_ref=v_ref, chunk=chunk):
            start = pl.multiple_of(c * chunk, chunk)
            return step(k_ref[pl.ds(start, chunk), :], v_ref[pl.ds(start, chunk), :], carry)

        trips = slen // chunk
        carry = lax.fori_loop(0, trips, body, carry, unroll=2 if trips % 2 == 0 else 1)

    outs = [acc / l for (_, l, acc) in carry]
    if n_maps == 1:
        o_ref[...] = outs[0].astype(o_ref.dtype)
    else:
        lp = lam_ref[...]
        lam = (jnp.exp(jnp.sum(lp[0:1] * lp[1:2], axis=-1, keepdims=True))
               - jnp.exp(jnp.sum(lp[2:3] * lp[3:4], axis=-1, keepdims=True)) + lam_init)
        o = outs[0] - lam * outs[1]
        o_ref[...] = (_rms(o) * gain_ref[...] * (1.0 - lam_init)).astype(o_ref.dtype)


def _attention(q, k, v, n_maps, q_row0, q_len, segs, batch, qk_w, lam=None, gain=None, lam_init=0.0,
               tq=1024, tk=512):
    tq = min(tq, q_len)
    nq = q_len // tq
    qb0 = q_row0 // tq
    in_specs = [pl.BlockSpec((tq, qk_w), lambda b, h, i: (qb0 + b * nq + i, h))]
    args = [q]
    for (r0, slen) in segs:
        kb0 = r0 // slen
        in_specs.append(pl.BlockSpec((slen, qk_w), lambda b, h, i, kb0=kb0: (kb0 + b, h)))
        in_specs.append(pl.BlockSpec((slen, HEAD_W), lambda b, h, i, kb0=kb0: (kb0 + b, h)))
        args += [k, v]
    if n_maps == 2:
        in_specs += [pl.BlockSpec(lam.shape, lambda b, h, i: (0, 0)),
                     pl.BlockSpec(gain.shape, lambda b, h, i: (0, 0))]
        args += [lam, gain]
    return pl.pallas_call(
        functools.partial(_attn_kernel, n_maps, tuple(s for _, s in segs), tk, lam_init),
        grid=(batch, N_HEADS, nq),
        in_specs=in_specs,
        out_specs=pl.BlockSpec((tq, HEAD_W), lambda b, h, i: (b * nq + i, h)),
        out_shape=jax.ShapeDtypeStruct((batch * q_len, N_HEADS * HEAD_W), BF16),
        compiler_params=_cparams(("parallel", "parallel", "arbitrary")),
        name="attention_maps%d" % n_maps,
    )(*args)


ML_STEP = 256


def _mlstm_chunk(rev, o, q_ref, k_ref, v_ref, g_ref, gt_ref, state, d, b, h_ref):
    ti = lax.broadcasted_iota(jnp.int32, (ML_CHUNK, ML_CHUNK), 0)
    si = lax.broadcasted_iota(jnp.int32, (ML_CHUNK, ML_CHUNK), 1)
    tri = (si >= ti) if rev else (si <= ti)
    tri_f = tri.astype(F32)
    tri_t = ((ti >= si) if rev else (ti <= si)).astype(F32)
    gcol = g_ref[o:o + ML_CHUNK, :]
    grow = gt_ref[0, :, o:o + ML_CHUNK]
    li_col = gcol[:, d * 8:d * 8 + 4]
    lf_col = _log_sigmoid(gcol[:, d * 8 + 4:d * 8 + 8])
    li_row = grow[d * 8:d * 8 + 4, :]
    lf_row = _log_sigmoid(grow[d * 8 + 4:d * 8 + 8, :])
    bc_col = _hdot(tri_f, lf_col)
    bc_row = _hdot(lf_row, tri_t)
    f_tot = jnp.sum(lf_col, axis=0, keepdims=True)
    for h in range(N_HEADS):
        hs = slice(h * HEAD_W, (h + 1) * HEAD_W)
        qc = q_ref[o:o + ML_CHUNK, hs]
        kc = k_ref[o:o + ML_CHUNK, hs]
        vc = v_ref[o:o + ML_CHUNK, hs]
        bcc, bcr = bc_col[:, h:h + 1], bc_row[h:h + 1, :]
        icol, irow = li_col[:, h:h + 1], li_row[h:h + 1, :]
        fh = f_tot[:, h:h + 1]
        c_mat, n_vec, m = state[d, b, h]
        g_end_r = fh - bcr + irow
        g_end_c = fh - bcc + icol
        m_new = jnp.maximum(fh + m, jnp.max(g_end_r, axis=-1, keepdims=True))
        decay = jnp.exp(fh + m - m_new)
        wk = jnp.exp(g_end_c - m_new)
        kw = kc * wk
        upd = lax.dot_general(kw.astype(BF16), vc.astype(BF16), (((0,), (0,)), ((), ())),
                              preferred_element_type=F32)
        if h_ref is not None:
            inter = bcc + m
            log_d = jnp.where(tri, bcc - bcr + irow, -jnp.inf)
            m_t = jnp.maximum(inter, jnp.max(log_d, axis=-1, keepdims=True))
            s = _bdot_nt(qc, kc) * jnp.exp(log_d - m_t)
            w_inter = jnp.exp(inter - m_t)
            num = _bdot(s, vc) + w_inter * _bdot(qc, c_mat)
            den = jnp.sum(s, axis=-1, keepdims=True) + w_inter * jnp.sum(qc * n_vec, axis=-1, keepdims=True)
            h_ref[b, o:o + ML_CHUNK, hs] = num / jnp.maximum(jnp.abs(den), jnp.exp(-m_t))
        state[d, b, h] = (decay * c_mat + upd, decay * n_vec + jnp.sum(kw, axis=0, keepdims=True), m_new)


def _mlstm_kernel(batch, with_out, n_steps, *refs):
    n_in = 2 * batch * 5
    chain_refs = refs[:n_in]
    c0_ref, n0_ref, m0_ref = refs[n_in:n_in + 3]
    pos = n_in + 3
    h_refs = (None, None)
    if with_out:
        h_refs = (refs[pos], refs[pos + 1])
        pos += 2
    cf_ref, nf_ref, mf_ref, c_scr, n_scr, m_scr = refs[pos:pos + 6]
    j = pl.program_id(0)

    @pl.when(j == 0)
    def _():
        c_scr[...] = c0_ref[...]
        n_scr[...] = n0_ref[...]
        m_scr[...] = m0_ref[...]

    keys = [(d, b, h) for d in range(2) for b in range(batch) for h in range(N_HEADS)]
    state = {key: (c_scr[key], n_scr[key], m_scr[key][:, 0:1]) for key in keys}
    n_chunks = ML_STEP // ML_CHUNK
    for ci in range(n_chunks):
        for d in range(2):
            c = (n_chunks - 1 - ci) if d == 1 else ci
            for b in range(batch):
                q_ref, k_ref, v_ref, g_ref, gt_ref = chain_refs[(d * batch + b) * 5:(d * batch + b + 1) * 5]
                _mlstm_chunk(d == 1, c * ML_CHUNK, q_ref, k_ref, v_ref, g_ref, gt_ref, state, d, b, h_refs[d])
    for key in keys:
        c_mat, n_vec, m = state[key]
        c_scr[key] = c_mat
        n_scr[key] = n_vec
        m_scr[key] = jnp.broadcast_to(m, (1, HEAD_W))

    @pl.when(j == n_steps - 1)
    def _():
        cf_ref[...] = c_scr[...]
        nf_ref[...] = n_scr[...]
        mf_ref[...] = m_scr[...]


def _mlstm(q, k, u, g, gt, init, with_out, row0, seq_len, batch):
    n_steps = seq_len // ML_STEP
    b0 = row0 // ML_STEP
    in_specs, args = [], []
    for d in range(2):
        for b in range(batch):
            step = (lambda j: n_steps - 1 - j) if d == 1 else (lambda j: j)
            blk = lambda j, b=b, step=step: b0 + b * n_steps + step(j)
            in_specs += [pl.BlockSpec((ML_STEP, 512), lambda j, blk=blk: (blk(j), 0)),
                         pl.BlockSpec((ML_STEP, 512), lambda j, blk=blk: (blk(j), 0)),
                         pl.BlockSpec((ML_STEP, 512), lambda j, blk=blk: (blk(j), CB_MV)),
                         pl.BlockSpec((ML_STEP, 16), lambda j, blk=blk: (blk(j), 0)),
                         pl.BlockSpec((1, 16, ML_STEP), lambda j, b=b, step=step: (b, 0, step(j)))]
            args += [q, k, u, g, gt]
    full = lambda shape: pl.BlockSpec(shape, lambda j: (0,) * len(shape))
    st_dims = [(2, batch, N_HEADS, HEAD_W, HEAD_W), (2, batch, N_HEADS, 1, HEAD_W), (2, batch, N_HEADS, 1, HEAD_W)]
    st_specs = [full(s) for s in st_dims]
    st_shapes = [jax.ShapeDtypeStruct(s, F32) for s in st_dims]
    out_specs, out_shapes = list(st_specs), list(st_shapes)
    if with_out:
        out_specs = [pl.BlockSpec((batch, ML_STEP, 512), lambda j: (0, j, 0)),
                     pl.BlockSpec((batch, ML_STEP, 512), lambda j: (0, n_steps - 1 - j, 0))] + out_specs
        out_shapes = [jax.ShapeDtypeStruct((batch, seq_len, 512), F32)] * 2 + out_shapes
    res = pl.pallas_call(
        functools.partial(_mlstm_kernel, batch, with_out, n_steps),
        grid=(n_steps,),
        in_specs=in_specs + st_specs,
        out_specs=out_specs,
        out_shape=out_shapes,
        scratch_shapes=[pltpu.VMEM(s, F32) for s in st_dims],
        compiler_params=_cparams(("arbitrary",)),
        name="mlstm",
    )(*args, *init)
    if with_out:
        return res[0], res[1], tuple(res[2:])
    return None, None, tuple(res)


def _hy_filter_kernel(n, feat_ref, w1_ref, b1_ref, w2_ref, b2_ref, w3_ref, freq_ref, decay_ref,
                      h_ref, norm_ref):
    i = pl.program_id(0)
    tm = feat_ref.shape[0]
    freq = freq_ref[...]
    hid = jnp.sin(freq[0:1, :] * (_hdot(feat_ref[...], w1_ref[...]) + b1_ref[...]))
    hid = jnp.sin(freq[1:2, :] * (_hdot(hid, w2_ref[...]) + b2_ref[...]))
    row = lax.broadcasted_iota(jnp.int32, (tm, 2 * BRANCH_W), 0) + i * tm
    col = lax.broadcasted_iota(jnp.int32, (tm, 2 * BRANCH_W), 1)
    t = row.astype(F32) / n
    h = _hdot(hid, w3_ref[...]) * jnp.exp(-t * jnp.abs(decay_ref[...]))
    h = jnp.where((row == 0) & (col >= BRANCH_W), 0.0, h)
    h_ref[0] = h[:, :BRANCH_W]
    h_ref[1] = h[:, BRANCH_W:]

    @pl.when(i == 0)
    def _():
        norm_ref[...] = jnp.zeros_like(norm_ref)

    norm_ref[...] += jnp.sum(jnp.abs(h), axis=0, keepdims=True)


def _hy_filter(n, feat, w1p, b1, w2, b2, w3, freq, decay_flat):
    tm = min(n, 512)
    full = lambda a: pl.BlockSpec(a.shape, lambda i: (0,) * a.ndim)
    return pl.pallas_call(
        functools.partial(_hy_filter_kernel, n),
        grid=(n // tm,),
        in_specs=[pl.BlockSpec((tm, LANE), lambda i: (i, 0)), full(w1p), full(b1), full(w2), full(b2),
                  full(w3), full(freq), full(decay_flat)],
        out_specs=[pl.BlockSpec((2, tm, BRANCH_W), lambda i: (0, i, 0)),
                   pl.BlockSpec((1, 2 * BRANCH_W), lambda i: (0, 0))],
        out_shape=[jax.ShapeDtypeStruct((2, n, BRANCH_W), F32), jax.ShapeDtypeStruct((1, 2 * BRANCH_W), F32)],
        compiler_params=_cparams(("arbitrary",)),
        name="hyena_filter",
    )(feat, w1p, b1, w2, b2, w3, freq, decay_flat)


def _split_bf16(x):
    hi = x.astype(BF16)
    return hi, (x - hi.astype(F32)).astype(BF16)


def _dot3(a, b):
    d = lambda p, q: jnp.dot(p, q, preferred_element_type=F32)
    return d(a[0], b[0]) + (d(a[0], b[1]) + d(a[1], b[0]))


def _dft_a_kernel(f_ref, x_ref, o_ref):
    o_ref[...] = _dot3((f_ref[0], f_ref[1]), _split_bf16(x_ref[...]))


def _dft_stage_a(table, x, tc=2048):
    k, cols = x.shape
    rows = table.shape[1]
    return pl.pallas_call(
        _dft_a_kernel,
        grid=(cols // tc,),
        in_specs=[pl.BlockSpec(table.shape, lambda j: (0, 0, 0)),
                  pl.BlockSpec((k, tc), lambda j: (0, j))],
        out_specs=pl.BlockSpec((rows, tc), lambda j: (0, j)),
        out_shape=jax.ShapeDtypeStruct((rows, cols), F32),
        compiler_params=_cparams(("parallel",)),
        name="hyena_dft_a",
    )(table, x)


def _spectral_product(s, batch):
    kk = s.shape[0] // 2
    c = BRANCH_W
    x0, x1 = s[:, batch * c:(batch + 1) * c], s[:, (batch + 1) * c:(batch + 2) * c]
    hr = x0[:kk] + x1[:kk]
    hi = x0[kk:] - x1[kk:]
    outs = []
    for b in range(batch):
        z = s[:, b * c:(b + 1) * c]
        zr, zi = z[:kk], z[kk:]
        outs.append(jnp.concatenate([zr * hr - zi * hi, zr * hi + zi * hr], axis=0))
    return jnp.concatenate(outs, axis=1)


def _dft_c_kernel(az_ref, ah_ref, g_ref, gt_ref, o_ref):
    x = jnp.concatenate([az_ref[:, 0].reshape(2 * LANE, BRANCH_W), ah_ref[:, 0].reshape(2 * LANE, BRANCH_W)],
                        axis=1)
    s = _dot3((g_ref[0, 0], g_ref[1, 0]), _split_bf16(x))
    zr, zi = s[:LANE, :BRANCH_W], s[LANE:, :BRANCH_W]
    hr, hi = s[:LANE, BRANCH_W:], s[LANE:, BRANCH_W:]
    prod = jnp.concatenate([zr * hr - zi * hi, zr * hi + zi * hr], axis=0)
    p = _dot3((gt_ref[0, 0], gt_ref[1, 0]), _split_bf16(prod))
    o_ref[:, 0] = p.reshape(2, LANE, BRANCH_W)


def _dft_stage_c(az, ah, g, gt):
    _, n1, _, c = az.shape
    slab = pl.BlockSpec((2, 1, LANE, c), lambda k: (0, k, 0, 0))
    tab = pl.BlockSpec((2, 1, 2 * LANE, 2 * LANE), lambda k: (0, k, 0, 0))
    return pl.pallas_call(
        _dft_c_kernel,
        grid=(n1,),
        in_specs=[slab, slab, tab, tab],
        out_specs=slab,
        out_shape=jax.ShapeDtypeStruct(az.shape, F32),
        compiler_params=_cparams(("parallel",)),
        name="hyena_dft_c",
    )(az, ah, g, gt)


def _hy_epilogue(y, z, x0, norm, skip):
    reps = y.shape[1] // BRANCH_W
    nsum = norm[:, :BRANCH_W] + norm[:, BRANCH_W:]
    inv = jnp.concatenate([1.0 / nsum] * reps, axis=1)
    sk = jnp.concatenate([skip] * reps, axis=1)
    return (x0 * (y * inv + z * sk)).astype(BF16)


def _dft_a_inv_kernel(f_ref, p_ref, z_ref, x0_ref, norm_ref, skip_ref, o_ref):
    y = _dot3((f_ref[0], f_ref[1]), _split_bf16(p_ref[...]))
    o_ref[...] = _hy_epilogue(y, z_ref[...], x0_ref[...], norm_ref[...], skip_ref[...])


def _dft_stage_a_inv(table, p, z, x0, norm, skip, tc=2048):
    rows, cols = p.shape
    zr = z.shape[0]
    col = lambda r: pl.BlockSpec((r, tc), lambda j: (0, j))
    return pl.pallas_call(
        _dft_a_inv_kernel,
        grid=(cols // tc,),
        in_specs=[pl.BlockSpec(table.shape, lambda j: (0, 0, 0)), col(rows), col(zr), col(zr),
                  pl.BlockSpec(norm.shape, lambda j: (0, 0)), pl.BlockSpec(skip.shape, lambda j: (0, 0))],
        out_specs=col(zr),
        out_shape=jax.ShapeDtypeStruct((zr, cols), BF16),
        compiler_params=_cparams(("parallel",)),
        name="hyena_dft_a_inv",
    )(table, p, z, x0, norm, skip)


def _dft_tables(n):
    big_l = 2 * n
    n1 = big_l // LANE
    n1h = n1 // 2
    two_pi = 2.0 * math.pi
    k1 = jnp.arange(n1, dtype=jnp.int32)
    th = two_pi * ((k1[:, None] * k1[None, :]) % n1).astype(F32) / n1
    c_full, s_full = jnp.cos(th), jnp.sin(th)
    ch, sh = c_full[:, :n1h], s_full[:, :n1h]
    fz = jnp.concatenate([jnp.concatenate([ch, sh], axis=1), jnp.concatenate([-sh, ch], axis=1)], axis=0)
    ff = jnp.concatenate([c_full, -s_full], axis=0)
    ct, st = ch.T / big_l, sh.T / big_l
    fi = jnp.concatenate([jnp.concatenate([ct, -st], axis=1), jnp.concatenate([st, ct], axis=1)], axis=0)
    s2 = jnp.arange(LANE, dtype=jnp.int32)
    k = k1[:, None, None] + n1 * s2[None, :, None]
    ph = two_pi * ((k * s2[None, None, :]) % big_l).astype(F32) / big_l
    gr, gi = jnp.cos(ph), -jnp.sin(ph)
    g = jnp.concatenate([jnp.concatenate([gr, -gi], axis=2), jnp.concatenate([gi, gr], axis=2)], axis=1)
    pair = lambda t: jnp.stack(_split_bf16(t))
    return pair(fz), pair(ff), pair(fi), pair(g), pair(jnp.swapaxes(g, 1, 2))


def _hy_small_kernel(batch, n, z_ref, x0_ref, h_ref, f_ref, fi_ref, norm_ref, skip_ref, o_ref):
    parts = [z_ref[b] for b in range(batch)] + [h_ref[0], h_ref[1]]
    s = _hdot(f_ref[...], jnp.concatenate(parts, axis=1))
    y = _hdot(fi_ref[...], _spectral_product(s, batch))
    for b in range(batch):
        o_ref[b] = _hy_epilogue(y[:, b * BRANCH_W:(b + 1) * BRANCH_W], z_ref[b], x0_ref[b],
                                norm_ref[...], skip_ref[...])


def _hy_small(z, x0, hfilt, norm, skip):
    batch, n, c = z.shape
    big_l = 2 * n
    kk = jnp.arange(big_l, dtype=jnp.int32)
    th = 2.0 * math.pi * ((kk[:, None] * jnp.arange(n, dtype=jnp.int32)[None, :]) % big_l).astype(F32) / big_l
    f = jnp.concatenate([jnp.cos(th), -jnp.sin(th)], axis=0)
    fi = jnp.concatenate([jnp.cos(th).T, -jnp.sin(th).T], axis=1) / big_l
    full = lambda a: pl.BlockSpec(a.shape, lambda i: (0,) * a.ndim)
    return pl.pallas_call(
        functools.partial(_hy_small_kernel, batch, n),
        grid=(1,),
        in_specs=[full(z), full(x0), full(hfilt), full(f), full(fi), full(norm), full(skip)],
        out_specs=full(z),
        out_shape=jax.ShapeDtypeStruct(z.shape, BF16),
        compiler_params=_cparams(("arbitrary",)),
        name="hyena_small",
    )(z, x0, hfilt, f, fi, norm, skip)


def _hy_features(n):
    t = jnp.arange(n, dtype=F32) / n
    bands = jnp.linspace(1e-4, HY_BANDS - 1, HY_BANDS, dtype=F32)
    ang = 2.0 * math.pi * t[:, None] * bands[None, :]
    feat = jnp.concatenate([t[:, None], jnp.cos(ang), jnp.sin(ang)], axis=-1)
    return jnp.pad(feat, ((0, 0), (0, LANE - HY_EMB)))


def _merge_kernel(hy_ref, mla_ref, dif_ref, hf_ref, hb_ref, mo_ref, gate_ref, wb_ref, mln_ref, o_ref):
    hsum = hf_ref[...] + hb_ref[...]
    mln = mln_ref[...]
    y_ml = jnp.concatenate(
        [_rms(hsum[:, h * HEAD_W:(h + 1) * HEAD_W]) * mln[:, h * HEAD_W:(h + 1) * HEAD_W]
         for h in range(N_HEADS)], axis=1)
    y_ml = _sigmoid(mo_ref[...]) * y_ml
    ys = (hy_ref[...], mla_ref[...], dif_ref[...], y_ml.astype(BF16))
    acc = None
    for i in range(4):
        gi = _sigmoid(gate_ref[:, i * D_MODEL:(i + 1) * D_MODEL].astype(F32))
        term = gi * jnp.dot(ys[i], wb_ref[i], preferred_element_type=F32)
        acc = term if acc is None else acc + term
    o_ref[...] = acc.astype(BF16)


def _merge(y_hy, y_mla, y_dif, h_f, h_b, u, gate, wb, mln, nrows, tm=256):
    row = pl.BlockSpec((tm, 512), lambda i: (i, 0))
    full = lambda a: pl.BlockSpec(a.shape, lambda i: (0,) * a.ndim)
    return pl.pallas_call(
        _merge_kernel,
        grid=(nrows // tm,),
        in_specs=[row, row, row, row, row,
                  pl.BlockSpec((tm, 512), lambda i: (i, CB_MO)),
                  pl.BlockSpec((tm, GATE_COLS), lambda i: (i, 0)),
                  full(wb), full(mln)],
        out_specs=pl.BlockSpec((tm, D_MODEL), lambda i: (i, 0)),
        out_shape=jax.ShapeDtypeStruct((nrows, D_MODEL), BF16),
        compiler_params=_cparams(("parallel",), 56),
        name="branch_merge",
    )(y_hy, y_mla, y_dif, h_f, h_b, u, gate, wb, mln)


def _out_kernel(m_ref, w_ref, x_ref, ga_ref, gain_ref, sc_ref, sh_ref, rw_ref, rb_ref,
                xo_ref, f_ref, idx_ref, gt_ref):
    y = jnp.dot(m_ref[...], w_ref[...], preferred_element_type=F32)
    x = x_ref[...] + ga_ref[0] * y
    xo_ref[...] = x
    f = _rms(x) * gain_ref[...] * (1.0 + sc_ref[0]) + sh_ref[0]
    tm = f.shape[0]
    for j in range(ROW_TILES):
        f_ref[pl.ds(j, tm, stride=ROW_TILES), :] = f[:, j * LANE:(j + 1) * LANE]
    logits = _bdot(f, rw_ref[...]) + rb_ref[...]
    lane = lax.broadcasted_iota(jnp.int32, logits.shape, 1)
    idx_out = jnp.zeros(logits.shape, jnp.int32)
    val_out = jnp.full(logits.shape, -jnp.inf, F32)
    work = logits
    for k in range(TOP_K):
        mx = jnp.max(work, axis=-1, keepdims=True)
        am = jnp.min(jnp.where(work == mx, lane, LANE), axis=-1, keepdims=True)
        idx_out = jnp.where(lane == k, am, idx_out)
        val_out = jnp.where(lane == k, mx, val_out)
        work = jnp.where(lane == am, -jnp.inf, work)
    e = jnp.exp(val_out - jnp.max(val_out, axis=-1, keepdims=True))
    gt_ref[...] = e / jnp.sum(e, axis=-1, keepdims=True)
    idx_ref[...] = idx_out


def _out_proj(merged, w_out, x_all, ga, gain, sc, sh, rw, rb, nrows, lat_rows, seq_rows, tm=256):
    grp = functools.partial(_group_of_tile, tm=tm, lat_rows=lat_rows, seq_rows=seq_rows)
    row = lambda w: pl.BlockSpec((tm, w), lambda i: (i, 0))
    full = lambda a: pl.BlockSpec(a.shape, lambda i: (0,) * a.ndim)
    mod = pl.BlockSpec((1, 1, D_MODEL), lambda i: (grp(i), 0, 0))
    return pl.pallas_call(
        _out_kernel,
        grid=(nrows // tm,),
        in_specs=[row(D_MODEL), full(w_out), row(D_MODEL), mod, full(gain), mod, mod, full(rw), full(rb)],
        out_specs=[row(D_MODEL), pl.BlockSpec((tm * ROW_TILES, LANE), lambda i: (i, 0)), row(LANE), row(LANE)],
        out_shape=[jax.ShapeDtypeStruct((nrows, D_MODEL), F32), jax.ShapeDtypeStruct((nrows * ROW_TILES, LANE), F32),
                   jax.ShapeDtypeStruct((nrows, LANE), jnp.int32), jax.ShapeDtypeStruct((nrows, LANE), F32)],
        compiler_params=_cparams(("parallel",)),
        name="out_proj_router",
    )(merged, w_out, x_all, ga, gain, sc, sh, rw, rb)


def _moe_kernel(blk_e_ref, tok_ref, nxt_ref, f_hbm, wgu_ref, bgu_ref, wd_ref, bd_ref, o_ref, xbuf, sem):
    i = pl.program_id(0)
    last = pl.num_programs(0) - 1
    cur = i % 2

    buf_rows = MOE_BLOCK * ROW_TILES

    def start_rows(idx_ref, s):
        for r in range(MOE_BLOCK):
            pltpu.make_async_copy(f_hbm.at[idx_ref[0, 0, r]],
                                  xbuf.at[pl.ds(s * buf_rows + r * ROW_TILES, ROW_TILES), :], sem.at[s]).start()

    def wait_rows(s):
        whole = xbuf.at[pl.ds(s * buf_rows, buf_rows), :]
        pltpu.make_async_copy(whole, whole, sem.at[s]).wait()

    @pl.when(i == 0)
    def _():
        start_rows(tok_ref, 0)

    wait_rows(cur)
    start_rows(nxt_ref, 1 - cur)
    base = cur * buf_rows
    x = jnp.concatenate([xbuf[pl.ds(base + j, MOE_BLOCK, stride=ROW_TILES), :].astype(BF16)
                         for j in range(ROW_TILES)], axis=1)
    gu = jnp.dot(x, wgu_ref[0], preferred_element_type=F32) + bgu_ref[0]
    g = jnp.minimum(gu[:, :D_EXPERT], SWIGLU_LIMIT)
    u = jnp.clip(gu[:, D_EXPERT:], -SWIGLU_LIMIT, SWIGLU_LIMIT)
    act = g * _sigmoid(SWIGLU_ALPHA * g) * (u + 1.0)
    o_ref[...] = _bdot(act, wd_ref[0]) + bd_ref[0]

    @pl.when(i == last)
    def _():
        wait_rows(1 - cur)


def _moe_experts(f_all, blk_e, slot_tok, wgu, bgu, wd, bd):
    n_blocks = blk_e.shape[0]
    slots = n_blocks * MOE_BLOCK
    grid_spec = pltpu.PrefetchScalarGridSpec(
        num_scalar_prefetch=1,
        grid=(n_blocks,),
        in_specs=[pl.BlockSpec((1, 1, MOE_BLOCK), lambda i, be: (i, 0, 0), memory_space=pltpu.SMEM),
                  pl.BlockSpec((1, 1, MOE_BLOCK), lambda i, be: (jnp.minimum(i + 1, n_blocks - 1), 0, 0),
                               memory_space=pltpu.SMEM),
                  pl.BlockSpec(memory_space=pl.ANY),
                  pl.BlockSpec((1, D_MODEL, 2 * D_EXPERT), lambda i, be: (be[i], 0, 0)),
                  pl.BlockSpec((1, 1, 2 * D_EXPERT), lambda i, be: (be[i], 0, 0)),
                  pl.BlockSpec((1, D_EXPERT, D_MODEL), lambda i, be: (be[i], 0, 0)),
                  pl.BlockSpec((1, 1, D_MODEL), lambda i, be: (be[i], 0, 0))],
        out_specs=pl.BlockSpec((MOE_BLOCK, D_MODEL), lambda i, be: (i, 0)),
        scratch_shapes=[pltpu.VMEM((2 * MOE_BLOCK * ROW_TILES, LANE), F32), pltpu.SemaphoreType.DMA((2,))],
    )
    tok3 = slot_tok.reshape(n_blocks, 1, MOE_BLOCK)
    return pl.pallas_call(
        _moe_kernel,
        grid_spec=grid_spec,
        out_shape=jax.ShapeDtypeStruct((slots, D_MODEL), F32),
        compiler_params=_cparams(("arbitrary",), 56),
        name="moe_experts",
    )(blk_e, tok3, tok3, f_all,
      wgu, bgu.reshape(N_EXPERTS, 1, 2 * D_EXPERT), wd, bd.reshape(N_EXPERTS, 1, D_MODEL))


def _combine_kernel(final, dest_ref, nxt_ref, y_hbm, x_ref, tg_ref, gf_ref, fn_ref, o_ref, ybuf, sem):
    tm = x_ref.shape[0]
    i = pl.program_id(0)
    last = pl.num_programs(0) - 1
    cur = i % 2

    def start_rows(idx_ref, s):
        for r in range(tm):
            for k in range(TOP_K):
                pltpu.make_async_copy(y_hbm.at[pl.ds(idx_ref[0, 0, r * TOP_K + k], 1), :],
                                      ybuf.at[s, k, pl.ds(r, 1), :], sem.at[s]).start()

    def wait_rows(s):
        for k in range(TOP_K):
            pltpu.make_async_copy(y_hbm.at[pl.ds(0, tm), :], ybuf.at[s, k], sem.at[s]).wait()

    @pl.when(i == 0)
    def _():
        start_rows(dest_ref, 0)

    start_rows(nxt_ref, 1 - cur)
    wait_rows(cur)
    tg = tg_ref[...]
    moe = None
    for k in range(TOP_K):
        term = tg[:, k:k + 1] * ybuf[cur, k]
        moe = term if moe is None else moe + term
    x = x_ref[...] + gf_ref[0] * moe
    if final:
        x = _rms(x) * fn_ref[...]
    o_ref[...] = x

    @pl.when(i == last)
    def _():
        wait_rows(1 - cur)


def _moe_combine(y_slots, dest, top_gate, x_all, gf, final_gain, final, nrows, lat_rows, seq_rows, tm=128):
    grp = functools.partial(_group_of_tile, tm=tm, lat_rows=lat_rows, seq_rows=seq_rows)
    nt = nrows // tm
    grid_spec = pl.GridSpec(
        grid=(nt,),
        in_specs=[pl.BlockSpec((1, 1, tm * TOP_K), lambda i: (i, 0, 0), memory_space=pltpu.SMEM),
                  pl.BlockSpec((1, 1, tm * TOP_K), lambda i: (jnp.minimum(i + 1, nt - 1), 0, 0),
                               memory_space=pltpu.SMEM),
                  pl.BlockSpec(memory_space=pl.ANY),
                  pl.BlockSpec((tm, D_MODEL), lambda i: (i, 0)),
                  pl.BlockSpec((tm, LANE), lambda i: (i, 0)),
                  pl.BlockSpec((1, 1, D_MODEL), lambda i: (grp(i), 0, 0)),
                  pl.BlockSpec((1, D_MODEL), lambda i: (0, 0))],
        out_specs=pl.BlockSpec((tm, D_MODEL), lambda i: (i, 0)),
        scratch_shapes=[pltpu.VMEM((2, TOP_K, tm, D_MODEL), F32), pltpu.SemaphoreType.DMA((2,))],
    )
    dest3 = dest.reshape(nt, 1, tm * TOP_K)
    return pl.pallas_call(
        functools.partial(_combine_kernel, final),
        grid_spec=grid_spec,
        out_shape=jax.ShapeDtypeStruct((nrows, D_MODEL), F32),
        compiler_params=_cparams(("arbitrary",)),
        name="moe_combine",
    )(dest3, dest3, y_slots, x_all, top_gate, gf, final_gain)


def _route(top_idx):
    n = top_idx.shape[0]
    a = n * TOP_K
    onehot = jnp.sum((top_idx[:, :, None] == jnp.arange(N_EXPERTS, dtype=jnp.int32)).astype(jnp.int32), axis=1)
    csum = jnp.cumsum(onehot, axis=0)
    rank = csum - onehot
    counts = csum[-1]
    padded = (counts + MOE_BLOCK - 1) // MOE_BLOCK * MOE_BLOCK
    pend = jnp.cumsum(padded)
    pstart = pend - padded
    dest = pstart[top_idx] + jnp.take_along_axis(rank, top_idx, axis=1)
    n_blocks = (a + N_EXPERTS * (MOE_BLOCK - 1)) // MOE_BLOCK + 1
    slots = n_blocks * MOE_BLOCK
    tok = jnp.broadcast_to(jnp.arange(n, dtype=jnp.int32)[:, None], (n, TOP_K))
    slot_tok = jnp.zeros((slots,), jnp.int32).at[dest.reshape(-1)].set(tok.reshape(-1))
    blk_e = jnp.minimum(jnp.searchsorted(pend, jnp.arange(n_blocks, dtype=jnp.int32) * MOE_BLOCK, side='right'),
                        N_EXPERTS - 1).astype(jnp.int32)
    return blk_e, slot_tok, dest.astype(jnp.int32)


def _rope_tables(n_lat_rows, seq_len, ctx_rows):
    t = jnp.arange(seq_len, dtype=jnp.int32)
    inv = ROPE_THETA ** (-jnp.arange(0, 32, 2, dtype=F32) / 32)

    def cs(p):
        ang = p.astype(F32)[:, None] * inv[None, :]
        return jnp.cos(ang), jnp.sin(ang)

    cr, sr = cs(t // GRID_W)
    cc, sc = cs(t % GRID_W)
    c64 = jnp.concatenate([cr, cr, cc, cc], axis=1)
    s64 = jnp.concatenate([-sr, sr, -sc, sc], axis=1)
    reps = n_lat_rows // seq_len

    def build(c_half2, s_half2):
        c = jnp.concatenate([c64, c_half2], axis=1)
        s = jnp.concatenate([s64, s_half2], axis=1)
        c = jnp.concatenate([c] * reps + [jnp.ones((ctx_rows, LANE), F32)], axis=0)
        s = jnp.concatenate([s] * reps + [jnp.zeros((ctx_rows, LANE), F32)], axis=0)
        return c, s

    cd, sd = build(c64, s64)
    cm, sm = build(jnp.ones_like(c64), jnp.zeros_like(s64))
    return cd, sd, cm, sm


def _layout_w_in(w):
    d = w.shape[0]
    kv_a, dk, dv, mk, mv, mg = (w[:, 0:320], w[:, 320:832], w[:, 832:1344], w[:, 1344:1856],
                                w[:, 1856:2368], w[:, 2368:2384])
    o = 2384
    q_a, dq, mq, mo, hy, gate = (w[:, o:o + 512], w[:, o + 512:o + 1024], w[:, o + 1024:o + 1536],
                                 w[:, o + 1536:o + 2048], w[:, o + 2048:o + 3584], w[:, o + 3584:])
    small = jnp.concatenate([dk, dv, mk, mv, q_a, dq, mq, mo, hy, kv_a[:, :256], kv_a[:, 256:], mg,
                             jnp.zeros((d, SMALL_COLS - 5968), w.dtype)], axis=1)
    return small.astype(BF16), gate.astype(BF16)


def _layout_w_qb(w):
    w = w.reshape(w.shape[0], N_HEADS, MLA_NOPE + MLA_ROPE)
    w = jnp.pad(w, ((0, 0), (0, 0), (0, MLA_QK_PAD - MLA_NOPE - MLA_ROPE)))
    return w.reshape(w.shape[0], N_HEADS * MLA_QK_PAD).astype(BF16)


def _layout_w_kvb(w):
    w = w.reshape(w.shape[0], N_HEADS, MLA_NOPE + HEAD_W)
    return jnp.concatenate([w[:, :, :MLA_NOPE].reshape(w.shape[0], -1),
                            w[:, :, MLA_NOPE:].reshape(w.shape[0], -1)], axis=1).astype(BF16)


def kernel(x, c, ctx, c_ctx, ada_w, ada_b, norm_mix, norm_ffn, w_in, hy_conv, hy_conv_b, hy_w1, hy_b1, hy_w2, hy_b2, hy_w3, hy_freq, hy_decay, hy_skip, mla_q_norm, mla_w_qb, mla_kv_norm, mla_w_kvb, dif_lambda, dif_norm, ml_conv_q, ml_conv_k, ml_gate_b, ml_norm, w_branch, w_out, router_w, router_b, exp_w_gu, exp_b_gu, exp_w_down, exp_b_down, final_norm):
    batch, n, d = x.shape
    n_ctx = ctx.shape[1]
    depth = w_in.shape[0]
    ctx_rows = batch * n_ctx
    lat_rows = batch * n
    t_all = ctx_rows + lat_rows

    assert batch + 1 <= SUBLANE
    cvec = jnp.zeros((SUBLANE, d), F32).at[:batch].set(c).at[batch].set(c_ctx)
    mod = _modulation(cvec, ada_w, ada_b)
    x_all = jnp.concatenate([x.reshape(lat_rows, d), ctx.reshape(ctx_rows, d)], axis=0)
    tabs = _rope_tables(lat_rows, n, ctx_rows)
    assert batch == 2, "the Hyena transform carries the two batches as one complex signal"
    fz_tab, ff_tab, fi_tab, g_tab, gt_tab = _dft_tables(n)
    feat_lat, feat_ctx = _hy_features(n), _hy_features(n_ctx)
    zero_state = (jnp.zeros((2, batch, N_HEADS, HEAD_W, HEAD_W), F32), jnp.zeros((2, batch, N_HEADS, 1, HEAD_W), F32),
                  jnp.zeros((2, batch, N_HEADS, 1, HEAD_W), F32))
    n1h = n // LANE

    for l in range(depth):
        last = l == depth - 1
        lam_init = 0.8 - 0.6 * math.exp(-0.3 * l)
        m6 = mod[l].reshape(SUBLANE, 6, 1, d)
        sh_a, sc_a, g_a, sh_f, sc_f, g_f = (m6[:, i] for i in range(6))
        w_small, w_gate = _layout_w_in(w_in[l])
        gain_mix = norm_mix[l].reshape(1, d)

        n_tok = lat_rows if last else t_all
        h_in = _norm_mod(x_all, gain_mix, sc_a, sh_a, lat_rows, n)
        u = _in_proj(h_in, w_small, F32, t_all)
        gate = _in_proj(h_in, w_gate, BF16, n_tok)

        mq, mk, mv, dq, dk, dv = _attn_prep(
            u, tabs, _layout_w_qb(mla_w_qb[l]), _layout_w_kvb(mla_w_kvb[l]),
            mla_q_norm[l].reshape(1, -1), mla_kv_norm[l].reshape(1, -1))
        ml_q, ml_k, ml_g = _ml_prep(u, ml_conv_q[l], ml_conv_k[l], ml_gate_b[l].reshape(1, 16), n, lat_rows, n_ctx)
        gt_lat = jnp.swapaxes(ml_g[:lat_rows].reshape(batch, n, 16), 1, 2)
        gt_ctx = jnp.swapaxes(ml_g[lat_rows:].reshape(batch, n_ctx, 16), 1, 2)

        lam_p = dif_lambda[l]
        dgain = dif_norm[l].reshape(1, HEAD_W)
        segs = [(lat_rows, n_ctx), (0, n)]
        y_mla = _attention(mq, mk, mv, 1, 0, n, segs, batch, MLA_QK_PAD)
        y_dif = _attention(dq, dk, dv, 2, 0, n, segs, batch, 2 * DIF_QK, lam_p, dgain, lam_init)

        hcf, hcb, st_ctx = _mlstm(ml_q, ml_k, u, ml_g, gt_ctx, zero_state, not last, lat_rows, n_ctx, batch)
        hlf, hlb, _ = _mlstm(ml_q, ml_k, u, ml_g, gt_lat, st_ctx, True, 0, n, batch)

        hw1 = jnp.pad(hy_w1[l], ((0, LANE - HY_EMB), (0, 0)))
        filt_args = (hw1, hy_b1[l].reshape(1, -1), hy_w2[l], hy_b2[l].reshape(1, -1), hy_w3[l], hy_freq[l],
                     hy_decay[l].reshape(1, -1))
        skip = hy_skip[l].reshape(1, -1)
        hconv_b = hy_conv_b[l].reshape(1, -1)
        z, x0 = _hy_prep(u, hy_conv[l], hconv_b, 0, lat_rows, n, lat_rows, n_ctx)
        hfilt, hnorm = _hy_filter(n, feat_lat, *filt_args)
        tail = jnp.flip(hfilt[1], axis=0)
        filt_full = jnp.concatenate([hfilt[0], tail[-1:], tail[:-1]], axis=0)
        n1 = 2 * n1h
        cols = LANE * BRANCH_W
        zr = z.reshape(batch * n1h, cols)
        az = _dft_stage_a(fz_tab, zr)
        ah = _dft_stage_a(ff_tab, filt_full.reshape(n1, cols))
        pz = _dft_stage_c(az.reshape(2, n1, LANE, BRANCH_W), ah.reshape(2, n1, LANE, BRANCH_W), g_tab, gt_tab)
        y_hy = _dft_stage_a_inv(fi_tab, pz.reshape(2 * n1, cols), zr, x0.reshape(batch * n1h, cols), hnorm,
                                skip).reshape(lat_rows, BRANCH_W)

        h_f, h_b = hlf.reshape(lat_rows, -1), hlb.reshape(lat_rows, -1)
        w_o = w_out[l].astype(BF16)
        gain_ffn = norm_ffn[l].reshape(1, d)
        rw = jnp.pad(router_w[l], ((0, 0), (0, LANE - N_EXPERTS))).astype(BF16)
        rb = jnp.concatenate([router_b[l], jnp.full((LANE - N_EXPERTS,), -jnp.inf, F32)]).reshape(1, LANE)

        if not last:
            cseg = [(lat_rows, n_ctx)]
            yc_mla = _attention(mq, mk, mv, 1, lat_rows, n_ctx, cseg, batch, MLA_QK_PAD)
            yc_dif = _attention(dq, dk, dv, 2, lat_rows, n_ctx, cseg, batch, 2 * DIF_QK, lam_p, dgain, lam_init)
            zc, x0c = _hy_prep(u, hy_conv[l], hconv_b, lat_rows, ctx_rows, n, lat_rows, n_ctx)
            hfc, hnc = _hy_filter(n_ctx, feat_ctx, *filt_args)
            yc_hy = _hy_small(zc.reshape(batch, n_ctx, BRANCH_W), x0c.reshape(batch, n_ctx, BRANCH_W), hfc, hnc,
                              skip).reshape(ctx_rows, BRANCH_W)
            cat = lambda a, b: jnp.concatenate([a, b], axis=0)
            y_hy, y_mla, y_dif = cat(y_hy, yc_hy), cat(y_mla, yc_mla), cat(y_dif, yc_dif)
            h_f, h_b = cat(h_f, hcf.reshape(ctx_rows, -1)), cat(h_b, hcb.reshape(ctx_rows, -1))

        merged = _merge(y_hy, y_mla, y_dif, h_f, h_b, u, gate, w_branch[l].astype(BF16), ml_norm[l].reshape(1, -1),
                        n_tok)
        xo, f_all, top_idx, top_gate = _out_proj(merged, w_o, x_all, g_a, gain_ffn, sc_f, sh_f, rw, rb,
                                                 n_tok, lat_rows, n)
        blk_e, slot_tok, dest = _route(top_idx[:, :TOP_K])
        y_slots = _moe_experts(f_all.reshape(n_tok, ROW_TILES, LANE), blk_e, slot_tok, exp_w_gu[l].astype(BF16), exp_b_gu[l],
                               exp_w_down[l].astype(BF16), exp_b_down[l])
        fgain = final_norm.reshape(1, d)
        x_all = _moe_combine(y_slots, dest, top_gate, xo, g_f, fgain, last, n_tok, lat_rows, n)

    return x_all.reshape(batch, n, d)
```

```python
import functools
import math

import jax
import jax.numpy as jnp
from jax import lax
from jax.experimental import pallas as pl
from jax.experimental.pallas import tpu as pltpu

F32 = jnp.float32
BF16 = jnp.bfloat16
HIGHEST = lax.Precision.HIGHEST

D_MODEL = 2048
GRID_W = 64
BRANCH_W = 512
N_HEADS = 4
HEAD_W = 128
MLA_NOPE = 128
MLA_ROPE = 64
MLA_QK_PAD = 256
MLA_SCALE = (MLA_NOPE + MLA_ROPE) ** -0.5
LOG2E = math.log2(math.e)
DIF_QK = 64
ML_CHUNK = 64
N_EXPERTS = 32
TOP_K = 4
D_EXPERT = 1024
SWIGLU_LIMIT = 7.0
SWIGLU_ALPHA = 1.702
MOE_BLOCK = 256
ROPE_THETA = 10000.0
NORM_EPS = 1e-6
HY_BANDS = 16
HY_EMB = 1 + 2 * HY_BANDS
HY_HID = 64
LANE = 128
SUBLANE = 8
ROW_TILES = D_MODEL // LANE
SMALL_COLS = 6144
GATE_COLS = 4 * D_MODEL

CB_DK, CB_DV, CB_MK, CB_MV, CB_QA, CB_DQ, CB_MQ, CB_MO, CB_HX0, CB_HX1, CB_HV = range(11)
CB_CKV_256 = 22
CB_KPE_128 = 46


def _cparams(sem, vmem_mb=48):
    return pltpu.CompilerParams(dimension_semantics=sem, vmem_limit_bytes=vmem_mb * 1024 * 1024)


def _rms(x):
    return x * lax.rsqrt(jnp.mean(x * x, axis=-1, keepdims=True) + NORM_EPS)


def _sigmoid(x):
    return 1.0 / (1.0 + jnp.exp(-x))


def _silu(x):
    return x * _sigmoid(x)


def _log_sigmoid(x):
    return jnp.minimum(x, 0.0) - jnp.log(1.0 + jnp.exp(-jnp.abs(x)))


def _bdot(a, b):
    return jnp.dot(a.astype(BF16), b.astype(BF16), preferred_element_type=F32)


def _bdot_nt(a, b):
    return lax.dot_general(a.astype(BF16), b.astype(BF16), (((1,), (1,)), ((), ())),
                           preferred_element_type=F32)


def _hdot(a, b):
    return jnp.dot(a, b, preferred_element_type=F32, precision=HIGHEST)


def _mod_kernel(c_ref, w_ref, b_ref, o_ref):
    o_ref[0] = _bdot(_silu(c_ref[...]), w_ref[0]) + b_ref[0]


def _modulation(cvec, ada_w, ada_b):
    n_layers, d, six_d = ada_w.shape
    tn = 1024
    return pl.pallas_call(
        _mod_kernel,
        grid=(n_layers, six_d // tn),
        in_specs=[pl.BlockSpec((SUBLANE, d), lambda l, j: (0, 0)),
                  pl.BlockSpec((1, d, tn), lambda l, j: (l, 0, j)),
                  pl.BlockSpec((1, 1, tn), lambda l, j: (l, 0, j))],
        out_specs=pl.BlockSpec((1, SUBLANE, tn), lambda l, j: (l, 0, j)),
        out_shape=jax.ShapeDtypeStruct((n_layers, SUBLANE, six_d), F32),
        compiler_params=_cparams(("parallel", "parallel")),
        name="adaln_modulation",
    )(cvec, ada_w, ada_b.reshape(n_layers, 1, six_d))


def _group_of_tile(i, tm, lat_rows, seq_rows):
    return jnp.minimum(i // (seq_rows // tm), lat_rows // seq_rows)


def _norm_mod_kernel(x_ref, g_ref, sc_ref, sh_ref, o_ref):
    h = _rms(x_ref[...]) * g_ref[...]
    o_ref[...] = (h * (1.0 + sc_ref[0]) + sh_ref[0]).astype(BF16)


def _norm_mod(x_all, gain, sc, sh, lat_rows, seq_rows, tm=512):
    t_all, d = x_all.shape
    grp = functools.partial(_group_of_tile, tm=tm, lat_rows=lat_rows, seq_rows=seq_rows)
    return pl.pallas_call(
        _norm_mod_kernel,
        grid=(t_all // tm,),
        in_specs=[pl.BlockSpec((tm, d), lambda i: (i, 0)),
                  pl.BlockSpec((1, d), lambda i: (0, 0)),
                  pl.BlockSpec((1, 1, d), lambda i: (grp(i), 0, 0)),
                  pl.BlockSpec((1, 1, d), lambda i: (grp(i), 0, 0))],
        out_specs=pl.BlockSpec((tm, d), lambda i: (i, 0)),
        out_shape=jax.ShapeDtypeStruct((t_all, d), BF16),
        compiler_params=_cparams(("parallel",)),
        name="norm_mod",
    )(x_all, gain, sc, sh)


def _proj_kernel(h_ref, w_ref, o_ref):
    o_ref[...] = jnp.dot(h_ref[...], w_ref[...], preferred_element_type=F32).astype(o_ref.dtype)


def _in_proj(h, w, out_dtype, nrows, tm=512, tn=1024):
    d = h.shape[1]
    ncol = w.shape[1]
    return pl.pallas_call(
        _proj_kernel,
        grid=(ncol // tn, nrows // tm),
        in_specs=[pl.BlockSpec((tm, d), lambda j, i: (i, 0)),
                  pl.BlockSpec((d, tn), lambda j, i: (0, j))],
        out_specs=pl.BlockSpec((tm, tn), lambda j, i: (i, j)),
        out_shape=jax.ShapeDtypeStruct((nrows, ncol), out_dtype),
        compiler_params=_cparams(("parallel", "arbitrary")),
        name="in_proj",
    )(h, w)


def _rope(x, c, s):
    w = x.shape[-1]
    lane = lax.broadcasted_iota(jnp.int32, x.shape, 1)
    up = pltpu.roll(x, w - 16, 1)
    dn = pltpu.roll(x, 16, 1)
    return x * c + jnp.where((lane % 32) < 16, up, dn) * s


def _attn_prep_kernel(dk_ref, dv_ref, qa_ref, dq_ref, ckv_ref, kpe_ref, cd_ref, sd_ref, cm_ref, sm_ref,
                      wqb_ref, wkvb_ref, qn_ref, kvn_ref,
                      mq_ref, mk_ref, mv_ref, dqo_ref, dko_ref, dvo_ref):
    cd4 = jnp.concatenate([cd_ref[...]] * N_HEADS, axis=1)
    sd4 = jnp.concatenate([sd_ref[...]] * N_HEADS, axis=1)
    cm, sm = cm_ref[...], sm_ref[...]
    dqo_ref[...] = (_rope(dq_ref[...], cd4, sd4) * (DIF_QK ** -0.5 * LOG2E)).astype(BF16)
    dko_ref[...] = _rope(dk_ref[...], cd4, sd4).astype(BF16)
    dvo_ref[...] = dv_ref[...].astype(BF16)

    q = _bdot(_rms(qa_ref[...]) * qn_ref[...], wqb_ref[...])
    kv = _bdot(_rms(ckv_ref[...]) * kvn_ref[...], wkvb_ref[...])
    kpe = kpe_ref[...]
    lane = lax.broadcasted_iota(jnp.int32, kpe.shape, 1)
    kpe = _rope(jnp.where(lane < MLA_ROPE, kpe, 0.0), cm, sm).astype(BF16)
    for h in range(N_HEADS):
        o = h * MLA_QK_PAD
        mq_ref[:, o:o + MLA_NOPE] = (q[:, o:o + MLA_NOPE] * (MLA_SCALE * LOG2E)).astype(BF16)
        mq_ref[:, o + MLA_NOPE:o + MLA_QK_PAD] = (
            _rope(q[:, o + MLA_NOPE:o + MLA_QK_PAD], cm, sm) * (MLA_SCALE * LOG2E)).astype(BF16)
        mk_ref[:, o:o + MLA_NOPE] = kv[:, h * MLA_NOPE:(h + 1) * MLA_NOPE].astype(BF16)
        mk_ref[:, o + MLA_NOPE:o + MLA_QK_PAD] = kpe
    mv_ref[...] = kv[:, N_HEADS * MLA_NOPE:].astype(BF16)


def _attn_prep(u, tabs, wqb, wkvb, qn, kvn, tm=256):
    t_all = u.shape[0]
    cd, sd, cm, sm = tabs
    col = lambda w, cb: pl.BlockSpec((tm, w), lambda i, cb=cb: (i, cb))
    tab = pl.BlockSpec((tm, LANE), lambda i: (i, 0))
    full = lambda a: pl.BlockSpec(a.shape, lambda i: (0,) * a.ndim)
    out = lambda w: pl.BlockSpec((tm, w), lambda i: (i, 0))
    shp = lambda w: jax.ShapeDtypeStruct((t_all, w), BF16)
    return pl.pallas_call(
        _attn_prep_kernel,
        grid=(t_all // tm,),
        in_specs=[col(512, CB_DK), col(512, CB_DV), col(512, CB_QA), col(512, CB_DQ),
                  col(256, CB_CKV_256), col(128, CB_KPE_128), tab, tab, tab, tab,
                  full(wqb), full(wkvb), full(qn), full(kvn)],
        out_specs=[out(1024), out(1024), out(512), out(512), out(512), out(512)],
        out_shape=[shp(1024), shp(1024), shp(512), shp(512), shp(512), shp(512)],
        compiler_params=_cparams(("parallel",)),
        name="attn_prep",
    )(u, u, u, u, u, u, cd, sd, cm, sm, wqb, wkvb, qn, kvn)


def _conv3(x, prev_blk, next_blk, w, first, last):
    tm = x.shape[0]
    row = lax.broadcasted_iota(jnp.int32, x.shape, 0)
    prev_row = jnp.where(first, 0.0, prev_blk[SUBLANE - 1:SUBLANE, :])
    next_row = jnp.where(last, 0.0, next_blk[0:1, :])
    xm = jnp.where(row == 0, prev_row, pltpu.roll(x, 1, 0))
    xp = jnp.where(row == tm - 1, next_row, pltpu.roll(x, tm - 1, 0))
    return xm * w[0:1, :] + x * w[1:2, :] + xp * w[2:3, :]


def _seq_edges(tile, tm, seq_rows, lat_rows, ctx_len):
    r = tile * tm
    in_lat = r < lat_rows
    pos = jnp.where(in_lat, r % seq_rows, (r - lat_rows) % ctx_len)
    length = jnp.where(in_lat, seq_rows, ctx_len)
    return pos == 0, pos + tm == length


def _ml_prep_kernel(seq_rows, lat_rows, ctx_len,
                    q_ref, qp_ref, qn_ref, k_ref, kp_ref, kn_ref, gblk_ref, wq_ref, wk_ref, gb_ref,
                    qo_ref, ko_ref, go_ref):
    tm = q_ref.shape[0]
    first, last = _seq_edges(pl.program_id(0), tm, seq_rows, lat_rows, ctx_len)
    qo_ref[...] = _silu(_conv3(q_ref[...], qp_ref[...], qn_ref[...], wq_ref[...], first, last))
    ko_ref[...] = _silu(_conv3(k_ref[...], kp_ref[...], kn_ref[...], wk_ref[...], first, last)) * HEAD_W ** -0.5
    go_ref[...] = gblk_ref[:, MLA_ROPE:MLA_ROPE + 16] + gb_ref[...]


def _halo_specs(tm, w, cb, t_all, r0=0):
    nb8 = t_all // SUBLANE
    r8 = tm // SUBLANE
    return [pl.BlockSpec((tm, w), lambda i: (i + r0, cb)),
            pl.BlockSpec((SUBLANE, w), lambda i: (jnp.maximum((i + r0) * r8 - 1, 0), cb)),
            pl.BlockSpec((SUBLANE, w), lambda i: (jnp.minimum((i + r0 + 1) * r8, nb8 - 1), cb))]


def _ml_prep(u, wq, wk, gb, seq_rows, lat_rows, ctx_len, tm=256):
    t_all = u.shape[0]
    full = lambda a: pl.BlockSpec(a.shape, lambda i: (0,) * a.ndim)
    return pl.pallas_call(
        functools.partial(_ml_prep_kernel, seq_rows, lat_rows, ctx_len),
        grid=(t_all // tm,),
        in_specs=_halo_specs(tm, 512, CB_MQ, t_all) + _halo_specs(tm, 512, CB_MK, t_all)
        + [pl.BlockSpec((tm, LANE), lambda i: (i, CB_KPE_128)), full(wq), full(wk), full(gb)],
        out_specs=[pl.BlockSpec((tm, 512), lambda i: (i, 0)), pl.BlockSpec((tm, 512), lambda i: (i, 0)),
                   pl.BlockSpec((tm, 16), lambda i: (i, 0))],
        out_shape=[jax.ShapeDtypeStruct((t_all, 512), F32), jax.ShapeDtypeStruct((t_all, 512), F32),
                   jax.ShapeDtypeStruct((t_all, 16), F32)],
        compiler_params=_cparams(("parallel",)),
        name="mlstm_prep",
    )(u, u, u, u, u, u, u, wq, wk, gb)


def _hy_prep_kernel(r0, seq_rows, lat_rows, ctx_len,
                    a_ref, ap_ref, an_ref, b_ref, bp_ref, bn_ref, c_ref, cp_ref, cn_ref, w_ref, bias_ref,
                    z_ref, x0_ref):
    tm = a_ref.shape[0]
    first, last = _seq_edges(pl.program_id(0) + r0, tm, seq_rows, lat_rows, ctx_len)
    w, bias = w_ref[...], bias_ref[...]
    outs = []
    for s, (m, p, n) in enumerate(((a_ref, ap_ref, an_ref), (b_ref, bp_ref, bn_ref), (c_ref, cp_ref, cn_ref))):
        sl = slice(s * BRANCH_W, (s + 1) * BRANCH_W)
        outs.append(_conv3(m[...], p[...], n[...], w[:, sl], first, last) + bias[:, sl])
    x0_ref[...] = outs[0]
    z_ref[...] = outs[2] * outs[1]


def _hy_prep(u, w, bias, row0, nrows, seq_rows, lat_rows, ctx_len, tm=256):
    t_all = u.shape[0]
    r0 = row0 // tm
    full = lambda a: pl.BlockSpec(a.shape, lambda i: (0,) * a.ndim)
    halo = lambda cb: _halo_specs(tm, 512, cb, t_all, r0)
    return pl.pallas_call(
        functools.partial(_hy_prep_kernel, r0, seq_rows, lat_rows, ctx_len),
        grid=(nrows // tm,),
        in_specs=halo(CB_HX0) + halo(CB_HX1) + halo(CB_HV) + [full(w), full(bias)],
        out_specs=[pl.BlockSpec((tm, 512), lambda i: (i, 0)), pl.BlockSpec((tm, 512), lambda i: (i, 0))],
        out_shape=[jax.ShapeDtypeStruct((nrows, 512), F32), jax.ShapeDtypeStruct((nrows, 512), F32)],
        compiler_params=_cparams(("parallel",)),
        name="hyena_prep",
    )(u, u, u, u, u, u, u, u, u, w, bias)


def _attn_kernel(n_maps, seg_lens, tk, lam_init, *refs):
    q_ref = refs[0]
    kv_refs = refs[1:1 + 2 * len(seg_lens)]
    pos = 1 + 2 * len(seg_lens)
    if n_maps == 2:
        lam_ref, gain_ref = refs[pos], refs[pos + 1]
        pos += 2
    o_ref = refs[pos]

    q = q_ref[...]
    tq = q.shape[0]
    if n_maps == 1:
        qs = [q]
    else:
        lane = lax.broadcasted_iota(jnp.int32, q.shape, 1)
        zero = jnp.zeros_like(q)
        qs = [jnp.where(lane < DIF_QK, q, zero), jnp.where(lane >= DIF_QK, q, zero)]

    def step(k, v, carry):
        new = []
        for qm, (m, l, acc) in zip(qs, carry):
            s = lax.dot_general(qm, k, (((1,), (1,)), ((), ())), preferred_element_type=F32)
            m_new = jnp.maximum(m, jnp.max(s, axis=-1, keepdims=True))
            p = jnp.exp2(s - m_new)
            alpha = jnp.exp2(m - m_new)
            l_new = alpha * l + jnp.sum(p, axis=-1, keepdims=True)
            acc_new = alpha * acc + jnp.dot(p.astype(BF16), v, preferred_element_type=F32)
            new.append((m_new, l_new, acc_new))
        return tuple(new)

    carry = tuple((jnp.full((tq, 1), -jnp.inf, F32), jnp.zeros((tq, 1), F32), jnp.zeros((tq, HEAD_W), F32))
                  for _ in range(n_maps))
    for si, slen in enumerate(seg_lens):
        k_ref, v_ref = kv_refs[2 * si], kv_refs[2 * si + 1]
        chunk = min(tk, slen)

        def body(c, carry, k_ref=k_ref, v_ref=v_ref, chunk=chunk):
            start = pl.multiple_of(c * chunk, chunk)
            return step(k_ref[pl.ds(start, chunk), :], v_ref[pl.ds(start, chunk), :], carry)

        trips = slen // chunk
        carry = lax.fori_loop(0, trips, body, carry, unroll=2 if trips % 2 == 0 else 1)

    outs = [acc / l for (_, l, acc) in carry]
    if n_maps == 1:
        o_ref[...] = outs[0].astype(o_ref.dtype)
    else:
        lp = lam_ref[...]
        lam = (jnp.exp(jnp.sum(lp[0:1] * lp[1:2], axis=-1, keepdims=True))
               - jnp.exp(jnp.sum(lp[2:3] * lp[3:4], axis=-1, keepdims=True)) + lam_init)
        o = outs[0] - lam * outs[1]
        o_ref[...] = (_rms(o) * gain_ref[...] * (1.0 - lam_init)).astype(o_ref.dtype)


def _attention(q, k, v, n_maps, q_row0, q_len, segs, batch, qk_w, lam=None, gain=None, lam_init=0.0,
               tq=1024, tk=512):
    tq = min(tq, q_len)
    nq = q_len // tq
    qb0 = q_row0 // tq
    in_specs = [pl.BlockSpec((tq, qk_w), lambda b, h, i: (qb0 + b * nq + i, h))]
    args = [q]
    for (r0, slen) in segs:
        kb0 = r0 // slen
        in_specs.append(pl.BlockSpec((slen, qk_w), lambda b, h, i, kb0=kb0: (kb0 + b, h)))
        in_specs.append(pl.BlockSpec((slen, HEAD_W), lambda b, h, i, kb0=kb0: (kb0 + b, h)))
        args += [k, v]
    if n_maps == 2:
        in_specs += [pl.BlockSpec(lam.shape, lambda b, h, i: (0, 0)),
                     pl.BlockSpec(gain.shape, lambda b, h, i: (0, 0))]
        args += [lam, gain]
    return pl.pallas_call(
        functools.partial(_attn_kernel, n_maps, tuple(s for _, s in segs), tk, lam_init),
        grid=(batch, N_HEADS, nq),
        in_specs=in_specs,
        out_specs=pl.BlockSpec((tq, HEAD_W), lambda b, h, i: (b * nq + i, h)),
        out_shape=jax.ShapeDtypeStruct((batch * q_len, N_HEADS * HEAD_W), BF16),
        compiler_params=_cparams(("parallel", "parallel", "arbitrary")),
        name="attention_maps%d" % n_maps,
    )(*args)


ML_STEP = 256


def _mlstm_chunk(rev, o, q_ref, k_ref, v_ref, g_ref, gt_ref, state, d, b, h_ref):
    ti = lax.broadcasted_iota(jnp.int32, (ML_CHUNK, ML_CHUNK), 0)
    si = lax.broadcasted_iota(jnp.int32, (ML_CHUNK, ML_CHUNK), 1)
    tri = (si >= ti) if rev else (si <= ti)
    tri_f = tri.astype(F32)
    tri_t = ((ti >= si) if rev else (ti <= si)).astype(F32)
    gcol = g_ref[o:o + ML_CHUNK, :]
    grow = gt_ref[0, :, o:o + ML_CHUNK]
    li_col = gcol[:, d * 8:d * 8 + 4]
    lf_col = _log_sigmoid(gcol[:, d * 8 + 4:d * 8 + 8])
    li_row = grow[d * 8:d * 8 + 4, :]
    lf_row = _log_sigmoid(grow[d * 8 + 4:d * 8 + 8, :])
    bc_col = _hdot(tri_f, lf_col)
    bc_row = _hdot(lf_row, tri_t)
    f_tot = jnp.sum(lf_col, axis=0, keepdims=True)
    for h in range(N_HEADS):
        hs = slice(h * HEAD_W, (h + 1) * HEAD_W)
        qc = q_ref[o:o + ML_CHUNK, hs]
        kc = k_ref[o:o + ML_CHUNK, hs]
        vc = v_ref[o:o + ML_CHUNK, hs]
        bcc, bcr = bc_col[:, h:h + 1], bc_row[h:h + 1, :]
        icol, irow = li_col[:, h:h + 1], li_row[h:h + 1, :]
        fh = f_tot[:, h:h + 1]
        c_mat, n_vec, m = state[d, b, h]
        g_end_r = fh - bcr + irow
        g_end_c = fh - bcc + icol
        m_new = jnp.maximum(fh + m, jnp.max(g_end_r, axis=-1, keepdims=True))
        decay = jnp.exp(fh + m - m_new)
        wk = jnp.exp(g_end_c - m_new)
        kw = kc * wk
        upd = lax.dot_general(kw.astype(BF16), vc.astype(BF16), (((0,), (0,)), ((), ())),
                              preferred_element_type=F32)
        if h_ref is not None:
            inter = bcc + m
            log_d = jnp.where(tri, bcc - bcr + irow, -jnp.inf)
            m_t = jnp.maximum(inter, jnp.max(log_d, axis=-1, keepdims=True))
            s = _bdot_nt(qc, kc) * jnp.exp(log_d - m_t)
            w_inter = jnp.exp(inter - m_t)
            num = _bdot(s, vc) + w_inter * _bdot(qc, c_mat)
            den = jnp.sum(s, axis=-1, keepdims=True) + w_inter * jnp.sum(qc * n_vec, axis=-1, keepdims=True)
            h_ref[b, o:o + ML_CHUNK, hs] = num / jnp.maximum(jnp.abs(den), jnp.exp(-m_t))
        state[d, b, h] = (decay * c_mat + upd, decay * n_vec + jnp.sum(kw, axis=0, keepdims=True), m_new)


def _mlstm_kernel(batch, with_out, n_steps, *refs):
    n_in = 2 * batch * 5
    chain_refs = refs[:n_in]
    c0_ref, n0_ref, m0_ref = refs[n_in:n_in + 3]
    pos = n_in + 3
    h_refs = (None, None)
    if with_out:
        h_refs = (refs[pos], refs[pos + 1])
        pos += 2
    cf_ref, nf_ref, mf_ref, c_scr, n_scr, m_scr = refs[pos:pos + 6]
    j = pl.program_id(0)

    @pl.when(j == 0)
    def _():
        c_scr[...] = c0_ref[...]
        n_scr[...] = n0_ref[...]
        m_scr[...] = m0_ref[...]

    keys = [(d, b, h) for d in range(2) for b in range(batch) for h in range(N_HEADS)]
    state = {key: (c_scr[key], n_scr[key], m_scr[key][:, 0:1]) for key in keys}
    n_chunks = ML_STEP // ML_CHUNK
    for ci in range(n_chunks):
        for d in range(2):
            c = (n_chunks - 1 - ci) if d == 1 else ci
            for b in range(batch):
                q_ref, k_ref, v_ref, g_ref, gt_ref = chain_refs[(d * batch + b) * 5:(d * batch + b + 1) * 5]
                _mlstm_chunk(d == 1, c * ML_CHUNK, q_ref, k_ref, v_ref, g_ref, gt_ref, state, d, b, h_refs[d])
    for key in keys:
        c_mat, n_vec, m = state[key]
        c_scr[key] = c_mat
        n_scr[key] = n_vec
        m_scr[key] = jnp.broadcast_to(m, (1, HEAD_W))

    @pl.when(j == n_steps - 1)
    def _():
        cf_ref[...] = c_scr[...]
        nf_ref[...] = n_scr[...]
        mf_ref[...] = m_scr[...]


def _mlstm(q, k, u, g, gt, init, with_out, row0, seq_len, batch):
    n_steps = seq_len // ML_STEP
    b0 = row0 // ML_STEP
    in_specs, args = [], []
    for d in range(2):
        for b in range(batch):
            step = (lambda j: n_steps - 1 - j) if d == 1 else (lambda j: j)
            blk = lambda j, b=b, step=step: b0 + b * n_steps + step(j)
            in_specs += [pl.BlockSpec((ML_STEP, 512), lambda j, blk=blk: (blk(j), 0)),
                         pl.BlockSpec((ML_STEP, 512), lambda j, blk=blk: (blk(j), 0)),
                         pl.BlockSpec((ML_STEP, 512), lambda j, blk=blk: (blk(j), CB_MV)),
                         pl.BlockSpec((ML_STEP, 16), lambda j, blk=blk: (blk(j), 0)),
                         pl.BlockSpec((1, 16, ML_STEP), lambda j, b=b, step=step: (b, 0, step(j)))]
            args += [q, k, u, g, gt]
    full = lambda shape: pl.BlockSpec(shape, lambda j: (0,) * len(shape))
    st_dims = [(2, batch, N_HEADS, HEAD_W, HEAD_W), (2, batch, N_HEADS, 1, HEAD_W), (2, batch, N_HEADS, 1, HEAD_W)]
    st_specs = [full(s) for s in st_dims]
    st_shapes = [jax.ShapeDtypeStruct(s, F32) for s in st_dims]
    out_specs, out_shapes = list(st_specs), list(st_shapes)
    if with_out:
        out_specs = [pl.BlockSpec((batch, ML_STEP, 512), lambda j: (0, j, 0)),
                     pl.BlockSpec((batch, ML_STEP, 512), lambda j: (0, n_steps - 1 - j, 0))] + out_specs
        out_shapes = [jax.ShapeDtypeStruct((batch, seq_len, 512), F32)] * 2 + out_shapes
    res = pl.pallas_call(
        functools.partial(_mlstm_kernel, batch, with_out, n_steps),
        grid=(n_steps,),
        in_specs=in_specs + st_specs,
        out_specs=out_specs,
        out_shape=out_shapes,
        scratch_shapes=[pltpu.VMEM(s, F32) for s in st_dims],
        compiler_params=_cparams(("arbitrary",)),
        name="mlstm",
    )(*args, *init)
    if with_out:
        return res[0], res[1], tuple(res[2:])
    return None, None, tuple(res)


def _hy_filter_kernel(n, feat_ref, w1_ref, b1_ref, w2_ref, b2_ref, w3_ref, freq_ref, decay_ref,
                      h_ref, norm_ref):
    i = pl.program_id(0)
    tm = h_ref.shape[1]
    freq = freq_ref[...]
    hid = jnp.sin(freq[0:1, :] * (_hdot(feat_ref[0], w1_ref[...]) + b1_ref[...]))
    hid = jnp.sin(freq[1:2, :] * (_hdot(hid, w2_ref[...]) + b2_ref[...]))
    hw = _hdot(hid, w3_ref[...])
    row = lax.broadcasted_iota(jnp.int32, (tm, BRANCH_W), 0) + i * tm
    decay = jnp.abs(decay_ref[...])
    h_fwd = hw[:tm, :BRANCH_W] * jnp.exp(-(row.astype(F32) / n) * decay[:, :BRANCH_W])
    h_bwd = hw[tm:, BRANCH_W:] * jnp.exp(-((n - row).astype(F32) / n) * decay[:, BRANCH_W:])
    h_bwd = jnp.where(row == 0, 0.0, h_bwd)
    h_ref[0] = h_fwd
    h_ref[1] = h_bwd

    @pl.when(i == 0)
    def _():
        norm_ref[...] = jnp.zeros_like(norm_ref)

    norm_ref[...] += jnp.concatenate([jnp.sum(jnp.abs(h_fwd), axis=0, keepdims=True),
                                      jnp.sum(jnp.abs(h_bwd), axis=0, keepdims=True)], axis=1)


def _hy_filter(n, feat2, w1p, b1, w2, b2, w3, freq, decay_flat):
    tm = feat2.shape[1] // 2
    full = lambda a: pl.BlockSpec(a.shape, lambda i: (0,) * a.ndim)
    return pl.pallas_call(
        functools.partial(_hy_filter_kernel, n),
        grid=(n // tm,),
        in_specs=[pl.BlockSpec((1, 2 * tm, LANE), lambda i: (i, 0, 0)), full(w1p), full(b1), full(w2), full(b2),
                  full(w3), full(freq), full(decay_flat)],
        out_specs=[pl.BlockSpec((2, tm, BRANCH_W), lambda i: (0, i, 0)),
                   pl.BlockSpec((1, 2 * BRANCH_W), lambda i: (0, 0))],
        out_shape=[jax.ShapeDtypeStruct((2, n, BRANCH_W), F32), jax.ShapeDtypeStruct((1, 2 * BRANCH_W), F32)],
        compiler_params=_cparams(("arbitrary",)),
        name="hyena_filter",
    )(feat2, w1p, b1, w2, b2, w3, freq, decay_flat)


def _split_bf16(x):
    hi = x.astype(BF16)
    return hi, (x - hi.astype(F32)).astype(BF16)


def _dot3(a, b):
    d = lambda p, q: jnp.dot(p, q, preferred_element_type=F32)
    return d(a[0], b[0]) + (d(a[0], b[1]) + d(a[1], b[0]))


def _dft_a_kernel(f_ref, x_ref, o_ref):
    o_ref[...] = _dot3((f_ref[0], f_ref[1]), _split_bf16(x_ref[...]))


def _dft_stage_a(table, x, tc=2048):
    k, cols = x.shape
    rows = table.shape[1]
    return pl.pallas_call(
        _dft_a_kernel,
        grid=(cols // tc,),
        in_specs=[pl.BlockSpec(table.shape, lambda j: (0, 0, 0)),
                  pl.BlockSpec((k, tc), lambda j: (0, j))],
        out_specs=pl.BlockSpec((rows, tc), lambda j: (0, j)),
        out_shape=jax.ShapeDtypeStruct((rows, cols), F32),
        compiler_params=_cparams(("parallel",)),
        name="hyena_dft_a",
    )(table, x)


def _dft_c_kernel(az_ref, ah_ref, g_ref, gt_ref, o_ref):
    x = jnp.concatenate([az_ref[:, 0].reshape(2 * LANE, BRANCH_W), ah_ref[:, 0].reshape(2 * LANE, BRANCH_W)],
                        axis=1)
    s = _dot3((g_ref[0, 0], g_ref[1, 0]), _split_bf16(x))
    zr, zi = s[:LANE, :BRANCH_W], s[LANE:, :BRANCH_W]
    hr, hi = s[:LANE, BRANCH_W:], s[LANE:, BRANCH_W:]
    prod = jnp.concatenate([zr * hr - zi * hi, zr * hi + zi * hr], axis=0)
    p = _dot3((gt_ref[0, 0], gt_ref[1, 0]), _split_bf16(prod))
    o_ref[:, 0] = p.reshape(2, LANE, BRANCH_W)


def _dft_stage_c(az, ah, g, gt):
    _, n1, _, c = az.shape
    slab = pl.BlockSpec((2, 1, LANE, c), lambda k: (0, k, 0, 0))
    tab = pl.BlockSpec((2, 1, 2 * LANE, 2 * LANE), lambda k: (0, k, 0, 0))
    return pl.pallas_call(
        _dft_c_kernel,
        grid=(n1,),
        in_specs=[slab, slab, tab, tab],
        out_specs=slab,
        out_shape=jax.ShapeDtypeStruct(az.shape, F32),
        compiler_params=_cparams(("parallel",)),
        name="hyena_dft_c",
    )(az, ah, g, gt)


def _hy_epilogue(y, z, x0, norm, skip):
    reps = y.shape[1] // BRANCH_W
    nsum = norm[:, :BRANCH_W] + norm[:, BRANCH_W:]
    inv = jnp.concatenate([1.0 / nsum] * reps, axis=1)
    sk = jnp.concatenate([skip] * reps, axis=1)
    return (x0 * (y * inv + z * sk)).astype(BF16)


def _dft_a_inv_kernel(f_ref, p_ref, z_ref, x0_ref, norm_ref, skip_ref, o_ref):
    y = _dot3((f_ref[0], f_ref[1]), _split_bf16(p_ref[...]))
    o_ref[...] = _hy_epilogue(y, z_ref[...], x0_ref[...], norm_ref[...], skip_ref[...])


def _dft_stage_a_inv(table, p, z, x0, norm, skip, tc=2048):
    rows, cols = p.shape
    zr = z.shape[0]
    col = lambda r: pl.BlockSpec((r, tc), lambda j: (0, j))
    return pl.pallas_call(
        _dft_a_inv_kernel,
        grid=(cols // tc,),
        in_specs=[pl.BlockSpec(table.shape, lambda j: (0, 0, 0)), col(rows), col(zr), col(zr),
                  pl.BlockSpec(norm.shape, lambda j: (0, 0)), pl.BlockSpec(skip.shape, lambda j: (0, 0))],
        out_specs=col(zr),
        out_shape=jax.ShapeDtypeStruct((zr, cols), BF16),
        compiler_params=_cparams(("parallel",)),
        name="hyena_dft_a_inv",
    )(table, p, z, x0, norm, skip)


def _dft_tables(n):
    big_l = 2 * n
    n1 = big_l // LANE
    n1h = n1 // 2
    two_pi = 2.0 * math.pi
    k1 = jnp.arange(n1, dtype=jnp.int32)
    th = two_pi * ((k1[:, None] * k1[None, :]) % n1).astype(F32) / n1
    c_full, s_full = jnp.cos(th), jnp.sin(th)
    ch, sh = c_full[:, :n1h], s_full[:, :n1h]
    fz = jnp.concatenate([jnp.concatenate([ch, sh], axis=1), jnp.concatenate([-sh, ch], axis=1)], axis=0)
    ff = jnp.concatenate([c_full, -s_full], axis=0)
    ct, st = ch.T / big_l, sh.T / big_l
    fi = jnp.concatenate([jnp.concatenate([ct, -st], axis=1), jnp.concatenate([st, ct], axis=1)], axis=0)
    s2 = jnp.arange(LANE, dtype=jnp.int32)
    k = k1[:, None, None] + n1 * s2[None, :, None]
    ph = two_pi * ((k * s2[None, None, :]) % big_l).astype(F32) / big_l
    gr, gi = jnp.cos(ph), -jnp.sin(ph)
    g = jnp.concatenate([jnp.concatenate([gr, -gi], axis=2), jnp.concatenate([gi, gr], axis=2)], axis=1)
    pair = lambda t: jnp.stack(_split_bf16(t))
    return pair(fz), pair(ff), pair(fi), pair(g), pair(jnp.swapaxes(g, 1, 2))


def _hy_small_kernel(batch, n, z_ref, x0_ref, h_ref, f_ref, fi_ref, norm_ref, skip_ref, o_ref):
    big_l = 2 * n
    f = f_ref[...]
    sz = _hdot(f[:, :n], jnp.concatenate([z_ref[b] for b in range(batch)], axis=1))
    sh = _hdot(f, jnp.concatenate([h_ref[0], h_ref[1]], axis=0))
    hr, hi = sh[:big_l], sh[big_l:]
    prods = []
    for b in range(batch):
        zr = sz[:big_l, b * BRANCH_W:(b + 1) * BRANCH_W]
        zi = sz[big_l:, b * BRANCH_W:(b + 1) * BRANCH_W]
        prods.append(jnp.concatenate([zr * hr - zi * hi, zr * hi + zi * hr], axis=0))
    y = _hdot(fi_ref[...], jnp.concatenate(prods, axis=1))
    for b in range(batch):
        o_ref[b] = _hy_epilogue(y[:, b * BRANCH_W:(b + 1) * BRANCH_W], z_ref[b], x0_ref[b],
                                norm_ref[...], skip_ref[...])


def _hy_small(z, x0, hfilt, norm, skip):
    batch, n, c = z.shape
    big_l = 2 * n
    kk = jnp.arange(big_l, dtype=jnp.int32)
    th = 2.0 * math.pi * ((kk[:, None] * kk[None, :]) % big_l).astype(F32) / big_l
    f = jnp.concatenate([jnp.cos(th), -jnp.sin(th)], axis=0)
    fi = jnp.concatenate([jnp.cos(th[:, :n]).T, -jnp.sin(th[:, :n]).T], axis=1) / big_l
    full = lambda a: pl.BlockSpec(a.shape, lambda i: (0,) * a.ndim)
    return pl.pallas_call(
        functools.partial(_hy_small_kernel, batch, n),
        grid=(1,),
        in_specs=[full(z), full(x0), full(hfilt), full(f), full(fi), full(norm), full(skip)],
        out_specs=full(z),
        out_shape=jax.ShapeDtypeStruct(z.shape, BF16),
        compiler_params=_cparams(("arbitrary",)),
        name="hyena_small",
    )(z, x0, hfilt, f, fi, norm, skip)


def _hy_features(n):
    tm = min(n, 512)
    bands = jnp.linspace(1e-4, HY_BANDS - 1, HY_BANDS, dtype=F32)

    def feats(pos):
        t = pos.astype(F32) / n
        ang = 2.0 * math.pi * t[:, None] * bands[None, :]
        feat = jnp.concatenate([t[:, None], jnp.cos(ang), jnp.sin(ang)], axis=-1)
        return jnp.pad(feat, ((0, 0), (0, LANE - HY_EMB))).reshape(n // tm, tm, LANE)

    r = jnp.arange(n, dtype=jnp.int32)
    return jnp.concatenate([feats(r), feats(n - r)], axis=1)


def _merge_kernel(hy_ref, mla_ref, dif_ref, hf_ref, hb_ref, mo_ref, gate_ref, wb_ref, mln_ref, o_ref):
    hsum = hf_ref[...] + hb_ref[...]
    mln = mln_ref[...]
    y_ml = jnp.concatenate(
        [_rms(hsum[:, h * HEAD_W:(h + 1) * HEAD_W]) * mln[:, h * HEAD_W:(h + 1) * HEAD_W]
         for h in range(N_HEADS)], axis=1)
    y_ml = _sigmoid(mo_ref[...]) * y_ml
    ys = (hy_ref[...], mla_ref[...], dif_ref[...], y_ml.astype(BF16))
    acc = None
    for i in range(4):
        gi = _sigmoid(gate_ref[:, i * D_MODEL:(i + 1) * D_MODEL].astype(F32))
        term = gi * jnp.dot(ys[i], wb_ref[i], preferred_element_type=F32)
        acc = term if acc is None else acc + term
    o_ref[...] = acc.astype(BF16)


def _merge(y_hy, y_mla, y_dif, h_f, h_b, u, gate, wb, mln, nrows, tm=256):
    row = pl.BlockSpec((tm, 512), lambda i: (i, 0))
    full = lambda a: pl.BlockSpec(a.shape, lambda i: (0,) * a.ndim)
    return pl.pallas_call(
        _merge_kernel,
        grid=(nrows // tm,),
        in_specs=[row, row, row, row, row,
                  pl.BlockSpec((tm, 512), lambda i: (i, CB_MO)),
                  pl.BlockSpec((tm, GATE_COLS), lambda i: (i, 0)),
                  full(wb), full(mln)],
        out_specs=pl.BlockSpec((tm, D_MODEL), lambda i: (i, 0)),
        out_shape=jax.ShapeDtypeStruct((nrows, D_MODEL), BF16),
        compiler_params=_cparams(("parallel",), 56),
        name="branch_merge",
    )(y_hy, y_mla, y_dif, h_f, h_b, u, gate, wb, mln)


def _out_kernel(m_ref, w_ref, x_ref, ga_ref, gain_ref, sc_ref, sh_ref, rw_ref, rb_ref,
                xo_ref, f_ref, idx_ref, gt_ref):
    y = jnp.dot(m_ref[...], w_ref[...], preferred_element_type=F32)
    x = x_ref[...] + ga_ref[0] * y
    xo_ref[...] = x
    f = _rms(x) * gain_ref[...] * (1.0 + sc_ref[0]) + sh_ref[0]
    tm = f.shape[0]
    for j in range(ROW_TILES):
        f_ref[pl.ds(j, tm, stride=ROW_TILES), :] = f[:, j * LANE:(j + 1) * LANE]
    logits = _bdot(f, rw_ref[...]) + rb_ref[...]
    lane = lax.broadcasted_iota(jnp.int32, logits.shape, 1)
    idx_out = jnp.zeros(logits.shape, jnp.int32)
    val_out = jnp.full(logits.shape, -jnp.inf, F32)
    work = logits
    for k in range(TOP_K):
        mx = jnp.max(work, axis=-1, keepdims=True)
        am = jnp.min(jnp.where(work == mx, lane, LANE), axis=-1, keepdims=True)
        idx_out = jnp.where(lane == k, am, idx_out)
        val_out = jnp.where(lane == k, mx, val_out)
        work = jnp.where(lane == am, -jnp.inf, work)
    e = jnp.exp(val_out - jnp.max(val_out, axis=-1, keepdims=True))
    gt_ref[...] = e / jnp.sum(e, axis=-1, keepdims=True)
    idx_ref[...] = idx_out


def _out_proj(merged, w_out, x_all, ga, gain, sc, sh, rw, rb, nrows, lat_rows, seq_rows, tm=256):
    grp = functools.partial(_group_of_tile, tm=tm, lat_rows=lat_rows, seq_rows=seq_rows)
    row = lambda w: pl.BlockSpec((tm, w), lambda i: (i, 0))
    full = lambda a: pl.BlockSpec(a.shape, lambda i: (0,) * a.ndim)
    mod = pl.BlockSpec((1, 1, D_MODEL), lambda i: (grp(i), 0, 0))
    return pl.pallas_call(
        _out_kernel,
        grid=(nrows // tm,),
        in_specs=[row(D_MODEL), full(w_out), row(D_MODEL), mod, full(gain), mod, mod, full(rw), full(rb)],
        out_specs=[row(D_MODEL), pl.BlockSpec((tm * ROW_TILES, LANE), lambda i: (i, 0)), row(LANE), row(LANE)],
        out_shape=[jax.ShapeDtypeStruct((nrows, D_MODEL), F32), jax.ShapeDtypeStruct((nrows * ROW_TILES, LANE), F32),
                   jax.ShapeDtypeStruct((nrows, LANE), jnp.int32), jax.ShapeDtypeStruct((nrows, LANE), F32)],
        compiler_params=_cparams(("parallel",)),
        name="out_proj_router",
    )(merged, w_out, x_all, ga, gain, sc, sh, rw, rb)


def _moe_kernel(blk_e_ref, tok_ref, nxt_ref, f_hbm, wgu_ref, bgu_ref, wd_ref, bd_ref, o_ref, xbuf, sem):
    i = pl.program_id(0)
    last = pl.num_programs(0) - 1
    cur = i % 2

    buf_rows = MOE_BLOCK * ROW_TILES

    def start_rows(idx_ref, s):
        for r in range(MOE_BLOCK):
            pltpu.make_async_copy(f_hbm.at[idx_ref[0, 0, r]],
                                  xbuf.at[pl.ds(s * buf_rows + r * ROW_TILES, ROW_TILES), :], sem.at[s]).start()

    def wait_rows(s):
        whole = xbuf.at[pl.ds(s * buf_rows, buf_rows), :]
        pltpu.make_async_copy(whole, whole, sem.at[s]).wait()

    @pl.when(i == 0)
    def _():
        start_rows(tok_ref, 0)

    wait_rows(cur)
    start_rows(nxt_ref, 1 - cur)
    base = cur * buf_rows
    x = jnp.concatenate([xbuf[pl.ds(base + j, MOE_BLOCK, stride=ROW_TILES), :].astype(BF16)
                         for j in range(ROW_TILES)], axis=1)
    gu = jnp.dot(x, wgu_ref[0], preferred_element_type=F32) + bgu_ref[0]
    g = jnp.minimum(gu[:, :D_EXPERT], SWIGLU_LIMIT)
    u = jnp.clip(gu[:, D_EXPERT:], -SWIGLU_LIMIT, SWIGLU_LIMIT)
    act = g * _sigmoid(SWIGLU_ALPHA * g) * (u + 1.0)
    o_ref[...] = _bdot(act, wd_ref[0]) + bd_ref[0]

    @pl.when(i == last)
    def _():
        wait_rows(1 - cur)


def _moe_experts(f_all, blk_e, slot_tok, wgu, bgu, wd, bd):
    n_blocks = blk_e.shape[0]
    slots = n_blocks * MOE_BLOCK
    grid_spec = pltpu.PrefetchScalarGridSpec(
        num_scalar_prefetch=1,
        grid=(n_blocks,),
        in_specs=[pl.BlockSpec((1, 1, MOE_BLOCK), lambda i, be: (i, 0, 0), memory_space=pltpu.SMEM),
                  pl.BlockSpec((1, 1, MOE_BLOCK), lambda i, be: (jnp.minimum(i + 1, n_blocks - 1), 0, 0),
                               memory_space=pltpu.SMEM),
                  pl.BlockSpec(memory_space=pl.ANY),
                  pl.BlockSpec((1, D_MODEL, 2 * D_EXPERT), lambda i, be: (be[i], 0, 0)),
                  pl.BlockSpec((1, 1, 2 * D_EXPERT), lambda i, be: (be[i], 0, 0)),
                  pl.BlockSpec((1, D_EXPERT, D_MODEL), lambda i, be: (be[i], 0, 0)),
                  pl.BlockSpec((1, 1, D_MODEL), lambda i, be: (be[i], 0, 0))],
        out_specs=pl.BlockSpec((MOE_BLOCK, D_MODEL), lambda i, be: (i, 0)),
        scratch_shapes=[pltpu.VMEM((2 * MOE_BLOCK * ROW_TILES, LANE), F32), pltpu.SemaphoreType.DMA((2,))],
    )
    tok3 = slot_tok.reshape(n_blocks, 1, MOE_BLOCK)
    return pl.pallas_call(
        _moe_kernel,
        grid_spec=grid_spec,
        out_shape=jax.ShapeDtypeStruct((slots, D_MODEL), F32),
        compiler_params=_cparams(("arbitrary",), 56),
        name="moe_experts",
    )(blk_e, tok3, tok3, f_all,
      wgu, bgu.reshape(N_EXPERTS, 1, 2 * D_EXPERT), wd, bd.reshape(N_EXPERTS, 1, D_MODEL))


def _combine_kernel(final, dest_ref, nxt_ref, y_hbm, x_ref, tg_ref, gf_ref, fn_ref, o_ref, ybuf, sem):
    tm = x_ref.shape[0]
    i = pl.program_id(0)
    last = pl.num_programs(0) - 1
    cur = i % 2

    def start_rows(idx_ref, s):
        for r in range(tm):
            for k in range(TOP_K):
                pltpu.make_async_copy(y_hbm.at[pl.ds(idx_ref[0, 0, r * TOP_K + k], 1), :],
                                      ybuf.at[s, k, pl.ds(r, 1), :], sem.at[s]).start()

    def wait_rows(s):
        for k in range(TOP_K):
            pltpu.make_async_copy(y_hbm.at[pl.ds(0, tm), :], ybuf.at[s, k], sem.at[s]).wait()

    @pl.when(i == 0)
    def _():
        start_rows(dest_ref, 0)

    start_rows(nxt_ref, 1 - cur)
    wait_rows(cur)
    tg = tg_ref[...]
    moe = None
    for k in range(TOP_K):
        term = tg[:, k:k + 1] * ybuf[cur, k]
        moe = term if moe is None else moe + term
    x = x_ref[...] + gf_ref[0] * moe
    if final:
        x = _rms(x) * fn_ref[...]
    o_ref[...] = x

    @pl.when(i == last)
    def _():
        wait_rows(1 - cur)


def _moe_combine(y_slots, dest, top_gate, x_all, gf, final_gain, final, nrows, lat_rows, seq_rows, tm=128):
    grp = functools.partial(_group_of_tile, tm=tm, lat_rows=lat_rows, seq_rows=seq_rows)
    nt = nrows // tm
    grid_spec = pl.GridSpec(
        grid=(nt,),
        in_specs=[pl.BlockSpec((1, 1, tm * TOP_K), lambda i: (i, 0, 0), memory_space=pltpu.SMEM),
                  pl.BlockSpec((1, 1, tm * TOP_K), lambda i: (jnp.minimum(i + 1, nt - 1), 0, 0),
                               memory_space=pltpu.SMEM),
                  pl.BlockSpec(memory_space=pl.ANY),
                  pl.BlockSpec((tm, D_MODEL), lambda i: (i, 0)),
                  pl.BlockSpec((tm, LANE), lambda i: (i, 0)),
                  pl.BlockSpec((1, 1, D_MODEL), lambda i: (grp(i), 0, 0)),
                  pl.BlockSpec((1, D_MODEL), lambda i: (0, 0))],
        out_specs=pl.BlockSpec((tm, D_MODEL), lambda i: (i, 0)),
        scratch_shapes=[pltpu.VMEM((2, TOP_K, tm, D_MODEL), F32), pltpu.SemaphoreType.DMA((2,))],
    )
    dest3 = dest.reshape(nt, 1, tm * TOP_K)
    return pl.pallas_call(
        functools.partial(_combine_kernel, final),
        grid_spec=grid_spec,
        out_shape=jax.ShapeDtypeStruct((nrows, D_MODEL), F32),
        compiler_params=_cparams(("arbitrary",)),
        name="moe_combine",
    )(dest3, dest3, y_slots, x_all, top_gate, gf, final_gain)


def _expert_onehot(idx):
    lane = lax.broadcasted_iota(jnp.int32, idx.shape, 1)
    oh = jnp.zeros(idx.shape, F32)
    for k in range(TOP_K):
        oh = oh + jnp.where(lane == idx[:, k:k + 1], 1.0, 0.0)
    return oh


def _rank_kernel(idx_ref, rank_ref, cnt_ref, carry):
    @pl.when(pl.program_id(0) == 0)
    def _():
        carry[...] = jnp.zeros_like(carry)

    oh = _expert_onehot(idx_ref[...])
    tm = oh.shape[0]
    r = lax.broadcasted_iota(jnp.int32, (tm, tm), 0)
    c = lax.broadcasted_iota(jnp.int32, (tm, tm), 1)
    earlier = jnp.where(c < r, 1.0, 0.0).astype(BF16)
    rank_ref[...] = jnp.dot(earlier, oh.astype(BF16), preferred_element_type=F32) + carry[...]
    carry[...] += jnp.sum(oh, axis=0, keepdims=True)
    cnt_ref[...] = carry[...]


def _dest_kernel(idx_ref, rank_ref, pstart_ref, dest_ref):
    idx = idx_ref[...]
    slot = rank_ref[...] + pstart_ref[...]
    lane = lax.broadcasted_iota(jnp.int32, idx.shape, 1)
    out = jnp.zeros(idx.shape, jnp.int32)
    for k in range(TOP_K):
        dk = jnp.sum(jnp.where(lane == idx[:, k:k + 1], slot, 0.0), axis=-1, keepdims=True)
        out = jnp.where(lane == k, dk.astype(jnp.int32), out)
    dest_ref[...] = out


def _route(top_idx128, tm=512):
    n = top_idx128.shape[0]
    a = n * TOP_K
    row = pl.BlockSpec((tm, LANE), lambda i: (i, 0))
    one = pl.BlockSpec((1, LANE), lambda i: (0, 0))
    rank, cnt = pl.pallas_call(
        _rank_kernel,
        grid=(n // tm,),
        in_specs=[row],
        out_specs=[row, one],
        out_shape=[jax.ShapeDtypeStruct((n, LANE), F32), jax.ShapeDtypeStruct((1, LANE), F32)],
        scratch_shapes=[pltpu.VMEM((1, LANE), F32)],
        compiler_params=_cparams(("arbitrary",)),
        name="route_rank",
    )(top_idx128)
    counts = cnt[0, :N_EXPERTS].astype(jnp.int32)
    padded = (counts + MOE_BLOCK - 1) // MOE_BLOCK * MOE_BLOCK
    pend = jnp.cumsum(padded)
    pstart = pend - padded
    pstart128 = jnp.pad(pstart.astype(F32), (0, LANE - N_EXPERTS)).reshape(1, LANE)
    dest = pl.pallas_call(
        _dest_kernel,
        grid=(n // tm,),
        in_specs=[row, row, one],
        out_specs=row,
        out_shape=jax.ShapeDtypeStruct((n, LANE), jnp.int32),
        compiler_params=_cparams(("parallel",)),
        name="route_dest",
    )(top_idx128, rank, pstart128)[:, :TOP_K]
    n_blocks = (a + N_EXPERTS * (MOE_BLOCK - 1)) // MOE_BLOCK + 1
    slots = n_blocks * MOE_BLOCK
    tok = jnp.broadcast_to(jnp.arange(n, dtype=jnp.int32)[:, None], (n, TOP_K))
    slot_tok = jnp.zeros((slots,), jnp.int32).at[dest.reshape(-1)].set(tok.reshape(-1))
    blk_e = jnp.minimum(jnp.searchsorted(pend, jnp.arange(n_blocks, dtype=jnp.int32) * MOE_BLOCK, side='right'),
                        N_EXPERTS - 1).astype(jnp.int32)
    return blk_e, slot_tok, dest.astype(jnp.int32)


def _rope_tables(n_lat_rows, seq_len, ctx_rows):
    t = jnp.arange(seq_len, dtype=jnp.int32)
    inv = ROPE_THETA ** (-jnp.arange(0, 32, 2, dtype=F32) / 32)

    def cs(p):
        ang = p.astype(F32)[:, None] * inv[None, :]
        return jnp.cos(ang), jnp.sin(ang)

    cr, sr = cs(t // GRID_W)
    cc, sc = cs(t % GRID_W)
    c64 = jnp.concatenate([cr, cr, cc, cc], axis=1)
    s64 = jnp.concatenate([-sr, sr, -sc, sc], axis=1)
    reps = n_lat_rows // seq_len

    def build(c_half2, s_half2):
        c = jnp.concatenate([c64, c_half2], axis=1)
        s = jnp.concatenate([s64, s_half2], axis=1)
        c = jnp.concatenate([c] * reps + [jnp.ones((ctx_rows, LANE), F32)], axis=0)
        s = jnp.concatenate([s] * reps + [jnp.zeros((ctx_rows, LANE), F32)], axis=0)
        return c, s

    cd, sd = build(c64, s64)
    cm, sm = build(jnp.ones_like(c64), jnp.zeros_like(s64))
    return cd, sd, cm, sm


W_IN_SEGMENTS = ((320, 0, 512), (832, 512, 512), (1344, 1024, 512), (1856, 1536, 512), (2384, 2048, 512),
                 (2896, 2560, 512), (3408, 3072, 512), (3920, 3584, 512), (4432, 4096, 1536),
                 (0, 5632, 256), (256, 5888, 64), (2368, 5952, 16))
W_IN_USED = 5968
W_IN_GATE0 = 5968


def _w_in_layout_kernel(w_ref, small_ref, gate_ref):
    for src, dst, width in W_IN_SEGMENTS:
        small_ref[:, dst:dst + width] = w_ref[:, src:src + width].astype(BF16)
    small_ref[:, W_IN_USED:] = jnp.zeros((small_ref.shape[0], SMALL_COLS - W_IN_USED), BF16)
    gate_ref[...] = w_ref[:, W_IN_GATE0:].astype(BF16)


def _layout_w_in(w, tk=128):
    d, ncol = w.shape
    return pl.pallas_call(
        _w_in_layout_kernel,
        grid=(d // tk,),
        in_specs=[pl.BlockSpec((tk, ncol), lambda i: (i, 0))],
        out_specs=[pl.BlockSpec((tk, SMALL_COLS), lambda i: (i, 0)), pl.BlockSpec((tk, GATE_COLS), lambda i: (i, 0))],
        out_shape=[jax.ShapeDtypeStruct((d, SMALL_COLS), BF16), jax.ShapeDtypeStruct((d, GATE_COLS), BF16)],
        compiler_params=_cparams(("parallel",)),
        name="w_in_layout",
    )(w)


def _layout_w_qb(w):
    w = w.reshape(w.shape[0], N_HEADS, MLA_NOPE + MLA_ROPE)
    w = jnp.pad(w, ((0, 0), (0, 0), (0, MLA_QK_PAD - MLA_NOPE - MLA_ROPE)))
    return w.reshape(w.shape[0], N_HEADS * MLA_QK_PAD).astype(BF16)


def _layout_w_kvb(w):
    w = w.reshape(w.shape[0], N_HEADS, MLA_NOPE + HEAD_W)
    return jnp.concatenate([w[:, :, :MLA_NOPE].reshape(w.shape[0], -1),
                            w[:, :, MLA_NOPE:].reshape(w.shape[0], -1)], axis=1).astype(BF16)


def kernel(x, c, ctx, c_ctx, ada_w, ada_b, norm_mix, norm_ffn, w_in, hy_conv, hy_conv_b, hy_w1, hy_b1, hy_w2, hy_b2, hy_w3, hy_freq, hy_decay, hy_skip, mla_q_norm, mla_w_qb, mla_kv_norm, mla_w_kvb, dif_lambda, dif_norm, ml_conv_q, ml_conv_k, ml_gate_b, ml_norm, w_branch, w_out, router_w, router_b, exp_w_gu, exp_b_gu, exp_w_down, exp_b_down, final_norm):
    batch, n, d = x.shape
    n_ctx = ctx.shape[1]
    depth = w_in.shape[0]
    ctx_rows = batch * n_ctx
    lat_rows = batch * n
    t_all = ctx_rows + lat_rows

    assert batch + 1 <= SUBLANE
    cvec = jnp.zeros((SUBLANE, d), F32).at[:batch].set(c).at[batch].set(c_ctx)
    mod = _modulation(cvec, ada_w, ada_b)
    x_all = jnp.concatenate([x.reshape(lat_rows, d), ctx.reshape(ctx_rows, d)], axis=0)
    tabs = _rope_tables(lat_rows, n, ctx_rows)
    assert batch == 2, "the Hyena transform carries the two batches as one complex signal"
    fz_tab, ff_tab, fi_tab, g_tab, gt_tab = _dft_tables(n)
    feat_lat, feat_ctx = _hy_features(n), _hy_features(n_ctx)
    zero_state = (jnp.zeros((2, batch, N_HEADS, HEAD_W, HEAD_W), F32), jnp.zeros((2, batch, N_HEADS, 1, HEAD_W), F32),
                  jnp.zeros((2, batch, N_HEADS, 1, HEAD_W), F32))
    n1h = n // LANE

    for l in range(depth):
        last = l == depth - 1
        lam_init = 0.8 - 0.6 * math.exp(-0.3 * l)
        m6 = mod[l].reshape(SUBLANE, 6, 1, d)
        sh_a, sc_a, g_a, sh_f, sc_f, g_f = (m6[:, i] for i in range(6))
        w_small, w_gate = _layout_w_in(w_in[l])
        gain_mix = norm_mix[l].reshape(1, d)

        n_tok = lat_rows if last else t_all
        h_in = _norm_mod(x_all, gain_mix, sc_a, sh_a, lat_rows, n)
        u = _in_proj(h_in, w_small, F32, t_all)
        gate = _in_proj(h_in, w_gate, BF16, n_tok)

        mq, mk, mv, dq, dk, dv = _attn_prep(
            u, tabs, _layout_w_qb(mla_w_qb[l]), _layout_w_kvb(mla_w_kvb[l]),
            mla_q_norm[l].reshape(1, -1), mla_kv_norm[l].reshape(1, -1))
        ml_q, ml_k, ml_g = _ml_prep(u, ml_conv_q[l], ml_conv_k[l], ml_gate_b[l].reshape(1, 16), n, lat_rows, n_ctx)
        gt_lat = jnp.swapaxes(ml_g[:lat_rows].reshape(batch, n, 16), 1, 2)
        gt_ctx = jnp.swapaxes(ml_g[lat_rows:].reshape(batch, n_ctx, 16), 1, 2)

        lam_p = dif_lambda[l]
        dgain = dif_norm[l].reshape(1, HEAD_W)
        segs = [(lat_rows, n_ctx), (0, n)]
        y_mla = _attention(mq, mk, mv, 1, 0, n, segs, batch, MLA_QK_PAD)
        y_dif = _attention(dq, dk, dv, 2, 0, n, segs, batch, 2 * DIF_QK, lam_p, dgain, lam_init)

        hcf, hcb, st_ctx = _mlstm(ml_q, ml_k, u, ml_g, gt_ctx, zero_state, not last, lat_rows, n_ctx, batch)
        hlf, hlb, _ = _mlstm(ml_q, ml_k, u, ml_g, gt_lat, st_ctx, True, 0, n, batch)

        hw1 = jnp.pad(hy_w1[l], ((0, LANE - HY_EMB), (0, 0)))
        filt_args = (hw1, hy_b1[l].reshape(1, -1), hy_w2[l], hy_b2[l].reshape(1, -1), hy_w3[l], hy_freq[l],
                     hy_decay[l].reshape(1, -1))
        skip = hy_skip[l].reshape(1, -1)
        hconv_b = hy_conv_b[l].reshape(1, -1)
        z, x0 = _hy_prep(u, hy_conv[l], hconv_b, 0, lat_rows, n, lat_rows, n_ctx)
        hfilt, hnorm = _hy_filter(n, feat_lat, *filt_args)
        filt_full = hfilt.reshape(2 * n, BRANCH_W)
        n1 = 2 * n1h
        cols = LANE * BRANCH_W
        zr = z.reshape(batch * n1h, cols)
        az = _dft_stage_a(fz_tab, zr)
        ah = _dft_stage_a(ff_tab, filt_full.reshape(n1, cols))
        pz = _dft_stage_c(az.reshape(2, n1, LANE, BRANCH_W), ah.reshape(2, n1, LANE, BRANCH_W), g_tab, gt_tab)
        y_hy = _dft_stage_a_inv(fi_tab, pz.reshape(2 * n1, cols), zr, x0.reshape(batch * n1h, cols), hnorm,
                                skip).reshape(lat_rows, BRANCH_W)

        h_f, h_b = hlf.reshape(lat_rows, -1), hlb.reshape(lat_rows, -1)
        w_o = w_out[l].astype(BF16)
        gain_ffn = norm_ffn[l].reshape(1, d)
        rw = jnp.pad(router_w[l], ((0, 0), (0, LANE - N_EXPERTS))).astype(BF16)
        rb = jnp.concatenate([router_b[l], jnp.full((LANE - N_EXPERTS,), -jnp.inf, F32)]).reshape(1, LANE)

        if not last:
            cseg = [(lat_rows, n_ctx)]
            yc_mla = _attention(mq, mk, mv, 1, lat_rows, n_ctx, cseg, batch, MLA_QK_PAD)
            yc_dif = _attention(dq, dk, dv, 2, lat_rows, n_ctx, cseg, batch, 2 * DIF_QK, lam_p, dgain, lam_init)
            zc, x0c = _hy_prep(u, hy_conv[l], hconv_b, lat_rows, ctx_rows, n, lat_rows, n_ctx)
            hfc, hnc = _hy_filter(n_ctx, feat_ctx, *filt_args)
            yc_hy = _hy_small(zc.reshape(batch, n_ctx, BRANCH_W), x0c.reshape(batch, n_ctx, BRANCH_W), hfc, hnc,
                              skip).reshape(ctx_rows, BRANCH_W)
            cat = lambda a, b: jnp.concatenate([a, b], axis=0)
            y_hy, y_mla, y_dif = cat(y_hy, yc_hy), cat(y_mla, yc_mla), cat(y_dif, yc_dif)
            h_f, h_b = cat(h_f, hcf.reshape(ctx_rows, -1)), cat(h_b, hcb.reshape(ctx_rows, -1))

        merged = _merge(y_hy, y_mla, y_dif, h_f, h_b, u, gate, w_branch[l].astype(BF16), ml_norm[l].reshape(1, -1),
                        n_tok)
        xo, f_all, top_idx, top_gate = _out_proj(merged, w_o, x_all, g_a, gain_ffn, sc_f, sh_f, rw, rb,
                                                 n_tok, lat_rows, n)
        blk_e, slot_tok, dest = _route(top_idx)
        y_slots = _moe_experts(f_all.reshape(n_tok, ROW_TILES, LANE), blk_e, slot_tok, exp_w_gu[l].astype(BF16), exp_b_gu[l],
                               exp_w_down[l].astype(BF16), exp_b_down[l])
        fgain = final_norm.reshape(1, d)
        x_all = _moe_combine(y_slots, dest, top_gate, xo, g_f, fgain, last, n_tok, lat_rows, n)

    return x_all.reshape(batch, n, d)
```

```python
import functools
import math

import jax
import jax.numpy as jnp
from jax import lax
from jax.experimental import pallas as pl
from jax.experimental.pallas import tpu as pltpu

F32 = jnp.float32
BF16 = jnp.bfloat16
HIGHEST = lax.Precision.HIGHEST

D_MODEL = 2048
GRID_W = 64
BRANCH_W = 512
N_HEADS = 4
HEAD_W = 128
MLA_NOPE = 128
MLA_ROPE = 64
MLA_QK_PAD = 256
MLA_SCALE = (MLA_NOPE + MLA_ROPE) ** -0.5
LOG2E = math.log2(math.e)
DIF_QK = 64
ML_CHUNK = 64
N_EXPERTS = 32
TOP_K = 4
D_EXPERT = 1024
SWIGLU_LIMIT = 7.0
SWIGLU_ALPHA = 1.702
MOE_BLOCK = 512
ROPE_THETA = 10000.0
NORM_EPS = 1e-6
HY_BANDS = 16
HY_EMB = 1 + 2 * HY_BANDS
HY_HID = 64
LANE = 128
SUBLANE = 8
ROW_TILES = D_MODEL // LANE
SMALL_COLS = 6144
GATE_COLS = 4 * D_MODEL

CB_DK, CB_DV, CB_MK, CB_MV, CB_QA, CB_DQ, CB_MQ, CB_MO, CB_HX0, CB_HX1, CB_HV = range(11)
CB_CKV_256 = 22
CB_KPE_128 = 46


def _cparams(sem, vmem_mb=48):
    return pltpu.CompilerParams(dimension_semantics=sem, vmem_limit_bytes=vmem_mb * 1024 * 1024)


def _rms(x):
    return x * lax.rsqrt(jnp.mean(x * x, axis=-1, keepdims=True) + NORM_EPS)


def _sigmoid(x):
    return 1.0 / (1.0 + jnp.exp(-x))


def _silu(x):
    return x * _sigmoid(x)


def _log_sigmoid(x):
    return jnp.minimum(x, 0.0) - jnp.log(1.0 + jnp.exp(-jnp.abs(x)))


def _bdot(a, b):
    return jnp.dot(a.astype(BF16), b.astype(BF16), preferred_element_type=F32)


def _bdot_nt(a, b):
    return lax.dot_general(a.astype(BF16), b.astype(BF16), (((1,), (1,)), ((), ())),
                           preferred_element_type=F32)


def _hdot(a, b):
    return jnp.dot(a, b, preferred_element_type=F32, precision=HIGHEST)


def _mod_kernel(c_ref, w_ref, b_ref, o_ref):
    o_ref[0] = _bdot(_silu(c_ref[...]), w_ref[0]) + b_ref[0]


def _modulation(cvec, ada_w, ada_b):
    n_layers, d, six_d = ada_w.shape
    tn = 1024
    return pl.pallas_call(
        _mod_kernel,
        grid=(n_layers, six_d // tn),
        in_specs=[pl.BlockSpec((SUBLANE, d), lambda l, j: (0, 0)),
                  pl.BlockSpec((1, d, tn), lambda l, j: (l, 0, j)),
                  pl.BlockSpec((1, 1, tn), lambda l, j: (l, 0, j))],
        out_specs=pl.BlockSpec((1, SUBLANE, tn), lambda l, j: (l, 0, j)),
        out_shape=jax.ShapeDtypeStruct((n_layers, SUBLANE, six_d), F32),
        compiler_params=_cparams(("parallel", "parallel")),
        name="adaln_modulation",
    )(cvec, ada_w, ada_b.reshape(n_layers, 1, six_d))


def _group_of_tile(i, tm, lat_rows, seq_rows):
    return jnp.minimum(i // (seq_rows // tm), lat_rows // seq_rows)


def _norm_mod_kernel(x_ref, g_ref, sc_ref, sh_ref, o_ref):
    h = _rms(x_ref[...]) * g_ref[...]
    o_ref[...] = (h * (1.0 + sc_ref[0]) + sh_ref[0]).astype(BF16)


def _norm_mod(x_all, gain, sc, sh, lat_rows, seq_rows, tm=512):
    t_all, d = x_all.shape
    grp = functools.partial(_group_of_tile, tm=tm, lat_rows=lat_rows, seq_rows=seq_rows)
    return pl.pallas_call(
        _norm_mod_kernel,
        grid=(t_all // tm,),
        in_specs=[pl.BlockSpec((tm, d), lambda i: (i, 0)),
                  pl.BlockSpec((1, d), lambda i: (0, 0)),
                  pl.BlockSpec((1, 1, d), lambda i: (grp(i), 0, 0)),
                  pl.BlockSpec((1, 1, d), lambda i: (grp(i), 0, 0))],
        out_specs=pl.BlockSpec((tm, d), lambda i: (i, 0)),
        out_shape=jax.ShapeDtypeStruct((t_all, d), BF16),
        compiler_params=_cparams(("parallel",)),
        name="norm_mod",
    )(x_all, gain, sc, sh)


def _proj_kernel(h_ref, w_ref, o_ref):
    o_ref[...] = jnp.dot(h_ref[...], w_ref[...], preferred_element_type=F32).astype(o_ref.dtype)


def _in_proj(h, w, out_dtype, nrows, tm=512, tn=1024):
    d = h.shape[1]
    ncol = w.shape[1]
    return pl.pallas_call(
        _proj_kernel,
        grid=(ncol // tn, nrows // tm),
        in_specs=[pl.BlockSpec((tm, d), lambda j, i: (i, 0)),
                  pl.BlockSpec((d, tn), lambda j, i: (0, j))],
        out_specs=pl.BlockSpec((tm, tn), lambda j, i: (i, j)),
        out_shape=jax.ShapeDtypeStruct((nrows, ncol), out_dtype),
        compiler_params=_cparams(("parallel", "arbitrary")),
        name="in_proj",
    )(h, w)


def _rope(x, c, s):
    w = x.shape[-1]
    lane = lax.broadcasted_iota(jnp.int32, x.shape, 1)
    up = pltpu.roll(x, w - 16, 1)
    dn = pltpu.roll(x, 16, 1)
    return x * c + jnp.where((lane % 32) < 16, up, dn) * s


def _attn_prep_kernel(dk_ref, dv_ref, qa_ref, dq_ref, ckv_ref, kpe_ref, cd_ref, sd_ref, cm_ref, sm_ref,
                      wqb_ref, wkvb_ref, qn_ref, kvn_ref,
                      mq_ref, mk_ref, mv_ref, dqo_ref, dko_ref, dvo_ref):
    cd4 = jnp.concatenate([cd_ref[...]] * N_HEADS, axis=1)
    sd4 = jnp.concatenate([sd_ref[...]] * N_HEADS, axis=1)
    cm, sm = cm_ref[...], sm_ref[...]
    dqo_ref[...] = (_rope(dq_ref[...], cd4, sd4) * (DIF_QK ** -0.5 * LOG2E)).astype(BF16)
    dko_ref[...] = _rope(dk_ref[...], cd4, sd4).astype(BF16)
    dvo_ref[...] = dv_ref[...].astype(BF16)

    q = _bdot(_rms(qa_ref[...]) * qn_ref[...], wqb_ref[...])
    kv = _bdot(_rms(ckv_ref[...]) * kvn_ref[...], wkvb_ref[...])
    kpe = kpe_ref[...]
    lane = lax.broadcasted_iota(jnp.int32, kpe.shape, 1)
    kpe = _rope(jnp.where(lane < MLA_ROPE, kpe, 0.0), cm, sm).astype(BF16)
    for h in range(N_HEADS):
        o = h * MLA_QK_PAD
        mq_ref[:, o:o + MLA_NOPE] = (q[:, o:o + MLA_NOPE] * (MLA_SCALE * LOG2E)).astype(BF16)
        mq_ref[:, o + MLA_NOPE:o + MLA_QK_PAD] = (
            _rope(q[:, o + MLA_NOPE:o + MLA_QK_PAD], cm, sm) * (MLA_SCALE * LOG2E)).astype(BF16)
        mk_ref[:, o:o + MLA_NOPE] = kv[:, h * MLA_NOPE:(h + 1) * MLA_NOPE].astype(BF16)
        mk_ref[:, o + MLA_NOPE:o + MLA_QK_PAD] = kpe
    mv_ref[...] = kv[:, N_HEADS * MLA_NOPE:].astype(BF16)


def _attn_prep(u, tabs, wqb, wkvb, qn, kvn, tm=256):
    t_all = u.shape[0]
    cd, sd, cm, sm = tabs
    col = lambda w, cb: pl.BlockSpec((tm, w), lambda i, cb=cb: (i, cb))
    tab = pl.BlockSpec((tm, LANE), lambda i: (i, 0))
    full = lambda a: pl.BlockSpec(a.shape, lambda i: (0,) * a.ndim)
    out = lambda w: pl.BlockSpec((tm, w), lambda i: (i, 0))
    shp = lambda w: jax.ShapeDtypeStruct((t_all, w), BF16)
    return pl.pallas_call(
        _attn_prep_kernel,
        grid=(t_all // tm,),
        in_specs=[col(512, CB_DK), col(512, CB_DV), col(512, CB_QA), col(512, CB_DQ),
                  col(256, CB_CKV_256), col(128, CB_KPE_128), tab, tab, tab, tab,
                  full(wqb), full(wkvb), full(qn), full(kvn)],
        out_specs=[out(1024), out(1024), out(512), out(512), out(512), out(512)],
        out_shape=[shp(1024), shp(1024), shp(512), shp(512), shp(512), shp(512)],
        compiler_params=_cparams(("parallel",)),
        name="attn_prep",
    )(u, u, u, u, u, u, cd, sd, cm, sm, wqb, wkvb, qn, kvn)


def _conv3(x, prev_blk, next_blk, w, first, last):
    tm = x.shape[0]
    row = lax.broadcasted_iota(jnp.int32, x.shape, 0)
    prev_row = jnp.where(first, 0.0, prev_blk[SUBLANE - 1:SUBLANE, :])
    next_row = jnp.where(last, 0.0, next_blk[0:1, :])
    xm = jnp.where(row == 0, prev_row, pltpu.roll(x, 1, 0))
    xp = jnp.where(row == tm - 1, next_row, pltpu.roll(x, tm - 1, 0))
    return xm * w[0:1, :] + x * w[1:2, :] + xp * w[2:3, :]


def _seq_edges(tile, tm, seq_rows, lat_rows, ctx_len):
    r = tile * tm
    in_lat = r < lat_rows
    pos = jnp.where(in_lat, r % seq_rows, (r - lat_rows) % ctx_len)
    length = jnp.where(in_lat, seq_rows, ctx_len)
    return pos == 0, pos + tm == length


def _ml_prep_kernel(seq_rows, lat_rows, ctx_len,
                    q_ref, qp_ref, qn_ref, k_ref, kp_ref, kn_ref, gblk_ref, wq_ref, wk_ref, gb_ref,
                    qo_ref, ko_ref, go_ref):
    tm = q_ref.shape[0]
    first, last = _seq_edges(pl.program_id(0), tm, seq_rows, lat_rows, ctx_len)
    qo_ref[...] = _silu(_conv3(q_ref[...], qp_ref[...], qn_ref[...], wq_ref[...], first, last))
    ko_ref[...] = _silu(_conv3(k_ref[...], kp_ref[...], kn_ref[...], wk_ref[...], first, last)) * HEAD_W ** -0.5
    go_ref[...] = gblk_ref[:, MLA_ROPE:MLA_ROPE + 16] + gb_ref[...]


def _halo_specs(tm, w, cb, t_all, r0=0):
    nb8 = t_all // SUBLANE
    r8 = tm // SUBLANE
    return [pl.BlockSpec((tm, w), lambda i: (i + r0, cb)),
            pl.BlockSpec((SUBLANE, w), lambda i: (jnp.maximum((i + r0) * r8 - 1, 0), cb)),
            pl.BlockSpec((SUBLANE, w), lambda i: (jnp.minimum((i + r0 + 1) * r8, nb8 - 1), cb))]


def _ml_prep(u, wq, wk, gb, seq_rows, lat_rows, ctx_len, tm=256):
    t_all = u.shape[0]
    full = lambda a: pl.BlockSpec(a.shape, lambda i: (0,) * a.ndim)
    return pl.pallas_call(
        functools.partial(_ml_prep_kernel, seq_rows, lat_rows, ctx_len),
        grid=(t_all // tm,),
        in_specs=_halo_specs(tm, 512, CB_MQ, t_all) + _halo_specs(tm, 512, CB_MK, t_all)
        + [pl.BlockSpec((tm, LANE), lambda i: (i, CB_KPE_128)), full(wq), full(wk), full(gb)],
        out_specs=[pl.BlockSpec((tm, 512), lambda i: (i, 0)), pl.BlockSpec((tm, 512), lambda i: (i, 0)),
                   pl.BlockSpec((tm, 16), lambda i: (i, 0))],
        out_shape=[jax.ShapeDtypeStruct((t_all, 512), F32), jax.ShapeDtypeStruct((t_all, 512), F32),
                   jax.ShapeDtypeStruct((t_all, 16), F32)],
        compiler_params=_cparams(("parallel",)),
        name="mlstm_prep",
    )(u, u, u, u, u, u, u, wq, wk, gb)


def _hy_prep_kernel(r0, seq_rows, lat_rows, ctx_len,
                    a_ref, ap_ref, an_ref, b_ref, bp_ref, bn_ref, c_ref, cp_ref, cn_ref, w_ref, bias_ref,
                    z_ref, x0_ref):
    tm = a_ref.shape[0]
    first, last = _seq_edges(pl.program_id(0) + r0, tm, seq_rows, lat_rows, ctx_len)
    w, bias = w_ref[...], bias_ref[...]
    outs = []
    for s, (m, p, n) in enumerate(((a_ref, ap_ref, an_ref), (b_ref, bp_ref, bn_ref), (c_ref, cp_ref, cn_ref))):
        sl = slice(s * BRANCH_W, (s + 1) * BRANCH_W)
        outs.append(_conv3(m[...], p[...], n[...], w[:, sl], first, last) + bias[:, sl])
    x0_ref[...] = outs[0]
    z_ref[...] = outs[2] * outs[1]


def _hy_prep(u, w, bias, row0, nrows, seq_rows, lat_rows, ctx_len, tm=256):
    t_all = u.shape[0]
    r0 = row0 // tm
    full = lambda a: pl.BlockSpec(a.shape, lambda i: (0,) * a.ndim)
    halo = lambda cb: _halo_specs(tm, 512, cb, t_all, r0)
    return pl.pallas_call(
        functools.partial(_hy_prep_kernel, r0, seq_rows, lat_rows, ctx_len),
        grid=(nrows // tm,),
        in_specs=halo(CB_HX0) + halo(CB_HX1) + halo(CB_HV) + [full(w), full(bias)],
        out_specs=[pl.BlockSpec((tm, 512), lambda i: (i, 0)), pl.BlockSpec((tm, 512), lambda i: (i, 0))],
        out_shape=[jax.ShapeDtypeStruct((nrows, 512), F32), jax.ShapeDtypeStruct((nrows, 512), F32)],
        compiler_params=_cparams(("parallel",)),
        name="hyena_prep",
    )(u, u, u, u, u, u, u, u, u, w, bias)


def _lane_fold(x, op):
    parts = [x[:, j * LANE:(j + 1) * LANE] for j in range(x.shape[1] // LANE)]
    while len(parts) > 1:
        parts = [op(parts[i], parts[i + 1]) for i in range(0, len(parts) - 1, 2)] + (
            [parts[-1]] if len(parts) % 2 else [])
    return parts[0]


def _attn_kernel(n_maps, seg_lens, tk, lam_init, *refs):
    q_ref = refs[0]
    kv_refs = refs[1:1 + 2 * len(seg_lens)]
    pos = 1 + 2 * len(seg_lens)
    if n_maps == 2:
        lam_ref, gain_ref = refs[pos], refs[pos + 1]
        pos += 2
    o_ref = refs[pos]

    q = q_ref[...]
    tq = q.shape[0]
    if n_maps == 1:
        qs = [q]
    else:
        lane = lax.broadcasted_iota(jnp.int32, q.shape, 1)
        zero = jnp.zeros_like(q)
        qs = [jnp.where(lane < DIF_QK, q, zero), jnp.where(lane >= DIF_QK, q, zero)]

    def step(k, v, carry):
        new = []
        for qm, (m, l, acc) in zip(qs, carry):
            s = lax.dot_general(qm, k, (((1,), (1,)), ((), ())), preferred_element_type=F32)
            m_new = jnp.maximum(m, jnp.max(_lane_fold(s, jnp.maximum), axis=-1, keepdims=True))
            p = jnp.exp2(s - m_new)
            alpha = jnp.exp2(m - m_new)
            l_new = alpha * l + jnp.sum(_lane_fold(p, jnp.add), axis=-1, keepdims=True)
            acc_new = alpha * acc + jnp.dot(p.astype(BF16), v, preferred_element_type=F32)
            new.append((m_new, l_new, acc_new))
        return tuple(new)

    carry = tuple((jnp.full((tq, 1), -jnp.inf, F32), jnp.zeros((tq, 1), F32), jnp.zeros((tq, HEAD_W), F32))
                  for _ in range(n_maps))
    for si, slen in enumerate(seg_lens):
        k_ref, v_ref = kv_refs[2 * si], kv_refs[2 * si + 1]
        chunk = min(tk, slen)

        def body(c, carry, k_ref=k_ref, v_ref=v_ref, chunk=chunk):
            start = pl.multiple_of(c * chunk, chunk)
            return step(k_ref[pl.ds(start, chunk), :], v_ref[pl.ds(start, chunk), :], carry)

        trips = slen // chunk
        carry = lax.fori_loop(0, trips, body, carry, unroll=math.gcd(trips, 8))

    outs = [acc / l for (_, l, acc) in carry]
    if n_maps == 1:
        o_ref[...] = outs[0].astype(o_ref.dtype)
    else:
        lp = lam_ref[...]
        lam = (jnp.exp(jnp.sum(lp[0:1] * lp[1:2], axis=-1, keepdims=True))
               - jnp.exp(jnp.sum(lp[2:3] * lp[3:4], axis=-1, keepdims=True)) + lam_init)
        o = outs[0] - lam * outs[1]
        o_ref[...] = (_rms(o) * gain_ref[...] * (1.0 - lam_init)).astype(o_ref.dtype)


def _attention(q, k, v, n_maps, q_row0, q_len, segs, batch, qk_w, lam=None, gain=None, lam_init=0.0,
               tq=1024, tk=512):
    tq = min(tq, q_len)
    nq = q_len // tq
    qb0 = q_row0 // tq
    in_specs = [pl.BlockSpec((tq, qk_w), lambda b, h, i: (qb0 + b * nq + i, h))]
    args = [q]
    for (r0, slen) in segs:
        kb0 = r0 // slen
        in_specs.append(pl.BlockSpec((slen, qk_w), lambda b, h, i, kb0=kb0: (kb0 + b, h)))
        in_specs.append(pl.BlockSpec((slen, HEAD_W), lambda b, h, i, kb0=kb0: (kb0 + b, h)))
        args += [k, v]
    if n_maps == 2:
        in_specs += [pl.BlockSpec(lam.shape, lambda b, h, i: (0, 0)),
                     pl.BlockSpec(gain.shape, lambda b, h, i: (0, 0))]
        args += [lam, gain]
    return pl.pallas_call(
        functools.partial(_attn_kernel, n_maps, tuple(s for _, s in segs), tk, lam_init),
        grid=(batch, N_HEADS, nq),
        in_specs=in_specs,
        out_specs=pl.BlockSpec((tq, HEAD_W), lambda b, h, i: (b * nq + i, h)),
        out_shape=jax.ShapeDtypeStruct((batch * q_len, N_HEADS * HEAD_W), BF16),
        compiler_params=_cparams(("parallel", "parallel", "arbitrary")),
        name="attention_maps%d" % n_maps,
    )(*args)


ML_STEP = 256


def _mlstm_chunk(rev, o, q_ref, k_ref, v_ref, g_ref, gt_ref, state, d, b, h_ref):
    ti = lax.broadcasted_iota(jnp.int32, (ML_CHUNK, ML_CHUNK), 0)
    si = lax.broadcasted_iota(jnp.int32, (ML_CHUNK, ML_CHUNK), 1)
    tri = (si >= ti) if rev else (si <= ti)
    tri_f = tri.astype(F32)
    tri_t = ((ti >= si) if rev else (ti <= si)).astype(F32)
    gcol = g_ref[o:o + ML_CHUNK, :]
    grow = gt_ref[0, :, o:o + ML_CHUNK]
    li_col = gcol[:, d * 8:d * 8 + 4]
    lf_col = _log_sigmoid(gcol[:, d * 8 + 4:d * 8 + 8])
    li_row = grow[d * 8:d * 8 + 4, :]
    lf_row = _log_sigmoid(grow[d * 8 + 4:d * 8 + 8, :])
    bc_col = _hdot(tri_f, lf_col)
    bc_row = _hdot(lf_row, tri_t)
    f_tot = jnp.sum(lf_col, axis=0, keepdims=True)
    for h in range(N_HEADS):
        hs = slice(h * HEAD_W, (h + 1) * HEAD_W)
        qc = q_ref[o:o + ML_CHUNK, hs]
        kc = k_ref[o:o + ML_CHUNK, hs]
        vc = v_ref[o:o + ML_CHUNK, hs]
        bcc, bcr = bc_col[:, h:h + 1], bc_row[h:h + 1, :]
        icol, irow = li_col[:, h:h + 1], li_row[h:h + 1, :]
        fh = f_tot[:, h:h + 1]
        c_mat, n_vec, m = state[d, b, h]
        g_end_r = fh - bcr + irow
        g_end_c = fh - bcc + icol
        m_new = jnp.maximum(fh + m, jnp.max(g_end_r, axis=-1, keepdims=True))
        decay = jnp.exp(fh + m - m_new)
        wk = jnp.exp(g_end_c - m_new)
        kw = kc * wk
        upd = lax.dot_general(kw.astype(BF16), vc.astype(BF16), (((0,), (0,)), ((), ())),
                              preferred_element_type=F32)
        if h_ref is not None:
            inter = bcc + m
            log_d = jnp.where(tri, bcc - bcr + irow, -jnp.inf)
            m_t = jnp.maximum(inter, jnp.max(log_d, axis=-1, keepdims=True))
            s = _bdot_nt(qc, kc) * jnp.exp(log_d - m_t)
            w_inter = jnp.exp(inter - m_t)
            num = _bdot(s, vc) + w_inter * _bdot(qc, c_mat)
            den = jnp.sum(s, axis=-1, keepdims=True) + w_inter * jnp.sum(qc * n_vec, axis=-1, keepdims=True)
            h_ref[b, o:o + ML_CHUNK, hs] = num / jnp.maximum(jnp.abs(den), jnp.exp(-m_t))
        state[d, b, h] = (decay * c_mat + upd, decay * n_vec + jnp.sum(kw, axis=0, keepdims=True), m_new)


def _mlstm_kernel(batch, with_out, n_steps, *refs):
    n_in = 2 * batch * 5
    chain_refs = refs[:n_in]
    c0_ref, n0_ref, m0_ref = refs[n_in:n_in + 3]
    pos = n_in + 3
    h_refs = (None, None)
    if with_out:
        h_refs = (refs[pos], refs[pos + 1])
        pos += 2
    cf_ref, nf_ref, mf_ref, c_scr, n_scr, m_scr = refs[pos:pos + 6]
    j = pl.program_id(0)

    @pl.when(j == 0)
    def _():
        c_scr[...] = c0_ref[...]
        n_scr[...] = n0_ref[...]
        m_scr[...] = m0_ref[...]

    keys = [(d, b, h) for d in range(2) for b in range(batch) for h in range(N_HEADS)]
    state = {key: (c_scr[key], n_scr[key], m_scr[key][:, 0:1]) for key in keys}
    n_chunks = ML_STEP // ML_CHUNK
    for ci in range(n_chunks):
        for d in range(2):
            c = (n_chunks - 1 - ci) if d == 1 else ci
            for b in range(batch):
                q_ref, k_ref, v_ref, g_ref, gt_ref = chain_refs[(d * batch + b) * 5:(d * batch + b + 1) * 5]
                _mlstm_chunk(d == 1, c * ML_CHUNK, q_ref, k_ref, v_ref, g_ref, gt_ref, state, d, b, h_refs[d])
    for key in keys:
        c_mat, n_vec, m = state[key]
        c_scr[key] = c_mat
        n_scr[key] = n_vec
        m_scr[key] = jnp.broadcast_to(m, (1, HEAD_W))

    @pl.when(j == n_steps - 1)
    def _():
        cf_ref[...] = c_scr[...]
        nf_ref[...] = n_scr[...]
        mf_ref[...] = m_scr[...]


def _mlstm(q, k, u, g, gt, init, with_out, row0, seq_len, batch):
    n_steps = seq_len // ML_STEP
    b0 = row0 // ML_STEP
    in_specs, args = [], []
    for d in range(2):
        for b in range(batch):
            step = (lambda j: n_steps - 1 - j) if d == 1 else (lambda j: j)
            blk = lambda j, b=b, step=step: b0 + b * n_steps + step(j)
            in_specs += [pl.BlockSpec((ML_STEP, 512), lambda j, blk=blk: (blk(j), 0)),
                         pl.BlockSpec((ML_STEP, 512), lambda j, blk=blk: (blk(j), 0)),
                         pl.BlockSpec((ML_STEP, 512), lambda j, blk=blk: (blk(j), CB_MV)),
                         pl.BlockSpec((ML_STEP, 16), lambda j, blk=blk: (blk(j), 0)),
                         pl.BlockSpec((1, 16, ML_STEP), lambda j, b=b, step=step: (b, 0, step(j)))]
            args += [q, k, u, g, gt]
    full = lambda shape: pl.BlockSpec(shape, lambda j: (0,) * len(shape))
    st_dims = [(2, batch, N_HEADS, HEAD_W, HEAD_W), (2, batch, N_HEADS, 1, HEAD_W), (2, batch, N_HEADS, 1, HEAD_W)]
    st_specs = [full(s) for s in st_dims]
    st_shapes = [jax.ShapeDtypeStruct(s, F32) for s in st_dims]
    out_specs, out_shapes = list(st_specs), list(st_shapes)
    if with_out:
        out_specs = [pl.BlockSpec((batch, ML_STEP, 512), lambda j: (0, j, 0)),
                     pl.BlockSpec((batch, ML_STEP, 512), lambda j: (0, n_steps - 1 - j, 0))] + out_specs
        out_shapes = [jax.ShapeDtypeStruct((batch, seq_len, 512), F32)] * 2 + out_shapes
    res = pl.pallas_call(
        functools.partial(_mlstm_kernel, batch, with_out, n_steps),
        grid=(n_steps,),
        in_specs=in_specs + st_specs,
        out_specs=out_specs,
        out_shape=out_shapes,
        scratch_shapes=[pltpu.VMEM(s, F32) for s in st_dims],
        compiler_params=_cparams(("arbitrary",)),
        name="mlstm",
    )(*args, *init)
    if with_out:
        return res[0], res[1], tuple(res[2:])
    return None, None, tuple(res)


def _hy_filter_kernel(n, feat_ref, w1_ref, b1_ref, w2_ref, b2_ref, w3_ref, freq_ref, decay_ref,
                      h_ref, norm_ref):
    i = pl.program_id(0)
    tm = h_ref.shape[1]
    freq = freq_ref[...]
    hid = jnp.sin(freq[0:1, :] * (_hdot(feat_ref[0], w1_ref[...]) + b1_ref[...]))
    hid = jnp.sin(freq[1:2, :] * (_hdot(hid, w2_ref[...]) + b2_ref[...]))
    hw = _hdot(hid, w3_ref[...])
    row = lax.broadcasted_iota(jnp.int32, (tm, BRANCH_W), 0) + i * tm
    decay = jnp.abs(decay_ref[...])
    h_fwd = hw[:tm, :BRANCH_W] * jnp.exp(-(row.astype(F32) / n) * decay[:, :BRANCH_W])
    h_bwd = hw[tm:, BRANCH_W:] * jnp.exp(-((n - row).astype(F32) / n) * decay[:, BRANCH_W:])
    h_bwd = jnp.where(row == 0, 0.0, h_bwd)
    h_ref[0] = h_fwd
    h_ref[1] = h_bwd

    @pl.when(i == 0)
    def _():
        norm_ref[...] = jnp.zeros_like(norm_ref)

    norm_ref[...] += jnp.concatenate([jnp.sum(jnp.abs(h_fwd), axis=0, keepdims=True),
                                      jnp.sum(jnp.abs(h_bwd), axis=0, keepdims=True)], axis=1)


def _hy_filter(n, feat2, w1p, b1, w2, b2, w3, freq, decay_flat):
    tm = feat2.shape[1] // 2
    full = lambda a: pl.BlockSpec(a.shape, lambda i: (0,) * a.ndim)
    return pl.pallas_call(
        functools.partial(_hy_filter_kernel, n),
        grid=(n // tm,),
        in_specs=[pl.BlockSpec((1, 2 * tm, LANE), lambda i: (i, 0, 0)), full(w1p), full(b1), full(w2), full(b2),
                  full(w3), full(freq), full(decay_flat)],
        out_specs=[pl.BlockSpec((2, tm, BRANCH_W), lambda i: (0, i, 0)),
                   pl.BlockSpec((1, 2 * BRANCH_W), lambda i: (0, 0))],
        out_shape=[jax.ShapeDtypeStruct((2, n, BRANCH_W), F32), jax.ShapeDtypeStruct((1, 2 * BRANCH_W), F32)],
        compiler_params=_cparams(("arbitrary",)),
        name="hyena_filter",
    )(feat2, w1p, b1, w2, b2, w3, freq, decay_flat)


def _split_bf16(x):
    hi = x.astype(BF16)
    return hi, (x - hi.astype(F32)).astype(BF16)


def _dot3(a, b):
    d = lambda p, q: jnp.dot(p, q, preferred_element_type=F32)
    return d(a[0], b[0]) + (d(a[0], b[1]) + d(a[1], b[0]))


def _dft_a_kernel(f_ref, x_ref, o_ref):
    o_ref[...] = _dot3((f_ref[0], f_ref[1]), _split_bf16(x_ref[...]))


def _dft_stage_a(table, x, tc=2048):
    k, cols = x.shape
    rows = table.shape[1]
    return pl.pallas_call(
        _dft_a_kernel,
        grid=(cols // tc,),
        in_specs=[pl.BlockSpec(table.shape, lambda j: (0, 0, 0)),
                  pl.BlockSpec((k, tc), lambda j: (0, j))],
        out_specs=pl.BlockSpec((rows, tc), lambda j: (0, j)),
        out_shape=jax.ShapeDtypeStruct((rows, cols), F32),
        compiler_params=_cparams(("parallel",)),
        name="hyena_dft_a",
    )(table, x)


def _dft_c_kernel(az_ref, ah_ref, g_ref, gt_ref, o_ref):
    x = jnp.concatenate([az_ref[:, 0].reshape(2 * LANE, BRANCH_W), ah_ref[:, 0].reshape(2 * LANE, BRANCH_W)],
                        axis=1)
    s = _dot3((g_ref[0, 0], g_ref[1, 0]), _split_bf16(x))
    zr, zi = s[:LANE, :BRANCH_W], s[LANE:, :BRANCH_W]
    hr, hi = s[:LANE, BRANCH_W:], s[LANE:, BRANCH_W:]
    prod = jnp.concatenate([zr * hr - zi * hi, zr * hi + zi * hr], axis=0)
    p = _dot3((gt_ref[0, 0], gt_ref[1, 0]), _split_bf16(prod))
    o_ref[:, 0] = p.reshape(2, LANE, BRANCH_W)


def _dft_stage_c(az, ah, g, gt):
    _, n1, _, c = az.shape
    slab = pl.BlockSpec((2, 1, LANE, c), lambda k: (0, k, 0, 0))
    tab = pl.BlockSpec((2, 1, 2 * LANE, 2 * LANE), lambda k: (0, k, 0, 0))
    return pl.pallas_call(
        _dft_c_kernel,
        grid=(n1,),
        in_specs=[slab, slab, tab, tab],
        out_specs=slab,
        out_shape=jax.ShapeDtypeStruct(az.shape, F32),
        compiler_params=_cparams(("parallel",)),
        name="hyena_dft_c",
    )(az, ah, g, gt)


def _hy_epilogue(y, z, x0, norm, skip):
    reps = y.shape[1] // BRANCH_W
    nsum = norm[:, :BRANCH_W] + norm[:, BRANCH_W:]
    inv = jnp.concatenate([1.0 / nsum] * reps, axis=1)
    sk = jnp.concatenate([skip] * reps, axis=1)
    return (x0 * (y * inv + z * sk)).astype(BF16)


def _dft_a_inv_kernel(f_ref, p_ref, z_ref, x0_ref, norm_ref, skip_ref, o_ref):
    y = _dot3((f_ref[0], f_ref[1]), _split_bf16(p_ref[...]))
    o_ref[...] = _hy_epilogue(y, z_ref[...], x0_ref[...], norm_ref[...], skip_ref[...])


def _dft_stage_a_inv(table, p, z, x0, norm, skip, tc=2048):
    rows, cols = p.shape
    zr = z.shape[0]
    col = lambda r: pl.BlockSpec((r, tc), lambda j: (0, j))
    return pl.pallas_call(
        _dft_a_inv_kernel,
        grid=(cols // tc,),
        in_specs=[pl.BlockSpec(table.shape, lambda j: (0, 0, 0)), col(rows), col(zr), col(zr),
                  pl.BlockSpec(norm.shape, lambda j: (0, 0)), pl.BlockSpec(skip.shape, lambda j: (0, 0))],
        out_specs=col(zr),
        out_shape=jax.ShapeDtypeStruct((zr, cols), BF16),
        compiler_params=_cparams(("parallel",)),
        name="hyena_dft_a_inv",
    )(table, p, z, x0, norm, skip)


def _dft_tables(n):
    big_l = 2 * n
    n1 = big_l // LANE
    n1h = n1 // 2
    two_pi = 2.0 * math.pi
    k1 = jnp.arange(n1, dtype=jnp.int32)
    th = two_pi * ((k1[:, None] * k1[None, :]) % n1).astype(F32) / n1
    c_full, s_full = jnp.cos(th), jnp.sin(th)
    ch, sh = c_full[:, :n1h], s_full[:, :n1h]
    fz = jnp.concatenate([jnp.concatenate([ch, sh], axis=1), jnp.concatenate([-sh, ch], axis=1)], axis=0)
    ff = jnp.concatenate([c_full, -s_full], axis=0)
    ct, st = ch.T / big_l, sh.T / big_l
    fi = jnp.concatenate([jnp.concatenate([ct, -st], axis=1), jnp.concatenate([st, ct], axis=1)], axis=0)
    s2 = jnp.arange(LANE, dtype=jnp.int32)
    k = k1[:, None, None] + n1 * s2[None, :, None]
    ph = two_pi * ((k * s2[None, None, :]) % big_l).astype(F32) / big_l
    gr, gi = jnp.cos(ph), -jnp.sin(ph)
    g = jnp.concatenate([jnp.concatenate([gr, -gi], axis=2), jnp.concatenate([gi, gr], axis=2)], axis=1)
    pair = lambda t: jnp.stack(_split_bf16(t))
    return pair(fz), pair(ff), pair(fi), pair(g), pair(jnp.swapaxes(g, 1, 2))


def _hy_small_kernel(batch, n, z_ref, x0_ref, h_ref, f_ref, fi_ref, norm_ref, skip_ref, o_ref):
    big_l = 2 * n
    f = f_ref[...]
    sz = _hdot(f[:, :n], jnp.concatenate([z_ref[b] for b in range(batch)], axis=1))
    sh = _hdot(f, jnp.concatenate([h_ref[0], h_ref[1]], axis=0))
    hr, hi = sh[:big_l], sh[big_l:]
    prods = []
    for b in range(batch):
        zr = sz[:big_l, b * BRANCH_W:(b + 1) * BRANCH_W]
        zi = sz[big_l:, b * BRANCH_W:(b + 1) * BRANCH_W]
        prods.append(jnp.concatenate([zr * hr - zi * hi, zr * hi + zi * hr], axis=0))
    y = _hdot(fi_ref[...], jnp.concatenate(prods, axis=1))
    for b in range(batch):
        o_ref[b] = _hy_epilogue(y[:, b * BRANCH_W:(b + 1) * BRANCH_W], z_ref[b], x0_ref[b],
                                norm_ref[...], skip_ref[...])


def _hy_small(z, x0, hfilt, norm, skip):
    batch, n, c = z.shape
    big_l = 2 * n
    kk = jnp.arange(big_l, dtype=jnp.int32)
    th = 2.0 * math.pi * ((kk[:, None] * kk[None, :]) % big_l).astype(F32) / big_l
    f = jnp.concatenate([jnp.cos(th), -jnp.sin(th)], axis=0)
    fi = jnp.concatenate([jnp.cos(th[:, :n]).T, -jnp.sin(th[:, :n]).T], axis=1) / big_l
    full = lambda a: pl.BlockSpec(a.shape, lambda i: (0,) * a.ndim)
    return pl.pallas_call(
        functools.partial(_hy_small_kernel, batch, n),
        grid=(1,),
        in_specs=[full(z), full(x0), full(hfilt), full(f), full(fi), full(norm), full(skip)],
        out_specs=full(z),
        out_shape=jax.ShapeDtypeStruct(z.shape, BF16),
        compiler_params=_cparams(("arbitrary",)),
        name="hyena_small",
    )(z, x0, hfilt, f, fi, norm, skip)


def _hy_features(n):
    tm = min(n, 512)
    bands = jnp.linspace(1e-4, HY_BANDS - 1, HY_BANDS, dtype=F32)

    def feats(pos):
        t = pos.astype(F32) / n
        ang = 2.0 * math.pi * t[:, None] * bands[None, :]
        feat = jnp.concatenate([t[:, None], jnp.cos(ang), jnp.sin(ang)], axis=-1)
        return jnp.pad(feat, ((0, 0), (0, LANE - HY_EMB))).reshape(n // tm, tm, LANE)

    r = jnp.arange(n, dtype=jnp.int32)
    return jnp.concatenate([feats(r), feats(n - r)], axis=1)


def _merge_kernel(hy_ref, mla_ref, dif_ref, hf_ref, hb_ref, mo_ref, gate_ref, wb_ref, mln_ref, o_ref):
    hsum = hf_ref[...] + hb_ref[...]
    mln = mln_ref[...]
    y_ml = jnp.concatenate(
        [_rms(hsum[:, h * HEAD_W:(h + 1) * HEAD_W]) * mln[:, h * HEAD_W:(h + 1) * HEAD_W]
         for h in range(N_HEADS)], axis=1)
    y_ml = _sigmoid(mo_ref[...]) * y_ml
    ys = (hy_ref[...], mla_ref[...], dif_ref[...], y_ml.astype(BF16))
    acc = None
    for i in range(4):
        gi = _sigmoid(gate_ref[:, i * D_MODEL:(i + 1) * D_MODEL].astype(F32))
        term = gi * jnp.dot(ys[i], wb_ref[i], preferred_element_type=F32)
        acc = term if acc is None else acc + term
    o_ref[...] = acc.astype(BF16)


def _merge(y_hy, y_mla, y_dif, h_f, h_b, u, gate, wb, mln, nrows, tm=256):
    row = pl.BlockSpec((tm, 512), lambda i: (i, 0))
    full = lambda a: pl.BlockSpec(a.shape, lambda i: (0,) * a.ndim)
    return pl.pallas_call(
        _merge_kernel,
        grid=(nrows // tm,),
        in_specs=[row, row, row, row, row,
                  pl.BlockSpec((tm, 512), lambda i: (i, CB_MO)),
                  pl.BlockSpec((tm, GATE_COLS), lambda i: (i, 0)),
                  full(wb), full(mln)],
        out_specs=pl.BlockSpec((tm, D_MODEL), lambda i: (i, 0)),
        out_shape=jax.ShapeDtypeStruct((nrows, D_MODEL), BF16),
        compiler_params=_cparams(("parallel",), 56),
        name="branch_merge",
    )(y_hy, y_mla, y_dif, h_f, h_b, u, gate, wb, mln)


def _out_kernel(m_ref, w_ref, x_ref, ga_ref, gain_ref, sc_ref, sh_ref, rw_ref, rb_ref,
                xo_ref, f_ref, idx_ref, gt_ref):
    y = jnp.dot(m_ref[...], w_ref[...], preferred_element_type=F32)
    x = x_ref[...] + ga_ref[0] * y
    xo_ref[...] = x
    f = _rms(x) * gain_ref[...] * (1.0 + sc_ref[0]) + sh_ref[0]
    tm = f.shape[0]
    for j in range(ROW_TILES):
        f_ref[pl.ds(j, tm, stride=ROW_TILES), :] = f[:, j * LANE:(j + 1) * LANE]
    logits = _bdot(f, rw_ref[...]) + rb_ref[...]
    lane = lax.broadcasted_iota(jnp.int32, logits.shape, 1)
    idx_out = jnp.zeros(logits.shape, jnp.int32)
    val_out = jnp.full(logits.shape, -jnp.inf, F32)
    work = logits
    for k in range(TOP_K):
        mx = jnp.max(work, axis=-1, keepdims=True)
        am = jnp.min(jnp.where(work == mx, lane, LANE), axis=-1, keepdims=True)
        idx_out = jnp.where(lane == k, am, idx_out)
        val_out = jnp.where(lane == k, mx, val_out)
        work = jnp.where(lane == am, -jnp.inf, work)
    e = jnp.exp(val_out - jnp.max(val_out, axis=-1, keepdims=True))
    gt_ref[...] = e / jnp.sum(e, axis=-1, keepdims=True)
    idx_ref[...] = idx_out


def _out_proj(merged, w_out, x_all, ga, gain, sc, sh, rw, rb, nrows, lat_rows, seq_rows, tm=256):
    grp = functools.partial(_group_of_tile, tm=tm, lat_rows=lat_rows, seq_rows=seq_rows)
    row = lambda w: pl.BlockSpec((tm, w), lambda i: (i, 0))
    full = lambda a: pl.BlockSpec(a.shape, lambda i: (0,) * a.ndim)
    mod = pl.BlockSpec((1, 1, D_MODEL), lambda i: (grp(i), 0, 0))
    return pl.pallas_call(
        _out_kernel,
        grid=(nrows // tm,),
        in_specs=[row(D_MODEL), full(w_out), row(D_MODEL), mod, full(gain), mod, mod, full(rw), full(rb)],
        out_specs=[row(D_MODEL), pl.BlockSpec((tm * ROW_TILES, LANE), lambda i: (i, 0)), row(LANE), row(LANE)],
        out_shape=[jax.ShapeDtypeStruct((nrows, D_MODEL), F32), jax.ShapeDtypeStruct((nrows * ROW_TILES, LANE), F32),
                   jax.ShapeDtypeStruct((nrows, LANE), jnp.int32), jax.ShapeDtypeStruct((nrows, LANE), F32)],
        compiler_params=_cparams(("parallel",)),
        name="out_proj_router",
    )(merged, w_out, x_all, ga, gain, sc, sh, rw, rb)


def _moe_kernel(blk_e_ref, tok_ref, nxt_ref, f_hbm, wgu_ref, bgu_ref, wd_ref, bd_ref, o_ref, xbuf, sem):
    i = pl.program_id(0)
    last = pl.num_programs(0) - 1
    cur = i % 2

    buf_rows = MOE_BLOCK * ROW_TILES

    def start_rows(idx_ref, s):
        for r in range(MOE_BLOCK):
            pltpu.make_async_copy(f_hbm.at[idx_ref[0, 0, r]],
                                  xbuf.at[pl.ds(s * buf_rows + r * ROW_TILES, ROW_TILES), :], sem.at[s]).start()

    def wait_rows(s):
        whole = xbuf.at[pl.ds(s * buf_rows, buf_rows), :]
        pltpu.make_async_copy(whole, whole, sem.at[s]).wait()

    @pl.when(i == 0)
    def _():
        start_rows(tok_ref, 0)

    wait_rows(cur)
    start_rows(nxt_ref, 1 - cur)
    base = cur * buf_rows
    x = jnp.concatenate([xbuf[pl.ds(base + j, MOE_BLOCK, stride=ROW_TILES), :].astype(BF16)
                         for j in range(ROW_TILES)], axis=1)
    gu = jnp.dot(x, wgu_ref[0], preferred_element_type=F32) + bgu_ref[0]
    g = jnp.minimum(gu[:, :D_EXPERT], SWIGLU_LIMIT)
    u = jnp.clip(gu[:, D_EXPERT:], -SWIGLU_LIMIT, SWIGLU_LIMIT)
    act = g * _sigmoid(SWIGLU_ALPHA * g) * (u + 1.0)
    o_ref[...] = _bdot(act, wd_ref[0]) + bd_ref[0]

    @pl.when(i == last)
    def _():
        wait_rows(1 - cur)


def _moe_experts(f_all, blk_e, slot_tok, wgu, bgu, wd, bd):
    n_blocks = blk_e.shape[0]
    slots = n_blocks * MOE_BLOCK
    n_exp = wgu.shape[0]
    grid_spec = pltpu.PrefetchScalarGridSpec(
        num_scalar_prefetch=1,
        grid=(n_blocks,),
        in_specs=[pl.BlockSpec((1, 1, MOE_BLOCK), lambda i, be: (i, 0, 0), memory_space=pltpu.SMEM),
                  pl.BlockSpec((1, 1, MOE_BLOCK), lambda i, be: (jnp.minimum(i + 1, n_blocks - 1), 0, 0),
                               memory_space=pltpu.SMEM),
                  pl.BlockSpec(memory_space=pl.ANY),
                  pl.BlockSpec((1, D_MODEL, 2 * D_EXPERT), lambda i, be: (be[i], 0, 0)),
                  pl.BlockSpec((1, 1, 2 * D_EXPERT), lambda i, be: (be[i], 0, 0)),
                  pl.BlockSpec((1, D_EXPERT, D_MODEL), lambda i, be: (be[i], 0, 0)),
                  pl.BlockSpec((1, 1, D_MODEL), lambda i, be: (be[i], 0, 0))],
        out_specs=pl.BlockSpec((MOE_BLOCK, D_MODEL), lambda i, be: (i, 0)),
        scratch_shapes=[pltpu.VMEM((2 * MOE_BLOCK * ROW_TILES, LANE), F32), pltpu.SemaphoreType.DMA((2,))],
    )
    tok3 = slot_tok.reshape(n_blocks, 1, MOE_BLOCK)
    return pl.pallas_call(
        _moe_kernel,
        grid_spec=grid_spec,
        out_shape=jax.ShapeDtypeStruct((slots, D_MODEL), F32),
        compiler_params=_cparams(("arbitrary",), 56),
        name="moe_experts",
    )(blk_e, tok3, tok3, f_all,
      wgu, bgu.reshape(n_exp, 1, 2 * D_EXPERT), wd, bd.reshape(n_exp, 1, D_MODEL))


def _combine_kernel(final, dest_ref, nxt_ref, y_hbm, x_ref, tg_ref, gf_ref, fn_ref, o_ref, ybuf, sem):
    tm = x_ref.shape[0]
    i = pl.program_id(0)
    last = pl.num_programs(0) - 1
    cur = i % 2

    def start_rows(idx_ref, s):
        for r in range(tm):
            for k in range(TOP_K):
                pltpu.make_async_copy(y_hbm.at[pl.ds(idx_ref[0, 0, r * TOP_K + k], 1), :],
                                      ybuf.at[s, k, pl.ds(r, 1), :], sem.at[s]).start()

    def wait_rows(s):
        for k in range(TOP_K):
            pltpu.make_async_copy(y_hbm.at[pl.ds(0, tm), :], ybuf.at[s, k], sem.at[s]).wait()

    @pl.when(i == 0)
    def _():
        start_rows(dest_ref, 0)

    start_rows(nxt_ref, 1 - cur)
    wait_rows(cur)
    tg = tg_ref[...]
    moe = None
    for k in range(TOP_K):
        term = tg[:, k:k + 1] * ybuf[cur, k]
        moe = term if moe is None else moe + term
    x = x_ref[...] + gf_ref[0] * moe
    if final:
        x = _rms(x) * fn_ref[...]
    o_ref[...] = x

    @pl.when(i == last)
    def _():
        wait_rows(1 - cur)


def _moe_combine(y_slots, dest, top_gate, x_all, gf, final_gain, final, nrows, lat_rows, seq_rows, tm=128):
    grp = functools.partial(_group_of_tile, tm=tm, lat_rows=lat_rows, seq_rows=seq_rows)
    nt = nrows // tm
    grid_spec = pl.GridSpec(
        grid=(nt,),
        in_specs=[pl.BlockSpec((1, 1, tm * TOP_K), lambda i: (i, 0, 0), memory_space=pltpu.SMEM),
                  pl.BlockSpec((1, 1, tm * TOP_K), lambda i: (jnp.minimum(i + 1, nt - 1), 0, 0),
                               memory_space=pltpu.SMEM),
                  pl.BlockSpec(memory_space=pl.ANY),
                  pl.BlockSpec((tm, D_MODEL), lambda i: (i, 0)),
                  pl.BlockSpec((tm, LANE), lambda i: (i, 0)),
                  pl.BlockSpec((1, 1, D_MODEL), lambda i: (grp(i), 0, 0)),
                  pl.BlockSpec((1, D_MODEL), lambda i: (0, 0))],
        out_specs=pl.BlockSpec((tm, D_MODEL), lambda i: (i, 0)),
        scratch_shapes=[pltpu.VMEM((2, TOP_K, tm, D_MODEL), F32), pltpu.SemaphoreType.DMA((2,))],
    )
    dest3 = dest.reshape(nt, 1, tm * TOP_K)
    return pl.pallas_call(
        functools.partial(_combine_kernel, final),
        grid_spec=grid_spec,
        out_shape=jax.ShapeDtypeStruct((nrows, D_MODEL), F32),
        compiler_params=_cparams(("arbitrary",)),
        name="moe_combine",
    )(dest3, dest3, y_slots, x_all, top_gate, gf, final_gain)


def _expert_onehot(idx):
    lane = lax.broadcasted_iota(jnp.int32, idx.shape, 1)
    oh = jnp.zeros(idx.shape, F32)
    for k in range(TOP_K):
        oh = oh + jnp.where(lane == idx[:, k:k + 1], 1.0, 0.0)
    return oh


def _rank_kernel(idx_ref, rank_ref, cnt_ref, carry):
    @pl.when(pl.program_id(0) == 0)
    def _():
        carry[...] = jnp.zeros_like(carry)

    oh = _expert_onehot(idx_ref[...])
    tm = oh.shape[0]
    r = lax.broadcasted_iota(jnp.int32, (tm, tm), 0)
    c = lax.broadcasted_iota(jnp.int32, (tm, tm), 1)
    earlier = jnp.where(c < r, 1.0, 0.0).astype(BF16)
    rank_ref[...] = jnp.dot(earlier, oh.astype(BF16), preferred_element_type=F32) + carry[...]
    carry[...] += jnp.sum(oh, axis=0, keepdims=True)
    cnt_ref[...] = carry[...]


def _dest_kernel(idx_ref, rank_ref, pstart_ref, dest_ref):
    idx = idx_ref[...]
    slot = rank_ref[...] + pstart_ref[...]
    lane = lax.broadcasted_iota(jnp.int32, idx.shape, 1)
    out = jnp.zeros(idx.shape, jnp.int32)
    for k in range(TOP_K):
        dk = jnp.sum(jnp.where(lane == idx[:, k:k + 1], slot, 0.0), axis=-1, keepdims=True)
        out = jnp.where(lane == k, dk.astype(jnp.int32), out)
    dest_ref[...] = out


def _route(top_idx128, tm=512):
    n = top_idx128.shape[0]
    a = n * TOP_K
    row = pl.BlockSpec((tm, LANE), lambda i: (i, 0))
    one = pl.BlockSpec((1, LANE), lambda i: (0, 0))
    rank, cnt = pl.pallas_call(
        _rank_kernel,
        grid=(n // tm,),
        in_specs=[row],
        out_specs=[row, one],
        out_shape=[jax.ShapeDtypeStruct((n, LANE), F32), jax.ShapeDtypeStruct((1, LANE), F32)],
        scratch_shapes=[pltpu.VMEM((1, LANE), F32)],
        compiler_params=_cparams(("arbitrary",)),
        name="route_rank",
    )(top_idx128)
    counts = cnt[0, :N_EXPERTS].astype(jnp.int32)
    padded = (counts + MOE_BLOCK - 1) // MOE_BLOCK * MOE_BLOCK
    pend = jnp.cumsum(padded)
    pstart = pend - padded
    pstart128 = jnp.pad(pstart.astype(F32), (0, LANE - N_EXPERTS)).reshape(1, LANE)
    dest = pl.pallas_call(
        _dest_kernel,
        grid=(n // tm,),
        in_specs=[row, row, one],
        out_specs=row,
        out_shape=jax.ShapeDtypeStruct((n, LANE), jnp.int32),
        compiler_params=_cparams(("parallel",)),
        name="route_dest",
    )(top_idx128, rank, pstart128)[:, :TOP_K]
    n_blocks = (a + N_EXPERTS * (MOE_BLOCK - 1)) // MOE_BLOCK + 1
    slots = n_blocks * MOE_BLOCK
    tok = jnp.broadcast_to(jnp.arange(n, dtype=jnp.int32)[:, None], (n, TOP_K))
    slot_tok = jnp.zeros((slots,), jnp.int32).at[dest.reshape(-1)].set(tok.reshape(-1))
    first_slot = jnp.arange(n_blocks, dtype=jnp.int32) * MOE_BLOCK
    blk_e = jnp.minimum(jnp.sum((pend[None, :] <= first_slot[:, None]).astype(jnp.int32), axis=1), N_EXPERTS - 1)
    return blk_e, slot_tok, dest.astype(jnp.int32)


def _rope_tables(n_lat_rows, seq_len, ctx_rows):
    t = jnp.arange(seq_len, dtype=jnp.int32)
    inv = ROPE_THETA ** (-jnp.arange(0, 32, 2, dtype=F32) / 32)

    def cs(p):
        ang = p.astype(F32)[:, None] * inv[None, :]
        return jnp.cos(ang), jnp.sin(ang)

    cr, sr = cs(t // GRID_W)
    cc, sc = cs(t % GRID_W)
    c64 = jnp.concatenate([cr, cr, cc, cc], axis=1)
    s64 = jnp.concatenate([-sr, sr, -sc, sc], axis=1)
    reps = n_lat_rows // seq_len

    def build(c_half2, s_half2):
        c = jnp.concatenate([c64, c_half2], axis=1)
        s = jnp.concatenate([s64, s_half2], axis=1)
        c = jnp.concatenate([c] * reps + [jnp.ones((ctx_rows, LANE), F32)], axis=0)
        s = jnp.concatenate([s] * reps + [jnp.zeros((ctx_rows, LANE), F32)], axis=0)
        return c, s

    cd, sd = build(c64, s64)
    cm, sm = build(jnp.ones_like(c64), jnp.zeros_like(s64))
    return cd, sd, cm, sm


W_IN_SEGMENTS = ((320, 0, 512), (832, 512, 512), (1344, 1024, 512), (1856, 1536, 512), (2384, 2048, 512),
                 (2896, 2560, 512), (3408, 3072, 512), (3920, 3584, 512), (4432, 4096, 1536),
                 (0, 5632, 256), (256, 5888, 64), (2368, 5952, 16))
W_IN_USED = 5968
W_IN_GATE0 = 5968


def _w_in_layout_kernel(w_ref, small_ref, gate_ref):
    for src, dst, width in W_IN_SEGMENTS:
        small_ref[:, dst:dst + width] = w_ref[0, :, src:src + width].astype(BF16)
    small_ref[:, W_IN_USED:] = jnp.zeros((small_ref.shape[0], SMALL_COLS - W_IN_USED), BF16)
    gate_ref[...] = w_ref[0, :, W_IN_GATE0:].astype(BF16)


def _layout_w_in(w, layer, tk=128):
    _, d, ncol = w.shape
    return pl.pallas_call(
        _w_in_layout_kernel,
        grid=(d // tk,),
        in_specs=[pl.BlockSpec((1, tk, ncol), lambda i: (layer, i, 0))],
        out_specs=[pl.BlockSpec((tk, SMALL_COLS), lambda i: (i, 0)), pl.BlockSpec((tk, GATE_COLS), lambda i: (i, 0))],
        out_shape=[jax.ShapeDtypeStruct((d, SMALL_COLS), BF16), jax.ShapeDtypeStruct((d, GATE_COLS), BF16)],
        compiler_params=_cparams(("parallel",)),
        name="w_in_layout",
    )(w)


def _layout_w_qb(w):
    w = w.reshape(w.shape[0], N_HEADS, MLA_NOPE + MLA_ROPE)
    w = jnp.pad(w, ((0, 0), (0, 0), (0, MLA_QK_PAD - MLA_NOPE - MLA_ROPE)))
    return w.reshape(w.shape[0], N_HEADS * MLA_QK_PAD).astype(BF16)


def _layout_w_kvb(w):
    w = w.reshape(w.shape[0], N_HEADS, MLA_NOPE + HEAD_W)
    return jnp.concatenate([w[:, :, :MLA_NOPE].reshape(w.shape[0], -1),
                            w[:, :, MLA_NOPE:].reshape(w.shape[0], -1)], axis=1).astype(BF16)


def kernel(x, c, ctx, c_ctx, ada_w, ada_b, norm_mix, norm_ffn, w_in, hy_conv, hy_conv_b, hy_w1, hy_b1, hy_w2, hy_b2, hy_w3, hy_freq, hy_decay, hy_skip, mla_q_norm, mla_w_qb, mla_kv_norm, mla_w_kvb, dif_lambda, dif_norm, ml_conv_q, ml_conv_k, ml_gate_b, ml_norm, w_branch, w_out, router_w, router_b, exp_w_gu, exp_b_gu, exp_w_down, exp_b_down, final_norm):
    batch, n, d = x.shape
    n_ctx = ctx.shape[1]
    depth = w_in.shape[0]
    ctx_rows = batch * n_ctx
    lat_rows = batch * n
    t_all = ctx_rows + lat_rows

    assert batch + 1 <= SUBLANE
    cvec = jnp.zeros((SUBLANE, d), F32).at[:batch].set(c).at[batch].set(c_ctx)
    mod = _modulation(cvec, ada_w, ada_b)
    x_all = jnp.concatenate([x.reshape(lat_rows, d), ctx.reshape(ctx_rows, d)], axis=0)
    tabs = _rope_tables(lat_rows, n, ctx_rows)
    assert batch == 2, "the Hyena transform carries the two batches as one complex signal"
    fz_tab, ff_tab, fi_tab, g_tab, gt_tab = _dft_tables(n)
    feat_lat, feat_ctx = _hy_features(n), _hy_features(n_ctx)
    zero_state = (jnp.zeros((2, batch, N_HEADS, HEAD_W, HEAD_W), F32), jnp.zeros((2, batch, N_HEADS, 1, HEAD_W), F32),
                  jnp.zeros((2, batch, N_HEADS, 1, HEAD_W), F32))
    n1h = n // LANE
    wgu_all = exp_w_gu.astype(BF16).reshape(depth * N_EXPERTS, d, 2 * D_EXPERT)
    wd_all = exp_w_down.astype(BF16).reshape(depth * N_EXPERTS, D_EXPERT, d)
    bgu_all = exp_b_gu.reshape(depth * N_EXPERTS, 2 * D_EXPERT)
    bd_all = exp_b_down.reshape(depth * N_EXPERTS, d)

    for l in range(depth):
        last = l == depth - 1
        lam_init = 0.8 - 0.6 * math.exp(-0.3 * l)
        m6 = mod[l].reshape(SUBLANE, 6, 1, d)
        sh_a, sc_a, g_a, sh_f, sc_f, g_f = (m6[:, i] for i in range(6))
        w_small, w_gate = _layout_w_in(w_in, l)
        gain_mix = norm_mix[l].reshape(1, d)

        n_tok = lat_rows if last else t_all
        h_in = _norm_mod(x_all, gain_mix, sc_a, sh_a, lat_rows, n)
        u = _in_proj(h_in, w_small, F32, t_all)
        gate = _in_proj(h_in, w_gate, BF16, n_tok)

        mq, mk, mv, dq, dk, dv = _attn_prep(
            u, tabs, _layout_w_qb(mla_w_qb[l]), _layout_w_kvb(mla_w_kvb[l]),
            mla_q_norm[l].reshape(1, -1), mla_kv_norm[l].reshape(1, -1))
        ml_q, ml_k, ml_g = _ml_prep(u, ml_conv_q[l], ml_conv_k[l], ml_gate_b[l].reshape(1, 16), n, lat_rows, n_ctx)
        gt_lat = jnp.swapaxes(ml_g[:lat_rows].reshape(batch, n, 16), 1, 2)
        gt_ctx = jnp.swapaxes(ml_g[lat_rows:].reshape(batch, n_ctx, 16), 1, 2)

        lam_p = dif_lambda[l]
        dgain = dif_norm[l].reshape(1, HEAD_W)
        segs = [(lat_rows, n_ctx), (0, n)]
        y_mla = _attention(mq, mk, mv, 1, 0, n, segs, batch, MLA_QK_PAD)
        y_dif = _attention(dq, dk, dv, 2, 0, n, segs, batch, 2 * DIF_QK, lam_p, dgain, lam_init)

        hcf, hcb, st_ctx = _mlstm(ml_q, ml_k, u, ml_g, gt_ctx, zero_state, not last, lat_rows, n_ctx, batch)
        hlf, hlb, _ = _mlstm(ml_q, ml_k, u, ml_g, gt_lat, st_ctx, True, 0, n, batch)

        hw1 = jnp.pad(hy_w1[l], ((0, LANE - HY_EMB), (0, 0)))
        filt_args = (hw1, hy_b1[l].reshape(1, -1), hy_w2[l], hy_b2[l].reshape(1, -1), hy_w3[l], hy_freq[l],
                     hy_decay[l].reshape(1, -1))
        skip = hy_skip[l].reshape(1, -1)
        hconv_b = hy_conv_b[l].reshape(1, -1)
        z, x0 = _hy_prep(u, hy_conv[l], hconv_b, 0, lat_rows, n, lat_rows, n_ctx)
        hfilt, hnorm = _hy_filter(n, feat_lat, *filt_args)
        filt_full = hfilt.reshape(2 * n, BRANCH_W)
        n1 = 2 * n1h
        cols = LANE * BRANCH_W
        zr = z.reshape(batch * n1h, cols)
        az = _dft_stage_a(fz_tab, zr)
        ah = _dft_stage_a(ff_tab, filt_full.reshape(n1, cols))
        pz = _dft_stage_c(az.reshape(2, n1, LANE, BRANCH_W), ah.reshape(2, n1, LANE, BRANCH_W), g_tab, gt_tab)
        y_hy = _dft_stage_a_inv(fi_tab, pz.reshape(2 * n1, cols), zr, x0.reshape(batch * n1h, cols), hnorm,
                                skip).reshape(lat_rows, BRANCH_W)

        h_f, h_b = hlf.reshape(lat_rows, -1), hlb.reshape(lat_rows, -1)
        w_o = w_out[l].astype(BF16)
        gain_ffn = norm_ffn[l].reshape(1, d)
        rw = jnp.pad(router_w[l], ((0, 0), (0, LANE - N_EXPERTS))).astype(BF16)
        rb = jnp.concatenate([router_b[l], jnp.full((LANE - N_EXPERTS,), -jnp.inf, F32)]).reshape(1, LANE)

        if not last:
            cseg = [(lat_rows, n_ctx)]
            yc_mla = _attention(mq, mk, mv, 1, lat_rows, n_ctx, cseg, batch, MLA_QK_PAD)
            yc_dif = _attention(dq, dk, dv, 2, lat_rows, n_ctx, cseg, batch, 2 * DIF_QK, lam_p, dgain, lam_init)
            zc, x0c = _hy_prep(u, hy_conv[l], hconv_b, lat_rows, ctx_rows, n, lat_rows, n_ctx)
            hfc, hnc = _hy_filter(n_ctx, feat_ctx, *filt_args)
            yc_hy = _hy_small(zc.reshape(batch, n_ctx, BRANCH_W), x0c.reshape(batch, n_ctx, BRANCH_W), hfc, hnc,
                              skip).reshape(ctx_rows, BRANCH_W)
            cat = lambda a, b: jnp.concatenate([a, b], axis=0)
            y_hy, y_mla, y_dif = cat(y_hy, yc_hy), cat(y_mla, yc_mla), cat(y_dif, yc_dif)
            h_f, h_b = cat(h_f, hcf.reshape(ctx_rows, -1)), cat(h_b, hcb.reshape(ctx_rows, -1))

        merged = _merge(y_hy, y_mla, y_dif, h_f, h_b, u, gate, w_branch[l].astype(BF16), ml_norm[l].reshape(1, -1),
                        n_tok)
        xo, f_all, top_idx, top_gate = _out_proj(merged, w_o, x_all, g_a, gain_ffn, sc_f, sh_f, rw, rb,
                                                 n_tok, lat_rows, n)
        blk_e, slot_tok, dest = _route(top_idx)
        y_slots = _moe_experts(f_all.reshape(n_tok, ROW_TILES, LANE), blk_e + l * N_EXPERTS, slot_tok,
                               wgu_all, bgu_all, wd_all, bd_all)
        fgain = final_norm.reshape(1, d)
        x_all = _moe_combine(y_slots, dest, top_gate, xo, g_f, fgain, last, n_tok, lat_rows, n)

    return x_all.reshape(batch, n, d)
```

```python
import functools
import math

import jax
import jax.numpy as jnp
from jax import lax
from jax.experimental import pallas as pl
from jax.experimental.pallas import tpu as pltpu

F32 = jnp.float32
BF16 = jnp.bfloat16
HIGHEST = lax.Precision.HIGHEST

D_MODEL = 2048
GRID_W = 64
BRANCH_W = 512
N_HEADS = 4
HEAD_W = 128
MLA_NOPE = 128
MLA_ROPE = 64
MLA_QK_PAD = 256
MLA_SCALE = (MLA_NOPE + MLA_ROPE) ** -0.5
LOG2E = math.log2(math.e)
DIF_QK = 64
ML_CHUNK = 64
ML_GATE_W = 16
N_EXPERTS = 32
TOP_K = 4
D_EXPERT = 1024
SWIGLU_LIMIT = 7.0
SWIGLU_ALPHA = 1.702
MOE_BLOCK = 256
ROPE_THETA = 10000.0
NORM_EPS = 1e-6
HY_BANDS = 16
HY_EMB = 1 + 2 * HY_BANDS
HY_HID = 64
LANE = 128
SUBLANE = 8
ROW_TILES = D_MODEL // LANE
SMALL_COLS = 6144
GATE_COLS = 4 * D_MODEL

CB_DK, CB_DV, CB_MK, CB_MV, CB_QA, CB_DQ, CB_MQ, CB_MO, CB_HX0, CB_HX1, CB_HV = range(11)
CB_CKV_256 = 22
CB_KPE_128 = 46


def _cparams(sem, vmem_mb=48):
    return pltpu.CompilerParams(dimension_semantics=sem, vmem_limit_bytes=vmem_mb * 1024 * 1024)


def _rms(x):
    return x * lax.rsqrt(jnp.mean(x * x, axis=-1, keepdims=True) + NORM_EPS)


def _sigmoid(x):
    return 1.0 / (1.0 + jnp.exp(-x))


def _silu(x):
    return x * _sigmoid(x)


def _log_sigmoid(x):
    return jnp.minimum(x, 0.0) - jnp.log(1.0 + jnp.exp(-jnp.abs(x)))


def _bdot(a, b):
    return jnp.dot(a.astype(BF16), b.astype(BF16), preferred_element_type=F32)


def _bdot_nt(a, b):
    return lax.dot_general(a.astype(BF16), b.astype(BF16), (((1,), (1,)), ((), ())),
                           preferred_element_type=F32)


def _hdot(a, b):
    return jnp.dot(a, b, preferred_element_type=F32, precision=HIGHEST)


def _mod_kernel(c_ref, w_ref, b_ref, o_ref):
    o_ref[0] = _bdot(_silu(c_ref[...]), w_ref[0]) + b_ref[0]


def _modulation(cvec, ada_w, ada_b):
    n_layers, d, six_d = ada_w.shape
    tn = 1024
    return pl.pallas_call(
        _mod_kernel,
        grid=(n_layers, six_d // tn),
        in_specs=[pl.BlockSpec((SUBLANE, d), lambda l, j: (0, 0)),
                  pl.BlockSpec((1, d, tn), lambda l, j: (l, 0, j)),
                  pl.BlockSpec((1, 1, tn), lambda l, j: (l, 0, j))],
        out_specs=pl.BlockSpec((1, SUBLANE, tn), lambda l, j: (l, 0, j)),
        out_shape=jax.ShapeDtypeStruct((n_layers, SUBLANE, six_d), F32),
        compiler_params=_cparams(("parallel", "parallel")),
        name="adaln_modulation",
    )(cvec, ada_w, ada_b.reshape(n_layers, 1, six_d))


def _group_of_tile(i, tm, lat_rows, seq_rows):
    return jnp.minimum(i // (seq_rows // tm), lat_rows // seq_rows)


def _norm_mod_kernel(x_ref, g_ref, sc_ref, sh_ref, o_ref):
    h = _rms(x_ref[...]) * g_ref[...]
    o_ref[...] = (h * (1.0 + sc_ref[0]) + sh_ref[0]).astype(BF16)


def _norm_mod(x_all, gain, sc, sh, lat_rows, seq_rows, tm=512):
    t_all, d = x_all.shape
    grp = functools.partial(_group_of_tile, tm=tm, lat_rows=lat_rows, seq_rows=seq_rows)
    return pl.pallas_call(
        _norm_mod_kernel,
        grid=(t_all // tm,),
        in_specs=[pl.BlockSpec((tm, d), lambda i: (i, 0)),
                  pl.BlockSpec((1, d), lambda i: (0, 0)),
                  pl.BlockSpec((1, 1, d), lambda i: (grp(i), 0, 0)),
                  pl.BlockSpec((1, 1, d), lambda i: (grp(i), 0, 0))],
        out_specs=pl.BlockSpec((tm, d), lambda i: (i, 0)),
        out_shape=jax.ShapeDtypeStruct((t_all, d), BF16),
        compiler_params=_cparams(("parallel",)),
        name="norm_mod",
    )(x_all, gain, sc, sh)


def _proj_kernel(h_ref, w_ref, o_ref):
    o_ref[...] = jnp.dot(h_ref[...], w_ref[...], preferred_element_type=F32).astype(o_ref.dtype)


def _in_proj(h, w, out_dtype, nrows, tm=512, tn=1024):
    d = h.shape[1]
    ncol = w.shape[1]
    return pl.pallas_call(
        _proj_kernel,
        grid=(ncol // tn, nrows // tm),
        in_specs=[pl.BlockSpec((tm, d), lambda j, i: (i, 0)),
                  pl.BlockSpec((d, tn), lambda j, i: (0, j))],
        out_specs=pl.BlockSpec((tm, tn), lambda j, i: (i, j)),
        out_shape=jax.ShapeDtypeStruct((nrows, ncol), out_dtype),
        compiler_params=_cparams(("parallel", "arbitrary")),
        name="in_proj",
    )(h, w)


def _rope(x, c, s):
    w = x.shape[-1]
    lane = lax.broadcasted_iota(jnp.int32, x.shape, 1)
    up = pltpu.roll(x, w - 16, 1)
    dn = pltpu.roll(x, 16, 1)
    return x * c + jnp.where((lane % 32) < 16, up, dn) * s


def _attn_prep_kernel(dk_ref, dv_ref, qa_ref, dq_ref, ckv_ref, kpe_ref, cd_ref, sd_ref, cm_ref, sm_ref,
                      wqb_ref, wkvb_ref, qn_ref, kvn_ref,
                      mq_ref, mk_ref, mv_ref, dqo_ref, dko_ref, dvo_ref):
    cd4 = jnp.concatenate([cd_ref[...]] * N_HEADS, axis=1)
    sd4 = jnp.concatenate([sd_ref[...]] * N_HEADS, axis=1)
    cm, sm = cm_ref[...], sm_ref[...]
    dqo_ref[...] = (_rope(dq_ref[...], cd4, sd4) * (DIF_QK ** -0.5 * LOG2E)).astype(BF16)
    dko_ref[...] = _rope(dk_ref[...], cd4, sd4).astype(BF16)
    dvo_ref[...] = dv_ref[...].astype(BF16)

    q = _bdot(_rms(qa_ref[...]) * qn_ref[...], wqb_ref[...])
    kv = _bdot(_rms(ckv_ref[...]) * kvn_ref[...], wkvb_ref[...])
    kpe = kpe_ref[...]
    lane = lax.broadcasted_iota(jnp.int32, kpe.shape, 1)
    kpe = _rope(jnp.where(lane < MLA_ROPE, kpe, 0.0), cm, sm).astype(BF16)
    for h in range(N_HEADS):
        o = h * MLA_QK_PAD
        mq_ref[:, o:o + MLA_NOPE] = (q[:, o:o + MLA_NOPE] * (MLA_SCALE * LOG2E)).astype(BF16)
        mq_ref[:, o + MLA_NOPE:o + MLA_QK_PAD] = (
            _rope(q[:, o + MLA_NOPE:o + MLA_QK_PAD], cm, sm) * (MLA_SCALE * LOG2E)).astype(BF16)
        mk_ref[:, o:o + MLA_NOPE] = kv[:, h * MLA_NOPE:(h + 1) * MLA_NOPE].astype(BF16)
        mk_ref[:, o + MLA_NOPE:o + MLA_QK_PAD] = kpe
    mv_ref[...] = kv[:, N_HEADS * MLA_NOPE:].astype(BF16)


def _attn_prep(u, tabs, wqb, wkvb, qn, kvn, tm=256):
    t_all = u.shape[0]
    cd, sd, cm, sm = tabs
    col = lambda w, cb: pl.BlockSpec((tm, w), lambda i, cb=cb: (i, cb))
    tab = pl.BlockSpec((tm, LANE), lambda i: (i, 0))
    full = lambda a: pl.BlockSpec(a.shape, lambda i: (0,) * a.ndim)
    out = lambda w: pl.BlockSpec((tm, w), lambda i: (i, 0))
    shp = lambda w: jax.ShapeDtypeStruct((t_all, w), BF16)
    return pl.pallas_call(
        _attn_prep_kernel,
        grid=(t_all // tm,),
        in_specs=[col(512, CB_DK), col(512, CB_DV), col(512, CB_QA), col(512, CB_DQ),
                  col(256, CB_CKV_256), col(128, CB_KPE_128), tab, tab, tab, tab,
                  full(wqb), full(wkvb), full(qn), full(kvn)],
        out_specs=[out(1024), out(1024), out(512), out(512), out(512), out(512)],
        out_shape=[shp(1024), shp(1024), shp(512), shp(512), shp(512), shp(512)],
        compiler_params=_cparams(("parallel",)),
        name="attn_prep",
    )(u, u, u, u, u, u, cd, sd, cm, sm, wqb, wkvb, qn, kvn)


def _conv3(x, prev_blk, next_blk, w, first, last):
    tm = x.shape[0]
    row = lax.broadcasted_iota(jnp.int32, x.shape, 0)
    prev_row = jnp.where(first, 0.0, prev_blk[SUBLANE - 1:SUBLANE, :])
    next_row = jnp.where(last, 0.0, next_blk[0:1, :])
    xm = jnp.where(row == 0, prev_row, pltpu.roll(x, 1, 0))
    xp = jnp.where(row == tm - 1, next_row, pltpu.roll(x, tm - 1, 0))
    return xm * w[0:1, :] + x * w[1:2, :] + xp * w[2:3, :]


def _seq_edges(tile, tm, seq_rows, lat_rows, ctx_len):
    r = tile * tm
    in_lat = r < lat_rows
    pos = jnp.where(in_lat, r % seq_rows, (r - lat_rows) % ctx_len)
    length = jnp.where(in_lat, seq_rows, ctx_len)
    return pos == 0, pos + tm == length


def _ml_prep_kernel(seq_rows, lat_rows, ctx_len,
                    q_ref, qp_ref, qn_ref, k_ref, kp_ref, kn_ref, gblk_ref, wq_ref, wk_ref, gb_ref,
                    qo_ref, ko_ref, go_ref):
    tm = q_ref.shape[0]
    first, last = _seq_edges(pl.program_id(0), tm, seq_rows, lat_rows, ctx_len)
    qo_ref[...] = _silu(_conv3(q_ref[...], qp_ref[...], qn_ref[...], wq_ref[...], first, last))
    ko_ref[...] = _silu(_conv3(k_ref[...], kp_ref[...], kn_ref[...], wk_ref[...], first, last)) * HEAD_W ** -0.5
    go_ref[...] = gblk_ref[:, MLA_ROPE:MLA_ROPE + 16] + gb_ref[...]


def _halo_specs(tm, w, cb, t_all, r0=0):
    nb8 = t_all // SUBLANE
    r8 = tm // SUBLANE
    return [pl.BlockSpec((tm, w), lambda i: (i + r0, cb)),
            pl.BlockSpec((SUBLANE, w), lambda i: (jnp.maximum((i + r0) * r8 - 1, 0), cb)),
            pl.BlockSpec((SUBLANE, w), lambda i: (jnp.minimum((i + r0 + 1) * r8, nb8 - 1), cb))]


def _ml_prep(u, wq, wk, gb, seq_rows, lat_rows, ctx_len, tm=256):
    t_all = u.shape[0]
    full = lambda a: pl.BlockSpec(a.shape, lambda i: (0,) * a.ndim)
    return pl.pallas_call(
        functools.partial(_ml_prep_kernel, seq_rows, lat_rows, ctx_len),
        grid=(t_all // tm,),
        in_specs=_halo_specs(tm, 512, CB_MQ, t_all) + _halo_specs(tm, 512, CB_MK, t_all)
        + [pl.BlockSpec((tm, LANE), lambda i: (i, CB_KPE_128)), full(wq), full(wk), full(gb)],
        out_specs=[pl.BlockSpec((tm, 512), lambda i: (i, 0)), pl.BlockSpec((tm, 512), lambda i: (i, 0)),
                   pl.BlockSpec((tm, ML_GATE_W), lambda i: (i, 0))],
        out_shape=[jax.ShapeDtypeStruct((t_all, 512), F32), jax.ShapeDtypeStruct((t_all, 512), F32),
                   jax.ShapeDtypeStruct((t_all, ML_GATE_W), F32)],
        compiler_params=_cparams(("parallel",)),
        name="mlstm_prep",
    )(u, u, u, u, u, u, u, wq, wk, gb)


def _hy_prep_kernel(r0, seq_rows, lat_rows, ctx_len,
                    a_ref, ap_ref, an_ref, b_ref, bp_ref, bn_ref, c_ref, cp_ref, cn_ref, w_ref, bias_ref,
                    z_ref, x0_ref):
    tm = a_ref.shape[0]
    first, last = _seq_edges(pl.program_id(0) + r0, tm, seq_rows, lat_rows, ctx_len)
    w, bias = w_ref[...], bias_ref[...]
    outs = []
    for s, (m, p, n) in enumerate(((a_ref, ap_ref, an_ref), (b_ref, bp_ref, bn_ref), (c_ref, cp_ref, cn_ref))):
        sl = slice(s * BRANCH_W, (s + 1) * BRANCH_W)
        outs.append(_conv3(m[...], p[...], n[...], w[:, sl], first, last) + bias[:, sl])
    x0_ref[...] = outs[0]
    z_ref[...] = outs[2] * outs[1]


def _hy_prep(u, w, bias, row0, nrows, seq_rows, lat_rows, ctx_len, tm=256):
    t_all = u.shape[0]
    r0 = row0 // tm
    full = lambda a: pl.BlockSpec(a.shape, lambda i: (0,) * a.ndim)
    halo = lambda cb: _halo_specs(tm, 512, cb, t_all, r0)
    return pl.pallas_call(
        functools.partial(_hy_prep_kernel, r0, seq_rows, lat_rows, ctx_len),
        grid=(nrows // tm,),
        in_specs=halo(CB_HX0) + halo(CB_HX1) + halo(CB_HV) + [full(w), full(bias)],
        out_specs=[pl.BlockSpec((tm, 512), lambda i: (i, 0)), pl.BlockSpec((tm, 512), lambda i: (i, 0))],
        out_shape=[jax.ShapeDtypeStruct((nrows, 512), F32), jax.ShapeDtypeStruct((nrows, 512), F32)],
        compiler_params=_cparams(("parallel",)),
        name="hyena_prep",
    )(u, u, u, u, u, u, u, u, u, w, bias)


def _lane_fold(x, op):
    parts = [x[:, j * LANE:(j + 1) * LANE] for j in range(x.shape[1] // LANE)]
    while len(parts) > 1:
        parts = [op(parts[i], parts[i + 1]) for i in range(0, len(parts) - 1, 2)] + (
            [parts[-1]] if len(parts) % 2 else [])
    return parts[0]


def _attn_kernel(n_maps, seg_lens, tk, lam_init, *refs):
    q_ref = refs[0]
    kv_refs = refs[1:1 + 2 * len(seg_lens)]
    pos = 1 + 2 * len(seg_lens)
    if n_maps == 2:
        lam_ref, gain_ref = refs[pos], refs[pos + 1]
        pos += 2
    o_ref = refs[pos]

    q = q_ref[...]
    tq = q.shape[0]
    if n_maps == 1:
        qs = [q]
    else:
        lane = lax.broadcasted_iota(jnp.int32, q.shape, 1)
        zero = jnp.zeros_like(q)
        qs = [jnp.where(lane < DIF_QK, q, zero), jnp.where(lane >= DIF_QK, q, zero)]

    def step(k, v, carry):
        new = []
        for qm, (m, l, acc) in zip(qs, carry):
            s = lax.dot_general(qm, k, (((1,), (1,)), ((), ())), preferred_element_type=F32)
            m_new = jnp.maximum(m, jnp.max(_lane_fold(s, jnp.maximum), axis=-1, keepdims=True))
            p = jnp.exp2(s - m_new)
            alpha = jnp.exp2(m - m_new)
            l_new = alpha * l + jnp.sum(_lane_fold(p, jnp.add), axis=-1, keepdims=True)
            acc_new = alpha * acc + jnp.dot(p.astype(BF16), v, preferred_element_type=F32)
            new.append((m_new, l_new, acc_new))
        return tuple(new)

    carry = tuple((jnp.full((tq, 1), -jnp.inf, F32), jnp.zeros((tq, 1), F32), jnp.zeros((tq, HEAD_W), F32))
                  for _ in range(n_maps))
    for si, slen in enumerate(seg_lens):
        k_ref, v_ref = kv_refs[2 * si], kv_refs[2 * si + 1]
        chunk = min(tk, slen)

        def body(c, carry, k_ref=k_ref, v_ref=v_ref, chunk=chunk):
            start = pl.multiple_of(c * chunk, chunk)
            return step(k_ref[pl.ds(start, chunk), :], v_ref[pl.ds(start, chunk), :], carry)

        trips = slen // chunk
        carry = lax.fori_loop(0, trips, body, carry, unroll=math.gcd(trips, 8))

    outs = [acc / l for (_, l, acc) in carry]
    if n_maps == 1:
        o_ref[...] = outs[0].astype(o_ref.dtype)
    else:
        lp = lam_ref[...]
        lam = (jnp.exp(jnp.sum(lp[0:1] * lp[1:2], axis=-1, keepdims=True))
               - jnp.exp(jnp.sum(lp[2:3] * lp[3:4], axis=-1, keepdims=True)) + lam_init)
        o = outs[0] - lam * outs[1]
        o_ref[...] = (_rms(o) * gain_ref[...] * (1.0 - lam_init)).astype(o_ref.dtype)


def _attention(q, k, v, n_maps, q_row0, q_len, segs, batch, qk_w, lam=None, gain=None, lam_init=0.0,
               tq=1024, tk=512):
    tq = min(tq, q_len)
    nq = q_len // tq
    qb0 = q_row0 // tq
    in_specs = [pl.BlockSpec((tq, qk_w), lambda b, h, i: (qb0 + b * nq + i, h))]
    args = [q]
    for (r0, slen) in segs:
        kb0 = r0 // slen
        in_specs.append(pl.BlockSpec((slen, qk_w), lambda b, h, i, kb0=kb0: (kb0 + b, h)))
        in_specs.append(pl.BlockSpec((slen, HEAD_W), lambda b, h, i, kb0=kb0: (kb0 + b, h)))
        args += [k, v]
    if n_maps == 2:
        in_specs += [pl.BlockSpec(lam.shape, lambda b, h, i: (0, 0)),
                     pl.BlockSpec(gain.shape, lambda b, h, i: (0, 0))]
        args += [lam, gain]
    return pl.pallas_call(
        functools.partial(_attn_kernel, n_maps, tuple(s for _, s in segs), tk, lam_init),
        grid=(batch, N_HEADS, nq),
        in_specs=in_specs,
        out_specs=pl.BlockSpec((tq, HEAD_W), lambda b, h, i: (b * nq + i, h)),
        out_shape=jax.ShapeDtypeStruct((batch * q_len, N_HEADS * HEAD_W), BF16),
        compiler_params=_cparams(("parallel", "parallel", "arbitrary")),
        name="attention_maps%d" % n_maps,
    )(*args)


ML_STEP = 256


def _mlstm_chunk(rev, o, q_ref, k_ref, v_ref, g_ref, gt_ref, state, d, b, h_ref):
    ti = lax.broadcasted_iota(jnp.int32, (ML_CHUNK, ML_CHUNK), 0)
    si = lax.broadcasted_iota(jnp.int32, (ML_CHUNK, ML_CHUNK), 1)
    tri = (si >= ti) if rev else (si <= ti)
    tri_f = tri.astype(F32)
    tri_t = ((ti >= si) if rev else (ti <= si)).astype(F32)
    gcol = g_ref[o:o + ML_CHUNK, :]
    grow = gt_ref[0, :, o:o + ML_CHUNK]
    li_col = gcol[:, d * 8:d * 8 + 4]
    lf_col = _log_sigmoid(gcol[:, d * 8 + 4:d * 8 + 8])
    li_row = grow[d * 8:d * 8 + 4, :]
    lf_row = _log_sigmoid(grow[d * 8 + 4:d * 8 + 8, :])
    bc_col = _hdot(tri_f, lf_col)
    bc_row = _hdot(lf_row, tri_t)
    f_tot = jnp.sum(lf_col, axis=0, keepdims=True)
    for h in range(N_HEADS):
        hs = slice(h * HEAD_W, (h + 1) * HEAD_W)
        qc = q_ref[o:o + ML_CHUNK, hs]
        kc = k_ref[o:o + ML_CHUNK, hs]
        vc = v_ref[o:o + ML_CHUNK, hs]
        bcc, bcr = bc_col[:, h:h + 1], bc_row[h:h + 1, :]
        icol, irow = li_col[:, h:h + 1], li_row[h:h + 1, :]
        fh = f_tot[:, h:h + 1]
        c_mat, n_vec, m = state[d, b, h]
        g_end_r = fh - bcr + irow
        g_end_c = fh - bcc + icol
        m_new = jnp.maximum(fh + m, jnp.max(g_end_r, axis=-1, keepdims=True))
        decay = jnp.exp(fh + m - m_new)
        wk = jnp.exp(g_end_c - m_new)
        kw = kc * wk
        upd = lax.dot_general(kw.astype(BF16), vc.astype(BF16), (((0,), (0,)), ((), ())),
                              preferred_element_type=F32)
        if h_ref is not None:
            inter = bcc + m
            log_d = jnp.where(tri, bcc - bcr + irow, -jnp.inf)
            m_t = jnp.maximum(inter, jnp.max(log_d, axis=-1, keepdims=True))
            s = _bdot_nt(qc, kc) * jnp.exp(log_d - m_t)
            w_inter = jnp.exp(inter - m_t)
            num = _bdot(s, vc) + w_inter * _bdot(qc, c_mat)
            den = jnp.sum(s, axis=-1, keepdims=True) + w_inter * jnp.sum(qc * n_vec, axis=-1, keepdims=True)
            h_ref[b, o:o + ML_CHUNK, hs] = num / jnp.maximum(jnp.abs(den), jnp.exp(-m_t))
        state[d, b, h] = (decay * c_mat + upd, decay * n_vec + jnp.sum(kw, axis=0, keepdims=True), m_new)


def _mlstm_kernel(batch, with_out, n_steps, *refs):
    n_in = 2 * batch * 5
    chain_refs = refs[:n_in]
    c0_ref, n0_ref, m0_ref = refs[n_in:n_in + 3]
    pos = n_in + 3
    h_refs = (None, None)
    if with_out:
        h_refs = (refs[pos], refs[pos + 1])
        pos += 2
    cf_ref, nf_ref, mf_ref, c_scr, n_scr, m_scr = refs[pos:pos + 6]
    j = pl.program_id(0)

    @pl.when(j == 0)
    def _():
        c_scr[...] = c0_ref[...]
        n_scr[...] = n0_ref[...]
        m_scr[...] = m0_ref[...]

    keys = [(d, b, h) for d in range(2) for b in range(batch) for h in range(N_HEADS)]
    state = {key: (c_scr[key], n_scr[key], m_scr[key][:, 0:1]) for key in keys}
    n_chunks = ML_STEP // ML_CHUNK
    for ci in range(n_chunks):
        for d in range(2):
            c = (n_chunks - 1 - ci) if d == 1 else ci
            for b in range(batch):
                q_ref, k_ref, v_ref, g_ref, gt_ref = chain_refs[(d * batch + b) * 5:(d * batch + b + 1) * 5]
                _mlstm_chunk(d == 1, c * ML_CHUNK, q_ref, k_ref, v_ref, g_ref, gt_ref, state, d, b, h_refs[d])
    for key in keys:
        c_mat, n_vec, m = state[key]
        c_scr[key] = c_mat
        n_scr[key] = n_vec
        m_scr[key] = jnp.broadcast_to(m, (1, HEAD_W))

    @pl.when(j == n_steps - 1)
    def _():
        cf_ref[...] = c_scr[...]
        nf_ref[...] = n_scr[...]
        mf_ref[...] = m_scr[...]


def _mlstm(q, k, u, g, gt, init, with_out, row0, seq_len, batch):
    n_steps = seq_len // ML_STEP
    b0 = row0 // ML_STEP
    in_specs, args = [], []
    for d in range(2):
        for b in range(batch):
            step = (lambda j: n_steps - 1 - j) if d == 1 else (lambda j: j)
            blk = lambda j, b=b, step=step: b0 + b * n_steps + step(j)
            in_specs += [pl.BlockSpec((ML_STEP, 512), lambda j, blk=blk: (blk(j), 0)),
                         pl.BlockSpec((ML_STEP, 512), lambda j, blk=blk: (blk(j), 0)),
                         pl.BlockSpec((ML_STEP, 512), lambda j, blk=blk: (blk(j), CB_MV)),
                         pl.BlockSpec((ML_STEP, ML_GATE_W), lambda j, blk=blk: (blk(j), 0)),
                         pl.BlockSpec((1, ML_GATE_W, ML_STEP), lambda j, b=b, step=step: (b, 0, step(j)))]
            args += [q, k, u, g, gt]
    full = lambda shape: pl.BlockSpec(shape, lambda j: (0,) * len(shape))
    st_dims = [(2, batch, N_HEADS, HEAD_W, HEAD_W), (2, batch, N_HEADS, 1, HEAD_W), (2, batch, N_HEADS, 1, HEAD_W)]
    st_specs = [full(s) for s in st_dims]
    st_shapes = [jax.ShapeDtypeStruct(s, F32) for s in st_dims]
    out_specs, out_shapes = list(st_specs), list(st_shapes)
    if with_out:
        out_specs = [pl.BlockSpec((batch, ML_STEP, 512), lambda j: (0, j, 0)),
                     pl.BlockSpec((batch, ML_STEP, 512), lambda j: (0, n_steps - 1 - j, 0))] + out_specs
        out_shapes = [jax.ShapeDtypeStruct((batch, seq_len, 512), F32)] * 2 + out_shapes
    res = pl.pallas_call(
        functools.partial(_mlstm_kernel, batch, with_out, n_steps),
        grid=(n_steps,),
        in_specs=in_specs + st_specs,
        out_specs=out_specs,
        out_shape=out_shapes,
        scratch_shapes=[pltpu.VMEM(s, F32) for s in st_dims],
        compiler_params=_cparams(("arbitrary",)),
        name="mlstm",
    )(*args, *init)
    if with_out:
        return res[0], res[1], tuple(res[2:])
    return None, None, tuple(res)


def _hy_filter_kernel(n, feat_ref, w1_ref, b1_ref, w2_ref, b2_ref, w3_ref, freq_ref, decay_ref,
                      h_ref, norm_ref):
    i = pl.program_id(0)
    tm = h_ref.shape[1]
    freq = freq_ref[...]
    hid = jnp.sin(freq[0:1, :] * (_hdot(feat_ref[0], w1_ref[...]) + b1_ref[...]))
    hid = jnp.sin(freq[1:2, :] * (_hdot(hid, w2_ref[...]) + b2_ref[...]))
    hw = _hdot(hid, w3_ref[...])
    row = lax.broadcasted_iota(jnp.int32, (tm, BRANCH_W), 0) + i * tm
    decay = jnp.abs(decay_ref[...])
    h_fwd = hw[:tm, :BRANCH_W] * jnp.exp(-(row.astype(F32) / n) * decay[:, :BRANCH_W])
    h_bwd = hw[tm:, BRANCH_W:] * jnp.exp(-((n - row).astype(F32) / n) * decay[:, BRANCH_W:])
    h_bwd = jnp.where(row == 0, 0.0, h_bwd)
    h_ref[0] = h_fwd
    h_ref[1] = h_bwd

    @pl.when(i == 0)
    def _():
        norm_ref[...] = jnp.zeros_like(norm_ref)

    norm_ref[...] += jnp.concatenate([jnp.sum(jnp.abs(h_fwd), axis=0, keepdims=True),
                                      jnp.sum(jnp.abs(h_bwd), axis=0, keepdims=True)], axis=1)


def _hy_filter(n, feat2, w1p, b1, w2, b2, w3, freq, decay_flat):
    tm = feat2.shape[1] // 2
    full = lambda a: pl.BlockSpec(a.shape, lambda i: (0,) * a.ndim)
    return pl.pallas_call(
        functools.partial(_hy_filter_kernel, n),
        grid=(n // tm,),
        in_specs=[pl.BlockSpec((1, 2 * tm, LANE), lambda i: (i, 0, 0)), full(w1p), full(b1), full(w2), full(b2),
                  full(w3), full(freq), full(decay_flat)],
        out_specs=[pl.BlockSpec((2, tm, BRANCH_W), lambda i: (0, i, 0)),
                   pl.BlockSpec((1, 2 * BRANCH_W), lambda i: (0, 0))],
        out_shape=[jax.ShapeDtypeStruct((2, n, BRANCH_W), F32), jax.ShapeDtypeStruct((1, 2 * BRANCH_W), F32)],
        compiler_params=_cparams(("arbitrary",)),
        name="hyena_filter",
    )(feat2, w1p, b1, w2, b2, w3, freq, decay_flat)


def _split_bf16(x):
    hi = x.astype(BF16)
    return hi, (x - hi.astype(F32)).astype(BF16)


def _dot3(a, b):
    d = lambda p, q: jnp.dot(p, q, preferred_element_type=F32)
    return d(a[0], b[0]) + (d(a[0], b[1]) + d(a[1], b[0]))


def _dft_a_kernel(f_ref, x_ref, o_ref):
    o_ref[...] = _dot3((f_ref[0], f_ref[1]), _split_bf16(x_ref[...]))


def _dft_stage_a(table, x, tc=2048):
    k, cols = x.shape
    rows = table.shape[1]
    return pl.pallas_call(
        _dft_a_kernel,
        grid=(cols // tc,),
        in_specs=[pl.BlockSpec(table.shape, lambda j: (0, 0, 0)),
                  pl.BlockSpec((k, tc), lambda j: (0, j))],
        out_specs=pl.BlockSpec((rows, tc), lambda j: (0, j)),
        out_shape=jax.ShapeDtypeStruct((rows, cols), F32),
        compiler_params=_cparams(("parallel",)),
        name="hyena_dft_a",
    )(table, x)


def _dft_c_kernel(az_ref, ah_ref, g_ref, gt_ref, o_ref):
    x = jnp.concatenate([az_ref[:, 0].reshape(2 * LANE, BRANCH_W), ah_ref[:, 0].reshape(2 * LANE, BRANCH_W)],
                        axis=1)
    s = _dot3((g_ref[0, 0], g_ref[1, 0]), _split_bf16(x))
    zr, zi = s[:LANE, :BRANCH_W], s[LANE:, :BRANCH_W]
    hr, hi = s[:LANE, BRANCH_W:], s[LANE:, BRANCH_W:]
    prod = jnp.concatenate([zr * hr - zi * hi, zr * hi + zi * hr], axis=0)
    p = _dot3((gt_ref[0, 0], gt_ref[1, 0]), _split_bf16(prod))
    o_ref[:, 0] = p.reshape(2, LANE, BRANCH_W)


def _dft_stage_c(az, ah, g, gt):
    _, n1, _, c = az.shape
    slab = pl.BlockSpec((2, 1, LANE, c), lambda k: (0, k, 0, 0))
    tab = pl.BlockSpec((2, 1, 2 * LANE, 2 * LANE), lambda k: (0, k, 0, 0))
    return pl.pallas_call(
        _dft_c_kernel,
        grid=(n1,),
        in_specs=[slab, slab, tab, tab],
        out_specs=slab,
        out_shape=jax.ShapeDtypeStruct(az.shape, F32),
        compiler_params=_cparams(("parallel",)),
        name="hyena_dft_c",
    )(az, ah, g, gt)


def _hy_epilogue(y, z, x0, norm, skip):
    reps = y.shape[1] // BRANCH_W
    nsum = norm[:, :BRANCH_W] + norm[:, BRANCH_W:]
    inv = jnp.concatenate([1.0 / nsum] * reps, axis=1)
    sk = jnp.concatenate([skip] * reps, axis=1)
    return (x0 * (y * inv + z * sk)).astype(BF16)


def _dft_a_inv_kernel(f_ref, p_ref, z_ref, x0_ref, norm_ref, skip_ref, o_ref):
    y = _dot3((f_ref[0], f_ref[1]), _split_bf16(p_ref[...]))
    o_ref[...] = _hy_epilogue(y, z_ref[...], x0_ref[...], norm_ref[...], skip_ref[...])


def _dft_stage_a_inv(table, p, z, x0, norm, skip, tc=2048):
    rows, cols = p.shape
    zr = z.shape[0]
    col = lambda r: pl.BlockSpec((r, tc), lambda j: (0, j))
    return pl.pallas_call(
        _dft_a_inv_kernel,
        grid=(cols // tc,),
        in_specs=[pl.BlockSpec(table.shape, lambda j: (0, 0, 0)), col(rows), col(zr), col(zr),
                  pl.BlockSpec(norm.shape, lambda j: (0, 0)), pl.BlockSpec(skip.shape, lambda j: (0, 0))],
        out_specs=col(zr),
        out_shape=jax.ShapeDtypeStruct((zr, cols), BF16),
        compiler_params=_cparams(("parallel",)),
        name="hyena_dft_a_inv",
    )(table, p, z, x0, norm, skip)


def _dft_tables(n):
    big_l = 2 * n
    n1 = big_l // LANE
    n1h = n1 // 2
    two_pi = 2.0 * math.pi
    k1 = jnp.arange(n1, dtype=jnp.int32)
    th = two_pi * ((k1[:, None] * k1[None, :]) % n1).astype(F32) / n1
    c_full, s_full = jnp.cos(th), jnp.sin(th)
    ch, sh = c_full[:, :n1h], s_full[:, :n1h]
    fz = jnp.concatenate([jnp.concatenate([ch, sh], axis=1), jnp.concatenate([-sh, ch], axis=1)], axis=0)
    ff = jnp.concatenate([c_full, -s_full], axis=0)
    ct, st = ch.T / big_l, sh.T / big_l
    fi = jnp.concatenate([jnp.concatenate([ct, -st], axis=1), jnp.concatenate([st, ct], axis=1)], axis=0)
    s2 = jnp.arange(LANE, dtype=jnp.int32)
    k = k1[:, None, None] + n1 * s2[None, :, None]
    ph = two_pi * ((k * s2[None, None, :]) % big_l).astype(F32) / big_l
    gr, gi = jnp.cos(ph), -jnp.sin(ph)
    g = jnp.concatenate([jnp.concatenate([gr, -gi], axis=2), jnp.concatenate([gi, gr], axis=2)], axis=1)
    pair = lambda t: jnp.stack(_split_bf16(t))
    return pair(fz), pair(ff), pair(fi), pair(g), pair(jnp.swapaxes(g, 1, 2))


def _hy_small_kernel(batch, n, z_ref, x0_ref, h_ref, f_ref, fi_ref, norm_ref, skip_ref, o_ref):
    big_l = 2 * n
    f = f_ref[...]
    sz = _hdot(f[:, :n], jnp.concatenate([z_ref[b] for b in range(batch)], axis=1))
    sh = _hdot(f, jnp.concatenate([h_ref[0], h_ref[1]], axis=0))
    hr, hi = sh[:big_l], sh[big_l:]
    prods = []
    for b in range(batch):
        zr = sz[:big_l, b * BRANCH_W:(b + 1) * BRANCH_W]
        zi = sz[big_l:, b * BRANCH_W:(b + 1) * BRANCH_W]
        prods.append(jnp.concatenate([zr * hr - zi * hi, zr * hi + zi * hr], axis=0))
    y = _hdot(fi_ref[...], jnp.concatenate(prods, axis=1))
    for b in range(batch):
        o_ref[b] = _hy_epilogue(y[:, b * BRANCH_W:(b + 1) * BRANCH_W], z_ref[b], x0_ref[b],
                                norm_ref[...], skip_ref[...])


def _hy_small(z, x0, hfilt, norm, skip):
    batch, n, c = z.shape
    big_l = 2 * n
    kk = jnp.arange(big_l, dtype=jnp.int32)
    th = 2.0 * math.pi * ((kk[:, None] * kk[None, :]) % big_l).astype(F32) / big_l
    f = jnp.concatenate([jnp.cos(th), -jnp.sin(th)], axis=0)
    fi = jnp.concatenate([jnp.cos(th[:, :n]).T, -jnp.sin(th[:, :n]).T], axis=1) / big_l
    full = lambda a: pl.BlockSpec(a.shape, lambda i: (0,) * a.ndim)
    return pl.pallas_call(
        functools.partial(_hy_small_kernel, batch, n),
        grid=(1,),
        in_specs=[full(z), full(x0), full(hfilt), full(f), full(fi), full(norm), full(skip)],
        out_specs=full(z),
        out_shape=jax.ShapeDtypeStruct(z.shape, BF16),
        compiler_params=_cparams(("arbitrary",)),
        name="hyena_small",
    )(z, x0, hfilt, f, fi, norm, skip)


def _hy_features(n):
    tm = min(n, 512)
    bands = jnp.linspace(1e-4, HY_BANDS - 1, HY_BANDS, dtype=F32)

    def feats(pos):
        t = pos.astype(F32) / n
        ang = 2.0 * math.pi * t[:, None] * bands[None, :]
        feat = jnp.concatenate([t[:, None], jnp.cos(ang), jnp.sin(ang)], axis=-1)
        return jnp.pad(feat, ((0, 0), (0, LANE - HY_EMB))).reshape(n // tm, tm, LANE)

    r = jnp.arange(n, dtype=jnp.int32)
    return jnp.concatenate([feats(r), feats(n - r)], axis=1)


def _merge_kernel(hy_ref, mla_ref, dif_ref, hf_ref, hb_ref, mo_ref, gate_ref, wb_ref, mln_ref, o_ref):
    hsum = hf_ref[...] + hb_ref[...]
    mln = mln_ref[...]
    y_ml = jnp.concatenate(
        [_rms(hsum[:, h * HEAD_W:(h + 1) * HEAD_W]) * mln[:, h * HEAD_W:(h + 1) * HEAD_W]
         for h in range(N_HEADS)], axis=1)
    y_ml = _sigmoid(mo_ref[...]) * y_ml
    ys = (hy_ref[...], mla_ref[...], dif_ref[...], y_ml.astype(BF16))
    acc = None
    for i in range(4):
        gi = _sigmoid(gate_ref[:, i * D_MODEL:(i + 1) * D_MODEL].astype(F32))
        term = gi * jnp.dot(ys[i], wb_ref[i], preferred_element_type=F32)
        acc = term if acc is None else acc + term
    o_ref[...] = acc.astype(BF16)


def _merge(y_hy, y_mla, y_dif, h_f, h_b, u, gate, wb, mln, nrows, tm=256):
    row = pl.BlockSpec((tm, 512), lambda i: (i, 0))
    full = lambda a: pl.BlockSpec(a.shape, lambda i: (0,) * a.ndim)
    return pl.pallas_call(
        _merge_kernel,
        grid=(nrows // tm,),
        in_specs=[row, row, row, row, row,
                  pl.BlockSpec((tm, 512), lambda i: (i, CB_MO)),
                  pl.BlockSpec((tm, GATE_COLS), lambda i: (i, 0)),
                  full(wb), full(mln)],
        out_specs=pl.BlockSpec((tm, D_MODEL), lambda i: (i, 0)),
        out_shape=jax.ShapeDtypeStruct((nrows, D_MODEL), BF16),
        compiler_params=_cparams(("parallel",), 56),
        name="branch_merge",
    )(y_hy, y_mla, y_dif, h_f, h_b, u, gate, wb, mln)


def _out_kernel(m_ref, w_ref, x_ref, ga_ref, gain_ref, sc_ref, sh_ref, rw_ref, rb_ref,
                xo_ref, f_ref, idx_ref, gt_ref):
    y = jnp.dot(m_ref[...], w_ref[...], preferred_element_type=F32)
    x = x_ref[...] + ga_ref[0] * y
    xo_ref[...] = x
    f = _rms(x) * gain_ref[...] * (1.0 + sc_ref[0]) + sh_ref[0]
    tm = f.shape[0]
    for j in range(ROW_TILES):
        f_ref[pl.ds(j, tm, stride=ROW_TILES), :] = f[:, j * LANE:(j + 1) * LANE]
    logits = _bdot(f, rw_ref[...]) + rb_ref[...]
    lane = lax.broadcasted_iota(jnp.int32, logits.shape, 1)
    idx_out = jnp.zeros(logits.shape, jnp.int32)
    val_out = jnp.full(logits.shape, -jnp.inf, F32)
    work = logits
    for k in range(TOP_K):
        mx = jnp.max(work, axis=-1, keepdims=True)
        am = jnp.min(jnp.where(work == mx, lane, LANE), axis=-1, keepdims=True)
        idx_out = jnp.where(lane == k, am, idx_out)
        val_out = jnp.where(lane == k, mx, val_out)
        work = jnp.where(lane == am, -jnp.inf, work)
    e = jnp.exp(val_out - jnp.max(val_out, axis=-1, keepdims=True))
    gt_ref[...] = e / jnp.sum(e, axis=-1, keepdims=True)
    idx_ref[...] = idx_out


def _out_proj(merged, w_out, x_all, ga, gain, sc, sh, rw, rb, nrows, lat_rows, seq_rows, tm=256):
    grp = functools.partial(_group_of_tile, tm=tm, lat_rows=lat_rows, seq_rows=seq_rows)
    row = lambda w: pl.BlockSpec((tm, w), lambda i: (i, 0))
    full = lambda a: pl.BlockSpec(a.shape, lambda i: (0,) * a.ndim)
    mod = pl.BlockSpec((1, 1, D_MODEL), lambda i: (grp(i), 0, 0))
    return pl.pallas_call(
        _out_kernel,
        grid=(nrows // tm,),
        in_specs=[row(D_MODEL), full(w_out), row(D_MODEL), mod, full(gain), mod, mod, full(rw), full(rb)],
        out_specs=[row(D_MODEL), pl.BlockSpec((tm * ROW_TILES, LANE), lambda i: (i, 0)), row(LANE), row(LANE)],
        out_shape=[jax.ShapeDtypeStruct((nrows, D_MODEL), F32), jax.ShapeDtypeStruct((nrows * ROW_TILES, LANE), F32),
                   jax.ShapeDtypeStruct((nrows, LANE), jnp.int32), jax.ShapeDtypeStruct((nrows, LANE), F32)],
        compiler_params=_cparams(("parallel",)),
        name="out_proj_router",
    )(merged, w_out, x_all, ga, gain, sc, sh, rw, rb)


def _moe_kernel(blk_e_ref, used_ref, tok_ref, nxt_ref, f_hbm, wgu_ref, bgu_ref, wd_ref, bd_ref, o_ref, xbuf, sem):
    i = pl.program_id(0)
    last = pl.num_programs(0) - 1
    cur = i % 2

    buf_rows = MOE_BLOCK * ROW_TILES

    def start_rows(idx_ref, s, r0, r1):
        for r in range(r0, r1):
            pltpu.make_async_copy(f_hbm.at[idx_ref[0, 0, r]],
                                  xbuf.at[pl.ds(s * buf_rows + r * ROW_TILES, ROW_TILES), :], sem.at[s]).start()

    def wait_rows(s):
        whole = xbuf.at[pl.ds(s * buf_rows, buf_rows), :]
        pltpu.make_async_copy(whole, whole, sem.at[s]).wait()

    @pl.when(i == 0)
    def _():
        start_rows(tok_ref, 0, 0, MOE_BLOCK)

    wait_rows(cur)

    start_rows(nxt_ref, 1 - cur, 0, MOE_BLOCK)

    @pl.when(i >= used_ref[0])
    def _():
        o_ref[...] = jnp.zeros_like(o_ref)

    @pl.when(i < used_ref[0])
    def _():
        base = cur * buf_rows
        x = jnp.concatenate([xbuf[pl.ds(base + j, MOE_BLOCK, stride=ROW_TILES), :].astype(BF16)
                             for j in range(ROW_TILES)], axis=1)
        gu = jnp.dot(x, wgu_ref[0], preferred_element_type=F32) + bgu_ref[0]
        g = jnp.minimum(gu[:, :D_EXPERT], SWIGLU_LIMIT)
        u = jnp.clip(gu[:, D_EXPERT:], -SWIGLU_LIMIT, SWIGLU_LIMIT)
        act = g * _sigmoid(SWIGLU_ALPHA * g) * (u + 1.0)
        o_ref[...] = _bdot(act, wd_ref[0]) + bd_ref[0]

    @pl.when(i == last)
    def _():
        wait_rows(1 - cur)


def _moe_experts(f_all, blk_e, n_used, slot_tok, wgu, bgu, wd, bd):
    n_blocks = blk_e.shape[0]
    slots = n_blocks * MOE_BLOCK
    n_exp = wgu.shape[0]
    grid_spec = pltpu.PrefetchScalarGridSpec(
        num_scalar_prefetch=2,
        grid=(n_blocks,),
        in_specs=[pl.BlockSpec((1, 1, MOE_BLOCK), lambda i, be, nu: (i, 0, 0), memory_space=pltpu.SMEM),
                  pl.BlockSpec((1, 1, MOE_BLOCK), lambda i, be, nu: (jnp.minimum(i + 1, n_blocks - 1), 0, 0),
                               memory_space=pltpu.SMEM),
                  pl.BlockSpec(memory_space=pl.ANY),
                  pl.BlockSpec((1, D_MODEL, 2 * D_EXPERT), lambda i, be, nu: (be[i], 0, 0)),
                  pl.BlockSpec((1, 1, 2 * D_EXPERT), lambda i, be, nu: (be[i], 0, 0)),
                  pl.BlockSpec((1, D_EXPERT, D_MODEL), lambda i, be, nu: (be[i], 0, 0)),
                  pl.BlockSpec((1, 1, D_MODEL), lambda i, be, nu: (be[i], 0, 0))],
        out_specs=pl.BlockSpec((MOE_BLOCK, D_MODEL), lambda i, be, nu: (i, 0)),
        scratch_shapes=[pltpu.VMEM((2 * MOE_BLOCK * ROW_TILES, LANE), F32), pltpu.SemaphoreType.DMA((2,))],
    )
    tok3 = slot_tok.reshape(n_blocks, 1, MOE_BLOCK)
    return pl.pallas_call(
        _moe_kernel,
        grid_spec=grid_spec,
        out_shape=jax.ShapeDtypeStruct((slots, D_MODEL), F32),
        compiler_params=_cparams(("arbitrary",), 56),
        name="moe_experts",
    )(blk_e, n_used, tok3, tok3, f_all,
      wgu, bgu.reshape(n_exp, 1, 2 * D_EXPERT), wd, bd.reshape(n_exp, 1, D_MODEL))


def _combine_kernel(final, dest_ref, nxt_ref, y_hbm, x_ref, tg_ref, gf_ref, fn_ref, o_ref, ybuf, sem):
    tm = x_ref.shape[0]
    i = pl.program_id(0)
    last = pl.num_programs(0) - 1
    cur = i % 2

    def start_rows(idx_ref, s):
        for r in range(tm):
            for k in range(TOP_K):
                pltpu.make_async_copy(y_hbm.at[pl.ds(idx_ref[0, 0, r * TOP_K + k], 1), :],
                                      ybuf.at[s, k, pl.ds(r, 1), :], sem.at[s]).start()

    def wait_rows(s):
        for k in range(TOP_K):
            pltpu.make_async_copy(y_hbm.at[pl.ds(0, tm), :], ybuf.at[s, k], sem.at[s]).wait()

    @pl.when(i == 0)
    def _():
        start_rows(dest_ref, 0)

    start_rows(nxt_ref, 1 - cur)
    wait_rows(cur)
    tg = tg_ref[...]
    moe = None
    for k in range(TOP_K):
        term = tg[:, k:k + 1] * ybuf[cur, k]
        moe = term if moe is None else moe + term
    x = x_ref[...] + gf_ref[0] * moe
    if final:
        x = _rms(x) * fn_ref[...]
    o_ref[...] = x

    @pl.when(i == last)
    def _():
        wait_rows(1 - cur)


def _moe_combine(y_slots, dest, top_gate, x_all, gf, final_gain, final, nrows, lat_rows, seq_rows, tm=128):
    grp = functools.partial(_group_of_tile, tm=tm, lat_rows=lat_rows, seq_rows=seq_rows)
    nt = nrows // tm
    grid_spec = pl.GridSpec(
        grid=(nt,),
        in_specs=[pl.BlockSpec((1, 1, tm * TOP_K), lambda i: (i, 0, 0), memory_space=pltpu.SMEM),
                  pl.BlockSpec((1, 1, tm * TOP_K), lambda i: (jnp.minimum(i + 1, nt - 1), 0, 0),
                               memory_space=pltpu.SMEM),
                  pl.BlockSpec(memory_space=pl.ANY),
                  pl.BlockSpec((tm, D_MODEL), lambda i: (i, 0)),
                  pl.BlockSpec((tm, LANE), lambda i: (i, 0)),
                  pl.BlockSpec((1, 1, D_MODEL), lambda i: (grp(i), 0, 0)),
                  pl.BlockSpec((1, D_MODEL), lambda i: (0, 0))],
        out_specs=pl.BlockSpec((tm, D_MODEL), lambda i: (i, 0)),
        scratch_shapes=[pltpu.VMEM((2, TOP_K, tm, D_MODEL), F32), pltpu.SemaphoreType.DMA((2,))],
    )
    dest3 = dest.reshape(nt, 1, tm * TOP_K)
    return pl.pallas_call(
        functools.partial(_combine_kernel, final),
        grid_spec=grid_spec,
        out_shape=jax.ShapeDtypeStruct((nrows, D_MODEL), F32),
        compiler_params=_cparams(("arbitrary",)),
        name="moe_combine",
    )(dest3, dest3, y_slots, x_all, top_gate, gf, final_gain)


def _expert_onehot(idx):
    lane = lax.broadcasted_iota(jnp.int32, idx.shape, 1)
    oh = jnp.zeros(idx.shape, F32)
    for k in range(TOP_K):
        oh = oh + jnp.where(lane == idx[:, k:k + 1], 1.0, 0.0)
    return oh


def _rank_kernel(idx_ref, rank_ref, cnt_ref, carry):
    @pl.when(pl.program_id(0) == 0)
    def _():
        carry[...] = jnp.zeros_like(carry)

    oh = _expert_onehot(idx_ref[...])
    tm = oh.shape[0]
    r = lax.broadcasted_iota(jnp.int32, (tm, tm), 0)
    c = lax.broadcasted_iota(jnp.int32, (tm, tm), 1)
    earlier = jnp.where(c < r, 1.0, 0.0).astype(BF16)
    rank_ref[...] = jnp.dot(earlier, oh.astype(BF16), preferred_element_type=F32) + carry[...]
    carry[...] += jnp.sum(oh, axis=0, keepdims=True)
    cnt_ref[...] = carry[...]


def _dest_kernel(idx_ref, rank_ref, pstart_ref, dest_ref):
    idx = idx_ref[...]
    slot = rank_ref[...] + pstart_ref[...]
    lane = lax.broadcasted_iota(jnp.int32, idx.shape, 1)
    out = jnp.zeros(idx.shape, jnp.int32)
    for k in range(TOP_K):
        dk = jnp.sum(jnp.where(lane == idx[:, k:k + 1], slot, 0.0), axis=-1, keepdims=True)
        out = jnp.where(lane == k, dk.astype(jnp.int32), out)
    dest_ref[...] = out


def _route(top_idx128, tm=512):
    n = top_idx128.shape[0]
    a = n * TOP_K
    row = pl.BlockSpec((tm, LANE), lambda i: (i, 0))
    one = pl.BlockSpec((1, LANE), lambda i: (0, 0))
    rank, cnt = pl.pallas_call(
        _rank_kernel,
        grid=(n // tm,),
        in_specs=[row],
        out_specs=[row, one],
        out_shape=[jax.ShapeDtypeStruct((n, LANE), F32), jax.ShapeDtypeStruct((1, LANE), F32)],
        scratch_shapes=[pltpu.VMEM((1, LANE), F32)],
        compiler_params=_cparams(("arbitrary",)),
        name="route_rank",
    )(top_idx128)
    counts = cnt[0, :N_EXPERTS].astype(jnp.int32)
    padded = (counts + MOE_BLOCK - 1) // MOE_BLOCK * MOE_BLOCK
    pend = jnp.cumsum(padded)
    pstart = pend - padded
    pstart128 = jnp.pad(pstart.astype(F32), (0, LANE - N_EXPERTS)).reshape(1, LANE)
    dest = pl.pallas_call(
        _dest_kernel,
        grid=(n // tm,),
        in_specs=[row, row, one],
        out_specs=row,
        out_shape=jax.ShapeDtypeStruct((n, LANE), jnp.int32),
        compiler_params=_cparams(("parallel",)),
        name="route_dest",
    )(top_idx128, rank, pstart128)[:, :TOP_K]
    n_blocks = (a + N_EXPERTS * (MOE_BLOCK - 1)) // MOE_BLOCK + 1
    slots = n_blocks * MOE_BLOCK
    tok = jnp.broadcast_to(jnp.arange(n, dtype=jnp.int32)[:, None], (n, TOP_K))
    slot_tok = jnp.zeros((slots,), jnp.int32).at[dest.reshape(-1)].set(tok.reshape(-1))
    first_slot = jnp.arange(n_blocks, dtype=jnp.int32) * MOE_BLOCK
    blk_e = jnp.minimum(jnp.sum((pend[None, :] <= first_slot[:, None]).astype(jnp.int32), axis=1), N_EXPERTS - 1)
    n_used = (pend[-1:] // MOE_BLOCK).astype(jnp.int32)
    return blk_e, n_used, slot_tok, dest.astype(jnp.int32)


def _rope_tables(n_lat_rows, seq_len, ctx_rows):
    t = jnp.arange(seq_len, dtype=jnp.int32)
    inv = ROPE_THETA ** (-jnp.arange(0, 32, 2, dtype=F32) / 32)

    def cs(p):
        ang = p.astype(F32)[:, None] * inv[None, :]
        return jnp.cos(ang), jnp.sin(ang)

    cr, sr = cs(t // GRID_W)
    cc, sc = cs(t % GRID_W)
    c64 = jnp.concatenate([cr, cr, cc, cc], axis=1)
    s64 = jnp.concatenate([-sr, sr, -sc, sc], axis=1)
    reps = n_lat_rows // seq_len

    def build(c_half2, s_half2):
        c = jnp.concatenate([c64, c_half2], axis=1)
        s = jnp.concatenate([s64, s_half2], axis=1)
        c = jnp.concatenate([c] * reps + [jnp.ones((ctx_rows, LANE), F32)], axis=0)
        s = jnp.concatenate([s] * reps + [jnp.zeros((ctx_rows, LANE), F32)], axis=0)
        return c, s

    cd, sd = build(c64, s64)
    cm, sm = build(jnp.ones_like(c64), jnp.zeros_like(s64))
    return cd, sd, cm, sm


W_IN_SEGMENTS = ((320, 0, 512), (832, 512, 512), (1344, 1024, 512), (1856, 1536, 512), (2384, 2048, 512),
                 (2896, 2560, 512), (3408, 3072, 512), (3920, 3584, 512), (4432, 4096, 1536),
                 (0, 5632, 256), (256, 5888, 64), (2368, 5952, 16))
W_IN_USED = 5968
W_IN_GATE0 = 5968


def _w_in_layout_kernel(w_ref, small_ref, gate_ref):
    for src, dst, width in W_IN_SEGMENTS:
        small_ref[:, dst:dst + width] = w_ref[0, :, src:src + width].astype(BF16)
    small_ref[:, W_IN_USED:] = jnp.zeros((small_ref.shape[0], SMALL_COLS - W_IN_USED), BF16)
    gate_ref[...] = w_ref[0, :, W_IN_GATE0:].astype(BF16)


def _layout_w_in(w, layer, tk=128):
    _, d, ncol = w.shape
    return pl.pallas_call(
        _w_in_layout_kernel,
        grid=(d // tk,),
        in_specs=[pl.BlockSpec((1, tk, ncol), lambda i: (layer, i, 0))],
        out_specs=[pl.BlockSpec((tk, SMALL_COLS), lambda i: (i, 0)), pl.BlockSpec((tk, GATE_COLS), lambda i: (i, 0))],
        out_shape=[jax.ShapeDtypeStruct((d, SMALL_COLS), BF16), jax.ShapeDtypeStruct((d, GATE_COLS), BF16)],
        compiler_params=_cparams(("parallel",)),
        name="w_in_layout",
    )(w)


def _layout_w_qb(w):
    w = w.reshape(w.shape[0], N_HEADS, MLA_NOPE + MLA_ROPE)
    w = jnp.pad(w, ((0, 0), (0, 0), (0, MLA_QK_PAD - MLA_NOPE - MLA_ROPE)))
    return w.reshape(w.shape[0], N_HEADS * MLA_QK_PAD).astype(BF16)


def _layout_w_kvb(w):
    w = w.reshape(w.shape[0], N_HEADS, MLA_NOPE + HEAD_W)
    return jnp.concatenate([w[:, :, :MLA_NOPE].reshape(w.shape[0], -1),
                            w[:, :, MLA_NOPE:].reshape(w.shape[0], -1)], axis=1).astype(BF16)


def kernel(x, c, ctx, c_ctx, ada_w, ada_b, norm_mix, norm_ffn, w_in, hy_conv, hy_conv_b, hy_w1, hy_b1, hy_w2, hy_b2, hy_w3, hy_freq, hy_decay, hy_skip, mla_q_norm, mla_w_qb, mla_kv_norm, mla_w_kvb, dif_lambda, dif_norm, ml_conv_q, ml_conv_k, ml_gate_b, ml_norm, w_branch, w_out, router_w, router_b, exp_w_gu, exp_b_gu, exp_w_down, exp_b_down, final_norm):
    batch, n, d = x.shape
    n_ctx = ctx.shape[1]
    depth = w_in.shape[0]
    ctx_rows = batch * n_ctx
    lat_rows = batch * n
    t_all = ctx_rows + lat_rows

    assert batch + 1 <= SUBLANE
    cvec = jnp.zeros((SUBLANE, d), F32).at[:batch].set(c).at[batch].set(c_ctx)
    mod = _modulation(cvec, ada_w, ada_b)
    x_all = jnp.concatenate([x.reshape(lat_rows, d), ctx.reshape(ctx_rows, d)], axis=0)
    tabs = _rope_tables(lat_rows, n, ctx_rows)
    assert batch == 2, "the Hyena transform carries the two batches as one complex signal"
    fz_tab, ff_tab, fi_tab, g_tab, gt_tab = _dft_tables(n)
    feat_lat, feat_ctx = _hy_features(n), _hy_features(n_ctx)
    zero_state = (jnp.zeros((2, batch, N_HEADS, HEAD_W, HEAD_W), F32), jnp.zeros((2, batch, N_HEADS, 1, HEAD_W), F32),
                  jnp.zeros((2, batch, N_HEADS, 1, HEAD_W), F32))
    n1h = n // LANE
    wgu_all = exp_w_gu.astype(BF16).reshape(depth * N_EXPERTS, d, 2 * D_EXPERT)
    wd_all = exp_w_down.astype(BF16).reshape(depth * N_EXPERTS, D_EXPERT, d)
    bgu_all = exp_b_gu.reshape(depth * N_EXPERTS, 2 * D_EXPERT)
    bd_all = exp_b_down.reshape(depth * N_EXPERTS, d)

    for l in range(depth):
        last = l == depth - 1
        lam_init = 0.8 - 0.6 * math.exp(-0.3 * l)
        m6 = mod[l].reshape(SUBLANE, 6, 1, d)
        sh_a, sc_a, g_a, sh_f, sc_f, g_f = (m6[:, i] for i in range(6))
        w_small, w_gate = _layout_w_in(w_in, l)
        gain_mix = norm_mix[l].reshape(1, d)

        n_tok = lat_rows if last else t_all
        h_in = _norm_mod(x_all, gain_mix, sc_a, sh_a, lat_rows, n)
        u = _in_proj(h_in, w_small, F32, t_all)
        gate = _in_proj(h_in, w_gate, BF16, n_tok)

        mq, mk, mv, dq, dk, dv = _attn_prep(
            u, tabs, _layout_w_qb(mla_w_qb[l]), _layout_w_kvb(mla_w_kvb[l]),
            mla_q_norm[l].reshape(1, -1), mla_kv_norm[l].reshape(1, -1))
        ml_q, ml_k, ml_g = _ml_prep(u, ml_conv_q[l], ml_conv_k[l], ml_gate_b[l].reshape(1, 16), n, lat_rows, n_ctx)
        gt_lat = jnp.swapaxes(ml_g[:lat_rows].reshape(batch, n, ML_GATE_W), 1, 2)
        gt_ctx = jnp.swapaxes(ml_g[lat_rows:].reshape(batch, n_ctx, ML_GATE_W), 1, 2)

        lam_p = dif_lambda[l]
        dgain = dif_norm[l].reshape(1, HEAD_W)
        segs = [(lat_rows, n_ctx), (0, n)]
        y_mla = _attention(mq, mk, mv, 1, 0, n, segs, batch, MLA_QK_PAD)
        y_dif = _attention(dq, dk, dv, 2, 0, n, segs, batch, 2 * DIF_QK, lam_p, dgain, lam_init)

        hcf, hcb, st_ctx = _mlstm(ml_q, ml_k, u, ml_g, gt_ctx, zero_state, not last, lat_rows, n_ctx, batch)
        hlf, hlb, _ = _mlstm(ml_q, ml_k, u, ml_g, gt_lat, st_ctx, True, 0, n, batch)

        hw1 = jnp.pad(hy_w1[l], ((0, LANE - HY_EMB), (0, 0)))
        filt_args = (hw1, hy_b1[l].reshape(1, -1), hy_w2[l], hy_b2[l].reshape(1, -1), hy_w3[l], hy_freq[l],
                     hy_decay[l].reshape(1, -1))
        skip = hy_skip[l].reshape(1, -1)
        hconv_b = hy_conv_b[l].reshape(1, -1)
        z, x0 = _hy_prep(u, hy_conv[l], hconv_b, 0, lat_rows, n, lat_rows, n_ctx)
        hfilt, hnorm = _hy_filter(n, feat_lat, *filt_args)
        filt_full = hfilt.reshape(2 * n, BRANCH_W)
        n1 = 2 * n1h
        cols = LANE * BRANCH_W
        zr = z.reshape(batch * n1h, cols)
        az = _dft_stage_a(fz_tab, zr)
        ah = _dft_stage_a(ff_tab, filt_full.reshape(n1, cols))
        pz = _dft_stage_c(az.reshape(2, n1, LANE, BRANCH_W), ah.reshape(2, n1, LANE, BRANCH_W), g_tab, gt_tab)
        y_hy = _dft_stage_a_inv(fi_tab, pz.reshape(2 * n1, cols), zr, x0.reshape(batch * n1h, cols), hnorm,
                                skip).reshape(lat_rows, BRANCH_W)

        h_f, h_b = hlf.reshape(lat_rows, -1), hlb.reshape(lat_rows, -1)
        w_o = w_out[l].astype(BF16)
        gain_ffn = norm_ffn[l].reshape(1, d)
        rw = jnp.pad(router_w[l], ((0, 0), (0, LANE - N_EXPERTS))).astype(BF16)
        rb = jnp.concatenate([router_b[l], jnp.full((LANE - N_EXPERTS,), -jnp.inf, F32)]).reshape(1, LANE)

        if not last:
            cseg = [(lat_rows, n_ctx)]
            yc_mla = _attention(mq, mk, mv, 1, lat_rows, n_ctx, cseg, batch, MLA_QK_PAD)
            yc_dif = _attention(dq, dk, dv, 2, lat_rows, n_ctx, cseg, batch, 2 * DIF_QK, lam_p, dgain, lam_init)
            zc, x0c = _hy_prep(u, hy_conv[l], hconv_b, lat_rows, ctx_rows, n, lat_rows, n_ctx)
            hfc, hnc = _hy_filter(n_ctx, feat_ctx, *filt_args)
            yc_hy = _hy_small(zc.reshape(batch, n_ctx, BRANCH_W), x0c.reshape(batch, n_ctx, BRANCH_W), hfc, hnc,
                              skip).reshape(ctx_rows, BRANCH_W)
            cat = lambda a, b: jnp.concatenate([a, b], axis=0)
            y_hy, y_mla, y_dif = cat(y_hy, yc_hy), cat(y_mla, yc_mla), cat(y_dif, yc_dif)
            h_f, h_b = cat(h_f, hcf.reshape(ctx_rows, -1)), cat(h_b, hcb.reshape(ctx_rows, -1))

        merged = _merge(y_hy, y_mla, y_dif, h_f, h_b, u, gate, w_branch[l].astype(BF16), ml_norm[l].reshape(1, -1),
                        n_tok)
        xo, f_all, top_idx, top_gate = _out_proj(merged, w_o, x_all, g_a, gain_ffn, sc_f, sh_f, rw, rb,
                                                 n_tok, lat_rows, n)
        blk_e, n_used, slot_tok, dest = _route(top_idx)
        y_slots = _moe_experts(f_all.reshape(n_tok, ROW_TILES, LANE), blk_e + l * N_EXPERTS, n_used, slot_tok,
                               wgu_all, bgu_all, wd_all, bd_all)
        fgain = final_norm.reshape(1, d)
        x_all = _moe_combine(y_slots, dest, top_gate, xo, g_f, fgain, last, n_tok, lat_rows, n)

    return x_all.reshape(batch, n, d)
```

```python
import functools
import math

import jax
import jax.numpy as jnp
from jax import lax
from jax.experimental import pallas as pl
from jax.experimental.pallas import tpu as pltpu

F32 = jnp.float32
BF16 = jnp.bfloat16
HIGHEST = lax.Precision.HIGHEST

D_MODEL = 2048
GRID_W = 64
BRANCH_W = 512
N_HEADS = 4
HEAD_W = 128
V_EXT_W = 256
MLA_NOPE = 128
MLA_ROPE = 64
MLA_QK_PAD = 256
MLA_SCALE = (MLA_NOPE + MLA_ROPE) ** -0.5
LOG2E = math.log2(math.e)
DIF_QK = 64
ML_CHUNK = 64
ML_GATE_W = 16
N_EXPERTS = 32
TOP_K = 4
D_EXPERT = 1024
SWIGLU_LIMIT = 7.0
SWIGLU_ALPHA = 1.702
MOE_BLOCK = 256
ROPE_THETA = 10000.0
NORM_EPS = 1e-6
HY_BANDS = 16
HY_EMB = 1 + 2 * HY_BANDS
HY_HID = 64
LANE = 128
SUBLANE = 8
ROW_TILES = D_MODEL // LANE
SMALL_COLS = 6144
GATE_COLS = 4 * D_MODEL

CB_DK, CB_DV, CB_MK, CB_MV, CB_QA, CB_DQ, CB_MQ, CB_MO, CB_HX0, CB_HX1, CB_HV = range(11)
CB_CKV_256 = 22
CB_KPE_128 = 46


def _cparams(sem, vmem_mb=48):
    return pltpu.CompilerParams(dimension_semantics=sem, vmem_limit_bytes=vmem_mb * 1024 * 1024)


def _rms(x):
    return x * lax.rsqrt(jnp.mean(x * x, axis=-1, keepdims=True) + NORM_EPS)


def _sigmoid(x):
    return 1.0 / (1.0 + jnp.exp(-x))


def _silu(x):
    return x * _sigmoid(x)


def _log_sigmoid(x):
    return jnp.minimum(x, 0.0) - jnp.log(1.0 + jnp.exp(-jnp.abs(x)))


def _bdot(a, b):
    return jnp.dot(a.astype(BF16), b.astype(BF16), preferred_element_type=F32)


def _bdot_nt(a, b):
    return lax.dot_general(a.astype(BF16), b.astype(BF16), (((1,), (1,)), ((), ())),
                           preferred_element_type=F32)


def _hdot(a, b):
    return jnp.dot(a, b, preferred_element_type=F32, precision=HIGHEST)


def _mod_kernel(c_ref, w_ref, b_ref, o_ref):
    o_ref[0] = _bdot(_silu(c_ref[...]), w_ref[0]) + b_ref[0]


def _modulation(cvec, ada_w, ada_b):
    n_layers, d, six_d = ada_w.shape
    tn = 1024
    return pl.pallas_call(
        _mod_kernel,
        grid=(n_layers, six_d // tn),
        in_specs=[pl.BlockSpec((SUBLANE, d), lambda l, j: (0, 0)),
                  pl.BlockSpec((1, d, tn), lambda l, j: (l, 0, j)),
                  pl.BlockSpec((1, 1, tn), lambda l, j: (l, 0, j))],
        out_specs=pl.BlockSpec((1, SUBLANE, tn), lambda l, j: (l, 0, j)),
        out_shape=jax.ShapeDtypeStruct((n_layers, SUBLANE, six_d), F32),
        compiler_params=_cparams(("parallel", "parallel")),
        name="adaln_modulation",
    )(cvec, ada_w, ada_b.reshape(n_layers, 1, six_d))


def _group_of_tile(i, tm, lat_rows, seq_rows):
    return jnp.minimum(i // (seq_rows // tm), lat_rows // seq_rows)


def _norm_mod_kernel(x_ref, g_ref, sc_ref, sh_ref, o_ref):
    h = _rms(x_ref[...]) * g_ref[...]
    o_ref[...] = (h * (1.0 + sc_ref[0]) + sh_ref[0]).astype(BF16)


def _norm_mod(x_all, gain, sc, sh, lat_rows, seq_rows, tm=512):
    t_all, d = x_all.shape
    grp = functools.partial(_group_of_tile, tm=tm, lat_rows=lat_rows, seq_rows=seq_rows)
    return pl.pallas_call(
        _norm_mod_kernel,
        grid=(t_all // tm,),
        in_specs=[pl.BlockSpec((tm, d), lambda i: (i, 0)),
                  pl.BlockSpec((1, d), lambda i: (0, 0)),
                  pl.BlockSpec((1, 1, d), lambda i: (grp(i), 0, 0)),
                  pl.BlockSpec((1, 1, d), lambda i: (grp(i), 0, 0))],
        out_specs=pl.BlockSpec((tm, d), lambda i: (i, 0)),
        out_shape=jax.ShapeDtypeStruct((t_all, d), BF16),
        compiler_params=_cparams(("parallel",)),
        name="norm_mod",
    )(x_all, gain, sc, sh)


def _proj_kernel(h_ref, w_ref, o_ref):
    o_ref[...] = jnp.dot(h_ref[...], w_ref[...], preferred_element_type=F32).astype(o_ref.dtype)


def _in_proj(h, w, out_dtype, nrows, tm=512, tn=1024):
    d = h.shape[1]
    ncol = w.shape[1]
    return pl.pallas_call(
        _proj_kernel,
        grid=(ncol // tn, nrows // tm),
        in_specs=[pl.BlockSpec((tm, d), lambda j, i: (i, 0)),
                  pl.BlockSpec((d, tn), lambda j, i: (0, j))],
        out_specs=pl.BlockSpec((tm, tn), lambda j, i: (i, j)),
        out_shape=jax.ShapeDtypeStruct((nrows, ncol), out_dtype),
        compiler_params=_cparams(("parallel", "arbitrary")),
        name="in_proj",
    )(h, w)


def _rope(x, c, s):
    w = x.shape[-1]
    lane = lax.broadcasted_iota(jnp.int32, x.shape, 1)
    up = pltpu.roll(x, w - 16, 1)
    dn = pltpu.roll(x, 16, 1)
    return x * c + jnp.where((lane % 32) < 16, up, dn) * s


def _attn_prep_kernel(dk_ref, dv_ref, qa_ref, dq_ref, ckv_ref, kpe_ref, cd_ref, sd_ref, cm_ref, sm_ref,
                      wqb_ref, wkvb_ref, qn_ref, kvn_ref,
                      mq_ref, mk_ref, mv_ref, dqo_ref, dko_ref, dvo_ref):
    cd4 = jnp.concatenate([cd_ref[...]] * N_HEADS, axis=1)
    sd4 = jnp.concatenate([sd_ref[...]] * N_HEADS, axis=1)
    cm, sm = cm_ref[...], sm_ref[...]
    dqo_ref[...] = (_rope(dq_ref[...], cd4, sd4) * (DIF_QK ** -0.5 * LOG2E)).astype(BF16)
    dko_ref[...] = _rope(dk_ref[...], cd4, sd4).astype(BF16)
    dv = dv_ref[...]
    vlane = lax.broadcasted_iota(jnp.int32, (dv.shape[0], V_EXT_W - HEAD_W), 1)
    ones_col = jnp.where(vlane == 0, 1.0, 0.0).astype(BF16)

    q = _bdot(_rms(qa_ref[...]) * qn_ref[...], wqb_ref[...])
    kv = _bdot(_rms(ckv_ref[...]) * kvn_ref[...], wkvb_ref[...])
    kpe = kpe_ref[...]
    lane = lax.broadcasted_iota(jnp.int32, kpe.shape, 1)
    kpe = _rope(jnp.where(lane < MLA_ROPE, kpe, 0.0), cm, sm).astype(BF16)
    for h in range(N_HEADS):
        o = h * MLA_QK_PAD
        mq_ref[:, o:o + MLA_NOPE] = (q[:, o:o + MLA_NOPE] * (MLA_SCALE * LOG2E)).astype(BF16)
        mq_ref[:, o + MLA_NOPE:o + MLA_QK_PAD] = (
            _rope(q[:, o + MLA_NOPE:o + MLA_QK_PAD], cm, sm) * (MLA_SCALE * LOG2E)).astype(BF16)
        mk_ref[:, o:o + MLA_NOPE] = kv[:, h * MLA_NOPE:(h + 1) * MLA_NOPE].astype(BF16)
        mk_ref[:, o + MLA_NOPE:o + MLA_QK_PAD] = kpe
    for h in range(N_HEADS):
        o = h * V_EXT_W
        mv_ref[:, o:o + HEAD_W] = kv[:, (N_HEADS + h) * HEAD_W:(N_HEADS + h + 1) * HEAD_W].astype(BF16)
        mv_ref[:, o + HEAD_W:o + V_EXT_W] = ones_col
        dvo_ref[:, o:o + HEAD_W] = dv[:, h * HEAD_W:(h + 1) * HEAD_W].astype(BF16)
        dvo_ref[:, o + HEAD_W:o + V_EXT_W] = ones_col


def _attn_prep(u, tabs, wqb, wkvb, qn, kvn, tm=256):
    t_all = u.shape[0]
    cd, sd, cm, sm = tabs
    col = lambda w, cb: pl.BlockSpec((tm, w), lambda i, cb=cb: (i, cb))
    tab = pl.BlockSpec((tm, LANE), lambda i: (i, 0))
    full = lambda a: pl.BlockSpec(a.shape, lambda i: (0,) * a.ndim)
    out = lambda w: pl.BlockSpec((tm, w), lambda i: (i, 0))
    shp = lambda w: jax.ShapeDtypeStruct((t_all, w), BF16)
    return pl.pallas_call(
        _attn_prep_kernel,
        grid=(t_all // tm,),
        in_specs=[col(512, CB_DK), col(512, CB_DV), col(512, CB_QA), col(512, CB_DQ),
                  col(256, CB_CKV_256), col(128, CB_KPE_128), tab, tab, tab, tab,
                  full(wqb), full(wkvb), full(qn), full(kvn)],
        out_specs=[out(1024), out(1024), out(1024), out(512), out(512), out(1024)],
        out_shape=[shp(1024), shp(1024), shp(1024), shp(512), shp(512), shp(1024)],
        compiler_params=_cparams(("parallel",)),
        name="attn_prep",
    )(u, u, u, u, u, u, cd, sd, cm, sm, wqb, wkvb, qn, kvn)


def _conv3(x, prev_blk, next_blk, w, first, last):
    tm = x.shape[0]
    row = lax.broadcasted_iota(jnp.int32, x.shape, 0)
    prev_row = jnp.where(first, 0.0, prev_blk[SUBLANE - 1:SUBLANE, :])
    next_row = jnp.where(last, 0.0, next_blk[0:1, :])
    xm = jnp.where(row == 0, prev_row, pltpu.roll(x, 1, 0))
    xp = jnp.where(row == tm - 1, next_row, pltpu.roll(x, tm - 1, 0))
    return xm * w[0:1, :] + x * w[1:2, :] + xp * w[2:3, :]


def _seq_edges(tile, tm, seq_rows, lat_rows, ctx_len):
    r = tile * tm
    in_lat = r < lat_rows
    pos = jnp.where(in_lat, r % seq_rows, (r - lat_rows) % ctx_len)
    length = jnp.where(in_lat, seq_rows, ctx_len)
    return pos == 0, pos + tm == length


def _ml_prep_kernel(seq_rows, lat_rows, ctx_len,
                    q_ref, qp_ref, qn_ref, k_ref, kp_ref, kn_ref, gblk_ref, wq_ref, wk_ref, gb_ref,
                    qo_ref, ko_ref, go_ref):
    tm = q_ref.shape[0]
    first, last = _seq_edges(pl.program_id(0), tm, seq_rows, lat_rows, ctx_len)
    qo_ref[...] = _silu(_conv3(q_ref[...], qp_ref[...], qn_ref[...], wq_ref[...], first, last))
    ko_ref[...] = _silu(_conv3(k_ref[...], kp_ref[...], kn_ref[...], wk_ref[...], first, last)) * HEAD_W ** -0.5
    go_ref[...] = gblk_ref[:, MLA_ROPE:MLA_ROPE + 16] + gb_ref[...]


def _halo_specs(tm, w, cb, t_all, r0=0):
    nb8 = t_all // SUBLANE
    r8 = tm // SUBLANE
    return [pl.BlockSpec((tm, w), lambda i: (i + r0, cb)),
            pl.BlockSpec((SUBLANE, w), lambda i: (jnp.maximum((i + r0) * r8 - 1, 0), cb)),
            pl.BlockSpec((SUBLANE, w), lambda i: (jnp.minimum((i + r0 + 1) * r8, nb8 - 1), cb))]


def _ml_prep(u, wq, wk, gb, seq_rows, lat_rows, ctx_len, tm=256):
    t_all = u.shape[0]
    full = lambda a: pl.BlockSpec(a.shape, lambda i: (0,) * a.ndim)
    return pl.pallas_call(
        functools.partial(_ml_prep_kernel, seq_rows, lat_rows, ctx_len),
        grid=(t_all // tm,),
        in_specs=_halo_specs(tm, 512, CB_MQ, t_all) + _halo_specs(tm, 512, CB_MK, t_all)
        + [pl.BlockSpec((tm, LANE), lambda i: (i, CB_KPE_128)), full(wq), full(wk), full(gb)],
        out_specs=[pl.BlockSpec((tm, 512), lambda i: (i, 0)), pl.BlockSpec((tm, 512), lambda i: (i, 0)),
                   pl.BlockSpec((tm, ML_GATE_W), lambda i: (i, 0))],
        out_shape=[jax.ShapeDtypeStruct((t_all, 512), F32), jax.ShapeDtypeStruct((t_all, 512), F32),
                   jax.ShapeDtypeStruct((t_all, ML_GATE_W), F32)],
        compiler_params=_cparams(("parallel",)),
        name="mlstm_prep",
    )(u, u, u, u, u, u, u, wq, wk, gb)


def _hy_prep_kernel(r0, seq_rows, lat_rows, ctx_len,
                    a_ref, ap_ref, an_ref, b_ref, bp_ref, bn_ref, c_ref, cp_ref, cn_ref, w_ref, bias_ref,
                    z_ref, x0_ref):
    tm = a_ref.shape[0]
    first, last = _seq_edges(pl.program_id(0) + r0, tm, seq_rows, lat_rows, ctx_len)
    w, bias = w_ref[...], bias_ref[...]
    outs = []
    for s, (m, p, n) in enumerate(((a_ref, ap_ref, an_ref), (b_ref, bp_ref, bn_ref), (c_ref, cp_ref, cn_ref))):
        sl = slice(s * BRANCH_W, (s + 1) * BRANCH_W)
        outs.append(_conv3(m[...], p[...], n[...], w[:, sl], first, last) + bias[:, sl])
    x0_ref[...] = outs[0]
    z_ref[...] = outs[2] * outs[1]


def _hy_prep(u, w, bias, row0, nrows, seq_rows, lat_rows, ctx_len, tm=256):
    t_all = u.shape[0]
    r0 = row0 // tm
    full = lambda a: pl.BlockSpec(a.shape, lambda i: (0,) * a.ndim)
    halo = lambda cb: _halo_specs(tm, 512, cb, t_all, r0)
    return pl.pallas_call(
        functools.partial(_hy_prep_kernel, r0, seq_rows, lat_rows, ctx_len),
        grid=(nrows // tm,),
        in_specs=halo(CB_HX0) + halo(CB_HX1) + halo(CB_HV) + [full(w), full(bias)],
        out_specs=[pl.BlockSpec((tm, 512), lambda i: (i, 0)), pl.BlockSpec((tm, 512), lambda i: (i, 0))],
        out_shape=[jax.ShapeDtypeStruct((nrows, 512), F32), jax.ShapeDtypeStruct((nrows, 512), F32)],
        compiler_params=_cparams(("parallel",)),
        name="hyena_prep",
    )(u, u, u, u, u, u, u, u, u, w, bias)


def _lane_fold(x, op):
    parts = [x[:, j * LANE:(j + 1) * LANE] for j in range(x.shape[1] // LANE)]
    while len(parts) > 1:
        parts = [op(parts[i], parts[i + 1]) for i in range(0, len(parts) - 1, 2)] + (
            [parts[-1]] if len(parts) % 2 else [])
    return parts[0]


def _attn_kernel(n_maps, seg_lens, tk, lam_init, *refs):
    q_ref = refs[0]
    kv_refs = refs[1:1 + 2 * len(seg_lens)]
    pos = 1 + 2 * len(seg_lens)
    if n_maps == 2:
        lam_ref, gain_ref = refs[pos], refs[pos + 1]
        pos += 2
    o_ref = refs[pos]

    q = q_ref[...]
    tq = q.shape[0]
    if n_maps == 1:
        qs = [q]
    else:
        lane = lax.broadcasted_iota(jnp.int32, q.shape, 1)
        zero = jnp.zeros_like(q)
        qs = [jnp.where(lane < DIF_QK, q, zero), jnp.where(lane >= DIF_QK, q, zero)]

    def step(k, v, carry):
        new = []
        for qm, (m, acc) in zip(qs, carry):
            s = lax.dot_general(qm, k, (((1,), (1,)), ((), ())), preferred_element_type=F32)
            m_new = jnp.maximum(m, jnp.max(_lane_fold(s, jnp.maximum), axis=-1, keepdims=True))
            p = jnp.exp2((s - m_new).astype(BF16))
            alpha = jnp.exp2(m - m_new)
            new.append((m_new, alpha * acc + jnp.dot(p, v, preferred_element_type=F32)))
        return tuple(new)

    carry = tuple((jnp.full((tq, 1), -jnp.inf, F32), jnp.zeros((tq, V_EXT_W), F32)) for _ in range(n_maps))
    for si, slen in enumerate(seg_lens):
        k_ref, v_ref = kv_refs[2 * si], kv_refs[2 * si + 1]
        chunk = min(tk, slen)

        def body(c, carry, k_ref=k_ref, v_ref=v_ref, chunk=chunk):
            start = pl.multiple_of(c * chunk, chunk)
            return step(k_ref[pl.ds(start, chunk), :], v_ref[pl.ds(start, chunk), :], carry)

        trips = slen // chunk
        carry = lax.fori_loop(0, trips, body, carry, unroll=math.gcd(trips, 8))

    outs = [acc[:, :HEAD_W] / acc[:, HEAD_W:HEAD_W + 1] for (_, acc) in carry]
    if n_maps == 1:
        o_ref[...] = outs[0].astype(o_ref.dtype)
    else:
        lp = lam_ref[...]
        lam = (jnp.exp(jnp.sum(lp[0:1] * lp[1:2], axis=-1, keepdims=True))
               - jnp.exp(jnp.sum(lp[2:3] * lp[3:4], axis=-1, keepdims=True)) + lam_init)
        o = outs[0] - lam * outs[1]
        o_ref[...] = (_rms(o) * gain_ref[...] * (1.0 - lam_init)).astype(o_ref.dtype)


def _attention(q, k, v, n_maps, q_row0, q_len, segs, batch, qk_w, lam=None, gain=None, lam_init=0.0,
               tq=1024, tk=512):
    tq = min(tq, q_len)
    nq = q_len // tq
    qb0 = q_row0 // tq
    in_specs = [pl.BlockSpec((tq, qk_w), lambda b, h, i: (qb0 + b * nq + i, h))]
    args = [q]
    for (r0, slen) in segs:
        kb0 = r0 // slen
        in_specs.append(pl.BlockSpec((slen, qk_w), lambda b, h, i, kb0=kb0: (kb0 + b, h)))
        in_specs.append(pl.BlockSpec((slen, V_EXT_W), lambda b, h, i, kb0=kb0: (kb0 + b, h)))
        args += [k, v]
    if n_maps == 2:
        in_specs += [pl.BlockSpec(lam.shape, lambda b, h, i: (0, 0)),
                     pl.BlockSpec(gain.shape, lambda b, h, i: (0, 0))]
        args += [lam, gain]
    return pl.pallas_call(
        functools.partial(_attn_kernel, n_maps, tuple(s for _, s in segs), tk, lam_init),
        grid=(batch, N_HEADS, nq),
        in_specs=in_specs,
        out_specs=pl.BlockSpec((tq, HEAD_W), lambda b, h, i: (b * nq + i, h)),
        out_shape=jax.ShapeDtypeStruct((batch * q_len, N_HEADS * HEAD_W), BF16),
        compiler_params=_cparams(("parallel", "parallel", "arbitrary")),
        name="attention_maps%d" % n_maps,
    )(*args)


ML_STEP = 256


def _mlstm_chunk(rev, o, q_ref, k_ref, v_ref, g_ref, gt_ref, state, d, b, h_ref):
    ti = lax.broadcasted_iota(jnp.int32, (ML_CHUNK, ML_CHUNK), 0)
    si = lax.broadcasted_iota(jnp.int32, (ML_CHUNK, ML_CHUNK), 1)
    tri = (si >= ti) if rev else (si <= ti)
    tri_f = tri.astype(F32)
    tri_t = ((ti >= si) if rev else (ti <= si)).astype(F32)
    gcol = g_ref[o:o + ML_CHUNK, :]
    grow = gt_ref[0, :, o:o + ML_CHUNK]
    li_col = gcol[:, d * 8:d * 8 + 4]
    lf_col = _log_sigmoid(gcol[:, d * 8 + 4:d * 8 + 8])
    li_row = grow[d * 8:d * 8 + 4, :]
    lf_row = _log_sigmoid(grow[d * 8 + 4:d * 8 + 8, :])
    bc_col = _hdot(tri_f, lf_col)
    bc_row = _hdot(lf_row, tri_t)
    f_tot = jnp.sum(lf_col, axis=0, keepdims=True)
    for h in range(N_HEADS):
        hs = slice(h * HEAD_W, (h + 1) * HEAD_W)
        qc = q_ref[o:o + ML_CHUNK, hs]
        kc = k_ref[o:o + ML_CHUNK, hs]
        vc = v_ref[o:o + ML_CHUNK, hs]
        bcc, bcr = bc_col[:, h:h + 1], bc_row[h:h + 1, :]
        icol, irow = li_col[:, h:h + 1], li_row[h:h + 1, :]
        fh = f_tot[:, h:h + 1]
        c_mat, n_vec, m = state[d, b, h]
        g_end_r = fh - bcr + irow
        g_end_c = fh - bcc + icol
        m_new = jnp.maximum(fh + m, jnp.max(g_end_r, axis=-1, keepdims=True))
        decay = jnp.exp(fh + m - m_new)
        wk = jnp.exp(g_end_c - m_new)
        kw = kc * wk
        upd = lax.dot_general(kw.astype(BF16), vc.astype(BF16), (((0,), (0,)), ((), ())),
                              preferred_element_type=F32)
        if h_ref is not None:
            inter = bcc + m
            log_d = jnp.where(tri, bcc - bcr + irow, -jnp.inf)
            m_t = jnp.maximum(inter, jnp.max(log_d, axis=-1, keepdims=True))
            s = _bdot_nt(qc, kc) * jnp.exp(log_d - m_t)
            w_inter = jnp.exp(inter - m_t)
            num = _bdot(s, vc) + w_inter * _bdot(qc, c_mat)
            den = jnp.sum(s, axis=-1, keepdims=True) + w_inter * jnp.sum(qc * n_vec, axis=-1, keepdims=True)
            h_ref[b, o:o + ML_CHUNK, hs] = num / jnp.maximum(jnp.abs(den), jnp.exp(-m_t))
        state[d, b, h] = (decay * c_mat + upd, decay * n_vec + jnp.sum(kw, axis=0, keepdims=True), m_new)


def _mlstm_kernel(batch, with_out, n_steps, *refs):
    n_in = 2 * batch * 5
    chain_refs = refs[:n_in]
    c0_ref, n0_ref, m0_ref = refs[n_in:n_in + 3]
    pos = n_in + 3
    h_refs = (None, None)
    if with_out:
        h_refs = (refs[pos], refs[pos + 1])
        pos += 2
    cf_ref, nf_ref, mf_ref, c_scr, n_scr, m_scr = refs[pos:pos + 6]
    j = pl.program_id(0)

    @pl.when(j == 0)
    def _():
        c_scr[...] = c0_ref[...]
        n_scr[...] = n0_ref[...]
        m_scr[...] = m0_ref[...]

    keys = [(d, b, h) for d in range(2) for b in range(batch) for h in range(N_HEADS)]
    state = {key: (c_scr[key], n_scr[key], m_scr[key][:, 0:1]) for key in keys}
    n_chunks = ML_STEP // ML_CHUNK
    for ci in range(n_chunks):
        for d in range(2):
            c = (n_chunks - 1 - ci) if d == 1 else ci
            for b in range(batch):
                q_ref, k_ref, v_ref, g_ref, gt_ref = chain_refs[(d * batch + b) * 5:(d * batch + b + 1) * 5]
                _mlstm_chunk(d == 1, c * ML_CHUNK, q_ref, k_ref, v_ref, g_ref, gt_ref, state, d, b, h_refs[d])
    for key in keys:
        c_mat, n_vec, m = state[key]
        c_scr[key] = c_mat
        n_scr[key] = n_vec
        m_scr[key] = jnp.broadcast_to(m, (1, HEAD_W))

    @pl.when(j == n_steps - 1)
    def _():
        cf_ref[...] = c_scr[...]
        nf_ref[...] = n_scr[...]
        mf_ref[...] = m_scr[...]


def _mlstm(q, k, u, g, gt, init, with_out, row0, seq_len, batch):
    n_steps = seq_len // ML_STEP
    b0 = row0 // ML_STEP
    in_specs, args = [], []
    for d in range(2):
        for b in range(batch):
            step = (lambda j: n_steps - 1 - j) if d == 1 else (lambda j: j)
            blk = lambda j, b=b, step=step: b0 + b * n_steps + step(j)
            in_specs += [pl.BlockSpec((ML_STEP, 512), lambda j, blk=blk: (blk(j), 0)),
                         pl.BlockSpec((ML_STEP, 512), lambda j, blk=blk: (blk(j), 0)),
                         pl.BlockSpec((ML_STEP, 512), lambda j, blk=blk: (blk(j), CB_MV)),
                         pl.BlockSpec((ML_STEP, ML_GATE_W), lambda j, blk=blk: (blk(j), 0)),
                         pl.BlockSpec((1, ML_GATE_W, ML_STEP), lambda j, b=b, step=step: (b, 0, step(j)))]
            args += [q, k, u, g, gt]
    full = lambda shape: pl.BlockSpec(shape, lambda j: (0,) * len(shape))
    st_dims = [(2, batch, N_HEADS, HEAD_W, HEAD_W), (2, batch, N_HEADS, 1, HEAD_W), (2, batch, N_HEADS, 1, HEAD_W)]
    st_specs = [full(s) for s in st_dims]
    st_shapes = [jax.ShapeDtypeStruct(s, F32) for s in st_dims]
    out_specs, out_shapes = list(st_specs), list(st_shapes)
    if with_out:
        out_specs = [pl.BlockSpec((batch, ML_STEP, 512), lambda j: (0, j, 0)),
                     pl.BlockSpec((batch, ML_STEP, 512), lambda j: (0, n_steps - 1 - j, 0))] + out_specs
        out_shapes = [jax.ShapeDtypeStruct((batch, seq_len, 512), F32)] * 2 + out_shapes
    res = pl.pallas_call(
        functools.partial(_mlstm_kernel, batch, with_out, n_steps),
        grid=(n_steps,),
        in_specs=in_specs + st_specs,
        out_specs=out_specs,
        out_shape=out_shapes,
        scratch_shapes=[pltpu.VMEM(s, F32) for s in st_dims],
        compiler_params=_cparams(("arbitrary",)),
        name="mlstm",
    )(*args, *init)
    if with_out:
        return res[0], res[1], tuple(res[2:])
    return None, None, tuple(res)


def _hy_filter_kernel(n, feat_ref, w1_ref, b1_ref, w2_ref, b2_ref, w3_ref, freq_ref, decay_ref,
                      h_ref, norm_ref):
    i = pl.program_id(0)
    tm = h_ref.shape[1]
    freq = freq_ref[...]
    hid = jnp.sin(freq[0:1, :] * (_hdot(feat_ref[0], w1_ref[...]) + b1_ref[...]))
    hid = jnp.sin(freq[1:2, :] * (_hdot(hid, w2_ref[...]) + b2_ref[...]))
    hw = _hdot(hid, w3_ref[...])
    row = lax.broadcasted_iota(jnp.int32, (tm, BRANCH_W), 0) + i * tm
    decay = jnp.abs(decay_ref[...])
    h_fwd = hw[:tm, :BRANCH_W] * jnp.exp(-(row.astype(F32) / n) * decay[:, :BRANCH_W])
    h_bwd = hw[tm:, BRANCH_W:] * jnp.exp(-((n - row).astype(F32) / n) * decay[:, BRANCH_W:])
    h_bwd = jnp.where(row == 0, 0.0, h_bwd)
    h_ref[0] = h_fwd
    h_ref[1] = h_bwd

    @pl.when(i == 0)
    def _():
        norm_ref[...] = jnp.zeros_like(norm_ref)

    norm_ref[...] += jnp.concatenate([jnp.sum(jnp.abs(h_fwd), axis=0, keepdims=True),
                                      jnp.sum(jnp.abs(h_bwd), axis=0, keepdims=True)], axis=1)


def _hy_filter(n, feat2, w1p, b1, w2, b2, w3, freq, decay_flat):
    tm = feat2.shape[1] // 2
    full = lambda a: pl.BlockSpec(a.shape, lambda i: (0,) * a.ndim)
    return pl.pallas_call(
        functools.partial(_hy_filter_kernel, n),
        grid=(n // tm,),
        in_specs=[pl.BlockSpec((1, 2 * tm, LANE), lambda i: (i, 0, 0)), full(w1p), full(b1), full(w2), full(b2),
                  full(w3), full(freq), full(decay_flat)],
        out_specs=[pl.BlockSpec((2, tm, BRANCH_W), lambda i: (0, i, 0)),
                   pl.BlockSpec((1, 2 * BRANCH_W), lambda i: (0, 0))],
        out_shape=[jax.ShapeDtypeStruct((2, n, BRANCH_W), F32), jax.ShapeDtypeStruct((1, 2 * BRANCH_W), F32)],
        compiler_params=_cparams(("arbitrary",)),
        name="hyena_filter",
    )(feat2, w1p, b1, w2, b2, w3, freq, decay_flat)


def _split_bf16(x):
    hi = x.astype(BF16)
    return hi, (x - hi.astype(F32)).astype(BF16)


def _dot3(a, b):
    d = lambda p, q: jnp.dot(p, q, preferred_element_type=F32)
    return d(a[0], b[0]) + (d(a[0], b[1]) + d(a[1], b[0]))


def _dft_a_kernel(f_ref, x_ref, o_ref):
    o_ref[...] = _dot3((f_ref[0], f_ref[1]), _split_bf16(x_ref[...]))


def _dft_stage_a(table, x, tc=2048):
    k, cols = x.shape
    rows = table.shape[1]
    return pl.pallas_call(
        _dft_a_kernel,
        grid=(cols // tc,),
        in_specs=[pl.BlockSpec(table.shape, lambda j: (0, 0, 0)),
                  pl.BlockSpec((k, tc), lambda j: (0, j))],
        out_specs=pl.BlockSpec((rows, tc), lambda j: (0, j)),
        out_shape=jax.ShapeDtypeStruct((rows, cols), F32),
        compiler_params=_cparams(("parallel",)),
        name="hyena_dft_a",
    )(table, x)


def _dft_c_kernel(az_ref, ah_ref, g_ref, gt_ref, o_ref):
    x = jnp.concatenate([az_ref[:, 0].reshape(2 * LANE, BRANCH_W), ah_ref[:, 0].reshape(2 * LANE, BRANCH_W)],
                        axis=1)
    s = _dot3((g_ref[0, 0], g_ref[1, 0]), _split_bf16(x))
    zr, zi = s[:LANE, :BRANCH_W], s[LANE:, :BRANCH_W]
    hr, hi = s[:LANE, BRANCH_W:], s[LANE:, BRANCH_W:]
    prod = jnp.concatenate([zr * hr - zi * hi, zr * hi + zi * hr], axis=0)
    p = _dot3((gt_ref[0, 0], gt_ref[1, 0]), _split_bf16(prod))
    o_ref[:, 0] = p.reshape(2, LANE, BRANCH_W)


def _dft_stage_c(az, ah, g, gt):
    _, n1, _, c = az.shape
    slab = pl.BlockSpec((2, 1, LANE, c), lambda k: (0, k, 0, 0))
    tab = pl.BlockSpec((2, 1, 2 * LANE, 2 * LANE), lambda k: (0, k, 0, 0))
    return pl.pallas_call(
        _dft_c_kernel,
        grid=(n1,),
        in_specs=[slab, slab, tab, tab],
        out_specs=slab,
        out_shape=jax.ShapeDtypeStruct(az.shape, F32),
        compiler_params=_cparams(("parallel",)),
        name="hyena_dft_c",
    )(az, ah, g, gt)


def _hy_epilogue(y, z, x0, norm, skip):
    reps = y.shape[1] // BRANCH_W
    nsum = norm[:, :BRANCH_W] + norm[:, BRANCH_W:]
    inv = jnp.concatenate([1.0 / nsum] * reps, axis=1)
    sk = jnp.concatenate([skip] * reps, axis=1)
    return (x0 * (y * inv + z * sk)).astype(BF16)


def _dft_a_inv_kernel(f_ref, p_ref, z_ref, x0_ref, norm_ref, skip_ref, o_ref):
    y = _dot3((f_ref[0], f_ref[1]), _split_bf16(p_ref[...]))
    o_ref[...] = _hy_epilogue(y, z_ref[...], x0_ref[...], norm_ref[...], skip_ref[...])


def _dft_stage_a_inv(table, p, z, x0, norm, skip, tc=2048):
    rows, cols = p.shape
    zr = z.shape[0]
    col = lambda r: pl.BlockSpec((r, tc), lambda j: (0, j))
    return pl.pallas_call(
        _dft_a_inv_kernel,
        grid=(cols // tc,),
        in_specs=[pl.BlockSpec(table.shape, lambda j: (0, 0, 0)), col(rows), col(zr), col(zr),
                  pl.BlockSpec(norm.shape, lambda j: (0, 0)), pl.BlockSpec(skip.shape, lambda j: (0, 0))],
        out_specs=col(zr),
        out_shape=jax.ShapeDtypeStruct((zr, cols), BF16),
        compiler_params=_cparams(("parallel",)),
        name="hyena_dft_a_inv",
    )(table, p, z, x0, norm, skip)


def _dft_tables(n):
    big_l = 2 * n
    n1 = big_l // LANE
    n1h = n1 // 2
    two_pi = 2.0 * math.pi
    k1 = jnp.arange(n1, dtype=jnp.int32)
    th = two_pi * ((k1[:, None] * k1[None, :]) % n1).astype(F32) / n1
    c_full, s_full = jnp.cos(th), jnp.sin(th)
    ch, sh = c_full[:, :n1h], s_full[:, :n1h]
    fz = jnp.concatenate([jnp.concatenate([ch, sh], axis=1), jnp.concatenate([-sh, ch], axis=1)], axis=0)
    ff = jnp.concatenate([c_full, -s_full], axis=0)
    ct, st = ch.T / big_l, sh.T / big_l
    fi = jnp.concatenate([jnp.concatenate([ct, -st], axis=1), jnp.concatenate([st, ct], axis=1)], axis=0)
    s2 = jnp.arange(LANE, dtype=jnp.int32)
    k = k1[:, None, None] + n1 * s2[None, :, None]
    ph = two_pi * ((k * s2[None, None, :]) % big_l).astype(F32) / big_l
    gr, gi = jnp.cos(ph), -jnp.sin(ph)
    g = jnp.concatenate([jnp.concatenate([gr, -gi], axis=2), jnp.concatenate([gi, gr], axis=2)], axis=1)
    pair = lambda t: jnp.stack(_split_bf16(t))
    return pair(fz), pair(ff), pair(fi), pair(g), pair(jnp.swapaxes(g, 1, 2))


def _hy_small_kernel(batch, n, z_ref, x0_ref, h_ref, f_ref, fi_ref, norm_ref, skip_ref, o_ref):
    big_l = 2 * n
    f = f_ref[...]
    sz = _hdot(f[:, :n], jnp.concatenate([z_ref[b] for b in range(batch)], axis=1))
    sh = _hdot(f, jnp.concatenate([h_ref[0], h_ref[1]], axis=0))
    hr, hi = sh[:big_l], sh[big_l:]
    prods = []
    for b in range(batch):
        zr = sz[:big_l, b * BRANCH_W:(b + 1) * BRANCH_W]
        zi = sz[big_l:, b * BRANCH_W:(b + 1) * BRANCH_W]
        prods.append(jnp.concatenate([zr * hr - zi * hi, zr * hi + zi * hr], axis=0))
    y = _hdot(fi_ref[...], jnp.concatenate(prods, axis=1))
    for b in range(batch):
        o_ref[b] = _hy_epilogue(y[:, b * BRANCH_W:(b + 1) * BRANCH_W], z_ref[b], x0_ref[b],
                                norm_ref[...], skip_ref[...])


def _hy_small(z, x0, hfilt, norm, skip):
    batch, n, c = z.shape
    big_l = 2 * n
    kk = jnp.arange(big_l, dtype=jnp.int32)
    th = 2.0 * math.pi * ((kk[:, None] * kk[None, :]) % big_l).astype(F32) / big_l
    f = jnp.concatenate([jnp.cos(th), -jnp.sin(th)], axis=0)
    fi = jnp.concatenate([jnp.cos(th[:, :n]).T, -jnp.sin(th[:, :n]).T], axis=1) / big_l
    full = lambda a: pl.BlockSpec(a.shape, lambda i: (0,) * a.ndim)
    return pl.pallas_call(
        functools.partial(_hy_small_kernel, batch, n),
        grid=(1,),
        in_specs=[full(z), full(x0), full(hfilt), full(f), full(fi), full(norm), full(skip)],
        out_specs=full(z),
        out_shape=jax.ShapeDtypeStruct(z.shape, BF16),
        compiler_params=_cparams(("arbitrary",)),
        name="hyena_small",
    )(z, x0, hfilt, f, fi, norm, skip)


def _hy_features(n):
    tm = min(n, 512)
    bands = jnp.linspace(1e-4, HY_BANDS - 1, HY_BANDS, dtype=F32)

    def feats(pos):
        t = pos.astype(F32) / n
        ang = 2.0 * math.pi * t[:, None] * bands[None, :]
        feat = jnp.concatenate([t[:, None], jnp.cos(ang), jnp.sin(ang)], axis=-1)
        return jnp.pad(feat, ((0, 0), (0, LANE - HY_EMB))).reshape(n // tm, tm, LANE)

    r = jnp.arange(n, dtype=jnp.int32)
    return jnp.concatenate([feats(r), feats(n - r)], axis=1)


def _merge_kernel(hy_ref, mla_ref, dif_ref, hf_ref, hb_ref, mo_ref, gate_ref, wb_ref, mln_ref, o_ref):
    hsum = hf_ref[...] + hb_ref[...]
    mln = mln_ref[...]
    y_ml = jnp.concatenate(
        [_rms(hsum[:, h * HEAD_W:(h + 1) * HEAD_W]) * mln[:, h * HEAD_W:(h + 1) * HEAD_W]
         for h in range(N_HEADS)], axis=1)
    y_ml = _sigmoid(mo_ref[...]) * y_ml
    ys = (hy_ref[...], mla_ref[...], dif_ref[...], y_ml.astype(BF16))
    acc = None
    for i in range(4):
        gi = _sigmoid(gate_ref[:, i * D_MODEL:(i + 1) * D_MODEL].astype(F32))
        term = gi * jnp.dot(ys[i], wb_ref[i], preferred_element_type=F32)
        acc = term if acc is None else acc + term
    o_ref[...] = acc.astype(BF16)


def _merge(y_hy, y_mla, y_dif, h_f, h_b, u, gate, wb, mln, nrows, tm=256):
    row = pl.BlockSpec((tm, 512), lambda i: (i, 0))
    full = lambda a: pl.BlockSpec(a.shape, lambda i: (0,) * a.ndim)
    return pl.pallas_call(
        _merge_kernel,
        grid=(nrows // tm,),
        in_specs=[row, row, row, row, row,
                  pl.BlockSpec((tm, 512), lambda i: (i, CB_MO)),
                  pl.BlockSpec((tm, GATE_COLS), lambda i: (i, 0)),
                  full(wb), full(mln)],
        out_specs=pl.BlockSpec((tm, D_MODEL), lambda i: (i, 0)),
        out_shape=jax.ShapeDtypeStruct((nrows, D_MODEL), BF16),
        compiler_params=_cparams(("parallel",), 56),
        name="branch_merge",
    )(y_hy, y_mla, y_dif, h_f, h_b, u, gate, wb, mln)


def _out_kernel(m_ref, w_ref, x_ref, ga_ref, gain_ref, sc_ref, sh_ref, rw_ref, rb_ref,
                xo_ref, f_ref, idx_ref, gt_ref):
    y = jnp.dot(m_ref[...], w_ref[...], preferred_element_type=F32)
    x = x_ref[...] + ga_ref[0] * y
    xo_ref[...] = x
    f = _rms(x) * gain_ref[...] * (1.0 + sc_ref[0]) + sh_ref[0]
    tm = f.shape[0]
    for j in range(ROW_TILES):
        f_ref[pl.ds(j, tm, stride=ROW_TILES), :] = f[:, j * LANE:(j + 1) * LANE]
    logits = _bdot(f, rw_ref[...]) + rb_ref[...]
    lane = lax.broadcasted_iota(jnp.int32, logits.shape, 1)
    idx_out = jnp.zeros(logits.shape, jnp.int32)
    val_out = jnp.full(logits.shape, -jnp.inf, F32)
    work = logits
    for k in range(TOP_K):
        mx = jnp.max(work, axis=-1, keepdims=True)
        am = jnp.min(jnp.where(work == mx, lane, LANE), axis=-1, keepdims=True)
        idx_out = jnp.where(lane == k, am, idx_out)
        val_out = jnp.where(lane == k, mx, val_out)
        work = jnp.where(lane == am, -jnp.inf, work)
    e = jnp.exp(val_out - jnp.max(val_out, axis=-1, keepdims=True))
    gt_ref[...] = e / jnp.sum(e, axis=-1, keepdims=True)
    idx_ref[...] = idx_out


def _out_proj(merged, w_out, x_all, ga, gain, sc, sh, rw, rb, nrows, lat_rows, seq_rows, tm=256):
    grp = functools.partial(_group_of_tile, tm=tm, lat_rows=lat_rows, seq_rows=seq_rows)
    row = lambda w: pl.BlockSpec((tm, w), lambda i: (i, 0))
    full = lambda a: pl.BlockSpec(a.shape, lambda i: (0,) * a.ndim)
    mod = pl.BlockSpec((1, 1, D_MODEL), lambda i: (grp(i), 0, 0))
    return pl.pallas_call(
        _out_kernel,
        grid=(nrows // tm,),
        in_specs=[row(D_MODEL), full(w_out), row(D_MODEL), mod, full(gain), mod, mod, full(rw), full(rb)],
        out_specs=[row(D_MODEL), pl.BlockSpec((tm * ROW_TILES, LANE), lambda i: (i, 0)), row(LANE), row(LANE)],
        out_shape=[jax.ShapeDtypeStruct((nrows, D_MODEL), F32), jax.ShapeDtypeStruct((nrows * ROW_TILES, LANE), F32),
                   jax.ShapeDtypeStruct((nrows, LANE), jnp.int32), jax.ShapeDtypeStruct((nrows, LANE), F32)],
        compiler_params=_cparams(("parallel",)),
        name="out_proj_router",
    )(merged, w_out, x_all, ga, gain, sc, sh, rw, rb)


def _moe_kernel(blk_e_ref, used_ref, tok_ref, nxt_ref, f_hbm, wgu_ref, bgu_ref, wd_ref, bd_ref, o_ref, xbuf, sem):
    i = pl.program_id(0)
    last = pl.num_programs(0) - 1
    cur = i % 2

    buf_rows = MOE_BLOCK * ROW_TILES

    def start_rows(idx_ref, s, r0, r1):
        for r in range(r0, r1):
            pltpu.make_async_copy(f_hbm.at[idx_ref[0, 0, r]],
                                  xbuf.at[pl.ds(s * buf_rows + r * ROW_TILES, ROW_TILES), :], sem.at[s]).start()

    def wait_rows(s):
        whole = xbuf.at[pl.ds(s * buf_rows, buf_rows), :]
        pltpu.make_async_copy(whole, whole, sem.at[s]).wait()

    @pl.when(i == 0)
    def _():
        start_rows(tok_ref, 0, 0, MOE_BLOCK)

    wait_rows(cur)

    start_rows(nxt_ref, 1 - cur, 0, MOE_BLOCK)

    @pl.when(i >= used_ref[0])
    def _():
        o_ref[...] = jnp.zeros_like(o_ref)

    @pl.when(i < used_ref[0])
    def _():
        base = cur * buf_rows
        x = jnp.concatenate([xbuf[pl.ds(base + j, MOE_BLOCK, stride=ROW_TILES), :].astype(BF16)
                             for j in range(ROW_TILES)], axis=1)
        gu = jnp.dot(x, wgu_ref[0], preferred_element_type=F32) + bgu_ref[0]
        g = jnp.minimum(gu[:, :D_EXPERT], SWIGLU_LIMIT)
        u = jnp.clip(gu[:, D_EXPERT:], -SWIGLU_LIMIT, SWIGLU_LIMIT)
        act = g * _sigmoid(SWIGLU_ALPHA * g) * (u + 1.0)
        o_ref[...] = _bdot(act, wd_ref[0]) + bd_ref[0]

    @pl.when(i == last)
    def _():
        wait_rows(1 - cur)


def _moe_experts(f_all, blk_e, n_used, slot_tok, wgu, bgu, wd, bd):
    n_blocks = blk_e.shape[0]
    slots = n_blocks * MOE_BLOCK
    n_exp = wgu.shape[0]
    grid_spec = pltpu.PrefetchScalarGridSpec(
        num_scalar_prefetch=2,
        grid=(n_blocks,),
        in_specs=[pl.BlockSpec((1, 1, MOE_BLOCK), lambda i, be, nu: (i, 0, 0), memory_space=pltpu.SMEM),
                  pl.BlockSpec((1, 1, MOE_BLOCK), lambda i, be, nu: (jnp.minimum(i + 1, n_blocks - 1), 0, 0),
                               memory_space=pltpu.SMEM),
                  pl.BlockSpec(memory_space=pl.ANY),
                  pl.BlockSpec((1, D_MODEL, 2 * D_EXPERT), lambda i, be, nu: (be[i], 0, 0)),
                  pl.BlockSpec((1, 1, 2 * D_EXPERT), lambda i, be, nu: (be[i], 0, 0)),
                  pl.BlockSpec((1, D_EXPERT, D_MODEL), lambda i, be, nu: (be[i], 0, 0)),
                  pl.BlockSpec((1, 1, D_MODEL), lambda i, be, nu: (be[i], 0, 0))],
        out_specs=pl.BlockSpec((MOE_BLOCK, D_MODEL), lambda i, be, nu: (i, 0)),
        scratch_shapes=[pltpu.VMEM((2 * MOE_BLOCK * ROW_TILES, LANE), F32), pltpu.SemaphoreType.DMA((2,))],
    )
    tok3 = slot_tok.reshape(n_blocks, 1, MOE_BLOCK)
    return pl.pallas_call(
        _moe_kernel,
        grid_spec=grid_spec,
        out_shape=jax.ShapeDtypeStruct((slots, D_MODEL), F32),
        compiler_params=_cparams(("arbitrary",), 56),
        name="moe_experts",
    )(blk_e, n_used, tok3, tok3, f_all,
      wgu, bgu.reshape(n_exp, 1, 2 * D_EXPERT), wd, bd.reshape(n_exp, 1, D_MODEL))


def _combine_kernel(final, dest_ref, nxt_ref, y_hbm, x_ref, tg_ref, gf_ref, fn_ref, o_ref, ybuf, sem):
    tm = x_ref.shape[0]
    i = pl.program_id(0)
    last = pl.num_programs(0) - 1
    cur = i % 2

    def start_rows(idx_ref, s):
        for r in range(tm):
            for k in range(TOP_K):
                pltpu.make_async_copy(y_hbm.at[pl.ds(idx_ref[0, 0, r * TOP_K + k], 1), :],
                                      ybuf.at[s, k, pl.ds(r, 1), :], sem.at[s]).start()

    def wait_rows(s):
        for k in range(TOP_K):
            pltpu.make_async_copy(y_hbm.at[pl.ds(0, tm), :], ybuf.at[s, k], sem.at[s]).wait()

    @pl.when(i == 0)
    def _():
        start_rows(dest_ref, 0)

    start_rows(nxt_ref, 1 - cur)
    wait_rows(cur)
    tg = tg_ref[...]
    moe = None
    for k in range(TOP_K):
        term = tg[:, k:k + 1] * ybuf[cur, k]
        moe = term if moe is None else moe + term
    x = x_ref[...] + gf_ref[0] * moe
    if final:
        x = _rms(x) * fn_ref[...]
    o_ref[...] = x

    @pl.when(i == last)
    def _():
        wait_rows(1 - cur)


def _moe_combine(y_slots, dest, top_gate, x_all, gf, final_gain, final, nrows, lat_rows, seq_rows, tm=128):
    grp = functools.partial(_group_of_tile, tm=tm, lat_rows=lat_rows, seq_rows=seq_rows)
    nt = nrows // tm
    grid_spec = pl.GridSpec(
        grid=(nt,),
        in_specs=[pl.BlockSpec((1, 1, tm * TOP_K), lambda i: (i, 0, 0), memory_space=pltpu.SMEM),
                  pl.BlockSpec((1, 1, tm * TOP_K), lambda i: (jnp.minimum(i + 1, nt - 1), 0, 0),
                               memory_space=pltpu.SMEM),
                  pl.BlockSpec(memory_space=pl.ANY),
                  pl.BlockSpec((tm, D_MODEL), lambda i: (i, 0)),
                  pl.BlockSpec((tm, LANE), lambda i: (i, 0)),
                  pl.BlockSpec((1, 1, D_MODEL), lambda i: (grp(i), 0, 0)),
                  pl.BlockSpec((1, D_MODEL), lambda i: (0, 0))],
        out_specs=pl.BlockSpec((tm, D_MODEL), lambda i: (i, 0)),
        scratch_shapes=[pltpu.VMEM((2, TOP_K, tm, D_MODEL), F32), pltpu.SemaphoreType.DMA((2,))],
    )
    dest3 = dest.reshape(nt, 1, tm * TOP_K)
    return pl.pallas_call(
        functools.partial(_combine_kernel, final),
        grid_spec=grid_spec,
        out_shape=jax.ShapeDtypeStruct((nrows, D_MODEL), F32),
        compiler_params=_cparams(("arbitrary",)),
        name="moe_combine",
    )(dest3, dest3, y_slots, x_all, top_gate, gf, final_gain)


def _expert_onehot(idx):
    lane = lax.broadcasted_iota(jnp.int32, idx.shape, 1)
    oh = jnp.zeros(idx.shape, F32)
    for k in range(TOP_K):
        oh = oh + jnp.where(lane == idx[:, k:k + 1], 1.0, 0.0)
    return oh


def _rank_kernel(idx_ref, rank_ref, cnt_ref, carry):
    @pl.when(pl.program_id(0) == 0)
    def _():
        carry[...] = jnp.zeros_like(carry)

    oh = _expert_onehot(idx_ref[...])
    tm = oh.shape[0]
    r = lax.broadcasted_iota(jnp.int32, (tm, tm), 0)
    c = lax.broadcasted_iota(jnp.int32, (tm, tm), 1)
    earlier = jnp.where(c < r, 1.0, 0.0).astype(BF16)
    rank_ref[...] = jnp.dot(earlier, oh.astype(BF16), preferred_element_type=F32) + carry[...]
    carry[...] += jnp.sum(oh, axis=0, keepdims=True)
    cnt_ref[...] = carry[...]


def _dest_kernel(idx_ref, rank_ref, pstart_ref, dest_ref):
    idx = idx_ref[...]
    slot = rank_ref[...] + pstart_ref[...]
    lane = lax.broadcasted_iota(jnp.int32, idx.shape, 1)
    out = jnp.zeros(idx.shape, jnp.int32)
    for k in range(TOP_K):
        dk = jnp.sum(jnp.where(lane == idx[:, k:k + 1], slot, 0.0), axis=-1, keepdims=True)
        out = jnp.where(lane == k, dk.astype(jnp.int32), out)
    dest_ref[...] = out


def _route(top_idx128, tm=512):
    n = top_idx128.shape[0]
    a = n * TOP_K
    row = pl.BlockSpec((tm, LANE), lambda i: (i, 0))
    one = pl.BlockSpec((1, LANE), lambda i: (0, 0))
    rank, cnt = pl.pallas_call(
        _rank_kernel,
        grid=(n // tm,),
        in_specs=[row],
        out_specs=[row, one],
        out_shape=[jax.ShapeDtypeStruct((n, LANE), F32), jax.ShapeDtypeStruct((1, LANE), F32)],
        scratch_shapes=[pltpu.VMEM((1, LANE), F32)],
        compiler_params=_cparams(("arbitrary",)),
        name="route_rank",
    )(top_idx128)
    counts = cnt[0, :N_EXPERTS].astype(jnp.int32)
    padded = (counts + MOE_BLOCK - 1) // MOE_BLOCK * MOE_BLOCK
    pend = jnp.cumsum(padded)
    pstart = pend - padded
    pstart128 = jnp.pad(pstart.astype(F32), (0, LANE - N_EXPERTS)).reshape(1, LANE)
    dest = pl.pallas_call(
        _dest_kernel,
        grid=(n // tm,),
        in_specs=[row, row, one],
        out_specs=row,
        out_shape=jax.ShapeDtypeStruct((n, LANE), jnp.int32),
        compiler_params=_cparams(("parallel",)),
        name="route_dest",
    )(top_idx128, rank, pstart128)[:, :TOP_K]
    n_blocks = (a + N_EXPERTS * (MOE_BLOCK - 1)) // MOE_BLOCK + 1
    slots = n_blocks * MOE_BLOCK
    tok = jnp.broadcast_to(jnp.arange(n, dtype=jnp.int32)[:, None], (n, TOP_K))
    slot_tok = jnp.zeros((slots,), jnp.int32).at[dest.reshape(-1)].set(tok.reshape(-1))
    first_slot = jnp.arange(n_blocks, dtype=jnp.int32) * MOE_BLOCK
    blk_e = jnp.minimum(jnp.sum((pend[None, :] <= first_slot[:, None]).astype(jnp.int32), axis=1), N_EXPERTS - 1)
    n_used = (pend[-1:] // MOE_BLOCK).astype(jnp.int32)
    return blk_e, n_used, slot_tok, dest.astype(jnp.int32)


def _rope_tables(n_lat_rows, seq_len, ctx_rows):
    t = jnp.arange(seq_len, dtype=jnp.int32)
    inv = ROPE_THETA ** (-jnp.arange(0, 32, 2, dtype=F32) / 32)

    def cs(p):
        ang = p.astype(F32)[:, None] * inv[None, :]
        return jnp.cos(ang), jnp.sin(ang)

    cr, sr = cs(t // GRID_W)
    cc, sc = cs(t % GRID_W)
    c64 = jnp.concatenate([cr, cr, cc, cc], axis=1)
    s64 = jnp.concatenate([-sr, sr, -sc, sc], axis=1)
    reps = n_lat_rows // seq_len

    def build(c_half2, s_half2):
        c = jnp.concatenate([c64, c_half2], axis=1)
        s = jnp.concatenate([s64, s_half2], axis=1)
        c = jnp.concatenate([c] * reps + [jnp.ones((ctx_rows, LANE), F32)], axis=0)
        s = jnp.concatenate([s] * reps + [jnp.zeros((ctx_rows, LANE), F32)], axis=0)
        return c, s

    cd, sd = build(c64, s64)
    cm, sm = build(jnp.ones_like(c64), jnp.zeros_like(s64))
    return cd, sd, cm, sm


W_IN_SEGMENTS = ((320, 0, 512), (832, 512, 512), (1344, 1024, 512), (1856, 1536, 512), (2384, 2048, 512),
                 (2896, 2560, 512), (3408, 3072, 512), (3920, 3584, 512), (4432, 4096, 1536),
                 (0, 5632, 256), (256, 5888, 64), (2368, 5952, 16))
W_IN_USED = 5968
W_IN_GATE0 = 5968


def _w_in_layout_kernel(w_ref, small_ref, gate_ref):
    for src, dst, width in W_IN_SEGMENTS:
        small_ref[:, dst:dst + width] = w_ref[0, :, src:src + width].astype(BF16)
    small_ref[:, W_IN_USED:] = jnp.zeros((small_ref.shape[0], SMALL_COLS - W_IN_USED), BF16)
    gate_ref[...] = w_ref[0, :, W_IN_GATE0:].astype(BF16)


def _layout_w_in(w, layer, tk=128):
    _, d, ncol = w.shape
    return pl.pallas_call(
        _w_in_layout_kernel,
        grid=(d // tk,),
        in_specs=[pl.BlockSpec((1, tk, ncol), lambda i: (layer, i, 0))],
        out_specs=[pl.BlockSpec((tk, SMALL_COLS), lambda i: (i, 0)), pl.BlockSpec((tk, GATE_COLS), lambda i: (i, 0))],
        out_shape=[jax.ShapeDtypeStruct((d, SMALL_COLS), BF16), jax.ShapeDtypeStruct((d, GATE_COLS), BF16)],
        compiler_params=_cparams(("parallel",)),
        name="w_in_layout",
    )(w)


def _layout_w_qb(w):
    w = w.reshape(w.shape[0], N_HEADS, MLA_NOPE + MLA_ROPE)
    w = jnp.pad(w, ((0, 0), (0, 0), (0, MLA_QK_PAD - MLA_NOPE - MLA_ROPE)))
    return w.reshape(w.shape[0], N_HEADS * MLA_QK_PAD).astype(BF16)


def _layout_w_kvb(w):
    w = w.reshape(w.shape[0], N_HEADS, MLA_NOPE + HEAD_W)
    return jnp.concatenate([w[:, :, :MLA_NOPE].reshape(w.shape[0], -1),
                            w[:, :, MLA_NOPE:].reshape(w.shape[0], -1)], axis=1).astype(BF16)


def kernel(x, c, ctx, c_ctx, ada_w, ada_b, norm_mix, norm_ffn, w_in, hy_conv, hy_conv_b, hy_w1, hy_b1, hy_w2, hy_b2, hy_w3, hy_freq, hy_decay, hy_skip, mla_q_norm, mla_w_qb, mla_kv_norm, mla_w_kvb, dif_lambda, dif_norm, ml_conv_q, ml_conv_k, ml_gate_b, ml_norm, w_branch, w_out, router_w, router_b, exp_w_gu, exp_b_gu, exp_w_down, exp_b_down, final_norm):
    batch, n, d = x.shape
    n_ctx = ctx.shape[1]
    depth = w_in.shape[0]
    ctx_rows = batch * n_ctx
    lat_rows = batch * n
    t_all = ctx_rows + lat_rows

    assert batch + 1 <= SUBLANE
    cvec = jnp.zeros((SUBLANE, d), F32).at[:batch].set(c).at[batch].set(c_ctx)
    mod = _modulation(cvec, ada_w, ada_b)
    x_all = jnp.concatenate([x.reshape(lat_rows, d), ctx.reshape(ctx_rows, d)], axis=0)
    tabs = _rope_tables(lat_rows, n, ctx_rows)
    assert batch == 2, "the Hyena transform carries the two batches as one complex signal"
    fz_tab, ff_tab, fi_tab, g_tab, gt_tab = _dft_tables(n)
    feat_lat, feat_ctx = _hy_features(n), _hy_features(n_ctx)
    zero_state = (jnp.zeros((2, batch, N_HEADS, HEAD_W, HEAD_W), F32), jnp.zeros((2, batch, N_HEADS, 1, HEAD_W), F32),
                  jnp.zeros((2, batch, N_HEADS, 1, HEAD_W), F32))
    n1h = n // LANE
    wgu_all = exp_w_gu.astype(BF16).reshape(depth * N_EXPERTS, d, 2 * D_EXPERT)
    wd_all = exp_w_down.astype(BF16).reshape(depth * N_EXPERTS, D_EXPERT, d)
    bgu_all = exp_b_gu.reshape(depth * N_EXPERTS, 2 * D_EXPERT)
    bd_all = exp_b_down.reshape(depth * N_EXPERTS, d)

    for l in range(depth):
        last = l == depth - 1
        lam_init = 0.8 - 0.6 * math.exp(-0.3 * l)
        m6 = mod[l].reshape(SUBLANE, 6, 1, d)
        sh_a, sc_a, g_a, sh_f, sc_f, g_f = (m6[:, i] for i in range(6))
        w_small, w_gate = _layout_w_in(w_in, l)
        gain_mix = norm_mix[l].reshape(1, d)

        n_tok = lat_rows if last else t_all
        h_in = _norm_mod(x_all, gain_mix, sc_a, sh_a, lat_rows, n)
        u = _in_proj(h_in, w_small, F32, t_all)
        gate = _in_proj(h_in, w_gate, BF16, n_tok)

        mq, mk, mv, dq, dk, dv = _attn_prep(
            u, tabs, _layout_w_qb(mla_w_qb[l]), _layout_w_kvb(mla_w_kvb[l]),
            mla_q_norm[l].reshape(1, -1), mla_kv_norm[l].reshape(1, -1))
        ml_q, ml_k, ml_g = _ml_prep(u, ml_conv_q[l], ml_conv_k[l], ml_gate_b[l].reshape(1, 16), n, lat_rows, n_ctx)
        gt_lat = jnp.swapaxes(ml_g[:lat_rows].reshape(batch, n, ML_GATE_W), 1, 2)
        gt_ctx = jnp.swapaxes(ml_g[lat_rows:].reshape(batch, n_ctx, ML_GATE_W), 1, 2)

        lam_p = dif_lambda[l]
        dgain = dif_norm[l].reshape(1, HEAD_W)
        segs = [(lat_rows, n_ctx), (0, n)]
        y_mla = _attention(mq, mk, mv, 1, 0, n, segs, batch, MLA_QK_PAD)
        y_dif = _attention(dq, dk, dv, 2, 0, n, segs, batch, 2 * DIF_QK, lam_p, dgain, lam_init)

        hcf, hcb, st_ctx = _mlstm(ml_q, ml_k, u, ml_g, gt_ctx, zero_state, not last, lat_rows, n_ctx, batch)
        hlf, hlb, _ = _mlstm(ml_q, ml_k, u, ml_g, gt_lat, st_ctx, True, 0, n, batch)

        hw1 = jnp.pad(hy_w1[l], ((0, LANE - HY_EMB), (0, 0)))
        filt_args = (hw1, hy_b1[l].reshape(1, -1), hy_w2[l], hy_b2[l].reshape(1, -1), hy_w3[l], hy_freq[l],
                     hy_decay[l].reshape(1, -1))
        skip = hy_skip[l].reshape(1, -1)
        hconv_b = hy_conv_b[l].reshape(1, -1)
        z, x0 = _hy_prep(u, hy_conv[l], hconv_b, 0, lat_rows, n, lat_rows, n_ctx)
        hfilt, hnorm = _hy_filter(n, feat_lat, *filt_args)
        filt_full = hfilt.reshape(2 * n, BRANCH_W)
        n1 = 2 * n1h
        cols = LANE * BRANCH_W
        zr = z.reshape(batch * n1h, cols)
        az = _dft_stage_a(fz_tab, zr)
        ah = _dft_stage_a(ff_tab, filt_full.reshape(n1, cols))
        pz = _dft_stage_c(az.reshape(2, n1, LANE, BRANCH_W), ah.reshape(2, n1, LANE, BRANCH_W), g_tab, gt_tab)
        y_hy = _dft_stage_a_inv(fi_tab, pz.reshape(2 * n1, cols), zr, x0.reshape(batch * n1h, cols), hnorm,
                                skip).reshape(lat_rows, BRANCH_W)

        h_f, h_b = hlf.reshape(lat_rows, -1), hlb.reshape(lat_rows, -1)
        w_o = w_out[l].astype(BF16)
        gain_ffn = norm_ffn[l].reshape(1, d)
        rw = jnp.pad(router_w[l], ((0, 0), (0, LANE - N_EXPERTS))).astype(BF16)
        rb = jnp.concatenate([router_b[l], jnp.full((LANE - N_EXPERTS,), -jnp.inf, F32)]).reshape(1, LANE)

        if not last:
            cseg = [(lat_rows, n_ctx)]
            yc_mla = _attention(mq, mk, mv, 1, lat_rows, n_ctx, cseg, batch, MLA_QK_PAD)
            yc_dif = _attention(dq, dk, dv, 2, lat_rows, n_ctx, cseg, batch, 2 * DIF_QK, lam_p, dgain, lam_init)
            zc, x0c = _hy_prep(u, hy_conv[l], hconv_b, lat_rows, ctx_rows, n, lat_rows, n_ctx)
            hfc, hnc = _hy_filter(n_ctx, feat_ctx, *filt_args)
            yc_hy = _hy_small(zc.reshape(batch, n_ctx, BRANCH_W), x0c.reshape(batch, n_ctx, BRANCH_W), hfc, hnc,
                              skip).reshape(ctx_rows, BRANCH_W)
            cat = lambda a, b: jnp.concatenate([a, b], axis=0)
            y_hy, y_mla, y_dif = cat(y_hy, yc_hy), cat(y_mla, yc_mla), cat(y_dif, yc_dif)
            h_f, h_b = cat(h_f, hcf.reshape(ctx_rows, -1)), cat(h_b, hcb.reshape(ctx_rows, -1))

        merged = _merge(y_hy, y_mla, y_dif, h_f, h_b, u, gate, w_branch[l].astype(BF16), ml_norm[l].reshape(1, -1),
                        n_tok)
        xo, f_all, top_idx, top_gate = _out_proj(merged, w_o, x_all, g_a, gain_ffn, sc_f, sh_f, rw, rb,
                                                 n_tok, lat_rows, n)
        blk_e, n_used, slot_tok, dest = _route(top_idx)
        y_slots = _moe_experts(f_all.reshape(n_tok, ROW_TILES, LANE), blk_e + l * N_EXPERTS, n_used, slot_tok,
                               wgu_all, bgu_all, wd_all, bd_all)
        fgain = final_norm.reshape(1, d)
        x_all = _moe_combine(y_slots, dest, top_gate, xo, g_f, fgain, last, n_tok, lat_rows, n)

    return x_all.reshape(batch, n, d)
```

```python
import functools
import math

import jax
import jax.numpy as jnp
from jax import lax
from jax.experimental import pallas as pl
from jax.experimental.pallas import tpu as pltpu

F32 = jnp.float32
BF16 = jnp.bfloat16
HIGHEST = lax.Precision.HIGHEST

D_MODEL = 2048
GRID_W = 64
BRANCH_W = 512
N_HEADS = 4
HEAD_W = 128
V_EXT_W = 256
MLA_NOPE = 128
MLA_ROPE = 64
MLA_QK_PAD = 256
MLA_SCALE = (MLA_NOPE + MLA_ROPE) ** -0.5
LOG2E = math.log2(math.e)
DIF_QK = 64
ML_CHUNK = 64
ML_GATE_W = 16
N_EXPERTS = 32
TOP_K = 4
D_EXPERT = 1024
SWIGLU_LIMIT = 7.0
SWIGLU_ALPHA = 1.702
MOE_BLOCK = 256
ROPE_THETA = 10000.0
NORM_EPS = 1e-6
HY_BANDS = 16
HY_EMB = 1 + 2 * HY_BANDS
HY_HID = 64
LANE = 128
SUBLANE = 8
ROW_TILES = D_MODEL // LANE
SMALL_COLS = 6144
GATE_COLS = 4 * D_MODEL

CB_DK, CB_DV, CB_MK, CB_MV, CB_QA, CB_DQ, CB_MQ, CB_MO, CB_HX0, CB_HX1, CB_HV = range(11)
CB_CKV_256 = 22
CB_KPE_128 = 46


def _cparams(sem, vmem_mb=48):
    return pltpu.CompilerParams(dimension_semantics=sem, vmem_limit_bytes=vmem_mb * 1024 * 1024)


def _rms(x):
    return x * lax.rsqrt(jnp.mean(x * x, axis=-1, keepdims=True) + NORM_EPS)


def _sigmoid(x):
    return 1.0 / (1.0 + jnp.exp(-x))


def _silu(x):
    return x * _sigmoid(x)


def _log_sigmoid(x):
    return jnp.minimum(x, 0.0) - jnp.log(1.0 + jnp.exp(-jnp.abs(x)))


def _bdot(a, b):
    return jnp.dot(a.astype(BF16), b.astype(BF16), preferred_element_type=F32)


def _bdot_nt(a, b):
    return lax.dot_general(a.astype(BF16), b.astype(BF16), (((1,), (1,)), ((), ())),
                           preferred_element_type=F32)


def _hdot(a, b):
    return jnp.dot(a, b, preferred_element_type=F32, precision=HIGHEST)


def _mod_kernel(c_ref, w_ref, b_ref, o_ref):
    o_ref[0] = _bdot(_silu(c_ref[...]), w_ref[0]) + b_ref[0]


def _modulation(cvec, ada_w, ada_b):
    n_layers, d, six_d = ada_w.shape
    tn = 1024
    return pl.pallas_call(
        _mod_kernel,
        grid=(n_layers, six_d // tn),
        in_specs=[pl.BlockSpec((SUBLANE, d), lambda l, j: (0, 0)),
                  pl.BlockSpec((1, d, tn), lambda l, j: (l, 0, j)),
                  pl.BlockSpec((1, 1, tn), lambda l, j: (l, 0, j))],
        out_specs=pl.BlockSpec((1, SUBLANE, tn), lambda l, j: (l, 0, j)),
        out_shape=jax.ShapeDtypeStruct((n_layers, SUBLANE, six_d), F32),
        compiler_params=_cparams(("parallel", "parallel")),
        name="adaln_modulation",
    )(cvec, ada_w, ada_b.reshape(n_layers, 1, six_d))


def _group_of_tile(i, tm, lat_rows, seq_rows):
    return jnp.minimum(i // (seq_rows // tm), lat_rows // seq_rows)


def _norm_mod_kernel(x_ref, g_ref, sc_ref, sh_ref, o_ref):
    h = _rms(x_ref[...]) * g_ref[...]
    o_ref[...] = (h * (1.0 + sc_ref[0]) + sh_ref[0]).astype(BF16)


def _norm_mod(x_all, gain, sc, sh, lat_rows, seq_rows, tm=512):
    t_all, d = x_all.shape
    grp = functools.partial(_group_of_tile, tm=tm, lat_rows=lat_rows, seq_rows=seq_rows)
    return pl.pallas_call(
        _norm_mod_kernel,
        grid=(t_all // tm,),
        in_specs=[pl.BlockSpec((tm, d), lambda i: (i, 0)),
                  pl.BlockSpec((1, d), lambda i: (0, 0)),
                  pl.BlockSpec((1, 1, d), lambda i: (grp(i), 0, 0)),
                  pl.BlockSpec((1, 1, d), lambda i: (grp(i), 0, 0))],
        out_specs=pl.BlockSpec((tm, d), lambda i: (i, 0)),
        out_shape=jax.ShapeDtypeStruct((t_all, d), BF16),
        compiler_params=_cparams(("parallel",)),
        name="norm_mod",
    )(x_all, gain, sc, sh)


def _proj_kernel(h_ref, w_ref, o_ref):
    o_ref[...] = jnp.dot(h_ref[...], w_ref[...], preferred_element_type=F32).astype(o_ref.dtype)


def _in_proj(h, w, out_dtype, nrows, tm=512, tn=1024):
    d = h.shape[1]
    ncol = w.shape[1]
    return pl.pallas_call(
        _proj_kernel,
        grid=(ncol // tn, nrows // tm),
        in_specs=[pl.BlockSpec((tm, d), lambda j, i: (i, 0)),
                  pl.BlockSpec((d, tn), lambda j, i: (0, j))],
        out_specs=pl.BlockSpec((tm, tn), lambda j, i: (i, j)),
        out_shape=jax.ShapeDtypeStruct((nrows, ncol), out_dtype),
        compiler_params=_cparams(("parallel", "arbitrary")),
        name="in_proj",
    )(h, w)


def _rope(x, c, s):
    w = x.shape[-1]
    lane = lax.broadcasted_iota(jnp.int32, x.shape, 1)
    up = pltpu.roll(x, w - 16, 1)
    dn = pltpu.roll(x, 16, 1)
    return x * c + jnp.where((lane % 32) < 16, up, dn) * s


def _attn_prep_kernel(dk_ref, dv_ref, qa_ref, dq_ref, ckv_ref, kpe_ref, cd_ref, sd_ref, cm_ref, sm_ref,
                      wqb_ref, wkvb_ref, qn_ref, kvn_ref,
                      mq_ref, mk_ref, mv_ref, dqo_ref, dko_ref, dvo_ref):
    cd4 = jnp.concatenate([cd_ref[...]] * N_HEADS, axis=1)
    sd4 = jnp.concatenate([sd_ref[...]] * N_HEADS, axis=1)
    cm, sm = cm_ref[...], sm_ref[...]
    dqo_ref[...] = (_rope(dq_ref[...], cd4, sd4) * (DIF_QK ** -0.5 * LOG2E)).astype(BF16)
    dko_ref[...] = _rope(dk_ref[...], cd4, sd4).astype(BF16)
    dv = dv_ref[...]
    vlane = lax.broadcasted_iota(jnp.int32, (dv.shape[0], V_EXT_W - HEAD_W), 1)
    ones_col = jnp.where(vlane == 0, 1.0, 0.0).astype(BF16)

    q = _bdot(_rms(qa_ref[...]) * qn_ref[...], wqb_ref[...])
    kv = _bdot(_rms(ckv_ref[...]) * kvn_ref[...], wkvb_ref[...])
    kpe = kpe_ref[...]
    lane = lax.broadcasted_iota(jnp.int32, kpe.shape, 1)
    kpe = _rope(jnp.where(lane < MLA_ROPE, kpe, 0.0), cm, sm).astype(BF16)
    for h in range(N_HEADS):
        o = h * MLA_QK_PAD
        mq_ref[:, o:o + MLA_NOPE] = (q[:, o:o + MLA_NOPE] * (MLA_SCALE * LOG2E)).astype(BF16)
        mq_ref[:, o + MLA_NOPE:o + MLA_QK_PAD] = (
            _rope(q[:, o + MLA_NOPE:o + MLA_QK_PAD], cm, sm) * (MLA_SCALE * LOG2E)).astype(BF16)
        mk_ref[:, o:o + MLA_NOPE] = kv[:, h * MLA_NOPE:(h + 1) * MLA_NOPE].astype(BF16)
        mk_ref[:, o + MLA_NOPE:o + MLA_QK_PAD] = kpe
    for h in range(N_HEADS):
        o = h * V_EXT_W
        mv_ref[:, o:o + HEAD_W] = kv[:, (N_HEADS + h) * HEAD_W:(N_HEADS + h + 1) * HEAD_W].astype(BF16)
        mv_ref[:, o + HEAD_W:o + V_EXT_W] = ones_col
        dvo_ref[:, o:o + HEAD_W] = dv[:, h * HEAD_W:(h + 1) * HEAD_W].astype(BF16)
        dvo_ref[:, o + HEAD_W:o + V_EXT_W] = ones_col


def _attn_prep(u, tabs, wqb, wkvb, qn, kvn, tm=256):
    t_all = u.shape[0]
    cd, sd, cm, sm = tabs
    col = lambda w, cb: pl.BlockSpec((tm, w), lambda i, cb=cb: (i, cb))
    tab = pl.BlockSpec((tm, LANE), lambda i: (i, 0))
    full = lambda a: pl.BlockSpec(a.shape, lambda i: (0,) * a.ndim)
    out = lambda w: pl.BlockSpec((tm, w), lambda i: (i, 0))
    shp = lambda w: jax.ShapeDtypeStruct((t_all, w), BF16)
    return pl.pallas_call(
        _attn_prep_kernel,
        grid=(t_all // tm,),
        in_specs=[col(512, CB_DK), col(512, CB_DV), col(512, CB_QA), col(512, CB_DQ),
                  col(256, CB_CKV_256), col(128, CB_KPE_128), tab, tab, tab, tab,
                  full(wqb), full(wkvb), full(qn), full(kvn)],
        out_specs=[out(1024), out(1024), out(1024), out(512), out(512), out(1024)],
        out_shape=[shp(1024), shp(1024), shp(1024), shp(512), shp(512), shp(1024)],
        compiler_params=_cparams(("parallel",)),
        name="attn_prep",
    )(u, u, u, u, u, u, cd, sd, cm, sm, wqb, wkvb, qn, kvn)


def _conv3(x, prev_blk, next_blk, w, first, last):
    tm = x.shape[0]
    row = lax.broadcasted_iota(jnp.int32, x.shape, 0)
    prev_row = jnp.where(first, 0.0, prev_blk[SUBLANE - 1:SUBLANE, :])
    next_row = jnp.where(last, 0.0, next_blk[0:1, :])
    xm = jnp.where(row == 0, prev_row, pltpu.roll(x, 1, 0))
    xp = jnp.where(row == tm - 1, next_row, pltpu.roll(x, tm - 1, 0))
    return xm * w[0:1, :] + x * w[1:2, :] + xp * w[2:3, :]


def _seq_edges(tile, tm, seq_rows, lat_rows, ctx_len):
    r = tile * tm
    in_lat = r < lat_rows
    pos = jnp.where(in_lat, r % seq_rows, (r - lat_rows) % ctx_len)
    length = jnp.where(in_lat, seq_rows, ctx_len)
    return pos == 0, pos + tm == length


def _ml_prep_kernel(seq_rows, lat_rows, ctx_len,
                    q_ref, qp_ref, qn_ref, k_ref, kp_ref, kn_ref, gblk_ref, wq_ref, wk_ref, gb_ref,
                    qo_ref, ko_ref, go_ref):
    tm = q_ref.shape[0]
    first, last = _seq_edges(pl.program_id(0), tm, seq_rows, lat_rows, ctx_len)
    qo_ref[...] = _silu(_conv3(q_ref[...], qp_ref[...], qn_ref[...], wq_ref[...], first, last))
    ko_ref[...] = _silu(_conv3(k_ref[...], kp_ref[...], kn_ref[...], wk_ref[...], first, last)) * HEAD_W ** -0.5
    go_ref[...] = gblk_ref[:, MLA_ROPE:MLA_ROPE + 16] + gb_ref[...]


def _halo_specs(tm, w, cb, t_all, r0=0):
    nb8 = t_all // SUBLANE
    r8 = tm // SUBLANE
    return [pl.BlockSpec((tm, w), lambda i: (i + r0, cb)),
            pl.BlockSpec((SUBLANE, w), lambda i: (jnp.maximum((i + r0) * r8 - 1, 0), cb)),
            pl.BlockSpec((SUBLANE, w), lambda i: (jnp.minimum((i + r0 + 1) * r8, nb8 - 1), cb))]


def _ml_prep(u, wq, wk, gb, seq_rows, lat_rows, ctx_len, tm=256):
    t_all = u.shape[0]
    full = lambda a: pl.BlockSpec(a.shape, lambda i: (0,) * a.ndim)
    return pl.pallas_call(
        functools.partial(_ml_prep_kernel, seq_rows, lat_rows, ctx_len),
        grid=(t_all // tm,),
        in_specs=_halo_specs(tm, 512, CB_MQ, t_all) + _halo_specs(tm, 512, CB_MK, t_all)
        + [pl.BlockSpec((tm, LANE), lambda i: (i, CB_KPE_128)), full(wq), full(wk), full(gb)],
        out_specs=[pl.BlockSpec((tm, 512), lambda i: (i, 0)), pl.BlockSpec((tm, 512), lambda i: (i, 0)),
                   pl.BlockSpec((tm, ML_GATE_W), lambda i: (i, 0))],
        out_shape=[jax.ShapeDtypeStruct((t_all, 512), F32), jax.ShapeDtypeStruct((t_all, 512), F32),
                   jax.ShapeDtypeStruct((t_all, ML_GATE_W), F32)],
        compiler_params=_cparams(("parallel",)),
        name="mlstm_prep",
    )(u, u, u, u, u, u, u, wq, wk, gb)


def _hy_prep_kernel(r0, seq_rows, lat_rows, ctx_len,
                    a_ref, ap_ref, an_ref, b_ref, bp_ref, bn_ref, c_ref, cp_ref, cn_ref, w_ref, bias_ref,
                    z_ref, x0_ref):
    tm = a_ref.shape[0]
    first, last = _seq_edges(pl.program_id(0) + r0, tm, seq_rows, lat_rows, ctx_len)
    w, bias = w_ref[...], bias_ref[...]
    outs = []
    for s, (m, p, n) in enumerate(((a_ref, ap_ref, an_ref), (b_ref, bp_ref, bn_ref), (c_ref, cp_ref, cn_ref))):
        sl = slice(s * BRANCH_W, (s + 1) * BRANCH_W)
        outs.append(_conv3(m[...], p[...], n[...], w[:, sl], first, last) + bias[:, sl])
    x0_ref[...] = outs[0]
    z_ref[...] = outs[2] * outs[1]


def _hy_prep(u, w, bias, row0, nrows, seq_rows, lat_rows, ctx_len, tm=256):
    t_all = u.shape[0]
    r0 = row0 // tm
    full = lambda a: pl.BlockSpec(a.shape, lambda i: (0,) * a.ndim)
    halo = lambda cb: _halo_specs(tm, 512, cb, t_all, r0)
    return pl.pallas_call(
        functools.partial(_hy_prep_kernel, r0, seq_rows, lat_rows, ctx_len),
        grid=(nrows // tm,),
        in_specs=halo(CB_HX0) + halo(CB_HX1) + halo(CB_HV) + [full(w), full(bias)],
        out_specs=[pl.BlockSpec((tm, 512), lambda i: (i, 0)), pl.BlockSpec((tm, 512), lambda i: (i, 0))],
        out_shape=[jax.ShapeDtypeStruct((nrows, 512), F32), jax.ShapeDtypeStruct((nrows, 512), F32)],
        compiler_params=_cparams(("parallel",)),
        name="hyena_prep",
    )(u, u, u, u, u, u, u, u, u, w, bias)


def _lane_fold(x, op):
    parts = [x[:, j * LANE:(j + 1) * LANE] for j in range(x.shape[1] // LANE)]
    while len(parts) > 1:
        parts = [op(parts[i], parts[i + 1]) for i in range(0, len(parts) - 1, 2)] + (
            [parts[-1]] if len(parts) % 2 else [])
    return parts[0]


def _attn_kernel(n_maps, seg_lens, tk, lam_init, *refs):
    q_ref = refs[0]
    kv_refs = refs[1:1 + 2 * len(seg_lens)]
    pos = 1 + 2 * len(seg_lens)
    if n_maps == 2:
        lam_ref, gain_ref = refs[pos], refs[pos + 1]
        pos += 2
    o_ref = refs[pos]

    q = q_ref[...]
    tq = q.shape[0]
    if n_maps == 1:
        qs = [q]
    else:
        lane = lax.broadcasted_iota(jnp.int32, q.shape, 1)
        zero = jnp.zeros_like(q)
        qs = [jnp.where(lane < DIF_QK, q, zero), jnp.where(lane >= DIF_QK, q, zero)]

    def step(k, v, carry):
        new = []
        for qm, (m, acc) in zip(qs, carry):
            s = lax.dot_general(qm, k, (((1,), (1,)), ((), ())), preferred_element_type=F32)
            m_new = jnp.maximum(m, jnp.max(_lane_fold(s, jnp.maximum), axis=-1, keepdims=True))
            p = jnp.exp2((s - m_new).astype(BF16))
            alpha = jnp.exp2(m - m_new)
            new.append((m_new, alpha * acc + jnp.dot(p, v, preferred_element_type=F32)))
        return tuple(new)

    carry = tuple((jnp.full((tq, 1), -jnp.inf, F32), jnp.zeros((tq, V_EXT_W), F32)) for _ in range(n_maps))
    for si, slen in enumerate(seg_lens):
        k_ref, v_ref = kv_refs[2 * si], kv_refs[2 * si + 1]
        chunk = min(tk, slen)

        def body(c, carry, k_ref=k_ref, v_ref=v_ref, chunk=chunk):
            start = pl.multiple_of(c * chunk, chunk)
            return step(k_ref[pl.ds(start, chunk), :], v_ref[pl.ds(start, chunk), :], carry)

        trips = slen // chunk
        carry = lax.fori_loop(0, trips, body, carry, unroll=math.gcd(trips, 8))

    outs = [acc[:, :HEAD_W] / acc[:, HEAD_W:HEAD_W + 1] for (_, acc) in carry]
    if n_maps == 1:
        o_ref[...] = outs[0].astype(o_ref.dtype)
    else:
        lp = lam_ref[...]
        lam = (jnp.exp(jnp.sum(lp[0:1] * lp[1:2], axis=-1, keepdims=True))
               - jnp.exp(jnp.sum(lp[2:3] * lp[3:4], axis=-1, keepdims=True)) + lam_init)
        o = outs[0] - lam * outs[1]
        o_ref[...] = (_rms(o) * gain_ref[...] * (1.0 - lam_init)).astype(o_ref.dtype)


def _attention(q, k, v, n_maps, q_row0, q_len, segs, batch, qk_w, lam=None, gain=None, lam_init=0.0,
               tq=1024, tk=512):
    tq = min(tq, q_len)
    nq = q_len // tq
    qb0 = q_row0 // tq
    in_specs = [pl.BlockSpec((tq, qk_w), lambda b, h, i: (qb0 + b * nq + i, h))]
    args = [q]
    for (r0, slen) in segs:
        kb0 = r0 // slen
        in_specs.append(pl.BlockSpec((slen, qk_w), lambda b, h, i, kb0=kb0: (kb0 + b, h)))
        in_specs.append(pl.BlockSpec((slen, V_EXT_W), lambda b, h, i, kb0=kb0: (kb0 + b, h)))
        args += [k, v]
    if n_maps == 2:
        in_specs += [pl.BlockSpec(lam.shape, lambda b, h, i: (0, 0)),
                     pl.BlockSpec(gain.shape, lambda b, h, i: (0, 0))]
        args += [lam, gain]
    return pl.pallas_call(
        functools.partial(_attn_kernel, n_maps, tuple(s for _, s in segs), tk, lam_init),
        grid=(batch, N_HEADS, nq),
        in_specs=in_specs,
        out_specs=pl.BlockSpec((tq, HEAD_W), lambda b, h, i: (b * nq + i, h)),
        out_shape=jax.ShapeDtypeStruct((batch * q_len, N_HEADS * HEAD_W), BF16),
        compiler_params=_cparams(("parallel", "parallel", "arbitrary")),
        name="attention_maps%d" % n_maps,
    )(*args)


ML_STEP = 256


def _mlstm_chunk(rev, o, q_ref, k_ref, v_ref, g_ref, gt_ref, state, d, b, h_ref):
    ti = lax.broadcasted_iota(jnp.int32, (ML_CHUNK, ML_CHUNK), 0)
    si = lax.broadcasted_iota(jnp.int32, (ML_CHUNK, ML_CHUNK), 1)
    tri = (si >= ti) if rev else (si <= ti)
    tri_f = tri.astype(F32)
    tri_t = ((ti >= si) if rev else (ti <= si)).astype(F32)
    gcol = g_ref[o:o + ML_CHUNK, :]
    grow = gt_ref[0, :, o:o + ML_CHUNK]
    li_col = gcol[:, d * 8:d * 8 + 4]
    lf_col = _log_sigmoid(gcol[:, d * 8 + 4:d * 8 + 8])
    li_row = grow[d * 8:d * 8 + 4, :]
    lf_row = _log_sigmoid(grow[d * 8 + 4:d * 8 + 8, :])
    bc_col = _hdot(tri_f, lf_col)
    bc_row = _hdot(lf_row, tri_t)
    f_tot = jnp.sum(lf_col, axis=0, keepdims=True)
    for h in range(N_HEADS):
        hs = slice(h * HEAD_W, (h + 1) * HEAD_W)
        qc = q_ref[o:o + ML_CHUNK, hs]
        kc = k_ref[o:o + ML_CHUNK, hs]
        vc = v_ref[o:o + ML_CHUNK, hs]
        bcc, bcr = bc_col[:, h:h + 1], bc_row[h:h + 1, :]
        icol, irow = li_col[:, h:h + 1], li_row[h:h + 1, :]
        fh = f_tot[:, h:h + 1]
        c_mat, n_vec, m = state[d, b, h]
        g_end_r = fh - bcr + irow
        g_end_c = fh - bcc + icol
        m_new = jnp.maximum(fh + m, jnp.max(g_end_r, axis=-1, keepdims=True))
        decay = jnp.exp(fh + m - m_new)
        wk = jnp.exp(g_end_c - m_new)
        kw = kc * wk
        upd = lax.dot_general(kw.astype(BF16), vc.astype(BF16), (((0,), (0,)), ((), ())),
                              preferred_element_type=F32)
        if h_ref is not None:
            inter = bcc + m
            log_d = jnp.where(tri, bcc - bcr + irow, -jnp.inf)
            m_t = jnp.maximum(inter, jnp.max(log_d, axis=-1, keepdims=True))
            s = _bdot_nt(qc, kc) * jnp.exp(log_d - m_t)
            w_inter = jnp.exp(inter - m_t)
            num = _bdot(s, vc) + w_inter * _bdot(qc, c_mat)
            den = jnp.sum(s, axis=-1, keepdims=True) + w_inter * jnp.sum(qc * n_vec, axis=-1, keepdims=True)
            h_ref[b, o:o + ML_CHUNK, hs] = num / jnp.maximum(jnp.abs(den), jnp.exp(-m_t))
        state[d, b, h] = (decay * c_mat + upd, decay * n_vec + jnp.sum(kw, axis=0, keepdims=True), m_new)


def _mlstm_kernel(batch, with_out, n_steps, *refs):
    n_in = 2 * batch * 5
    chain_refs = refs[:n_in]
    c0_ref, n0_ref, m0_ref = refs[n_in:n_in + 3]
    pos = n_in + 3
    h_refs = (None, None)
    if with_out:
        h_refs = (refs[pos], refs[pos + 1])
        pos += 2
    cf_ref, nf_ref, mf_ref, c_scr, n_scr, m_scr = refs[pos:pos + 6]
    j = pl.program_id(0)

    @pl.when(j == 0)
    def _():
        c_scr[...] = c0_ref[...]
        n_scr[...] = n0_ref[...]
        m_scr[...] = m0_ref[...]

    keys = [(d, b, h) for d in range(2) for b in range(batch) for h in range(N_HEADS)]
    state = {key: (c_scr[key], n_scr[key], m_scr[key][:, 0:1]) for key in keys}
    n_chunks = ML_STEP // ML_CHUNK
    for ci in range(n_chunks):
        for d in range(2):
            c = (n_chunks - 1 - ci) if d == 1 else ci
            for b in range(batch):
                q_ref, k_ref, v_ref, g_ref, gt_ref = chain_refs[(d * batch + b) * 5:(d * batch + b + 1) * 5]
                _mlstm_chunk(d == 1, c * ML_CHUNK, q_ref, k_ref, v_ref, g_ref, gt_ref, state, d, b, h_refs[d])
    for key in keys:
        c_mat, n_vec, m = state[key]
        c_scr[key] = c_mat
        n_scr[key] = n_vec
        m_scr[key] = jnp.broadcast_to(m, (1, HEAD_W))

    @pl.when(j == n_steps - 1)
    def _():
        cf_ref[...] = c_scr[...]
        nf_ref[...] = n_scr[...]
        mf_ref[...] = m_scr[...]


def _mlstm(q, k, u, g, gt, init, with_out, row0, seq_len, batch):
    n_steps = seq_len // ML_STEP
    b0 = row0 // ML_STEP
    in_specs, args = [], []
    for d in range(2):
        for b in range(batch):
            step = (lambda j: n_steps - 1 - j) if d == 1 else (lambda j: j)
            blk = lambda j, b=b, step=step: b0 + b * n_steps + step(j)
            in_specs += [pl.BlockSpec((ML_STEP, 512), lambda j, blk=blk: (blk(j), 0)),
                         pl.BlockSpec((ML_STEP, 512), lambda j, blk=blk: (blk(j), 0)),
                         pl.BlockSpec((ML_STEP, 512), lambda j, blk=blk: (blk(j), CB_MV)),
                         pl.BlockSpec((ML_STEP, ML_GATE_W), lambda j, blk=blk: (blk(j), 0)),
                         pl.BlockSpec((1, ML_GATE_W, ML_STEP), lambda j, b=b, step=step: (b, 0, step(j)))]
            args += [q, k, u, g, gt]
    full = lambda shape: pl.BlockSpec(shape, lambda j: (0,) * len(shape))
    st_dims = [(2, batch, N_HEADS, HEAD_W, HEAD_W), (2, batch, N_HEADS, 1, HEAD_W), (2, batch, N_HEADS, 1, HEAD_W)]
    st_specs = [full(s) for s in st_dims]
    st_shapes = [jax.ShapeDtypeStruct(s, F32) for s in st_dims]
    out_specs, out_shapes = list(st_specs), list(st_shapes)
    if with_out:
        out_specs = [pl.BlockSpec((batch, ML_STEP, 512), lambda j: (0, j, 0)),
                     pl.BlockSpec((batch, ML_STEP, 512), lambda j: (0, n_steps - 1 - j, 0))] + out_specs
        out_shapes = [jax.ShapeDtypeStruct((batch, seq_len, 512), F32)] * 2 + out_shapes
    res = pl.pallas_call(
        functools.partial(_mlstm_kernel, batch, with_out, n_steps),
        grid=(n_steps,),
        in_specs=in_specs + st_specs,
        out_specs=out_specs,
        out_shape=out_shapes,
        scratch_shapes=[pltpu.VMEM(s, F32) for s in st_dims],
        compiler_params=_cparams(("arbitrary",)),
        name="mlstm",
    )(*args, *init)
    if with_out:
        return res[0], res[1], tuple(res[2:])
    return None, None, tuple(res)


def _hy_filter_kernel(n, feat_ref, w1_ref, b1_ref, w2_ref, b2_ref, w3_ref, freq_ref, decay_ref,
                      h_ref, norm_ref):
    i = pl.program_id(0)
    tm = h_ref.shape[1]
    freq = freq_ref[...]
    hid = jnp.sin(freq[0:1, :] * (_hdot(feat_ref[0], w1_ref[...]) + b1_ref[...]))
    hid = jnp.sin(freq[1:2, :] * (_hdot(hid, w2_ref[...]) + b2_ref[...]))
    hw = _hdot(hid, w3_ref[...])
    row = lax.broadcasted_iota(jnp.int32, (tm, BRANCH_W), 0) + i * tm
    decay = jnp.abs(decay_ref[...])
    h_fwd = hw[:tm, :BRANCH_W] * jnp.exp(-(row.astype(F32) / n) * decay[:, :BRANCH_W])
    h_bwd = hw[tm:, BRANCH_W:] * jnp.exp(-((n - row).astype(F32) / n) * decay[:, BRANCH_W:])
    h_bwd = jnp.where(row == 0, 0.0, h_bwd)
    h_ref[0] = h_fwd
    h_ref[1] = h_bwd

    @pl.when(i == 0)
    def _():
        norm_ref[...] = jnp.zeros_like(norm_ref)

    norm_ref[...] += jnp.concatenate([jnp.sum(jnp.abs(h_fwd), axis=0, keepdims=True),
                                      jnp.sum(jnp.abs(h_bwd), axis=0, keepdims=True)], axis=1)


def _hy_filter(n, feat2, w1p, b1, w2, b2, w3, freq, decay_flat):
    tm = feat2.shape[1] // 2
    full = lambda a: pl.BlockSpec(a.shape, lambda i: (0,) * a.ndim)
    return pl.pallas_call(
        functools.partial(_hy_filter_kernel, n),
        grid=(n // tm,),
        in_specs=[pl.BlockSpec((1, 2 * tm, LANE), lambda i: (i, 0, 0)), full(w1p), full(b1), full(w2), full(b2),
                  full(w3), full(freq), full(decay_flat)],
        out_specs=[pl.BlockSpec((2, tm, BRANCH_W), lambda i: (0, i, 0)),
                   pl.BlockSpec((1, 2 * BRANCH_W), lambda i: (0, 0))],
        out_shape=[jax.ShapeDtypeStruct((2, n, BRANCH_W), F32), jax.ShapeDtypeStruct((1, 2 * BRANCH_W), F32)],
        compiler_params=_cparams(("arbitrary",)),
        name="hyena_filter",
    )(feat2, w1p, b1, w2, b2, w3, freq, decay_flat)


def _split_bf16(x):
    hi = x.astype(BF16)
    return hi, (x - hi.astype(F32)).astype(BF16)


def _dot3(a, b):
    d = lambda p, q: jnp.dot(p, q, preferred_element_type=F32)
    return d(a[0], b[0]) + (d(a[0], b[1]) + d(a[1], b[0]))


def _dft_a_kernel(f_ref, x_ref, o_ref):
    o_ref[...] = _dot3((f_ref[0], f_ref[1]), _split_bf16(x_ref[...]))


def _dft_stage_a(table, x, tc=2048):
    k, cols = x.shape
    rows = table.shape[1]
    return pl.pallas_call(
        _dft_a_kernel,
        grid=(cols // tc,),
        in_specs=[pl.BlockSpec(table.shape, lambda j: (0, 0, 0)),
                  pl.BlockSpec((k, tc), lambda j: (0, j))],
        out_specs=pl.BlockSpec((rows, tc), lambda j: (0, j)),
        out_shape=jax.ShapeDtypeStruct((rows, cols), F32),
        compiler_params=_cparams(("parallel",)),
        name="hyena_dft_a",
    )(table, x)


def _dft_c_kernel(az_ref, ah_ref, g_ref, gt_ref, o_ref):
    x = jnp.concatenate([az_ref[:, 0].reshape(2 * LANE, BRANCH_W), ah_ref[:, 0].reshape(2 * LANE, BRANCH_W)],
                        axis=1)
    s = _dot3((g_ref[0, 0], g_ref[1, 0]), _split_bf16(x))
    zr, zi = s[:LANE, :BRANCH_W], s[LANE:, :BRANCH_W]
    hr, hi = s[:LANE, BRANCH_W:], s[LANE:, BRANCH_W:]
    prod = jnp.concatenate([zr * hr - zi * hi, zr * hi + zi * hr], axis=0)
    p = _dot3((gt_ref[0, 0], gt_ref[1, 0]), _split_bf16(prod))
    o_ref[:, 0] = p.reshape(2, LANE, BRANCH_W)


def _dft_stage_c(az, ah, g, gt):
    _, n1, _, c = az.shape
    slab = pl.BlockSpec((2, 1, LANE, c), lambda k: (0, k, 0, 0))
    tab = pl.BlockSpec((2, 1, 2 * LANE, 2 * LANE), lambda k: (0, k, 0, 0))
    return pl.pallas_call(
        _dft_c_kernel,
        grid=(n1,),
        in_specs=[slab, slab, tab, tab],
        out_specs=slab,
        out_shape=jax.ShapeDtypeStruct(az.shape, F32),
        compiler_params=_cparams(("parallel",)),
        name="hyena_dft_c",
    )(az, ah, g, gt)


def _hy_epilogue(y, z, x0, norm, skip):
    reps = y.shape[1] // BRANCH_W
    nsum = norm[:, :BRANCH_W] + norm[:, BRANCH_W:]
    inv = jnp.concatenate([1.0 / nsum] * reps, axis=1)
    sk = jnp.concatenate([skip] * reps, axis=1)
    return (x0 * (y * inv + z * sk)).astype(BF16)


def _dft_a_inv_kernel(f_ref, p_ref, z_ref, x0_ref, norm_ref, skip_ref, o_ref):
    y = _dot3((f_ref[0], f_ref[1]), _split_bf16(p_ref[...]))
    o_ref[...] = _hy_epilogue(y, z_ref[...], x0_ref[...], norm_ref[...], skip_ref[...])


def _dft_stage_a_inv(table, p, z, x0, norm, skip, tc=2048):
    rows, cols = p.shape
    zr = z.shape[0]
    col = lambda r: pl.BlockSpec((r, tc), lambda j: (0, j))
    return pl.pallas_call(
        _dft_a_inv_kernel,
        grid=(cols // tc,),
        in_specs=[pl.BlockSpec(table.shape, lambda j: (0, 0, 0)), col(rows), col(zr), col(zr),
                  pl.BlockSpec(norm.shape, lambda j: (0, 0)), pl.BlockSpec(skip.shape, lambda j: (0, 0))],
        out_specs=col(zr),
        out_shape=jax.ShapeDtypeStruct((zr, cols), BF16),
        compiler_params=_cparams(("parallel",)),
        name="hyena_dft_a_inv",
    )(table, p, z, x0, norm, skip)


def _dft_tables(n):
    big_l = 2 * n
    n1 = big_l // LANE
    n1h = n1 // 2
    two_pi = 2.0 * math.pi
    k1 = jnp.arange(n1, dtype=jnp.int32)
    th = two_pi * ((k1[:, None] * k1[None, :]) % n1).astype(F32) / n1
    c_full, s_full = jnp.cos(th), jnp.sin(th)
    ch, sh = c_full[:, :n1h], s_full[:, :n1h]
    fz = jnp.concatenate([jnp.concatenate([ch, sh], axis=1), jnp.concatenate([-sh, ch], axis=1)], axis=0)
    ff = jnp.concatenate([c_full, -s_full], axis=0)
    ct, st = ch.T / big_l, sh.T / big_l
    fi = jnp.concatenate([jnp.concatenate([ct, -st], axis=1), jnp.concatenate([st, ct], axis=1)], axis=0)
    s2 = jnp.arange(LANE, dtype=jnp.int32)
    k = k1[:, None, None] + n1 * s2[None, :, None]
    ph = two_pi * ((k * s2[None, None, :]) % big_l).astype(F32) / big_l
    gr, gi = jnp.cos(ph), -jnp.sin(ph)
    g = jnp.concatenate([jnp.concatenate([gr, -gi], axis=2), jnp.concatenate([gi, gr], axis=2)], axis=1)
    pair = lambda t: jnp.stack(_split_bf16(t))
    return pair(fz), pair(ff), pair(fi), pair(g), pair(jnp.swapaxes(g, 1, 2))


def _hy_small_kernel(batch, n, z_ref, x0_ref, h_ref, f_ref, fi_ref, norm_ref, skip_ref, o_ref):
    big_l = 2 * n
    f = f_ref[...]
    sz = _hdot(f[:, :n], jnp.concatenate([z_ref[b] for b in range(batch)], axis=1))
    sh = _hdot(f, jnp.concatenate([h_ref[0], h_ref[1]], axis=0))
    hr, hi = sh[:big_l], sh[big_l:]
    prods = []
    for b in range(batch):
        zr = sz[:big_l, b * BRANCH_W:(b + 1) * BRANCH_W]
        zi = sz[big_l:, b * BRANCH_W:(b + 1) * BRANCH_W]
        prods.append(jnp.concatenate([zr * hr - zi * hi, zr * hi + zi * hr], axis=0))
    y = _hdot(fi_ref[...], jnp.concatenate(prods, axis=1))
    for b in range(batch):
        o_ref[b] = _hy_epilogue(y[:, b * BRANCH_W:(b + 1) * BRANCH_W], z_ref[b], x0_ref[b],
                                norm_ref[...], skip_ref[...])


def _hy_small(z, x0, hfilt, norm, skip):
    batch, n, c = z.shape
    big_l = 2 * n
    kk = jnp.arange(big_l, dtype=jnp.int32)
    th = 2.0 * math.pi * ((kk[:, None] * kk[None, :]) % big_l).astype(F32) / big_l
    f = jnp.concatenate([jnp.cos(th), -jnp.sin(th)], axis=0)
    fi = jnp.concatenate([jnp.cos(th[:, :n]).T, -jnp.sin(th[:, :n]).T], axis=1) / big_l
    full = lambda a: pl.BlockSpec(a.shape, lambda i: (0,) * a.ndim)
    return pl.pallas_call(
        functools.partial(_hy_small_kernel, batch, n),
        grid=(1,),
        in_specs=[full(z), full(x0), full(hfilt), full(f), full(fi), full(norm), full(skip)],
        out_specs=full(z),
        out_shape=jax.ShapeDtypeStruct(z.shape, BF16),
        compiler_params=_cparams(("arbitrary",)),
        name="hyena_small",
    )(z, x0, hfilt, f, fi, norm, skip)


def _hy_features(n):
    tm = min(n, 512)
    bands = jnp.linspace(1e-4, HY_BANDS - 1, HY_BANDS, dtype=F32)

    def feats(pos):
        t = pos.astype(F32) / n
        ang = 2.0 * math.pi * t[:, None] * bands[None, :]
        feat = jnp.concatenate([t[:, None], jnp.cos(ang), jnp.sin(ang)], axis=-1)
        return jnp.pad(feat, ((0, 0), (0, LANE - HY_EMB))).reshape(n // tm, tm, LANE)

    r = jnp.arange(n, dtype=jnp.int32)
    return jnp.concatenate([feats(r), feats(n - r)], axis=1)


def _merge_kernel(hy_ref, mla_ref, dif_ref, hf_ref, hb_ref, mo_ref, gate_ref, wb_ref, mln_ref, o_ref):
    hsum = hf_ref[...] + hb_ref[...]
    mln = mln_ref[...]
    y_ml = jnp.concatenate(
        [_rms(hsum[:, h * HEAD_W:(h + 1) * HEAD_W]) * mln[:, h * HEAD_W:(h + 1) * HEAD_W]
         for h in range(N_HEADS)], axis=1)
    y_ml = _sigmoid(mo_ref[...]) * y_ml
    ys = (hy_ref[...], mla_ref[...], dif_ref[...], y_ml.astype(BF16))
    acc = None
    for i in range(4):
        gi = _sigmoid(gate_ref[:, i * D_MODEL:(i + 1) * D_MODEL].astype(F32))
        term = gi * jnp.dot(ys[i], wb_ref[i], preferred_element_type=F32)
        acc = term if acc is None else acc + term
    o_ref[...] = acc.astype(BF16)


def _merge(y_hy, y_mla, y_dif, h_f, h_b, u, gate, wb, mln, nrows, tm=256):
    row = pl.BlockSpec((tm, 512), lambda i: (i, 0))
    full = lambda a: pl.BlockSpec(a.shape, lambda i: (0,) * a.ndim)
    return pl.pallas_call(
        _merge_kernel,
        grid=(nrows // tm,),
        in_specs=[row, row, row, row, row,
                  pl.BlockSpec((tm, 512), lambda i: (i, CB_MO)),
                  pl.BlockSpec((tm, GATE_COLS), lambda i: (i, 0)),
                  full(wb), full(mln)],
        out_specs=pl.BlockSpec((tm, D_MODEL), lambda i: (i, 0)),
        out_shape=jax.ShapeDtypeStruct((nrows, D_MODEL), BF16),
        compiler_params=_cparams(("parallel",), 56),
        name="branch_merge",
    )(y_hy, y_mla, y_dif, h_f, h_b, u, gate, wb, mln)


def _out_kernel(m_ref, w_ref, x_ref, ga_ref, gain_ref, sc_ref, sh_ref, rw_ref, rb_ref,
                xo_ref, f_ref, idx_ref, gt_ref):
    y = jnp.dot(m_ref[...], w_ref[...], preferred_element_type=F32)
    x = x_ref[...] + ga_ref[0] * y
    xo_ref[...] = x
    f = _rms(x) * gain_ref[...] * (1.0 + sc_ref[0]) + sh_ref[0]
    tm = f.shape[0]
    for j in range(ROW_TILES):
        f_ref[pl.ds(j, tm, stride=ROW_TILES), :] = f[:, j * LANE:(j + 1) * LANE]
    logits = _bdot(f, rw_ref[...]) + rb_ref[...]
    lane = lax.broadcasted_iota(jnp.int32, logits.shape, 1)
    idx_out = jnp.zeros(logits.shape, jnp.int32)
    val_out = jnp.full(logits.shape, -jnp.inf, F32)
    work = logits
    for k in range(TOP_K):
        mx = jnp.max(work, axis=-1, keepdims=True)
        am = jnp.min(jnp.where(work == mx, lane, LANE), axis=-1, keepdims=True)
        idx_out = jnp.where(lane == k, am, idx_out)
        val_out = jnp.where(lane == k, mx, val_out)
        work = jnp.where(lane == am, -jnp.inf, work)
    e = jnp.exp(val_out - jnp.max(val_out, axis=-1, keepdims=True))
    gt_ref[...] = e / jnp.sum(e, axis=-1, keepdims=True)
    idx_ref[...] = idx_out


def _out_proj(merged, w_out, x_all, ga, gain, sc, sh, rw, rb, nrows, lat_rows, seq_rows, tm=256):
    grp = functools.partial(_group_of_tile, tm=tm, lat_rows=lat_rows, seq_rows=seq_rows)
    row = lambda w: pl.BlockSpec((tm, w), lambda i: (i, 0))
    full = lambda a: pl.BlockSpec(a.shape, lambda i: (0,) * a.ndim)
    mod = pl.BlockSpec((1, 1, D_MODEL), lambda i: (grp(i), 0, 0))
    return pl.pallas_call(
        _out_kernel,
        grid=(nrows // tm,),
        in_specs=[row(D_MODEL), full(w_out), row(D_MODEL), mod, full(gain), mod, mod, full(rw), full(rb)],
        out_specs=[row(D_MODEL), pl.BlockSpec((tm * ROW_TILES, LANE), lambda i: (i, 0)), row(LANE), row(LANE)],
        out_shape=[jax.ShapeDtypeStruct((nrows, D_MODEL), F32), jax.ShapeDtypeStruct((nrows * ROW_TILES, LANE), F32),
                   jax.ShapeDtypeStruct((nrows, LANE), jnp.int32), jax.ShapeDtypeStruct((nrows, LANE), F32)],
        compiler_params=_cparams(("parallel",)),
        name="out_proj_router",
    )(merged, w_out, x_all, ga, gain, sc, sh, rw, rb)


def _moe_kernel(blk_e_ref, used_ref, tok_ref, nxt_ref, f_hbm, wgu_ref, bgu_ref, wd_ref, bd_ref, o_ref, xbuf, wd_bf,
                sem):
    i = pl.program_id(0)
    last = pl.num_programs(0) - 1
    cur = i % 2

    buf_rows = MOE_BLOCK * ROW_TILES

    def start_rows(idx_ref, s, r0, r1):
        for r in range(r0, r1):
            pltpu.make_async_copy(f_hbm.at[idx_ref[0, 0, r]],
                                  xbuf.at[pl.ds(s * buf_rows + r * ROW_TILES, ROW_TILES), :], sem.at[s]).start()

    def wait_rows(s):
        whole = xbuf.at[pl.ds(s * buf_rows, buf_rows), :]
        pltpu.make_async_copy(whole, whole, sem.at[s]).wait()

    @pl.when(i == 0)
    def _():
        start_rows(tok_ref, 0, 0, MOE_BLOCK)

    wait_rows(cur)

    start_rows(nxt_ref, 1 - cur, 0, MOE_BLOCK)

    @pl.when(i >= used_ref[0])
    def _():
        o_ref[...] = jnp.zeros_like(o_ref)

    @pl.when(i < used_ref[0])
    def _():
        base = cur * buf_rows
        x = jnp.concatenate([xbuf[pl.ds(base + j, MOE_BLOCK, stride=ROW_TILES), :].astype(BF16)
                             for j in range(ROW_TILES)], axis=1)
        @pl.when((i == 0) | (blk_e_ref[i] != blk_e_ref[jnp.maximum(i - 1, 0)]))
        def _():
            wd_bf[...] = wd_ref[0].astype(BF16)

        gu = jnp.dot(x, wgu_ref[0], preferred_element_type=F32) + bgu_ref[0]
        g = jnp.minimum(gu[:, :D_EXPERT], SWIGLU_LIMIT)
        u = jnp.clip(gu[:, D_EXPERT:], -SWIGLU_LIMIT, SWIGLU_LIMIT)
        act = g * _sigmoid(SWIGLU_ALPHA * g) * (u + 1.0)
        o_ref[...] = jnp.dot(act.astype(BF16), wd_bf[...], preferred_element_type=F32) + bd_ref[0]

    @pl.when(i == last)
    def _():
        wait_rows(1 - cur)


def _moe_experts(f_all, blk_e, n_used, slot_tok, wgu, bgu, wd, bd):
    n_blocks = blk_e.shape[0]
    slots = n_blocks * MOE_BLOCK
    n_exp = wgu.shape[0]
    grid_spec = pltpu.PrefetchScalarGridSpec(
        num_scalar_prefetch=2,
        grid=(n_blocks,),
        in_specs=[pl.BlockSpec((1, 1, MOE_BLOCK), lambda i, be, nu: (i, 0, 0), memory_space=pltpu.SMEM),
                  pl.BlockSpec((1, 1, MOE_BLOCK), lambda i, be, nu: (jnp.minimum(i + 1, n_blocks - 1), 0, 0),
                               memory_space=pltpu.SMEM),
                  pl.BlockSpec(memory_space=pl.ANY),
                  pl.BlockSpec((1, D_MODEL, 2 * D_EXPERT), lambda i, be, nu: (be[i], 0, 0)),
                  pl.BlockSpec((1, 1, 2 * D_EXPERT), lambda i, be, nu: (be[i], 0, 0)),
                  pl.BlockSpec((1, D_EXPERT, D_MODEL), lambda i, be, nu: (be[i], 0, 0)),
                  pl.BlockSpec((1, 1, D_MODEL), lambda i, be, nu: (be[i], 0, 0))],
        out_specs=pl.BlockSpec((MOE_BLOCK, D_MODEL), lambda i, be, nu: (i, 0)),
        scratch_shapes=[pltpu.VMEM((2 * MOE_BLOCK * ROW_TILES, LANE), F32), pltpu.VMEM((D_EXPERT, D_MODEL), BF16),
                        pltpu.SemaphoreType.DMA((2,))],
    )
    tok3 = slot_tok.reshape(n_blocks, 1, MOE_BLOCK)
    return pl.pallas_call(
        _moe_kernel,
        grid_spec=grid_spec,
        out_shape=jax.ShapeDtypeStruct((slots, D_MODEL), F32),
        compiler_params=_cparams(("arbitrary",), 56),
        name="moe_experts",
    )(blk_e, n_used, tok3, tok3, f_all,
      wgu, bgu.reshape(n_exp, 1, 2 * D_EXPERT), wd, bd.reshape(n_exp, 1, D_MODEL))


def _combine_kernel(final, dest_ref, nxt_ref, y_hbm, x_ref, tg_ref, gf_ref, fn_ref, o_ref, ybuf, sem):
    tm = x_ref.shape[0]
    i = pl.program_id(0)
    last = pl.num_programs(0) - 1
    cur = i % 2

    def start_rows(idx_ref, s):
        for r in range(tm):
            for k in range(TOP_K):
                pltpu.make_async_copy(y_hbm.at[pl.ds(idx_ref[0, 0, r * TOP_K + k], 1), :],
                                      ybuf.at[s, k, pl.ds(r, 1), :], sem.at[s]).start()

    def wait_rows(s):
        for k in range(TOP_K):
            pltpu.make_async_copy(y_hbm.at[pl.ds(0, tm), :], ybuf.at[s, k], sem.at[s]).wait()

    @pl.when(i == 0)
    def _():
        start_rows(dest_ref, 0)

    start_rows(nxt_ref, 1 - cur)
    wait_rows(cur)
    tg = tg_ref[...]
    moe = None
    for k in range(TOP_K):
        term = tg[:, k:k + 1] * ybuf[cur, k]
        moe = term if moe is None else moe + term
    x = x_ref[...] + gf_ref[0] * moe
    if final:
        x = _rms(x) * fn_ref[...]
    o_ref[...] = x

    @pl.when(i == last)
    def _():
        wait_rows(1 - cur)


def _moe_combine(y_slots, dest, top_gate, x_all, gf, final_gain, final, nrows, lat_rows, seq_rows, tm=128):
    grp = functools.partial(_group_of_tile, tm=tm, lat_rows=lat_rows, seq_rows=seq_rows)
    nt = nrows // tm
    grid_spec = pl.GridSpec(
        grid=(nt,),
        in_specs=[pl.BlockSpec((1, 1, tm * TOP_K), lambda i: (i, 0, 0), memory_space=pltpu.SMEM),
                  pl.BlockSpec((1, 1, tm * TOP_K), lambda i: (jnp.minimum(i + 1, nt - 1), 0, 0),
                               memory_space=pltpu.SMEM),
                  pl.BlockSpec(memory_space=pl.ANY),
                  pl.BlockSpec((tm, D_MODEL), lambda i: (i, 0)),
                  pl.BlockSpec((tm, LANE), lambda i: (i, 0)),
                  pl.BlockSpec((1, 1, D_MODEL), lambda i: (grp(i), 0, 0)),
                  pl.BlockSpec((1, D_MODEL), lambda i: (0, 0))],
        out_specs=pl.BlockSpec((tm, D_MODEL), lambda i: (i, 0)),
        scratch_shapes=[pltpu.VMEM((2, TOP_K, tm, D_MODEL), F32), pltpu.SemaphoreType.DMA((2,))],
    )
    dest3 = dest.reshape(nt, 1, tm * TOP_K)
    return pl.pallas_call(
        functools.partial(_combine_kernel, final),
        grid_spec=grid_spec,
        out_shape=jax.ShapeDtypeStruct((nrows, D_MODEL), F32),
        compiler_params=_cparams(("arbitrary",)),
        name="moe_combine",
    )(dest3, dest3, y_slots, x_all, top_gate, gf, final_gain)


def _expert_onehot(idx):
    lane = lax.broadcasted_iota(jnp.int32, idx.shape, 1)
    oh = jnp.zeros(idx.shape, F32)
    for k in range(TOP_K):
        oh = oh + jnp.where(lane == idx[:, k:k + 1], 1.0, 0.0)
    return oh


def _rank_kernel(idx_ref, rank_ref, cnt_ref, carry):
    @pl.when(pl.program_id(0) == 0)
    def _():
        carry[...] = jnp.zeros_like(carry)

    oh = _expert_onehot(idx_ref[...])
    tm = oh.shape[0]
    r = lax.broadcasted_iota(jnp.int32, (tm, tm), 0)
    c = lax.broadcasted_iota(jnp.int32, (tm, tm), 1)
    earlier = jnp.where(c < r, 1.0, 0.0).astype(BF16)
    rank_ref[...] = jnp.dot(earlier, oh.astype(BF16), preferred_element_type=F32) + carry[...]
    carry[...] += jnp.sum(oh, axis=0, keepdims=True)
    cnt_ref[...] = carry[...]


def _dest_kernel(idx_ref, rank_ref, pstart_ref, dest_ref):
    idx = idx_ref[...]
    slot = rank_ref[...] + pstart_ref[...]
    lane = lax.broadcasted_iota(jnp.int32, idx.shape, 1)
    out = jnp.zeros(idx.shape, jnp.int32)
    for k in range(TOP_K):
        dk = jnp.sum(jnp.where(lane == idx[:, k:k + 1], slot, 0.0), axis=-1, keepdims=True)
        out = jnp.where(lane == k, dk.astype(jnp.int32), out)
    dest_ref[...] = out


def _route(top_idx128, tm=512):
    n = top_idx128.shape[0]
    a = n * TOP_K
    row = pl.BlockSpec((tm, LANE), lambda i: (i, 0))
    one = pl.BlockSpec((1, LANE), lambda i: (0, 0))
    rank, cnt = pl.pallas_call(
        _rank_kernel,
        grid=(n // tm,),
        in_specs=[row],
        out_specs=[row, one],
        out_shape=[jax.ShapeDtypeStruct((n, LANE), F32), jax.ShapeDtypeStruct((1, LANE), F32)],
        scratch_shapes=[pltpu.VMEM((1, LANE), F32)],
        compiler_params=_cparams(("arbitrary",)),
        name="route_rank",
    )(top_idx128)
    counts = cnt[0, :N_EXPERTS].astype(jnp.int32)
    padded = (counts + MOE_BLOCK - 1) // MOE_BLOCK * MOE_BLOCK
    pend = jnp.cumsum(padded)
    pstart = pend - padded
    pstart128 = jnp.pad(pstart.astype(F32), (0, LANE - N_EXPERTS)).reshape(1, LANE)
    dest = pl.pallas_call(
        _dest_kernel,
        grid=(n // tm,),
        in_specs=[row, row, one],
        out_specs=row,
        out_shape=jax.ShapeDtypeStruct((n, LANE), jnp.int32),
        compiler_params=_cparams(("parallel",)),
        name="route_dest",
    )(top_idx128, rank, pstart128)[:, :TOP_K]
    n_blocks = (a + N_EXPERTS * (MOE_BLOCK - 1)) // MOE_BLOCK + 1
    slots = n_blocks * MOE_BLOCK
    tok = jnp.broadcast_to(jnp.arange(n, dtype=jnp.int32)[:, None], (n, TOP_K))
    slot_tok = jnp.zeros((slots,), jnp.int32).at[dest.reshape(-1)].set(tok.reshape(-1))
    first_slot = jnp.arange(n_blocks, dtype=jnp.int32) * MOE_BLOCK
    blk_e = jnp.minimum(jnp.sum((pend[None, :] <= first_slot[:, None]).astype(jnp.int32), axis=1), N_EXPERTS - 1)
    n_used = (pend[-1:] // MOE_BLOCK).astype(jnp.int32)
    return blk_e, n_used, slot_tok, dest.astype(jnp.int32)


def _rope_tables(n_lat_rows, seq_len, ctx_rows):
    t = jnp.arange(seq_len, dtype=jnp.int32)
    inv = ROPE_THETA ** (-jnp.arange(0, 32, 2, dtype=F32) / 32)

    def cs(p):
        ang = p.astype(F32)[:, None] * inv[None, :]
        return jnp.cos(ang), jnp.sin(ang)

    cr, sr = cs(t // GRID_W)
    cc, sc = cs(t % GRID_W)
    c64 = jnp.concatenate([cr, cr, cc, cc], axis=1)
    s64 = jnp.concatenate([-sr, sr, -sc, sc], axis=1)
    reps = n_lat_rows // seq_len

    def build(c_half2, s_half2):
        c = jnp.concatenate([c64, c_half2], axis=1)
        s = jnp.concatenate([s64, s_half2], axis=1)
        c = jnp.concatenate([c] * reps + [jnp.ones((ctx_rows, LANE), F32)], axis=0)
        s = jnp.concatenate([s] * reps + [jnp.zeros((ctx_rows, LANE), F32)], axis=0)
        return c, s

    cd, sd = build(c64, s64)
    cm, sm = build(jnp.ones_like(c64), jnp.zeros_like(s64))
    return cd, sd, cm, sm


W_IN_SEGMENTS = ((320, 0, 512), (832, 512, 512), (1344, 1024, 512), (1856, 1536, 512), (2384, 2048, 512),
                 (2896, 2560, 512), (3408, 3072, 512), (3920, 3584, 512), (4432, 4096, 1536),
                 (0, 5632, 256), (256, 5888, 64), (2368, 5952, 16))
W_IN_USED = 5968
W_IN_GATE0 = 5968


def _w_in_layout_kernel(w_ref, small_ref, gate_ref):
    for src, dst, width in W_IN_SEGMENTS:
        small_ref[:, dst:dst + width] = w_ref[0, :, src:src + width].astype(BF16)
    small_ref[:, W_IN_USED:] = jnp.zeros((small_ref.shape[0], SMALL_COLS - W_IN_USED), BF16)
    gate_ref[...] = w_ref[0, :, W_IN_GATE0:].astype(BF16)


def _layout_w_in(w, layer, tk=128):
    _, d, ncol = w.shape
    return pl.pallas_call(
        _w_in_layout_kernel,
        grid=(d // tk,),
        in_specs=[pl.BlockSpec((1, tk, ncol), lambda i: (layer, i, 0))],
        out_specs=[pl.BlockSpec((tk, SMALL_COLS), lambda i: (i, 0)), pl.BlockSpec((tk, GATE_COLS), lambda i: (i, 0))],
        out_shape=[jax.ShapeDtypeStruct((d, SMALL_COLS), BF16), jax.ShapeDtypeStruct((d, GATE_COLS), BF16)],
        compiler_params=_cparams(("parallel",)),
        name="w_in_layout",
    )(w)


def _layout_w_qb(w):
    w = w.reshape(w.shape[0], N_HEADS, MLA_NOPE + MLA_ROPE)
    w = jnp.pad(w, ((0, 0), (0, 0), (0, MLA_QK_PAD - MLA_NOPE - MLA_ROPE)))
    return w.reshape(w.shape[0], N_HEADS * MLA_QK_PAD).astype(BF16)


def _layout_w_kvb(w):
    w = w.reshape(w.shape[0], N_HEADS, MLA_NOPE + HEAD_W)
    return jnp.concatenate([w[:, :, :MLA_NOPE].reshape(w.shape[0], -1),
                            w[:, :, MLA_NOPE:].reshape(w.shape[0], -1)], axis=1).astype(BF16)


def kernel(x, c, ctx, c_ctx, ada_w, ada_b, norm_mix, norm_ffn, w_in, hy_conv, hy_conv_b, hy_w1, hy_b1, hy_w2, hy_b2, hy_w3, hy_freq, hy_decay, hy_skip, mla_q_norm, mla_w_qb, mla_kv_norm, mla_w_kvb, dif_lambda, dif_norm, ml_conv_q, ml_conv_k, ml_gate_b, ml_norm, w_branch, w_out, router_w, router_b, exp_w_gu, exp_b_gu, exp_w_down, exp_b_down, final_norm):
    batch, n, d = x.shape
    n_ctx = ctx.shape[1]
    depth = w_in.shape[0]
    ctx_rows = batch * n_ctx
    lat_rows = batch * n
    t_all = ctx_rows + lat_rows

    assert batch + 1 <= SUBLANE
    cvec = jnp.zeros((SUBLANE, d), F32).at[:batch].set(c).at[batch].set(c_ctx)
    mod = _modulation(cvec, ada_w, ada_b)
    x_all = jnp.concatenate([x.reshape(lat_rows, d), ctx.reshape(ctx_rows, d)], axis=0)
    tabs = _rope_tables(lat_rows, n, ctx_rows)
    assert batch == 2, "the Hyena transform carries the two batches as one complex signal"
    fz_tab, ff_tab, fi_tab, g_tab, gt_tab = _dft_tables(n)
    feat_lat, feat_ctx = _hy_features(n), _hy_features(n_ctx)
    zero_state = (jnp.zeros((2, batch, N_HEADS, HEAD_W, HEAD_W), F32), jnp.zeros((2, batch, N_HEADS, 1, HEAD_W), F32),
                  jnp.zeros((2, batch, N_HEADS, 1, HEAD_W), F32))
    n1h = n // LANE
    wgu_all = exp_w_gu.astype(BF16).reshape(depth * N_EXPERTS, d, 2 * D_EXPERT)
    wd_all = exp_w_down.reshape(depth * N_EXPERTS, D_EXPERT, d)
    bgu_all = exp_b_gu.reshape(depth * N_EXPERTS, 2 * D_EXPERT)
    bd_all = exp_b_down.reshape(depth * N_EXPERTS, d)

    for l in range(depth):
        last = l == depth - 1
        lam_init = 0.8 - 0.6 * math.exp(-0.3 * l)
        m6 = mod[l].reshape(SUBLANE, 6, 1, d)
        sh_a, sc_a, g_a, sh_f, sc_f, g_f = (m6[:, i] for i in range(6))
        w_small, w_gate = _layout_w_in(w_in, l)
        gain_mix = norm_mix[l].reshape(1, d)

        n_tok = lat_rows if last else t_all
        h_in = _norm_mod(x_all, gain_mix, sc_a, sh_a, lat_rows, n)
        u = _in_proj(h_in, w_small, F32, t_all)
        gate = _in_proj(h_in, w_gate, BF16, n_tok)

        mq, mk, mv, dq, dk, dv = _attn_prep(
            u, tabs, _layout_w_qb(mla_w_qb[l]), _layout_w_kvb(mla_w_kvb[l]),
            mla_q_norm[l].reshape(1, -1), mla_kv_norm[l].reshape(1, -1))
        ml_q, ml_k, ml_g = _ml_prep(u, ml_conv_q[l], ml_conv_k[l], ml_gate_b[l].reshape(1, 16), n, lat_rows, n_ctx)
        gt_lat = jnp.swapaxes(ml_g[:lat_rows].reshape(batch, n, ML_GATE_W), 1, 2)
        gt_ctx = jnp.swapaxes(ml_g[lat_rows:].reshape(batch, n_ctx, ML_GATE_W), 1, 2)

        lam_p = dif_lambda[l]
        dgain = dif_norm[l].reshape(1, HEAD_W)
        segs = [(lat_rows, n_ctx), (0, n)]
        y_mla = _attention(mq, mk, mv, 1, 0, n, segs, batch, MLA_QK_PAD)
        y_dif = _attention(dq, dk, dv, 2, 0, n, segs, batch, 2 * DIF_QK, lam_p, dgain, lam_init)

        hcf, hcb, st_ctx = _mlstm(ml_q, ml_k, u, ml_g, gt_ctx, zero_state, not last, lat_rows, n_ctx, batch)
        hlf, hlb, _ = _mlstm(ml_q, ml_k, u, ml_g, gt_lat, st_ctx, True, 0, n, batch)

        hw1 = jnp.pad(hy_w1[l], ((0, LANE - HY_EMB), (0, 0)))
        filt_args = (hw1, hy_b1[l].reshape(1, -1), hy_w2[l], hy_b2[l].reshape(1, -1), hy_w3[l], hy_freq[l],
                     hy_decay[l].reshape(1, -1))
        skip = hy_skip[l].reshape(1, -1)
        hconv_b = hy_conv_b[l].reshape(1, -1)
        z, x0 = _hy_prep(u, hy_conv[l], hconv_b, 0, lat_rows, n, lat_rows, n_ctx)
        hfilt, hnorm = _hy_filter(n, feat_lat, *filt_args)
        filt_full = hfilt.reshape(2 * n, BRANCH_W)
        n1 = 2 * n1h
        cols = LANE * BRANCH_W
        zr = z.reshape(batch * n1h, cols)
        az = _dft_stage_a(fz_tab, zr)
        ah = _dft_stage_a(ff_tab, filt_full.reshape(n1, cols))
        pz = _dft_stage_c(az.reshape(2, n1, LANE, BRANCH_W), ah.reshape(2, n1, LANE, BRANCH_W), g_tab, gt_tab)
        y_hy = _dft_stage_a_inv(fi_tab, pz.reshape(2 * n1, cols), zr, x0.reshape(batch * n1h, cols), hnorm,
                                skip).reshape(lat_rows, BRANCH_W)

        h_f, h_b = hlf.reshape(lat_rows, -1), hlb.reshape(lat_rows, -1)
        w_o = w_out[l].astype(BF16)
        gain_ffn = norm_ffn[l].reshape(1, d)
        rw = jnp.pad(router_w[l], ((0, 0), (0, LANE - N_EXPERTS))).astype(BF16)
        rb = jnp.concatenate([router_b[l], jnp.full((LANE - N_EXPERTS,), -jnp.inf, F32)]).reshape(1, LANE)

        if not last:
            cseg = [(lat_rows, n_ctx)]
            yc_mla = _attention(mq, mk, mv, 1, lat_rows, n_ctx, cseg, batch, MLA_QK_PAD)
            yc_dif = _attention(dq, dk, dv, 2, lat_rows, n_ctx, cseg, batch, 2 * DIF_QK, lam_p, dgain, lam_init)
            zc, x0c = _hy_prep(u, hy_conv[l], hconv_b, lat_rows, ctx_rows, n, lat_rows, n_ctx)
            hfc, hnc = _hy_filter(n_ctx, feat_ctx, *filt_args)
            yc_hy = _hy_small(zc.reshape(batch, n_ctx, BRANCH_W), x0c.reshape(batch, n_ctx, BRANCH_W), hfc, hnc,
                              skip).reshape(ctx_rows, BRANCH_W)
            cat = lambda a, b: jnp.concatenate([a, b], axis=0)
            y_hy, y_mla, y_dif = cat(y_hy, yc_hy), cat(y_mla, yc_mla), cat(y_dif, yc_dif)
            h_f, h_b = cat(h_f, hcf.reshape(ctx_rows, -1)), cat(h_b, hcb.reshape(ctx_rows, -1))

        merged = _merge(y_hy, y_mla, y_dif, h_f, h_b, u, gate, w_branch[l].astype(BF16), ml_norm[l].reshape(1, -1),
                        n_tok)
        xo, f_all, top_idx, top_gate = _out_proj(merged, w_o, x_all, g_a, gain_ffn, sc_f, sh_f, rw, rb,
                                                 n_tok, lat_rows, n)
        blk_e, n_used, slot_tok, dest = _route(top_idx)
        y_slots = _moe_experts(f_all.reshape(n_tok, ROW_TILES, LANE), blk_e + l * N_EXPERTS, n_used, slot_tok,
                               wgu_all, bgu_all, wd_all, bd_all)
        fgain = final_norm.reshape(1, d)
        x_all = _moe_combine(y_slots, dest, top_gate, xo, g_f, fgain, last, n_tok, lat_rows, n)

    return x_all.reshape(batch, n, d)
```

```python
import functools
import math

import jax
import jax.numpy as jnp
from jax import lax
from jax.experimental import pallas as pl
from jax.experimental.pallas import tpu as pltpu

F32 = jnp.float32
BF16 = jnp.bfloat16
HIGHEST = lax.Precision.HIGHEST

D_MODEL = 2048
GRID_W = 64
BRANCH_W = 512
N_HEADS = 4
HEAD_W = 128
V_EXT_W = 256
MLA_NOPE = 128
MLA_ROPE = 64
MLA_QK_PAD = 256
MLA_SCALE = (MLA_NOPE + MLA_ROPE) ** -0.5
LOG2E = math.log2(math.e)
DIF_QK = 64
ML_CHUNK = 64
ML_GATE_W = 16
N_EXPERTS = 32
TOP_K = 4
D_EXPERT = 1024
SWIGLU_LIMIT = 7.0
SWIGLU_ALPHA = 1.702
MOE_BLOCK = 256
MOE_BUFS = 3
ROPE_THETA = 10000.0
NORM_EPS = 1e-6
HY_BANDS = 16
HY_EMB = 1 + 2 * HY_BANDS
HY_HID = 64
LANE = 128
SUBLANE = 8
ROW_TILES = D_MODEL // LANE
SMALL_COLS = 6144
GATE_COLS = 4 * D_MODEL

CB_DK, CB_DV, CB_MK, CB_MV, CB_QA, CB_DQ, CB_MQ, CB_MO, CB_HX0, CB_HX1, CB_HV = range(11)
CB_CKV_256 = 22
CB_KPE_128 = 46


def _cparams(sem, vmem_mb=48):
    return pltpu.CompilerParams(dimension_semantics=sem, vmem_limit_bytes=vmem_mb * 1024 * 1024)


def _rms(x):
    return x * lax.rsqrt(jnp.mean(x * x, axis=-1, keepdims=True) + NORM_EPS)


def _sigmoid(x):
    return 1.0 / (1.0 + jnp.exp(-x))


def _silu(x):
    return x * _sigmoid(x)


def _log_sigmoid(x):
    return jnp.minimum(x, 0.0) - jnp.log(1.0 + jnp.exp(-jnp.abs(x)))


def _bdot(a, b):
    return jnp.dot(a.astype(BF16), b.astype(BF16), preferred_element_type=F32)


def _bdot_nt(a, b):
    return lax.dot_general(a.astype(BF16), b.astype(BF16), (((1,), (1,)), ((), ())),
                           preferred_element_type=F32)


def _hdot(a, b):
    return jnp.dot(a, b, preferred_element_type=F32, precision=HIGHEST)


def _mod_kernel(c_ref, w_ref, b_ref, o_ref):
    o_ref[0] = _bdot(_silu(c_ref[...]), w_ref[0]) + b_ref[0]


def _modulation(cvec, ada_w, ada_b):
    n_layers, d, six_d = ada_w.shape
    tn = 1024
    return pl.pallas_call(
        _mod_kernel,
        grid=(n_layers, six_d // tn),
        in_specs=[pl.BlockSpec((SUBLANE, d), lambda l, j: (0, 0)),
                  pl.BlockSpec((1, d, tn), lambda l, j: (l, 0, j)),
                  pl.BlockSpec((1, 1, tn), lambda l, j: (l, 0, j))],
        out_specs=pl.BlockSpec((1, SUBLANE, tn), lambda l, j: (l, 0, j)),
        out_shape=jax.ShapeDtypeStruct((n_layers, SUBLANE, six_d), F32),
        compiler_params=_cparams(("parallel", "parallel")),
        name="adaln_modulation",
    )(cvec, ada_w, ada_b.reshape(n_layers, 1, six_d))


def _group_of_tile(i, tm, lat_rows, seq_rows):
    return jnp.minimum(i // (seq_rows // tm), lat_rows // seq_rows)


def _norm_mod_kernel(x_ref, g_ref, sc_ref, sh_ref, o_ref):
    h = _rms(x_ref[...]) * g_ref[...]
    o_ref[...] = (h * (1.0 + sc_ref[0]) + sh_ref[0]).astype(BF16)


def _norm_mod(x_all, gain, sc, sh, lat_rows, seq_rows, tm=512):
    t_all, d = x_all.shape
    grp = functools.partial(_group_of_tile, tm=tm, lat_rows=lat_rows, seq_rows=seq_rows)
    return pl.pallas_call(
        _norm_mod_kernel,
        grid=(t_all // tm,),
        in_specs=[pl.BlockSpec((tm, d), lambda i: (i, 0)),
                  pl.BlockSpec((1, d), lambda i: (0, 0)),
                  pl.BlockSpec((1, 1, d), lambda i: (grp(i), 0, 0)),
                  pl.BlockSpec((1, 1, d), lambda i: (grp(i), 0, 0))],
        out_specs=pl.BlockSpec((tm, d), lambda i: (i, 0)),
        out_shape=jax.ShapeDtypeStruct((t_all, d), BF16),
        compiler_params=_cparams(("parallel",)),
        name="norm_mod",
    )(x_all, gain, sc, sh)


def _proj_kernel(h_ref, w_ref, o_ref):
    o_ref[...] = jnp.dot(h_ref[...], w_ref[...], preferred_element_type=F32).astype(o_ref.dtype)


def _in_proj(h, w, out_dtype, nrows, tm=512, tn=1024):
    d = h.shape[1]
    ncol = w.shape[1]
    return pl.pallas_call(
        _proj_kernel,
        grid=(ncol // tn, nrows // tm),
        in_specs=[pl.BlockSpec((tm, d), lambda j, i: (i, 0)),
                  pl.BlockSpec((d, tn), lambda j, i: (0, j))],
        out_specs=pl.BlockSpec((tm, tn), lambda j, i: (i, j)),
        out_shape=jax.ShapeDtypeStruct((nrows, ncol), out_dtype),
        compiler_params=_cparams(("parallel", "arbitrary")),
        name="in_proj",
    )(h, w)


def _rope(x, c, s):
    w = x.shape[-1]
    lane = lax.broadcasted_iota(jnp.int32, x.shape, 1)
    up = pltpu.roll(x, w - 16, 1)
    dn = pltpu.roll(x, 16, 1)
    return x * c + jnp.where((lane % 32) < 16, up, dn) * s


def _attn_prep_kernel(dk_ref, dv_ref, qa_ref, dq_ref, ckv_ref, kpe_ref, cd_ref, sd_ref, cm_ref, sm_ref,
                      wqb_ref, wkvb_ref, qn_ref, kvn_ref,
                      mq_ref, mk_ref, mv_ref, dqo_ref, dko_ref, dvo_ref):
    cd4 = jnp.concatenate([cd_ref[...]] * N_HEADS, axis=1)
    sd4 = jnp.concatenate([sd_ref[...]] * N_HEADS, axis=1)
    cm, sm = cm_ref[...], sm_ref[...]
    dqo_ref[...] = (_rope(dq_ref[...], cd4, sd4) * (DIF_QK ** -0.5 * LOG2E)).astype(BF16)
    dko_ref[...] = _rope(dk_ref[...], cd4, sd4).astype(BF16)
    dv = dv_ref[...]
    vlane = lax.broadcasted_iota(jnp.int32, (dv.shape[0], V_EXT_W - HEAD_W), 1)
    ones_col = jnp.where(vlane == 0, 1.0, 0.0).astype(BF16)

    q = _bdot(_rms(qa_ref[...]) * qn_ref[...], wqb_ref[...])
    kv = _bdot(_rms(ckv_ref[...]) * kvn_ref[...], wkvb_ref[...])
    kpe = kpe_ref[...]
    lane = lax.broadcasted_iota(jnp.int32, kpe.shape, 1)
    kpe = _rope(jnp.where(lane < MLA_ROPE, kpe, 0.0), cm, sm).astype(BF16)
    for h in range(N_HEADS):
        o = h * MLA_QK_PAD
        mq_ref[:, o:o + MLA_NOPE] = (q[:, o:o + MLA_NOPE] * (MLA_SCALE * LOG2E)).astype(BF16)
        mq_ref[:, o + MLA_NOPE:o + MLA_QK_PAD] = (
            _rope(q[:, o + MLA_NOPE:o + MLA_QK_PAD], cm, sm) * (MLA_SCALE * LOG2E)).astype(BF16)
        mk_ref[:, o:o + MLA_NOPE] = kv[:, h * MLA_NOPE:(h + 1) * MLA_NOPE].astype(BF16)
        mk_ref[:, o + MLA_NOPE:o + MLA_QK_PAD] = kpe
    for h in range(N_HEADS):
        o = h * V_EXT_W
        mv_ref[:, o:o + HEAD_W] = kv[:, (N_HEADS + h) * HEAD_W:(N_HEADS + h + 1) * HEAD_W].astype(BF16)
        mv_ref[:, o + HEAD_W:o + V_EXT_W] = ones_col
        dvo_ref[:, o:o + HEAD_W] = dv[:, h * HEAD_W:(h + 1) * HEAD_W].astype(BF16)
        dvo_ref[:, o + HEAD_W:o + V_EXT_W] = ones_col


def _attn_prep(u, tabs, wqb, wkvb, qn, kvn, tm=256):
    t_all = u.shape[0]
    cd, sd, cm, sm = tabs
    col = lambda w, cb: pl.BlockSpec((tm, w), lambda i, cb=cb: (i, cb))
    tab = pl.BlockSpec((tm, LANE), lambda i: (i, 0))
    full = lambda a: pl.BlockSpec(a.shape, lambda i: (0,) * a.ndim)
    out = lambda w: pl.BlockSpec((tm, w), lambda i: (i, 0))
    shp = lambda w: jax.ShapeDtypeStruct((t_all, w), BF16)
    return pl.pallas_call(
        _attn_prep_kernel,
        grid=(t_all // tm,),
        in_specs=[col(512, CB_DK), col(512, CB_DV), col(512, CB_QA), col(512, CB_DQ),
                  col(256, CB_CKV_256), col(128, CB_KPE_128), tab, tab, tab, tab,
                  full(wqb), full(wkvb), full(qn), full(kvn)],
        out_specs=[out(1024), out(1024), out(1024), out(512), out(512), out(1024)],
        out_shape=[shp(1024), shp(1024), shp(1024), shp(512), shp(512), shp(1024)],
        compiler_params=_cparams(("parallel",)),
        name="attn_prep",
    )(u, u, u, u, u, u, cd, sd, cm, sm, wqb, wkvb, qn, kvn)


def _conv3(x, prev_blk, next_blk, w, first, last):
    tm = x.shape[0]
    row = lax.broadcasted_iota(jnp.int32, x.shape, 0)
    prev_row = jnp.where(first, 0.0, prev_blk[SUBLANE - 1:SUBLANE, :])
    next_row = jnp.where(last, 0.0, next_blk[0:1, :])
    xm = jnp.where(row == 0, prev_row, pltpu.roll(x, 1, 0))
    xp = jnp.where(row == tm - 1, next_row, pltpu.roll(x, tm - 1, 0))
    return xm * w[0:1, :] + x * w[1:2, :] + xp * w[2:3, :]


def _seq_edges(tile, tm, seq_rows, lat_rows, ctx_len):
    r = tile * tm
    in_lat = r < lat_rows
    pos = jnp.where(in_lat, r % seq_rows, (r - lat_rows) % ctx_len)
    length = jnp.where(in_lat, seq_rows, ctx_len)
    return pos == 0, pos + tm == length


def _ml_prep_kernel(seq_rows, lat_rows, ctx_len,
                    q_ref, qp_ref, qn_ref, k_ref, kp_ref, kn_ref, gblk_ref, wq_ref, wk_ref, gb_ref,
                    qo_ref, ko_ref, go_ref):
    tm = q_ref.shape[0]
    first, last = _seq_edges(pl.program_id(0), tm, seq_rows, lat_rows, ctx_len)
    qo_ref[...] = _silu(_conv3(q_ref[...], qp_ref[...], qn_ref[...], wq_ref[...], first, last))
    ko_ref[...] = _silu(_conv3(k_ref[...], kp_ref[...], kn_ref[...], wk_ref[...], first, last)) * HEAD_W ** -0.5
    go_ref[...] = gblk_ref[:, MLA_ROPE:MLA_ROPE + 16] + gb_ref[...]


def _halo_specs(tm, w, cb, t_all, r0=0):
    nb8 = t_all // SUBLANE
    r8 = tm // SUBLANE
    return [pl.BlockSpec((tm, w), lambda i: (i + r0, cb)),
            pl.BlockSpec((SUBLANE, w), lambda i: (jnp.maximum((i + r0) * r8 - 1, 0), cb)),
            pl.BlockSpec((SUBLANE, w), lambda i: (jnp.minimum((i + r0 + 1) * r8, nb8 - 1), cb))]


def _ml_prep(u, wq, wk, gb, seq_rows, lat_rows, ctx_len, tm=256):
    t_all = u.shape[0]
    full = lambda a: pl.BlockSpec(a.shape, lambda i: (0,) * a.ndim)
    return pl.pallas_call(
        functools.partial(_ml_prep_kernel, seq_rows, lat_rows, ctx_len),
        grid=(t_all // tm,),
        in_specs=_halo_specs(tm, 512, CB_MQ, t_all) + _halo_specs(tm, 512, CB_MK, t_all)
        + [pl.BlockSpec((tm, LANE), lambda i: (i, CB_KPE_128)), full(wq), full(wk), full(gb)],
        out_specs=[pl.BlockSpec((tm, 512), lambda i: (i, 0)), pl.BlockSpec((tm, 512), lambda i: (i, 0)),
                   pl.BlockSpec((tm, ML_GATE_W), lambda i: (i, 0))],
        out_shape=[jax.ShapeDtypeStruct((t_all, 512), F32), jax.ShapeDtypeStruct((t_all, 512), F32),
                   jax.ShapeDtypeStruct((t_all, ML_GATE_W), F32)],
        compiler_params=_cparams(("parallel",)),
        name="mlstm_prep",
    )(u, u, u, u, u, u, u, wq, wk, gb)


def _hy_prep_kernel(r0, seq_rows, lat_rows, ctx_len,
                    a_ref, ap_ref, an_ref, b_ref, bp_ref, bn_ref, c_ref, cp_ref, cn_ref, w_ref, bias_ref,
                    z_ref, x0_ref):
    tm = a_ref.shape[0]
    first, last = _seq_edges(pl.program_id(0) + r0, tm, seq_rows, lat_rows, ctx_len)
    w, bias = w_ref[...], bias_ref[...]
    outs = []
    for s, (m, p, n) in enumerate(((a_ref, ap_ref, an_ref), (b_ref, bp_ref, bn_ref), (c_ref, cp_ref, cn_ref))):
        sl = slice(s * BRANCH_W, (s + 1) * BRANCH_W)
        outs.append(_conv3(m[...], p[...], n[...], w[:, sl], first, last) + bias[:, sl])
    x0_ref[...] = outs[0]
    z_ref[...] = outs[2] * outs[1]


def _hy_prep(u, w, bias, row0, nrows, seq_rows, lat_rows, ctx_len, tm=256):
    t_all = u.shape[0]
    r0 = row0 // tm
    full = lambda a: pl.BlockSpec(a.shape, lambda i: (0,) * a.ndim)
    halo = lambda cb: _halo_specs(tm, 512, cb, t_all, r0)
    return pl.pallas_call(
        functools.partial(_hy_prep_kernel, r0, seq_rows, lat_rows, ctx_len),
        grid=(nrows // tm,),
        in_specs=halo(CB_HX0) + halo(CB_HX1) + halo(CB_HV) + [full(w), full(bias)],
        out_specs=[pl.BlockSpec((tm, 512), lambda i: (i, 0)), pl.BlockSpec((tm, 512), lambda i: (i, 0))],
        out_shape=[jax.ShapeDtypeStruct((nrows, 512), F32), jax.ShapeDtypeStruct((nrows, 512), F32)],
        compiler_params=_cparams(("parallel",)),
        name="hyena_prep",
    )(u, u, u, u, u, u, u, u, u, w, bias)


def _lane_fold(x, op):
    parts = [x[:, j * LANE:(j + 1) * LANE] for j in range(x.shape[1] // LANE)]
    while len(parts) > 1:
        parts = [op(parts[i], parts[i + 1]) for i in range(0, len(parts) - 1, 2)] + (
            [parts[-1]] if len(parts) % 2 else [])
    return parts[0]


def _attn_kernel(n_maps, seg_lens, tk, lam_init, *refs):
    q_ref = refs[0]
    kv_refs = refs[1:1 + 2 * len(seg_lens)]
    pos = 1 + 2 * len(seg_lens)
    if n_maps == 2:
        lam_ref, gain_ref = refs[pos], refs[pos + 1]
        pos += 2
    o_ref = refs[pos]

    q = q_ref[...]
    tq = q.shape[0]
    if n_maps == 1:
        qs = [q]
    else:
        lane = lax.broadcasted_iota(jnp.int32, q.shape, 1)
        zero = jnp.zeros_like(q)
        qs = [jnp.where(lane < DIF_QK, q, zero), jnp.where(lane >= DIF_QK, q, zero)]

    def step(k, v, carry):
        new = []
        for qm, (m, acc) in zip(qs, carry):
            s = lax.dot_general(qm, k, (((1,), (1,)), ((), ())), preferred_element_type=F32)
            m_new = jnp.maximum(m, jnp.max(_lane_fold(s, jnp.maximum), axis=-1, keepdims=True))
            p = jnp.exp2((s - m_new).astype(BF16))
            alpha = jnp.exp2(m - m_new)
            new.append((m_new, alpha * acc + jnp.dot(p, v, preferred_element_type=F32)))
        return tuple(new)

    carry = tuple((jnp.full((tq, 1), -jnp.inf, F32), jnp.zeros((tq, V_EXT_W), F32)) for _ in range(n_maps))
    for si, slen in enumerate(seg_lens):
        k_ref, v_ref = kv_refs[2 * si], kv_refs[2 * si + 1]
        chunk = min(tk, slen)

        def body(c, carry, k_ref=k_ref, v_ref=v_ref, chunk=chunk):
            start = pl.multiple_of(c * chunk, chunk)
            return step(k_ref[pl.ds(start, chunk), :], v_ref[pl.ds(start, chunk), :], carry)

        trips = slen // chunk
        carry = lax.fori_loop(0, trips, body, carry, unroll=math.gcd(trips, 8))

    outs = [acc[:, :HEAD_W] / acc[:, HEAD_W:HEAD_W + 1] for (_, acc) in carry]
    if n_maps == 1:
        o_ref[...] = outs[0].astype(o_ref.dtype)
    else:
        lp = lam_ref[...]
        lam = (jnp.exp(jnp.sum(lp[0:1] * lp[1:2], axis=-1, keepdims=True))
               - jnp.exp(jnp.sum(lp[2:3] * lp[3:4], axis=-1, keepdims=True)) + lam_init)
        o = outs[0] - lam * outs[1]
        o_ref[...] = (_rms(o) * gain_ref[...] * (1.0 - lam_init)).astype(o_ref.dtype)


def _attention(q, k, v, n_maps, q_row0, q_len, segs, batch, qk_w, lam=None, gain=None, lam_init=0.0,
               tq=1024, tk=512):
    tq = min(tq, q_len)
    nq = q_len // tq
    qb0 = q_row0 // tq
    in_specs = [pl.BlockSpec((tq, qk_w), lambda b, h, i: (qb0 + b * nq + i, h))]
    args = [q]
    for (r0, slen) in segs:
        kb0 = r0 // slen
        in_specs.append(pl.BlockSpec((slen, qk_w), lambda b, h, i, kb0=kb0: (kb0 + b, h)))
        in_specs.append(pl.BlockSpec((slen, V_EXT_W), lambda b, h, i, kb0=kb0: (kb0 + b, h)))
        args += [k, v]
    if n_maps == 2:
        in_specs += [pl.BlockSpec(lam.shape, lambda b, h, i: (0, 0)),
                     pl.BlockSpec(gain.shape, lambda b, h, i: (0, 0))]
        args += [lam, gain]
    return pl.pallas_call(
        functools.partial(_attn_kernel, n_maps, tuple(s for _, s in segs), tk, lam_init),
        grid=(batch, N_HEADS, nq),
        in_specs=in_specs,
        out_specs=pl.BlockSpec((tq, HEAD_W), lambda b, h, i: (b * nq + i, h)),
        out_shape=jax.ShapeDtypeStruct((batch * q_len, N_HEADS * HEAD_W), BF16),
        compiler_params=_cparams(("parallel", "parallel", "arbitrary")),
        name="attention_maps%d" % n_maps,
    )(*args)


ML_STEP = 256


def _mlstm_chunk(rev, o, q_ref, k_ref, v_ref, g_ref, gt_ref, state, d, b, h_ref):
    ti = lax.broadcasted_iota(jnp.int32, (ML_CHUNK, ML_CHUNK), 0)
    si = lax.broadcasted_iota(jnp.int32, (ML_CHUNK, ML_CHUNK), 1)
    tri = (si >= ti) if rev else (si <= ti)
    tri_f = tri.astype(F32)
    tri_t = ((ti >= si) if rev else (ti <= si)).astype(F32)
    gcol = g_ref[o:o + ML_CHUNK, :]
    grow = gt_ref[0, :, o:o + ML_CHUNK]
    li_col = gcol[:, d * 8:d * 8 + 4]
    lf_col = _log_sigmoid(gcol[:, d * 8 + 4:d * 8 + 8])
    li_row = grow[d * 8:d * 8 + 4, :]
    lf_row = _log_sigmoid(grow[d * 8 + 4:d * 8 + 8, :])
    bc_col = _hdot(tri_f, lf_col)
    bc_row = _hdot(lf_row, tri_t)
    f_tot = jnp.sum(lf_col, axis=0, keepdims=True)
    for h in range(N_HEADS):
        hs = slice(h * HEAD_W, (h + 1) * HEAD_W)
        qc = q_ref[o:o + ML_CHUNK, hs]
        kc = k_ref[o:o + ML_CHUNK, hs]
        vc = v_ref[o:o + ML_CHUNK, hs]
        bcc, bcr = bc_col[:, h:h + 1], bc_row[h:h + 1, :]
        icol, irow = li_col[:, h:h + 1], li_row[h:h + 1, :]
        fh = f_tot[:, h:h + 1]
        c_mat, n_vec, m = state[d, b, h]
        g_end_r = fh - bcr + irow
        g_end_c = fh - bcc + icol
        m_new = jnp.maximum(fh + m, jnp.max(g_end_r, axis=-1, keepdims=True))
        decay = jnp.exp(fh + m - m_new)
        wk = jnp.exp(g_end_c - m_new)
        kw = kc * wk
        upd = lax.dot_general(kw.astype(BF16), vc.astype(BF16), (((0,), (0,)), ((), ())),
                              preferred_element_type=F32)
        if h_ref is not None:
            inter = bcc + m
            log_d = jnp.where(tri, bcc - bcr + irow, -jnp.inf)
            m_t = jnp.maximum(inter, jnp.max(log_d, axis=-1, keepdims=True))
            s = _bdot_nt(qc, kc) * jnp.exp(log_d - m_t)
            w_inter = jnp.exp(inter - m_t)
            num = _bdot(s, vc) + w_inter * _bdot(qc, c_mat)
            den = jnp.sum(s, axis=-1, keepdims=True) + w_inter * jnp.sum(qc * n_vec, axis=-1, keepdims=True)
            h_ref[b, o:o + ML_CHUNK, hs] = num / jnp.maximum(jnp.abs(den), jnp.exp(-m_t))
        state[d, b, h] = (decay * c_mat + upd, decay * n_vec + jnp.sum(kw, axis=0, keepdims=True), m_new)


def _mlstm_kernel(batch, with_out, n_steps, *refs):
    n_in = 2 * batch * 5
    chain_refs = refs[:n_in]
    c0_ref, n0_ref, m0_ref = refs[n_in:n_in + 3]
    pos = n_in + 3
    h_refs = (None, None)
    if with_out:
        h_refs = (refs[pos], refs[pos + 1])
        pos += 2
    cf_ref, nf_ref, mf_ref, c_scr, n_scr, m_scr = refs[pos:pos + 6]
    j = pl.program_id(0)

    @pl.when(j == 0)
    def _():
        c_scr[...] = c0_ref[...]
        n_scr[...] = n0_ref[...]
        m_scr[...] = m0_ref[...]

    keys = [(d, b, h) for d in range(2) for b in range(batch) for h in range(N_HEADS)]
    state = {key: (c_scr[key], n_scr[key], m_scr[key][:, 0:1]) for key in keys}
    n_chunks = ML_STEP // ML_CHUNK
    for ci in range(n_chunks):
        for d in range(2):
            c = (n_chunks - 1 - ci) if d == 1 else ci
            for b in range(batch):
                q_ref, k_ref, v_ref, g_ref, gt_ref = chain_refs[(d * batch + b) * 5:(d * batch + b + 1) * 5]
                _mlstm_chunk(d == 1, c * ML_CHUNK, q_ref, k_ref, v_ref, g_ref, gt_ref, state, d, b, h_refs[d])
    for key in keys:
        c_mat, n_vec, m = state[key]
        c_scr[key] = c_mat
        n_scr[key] = n_vec
        m_scr[key] = jnp.broadcast_to(m, (1, HEAD_W))

    @pl.when(j == n_steps - 1)
    def _():
        cf_ref[...] = c_scr[...]
        nf_ref[...] = n_scr[...]
        mf_ref[...] = m_scr[...]


def _mlstm(q, k, u, g, gt, init, with_out, row0, seq_len, batch):
    n_steps = seq_len // ML_STEP
    b0 = row0 // ML_STEP
    in_specs, args = [], []
    for d in range(2):
        for b in range(batch):
            step = (lambda j: n_steps - 1 - j) if d == 1 else (lambda j: j)
            blk = lambda j, b=b, step=step: b0 + b * n_steps + step(j)
            in_specs += [pl.BlockSpec((ML_STEP, 512), lambda j, blk=blk: (blk(j), 0)),
                         pl.BlockSpec((ML_STEP, 512), lambda j, blk=blk: (blk(j), 0)),
                         pl.BlockSpec((ML_STEP, 512), lambda j, blk=blk: (blk(j), CB_MV)),
                         pl.BlockSpec((ML_STEP, ML_GATE_W), lambda j, blk=blk: (blk(j), 0)),
                         pl.BlockSpec((1, ML_GATE_W, ML_STEP), lambda j, b=b, step=step: (b, 0, step(j)))]
            args += [q, k, u, g, gt]
    full = lambda shape: pl.BlockSpec(shape, lambda j: (0,) * len(shape))
    st_dims = [(2, batch, N_HEADS, HEAD_W, HEAD_W), (2, batch, N_HEADS, 1, HEAD_W), (2, batch, N_HEADS, 1, HEAD_W)]
    st_specs = [full(s) for s in st_dims]
    st_shapes = [jax.ShapeDtypeStruct(s, F32) for s in st_dims]
    out_specs, out_shapes = list(st_specs), list(st_shapes)
    if with_out:
        out_specs = [pl.BlockSpec((batch, ML_STEP, 512), lambda j: (0, j, 0)),
                     pl.BlockSpec((batch, ML_STEP, 512), lambda j: (0, n_steps - 1 - j, 0))] + out_specs
        out_shapes = [jax.ShapeDtypeStruct((batch, seq_len, 512), F32)] * 2 + out_shapes
    res = pl.pallas_call(
        functools.partial(_mlstm_kernel, batch, with_out, n_steps),
        grid=(n_steps,),
        in_specs=in_specs + st_specs,
        out_specs=out_specs,
        out_shape=out_shapes,
        scratch_shapes=[pltpu.VMEM(s, F32) for s in st_dims],
        compiler_params=_cparams(("arbitrary",)),
        name="mlstm",
    )(*args, *init)
    if with_out:
        return res[0], res[1], tuple(res[2:])
    return None, None, tuple(res)


def _hy_filter_kernel(n, feat_ref, w1_ref, b1_ref, w2_ref, b2_ref, w3_ref, freq_ref, decay_ref,
                      h_ref, norm_ref):
    i = pl.program_id(0)
    tm = h_ref.shape[1]
    freq = freq_ref[...]
    hid = jnp.sin(freq[0:1, :] * (_hdot(feat_ref[0], w1_ref[...]) + b1_ref[...]))
    hid = jnp.sin(freq[1:2, :] * (_hdot(hid, w2_ref[...]) + b2_ref[...]))
    hw = _hdot(hid, w3_ref[...])
    row = lax.broadcasted_iota(jnp.int32, (tm, BRANCH_W), 0) + i * tm
    decay = jnp.abs(decay_ref[...])
    h_fwd = hw[:tm, :BRANCH_W] * jnp.exp(-(row.astype(F32) / n) * decay[:, :BRANCH_W])
    h_bwd = hw[tm:, BRANCH_W:] * jnp.exp(-((n - row).astype(F32) / n) * decay[:, BRANCH_W:])
    h_bwd = jnp.where(row == 0, 0.0, h_bwd)
    h_ref[0] = h_fwd
    h_ref[1] = h_bwd

    @pl.when(i == 0)
    def _():
        norm_ref[...] = jnp.zeros_like(norm_ref)

    norm_ref[...] += jnp.concatenate([jnp.sum(jnp.abs(h_fwd), axis=0, keepdims=True),
                                      jnp.sum(jnp.abs(h_bwd), axis=0, keepdims=True)], axis=1)


def _hy_filter(n, feat2, w1p, b1, w2, b2, w3, freq, decay_flat):
    tm = feat2.shape[1] // 2
    full = lambda a: pl.BlockSpec(a.shape, lambda i: (0,) * a.ndim)
    return pl.pallas_call(
        functools.partial(_hy_filter_kernel, n),
        grid=(n // tm,),
        in_specs=[pl.BlockSpec((1, 2 * tm, LANE), lambda i: (i, 0, 0)), full(w1p), full(b1), full(w2), full(b2),
                  full(w3), full(freq), full(decay_flat)],
        out_specs=[pl.BlockSpec((2, tm, BRANCH_W), lambda i: (0, i, 0)),
                   pl.BlockSpec((1, 2 * BRANCH_W), lambda i: (0, 0))],
        out_shape=[jax.ShapeDtypeStruct((2, n, BRANCH_W), F32), jax.ShapeDtypeStruct((1, 2 * BRANCH_W), F32)],
        compiler_params=_cparams(("arbitrary",)),
        name="hyena_filter",
    )(feat2, w1p, b1, w2, b2, w3, freq, decay_flat)


def _split_bf16(x):
    hi = x.astype(BF16)
    return hi, (x - hi.astype(F32)).astype(BF16)


def _dot3(a, b):
    d = lambda p, q: jnp.dot(p, q, preferred_element_type=F32)
    return d(a[0], b[0]) + (d(a[0], b[1]) + d(a[1], b[0]))


def _dft_a_kernel(f_ref, x_ref, o_ref):
    o_ref[...] = _dot3((f_ref[0], f_ref[1]), _split_bf16(x_ref[...]))


def _dft_stage_a(table, x, tc=2048):
    k, cols = x.shape
    rows = table.shape[1]
    return pl.pallas_call(
        _dft_a_kernel,
        grid=(cols // tc,),
        in_specs=[pl.BlockSpec(table.shape, lambda j: (0, 0, 0)),
                  pl.BlockSpec((k, tc), lambda j: (0, j))],
        out_specs=pl.BlockSpec((rows, tc), lambda j: (0, j)),
        out_shape=jax.ShapeDtypeStruct((rows, cols), F32),
        compiler_params=_cparams(("parallel",)),
        name="hyena_dft_a",
    )(table, x)


def _dft_c_kernel(az_ref, ah_ref, g_ref, gt_ref, o_ref):
    x = jnp.concatenate([az_ref[:, 0].reshape(2 * LANE, BRANCH_W), ah_ref[:, 0].reshape(2 * LANE, BRANCH_W)],
                        axis=1)
    s = _dot3((g_ref[0, 0], g_ref[1, 0]), _split_bf16(x))
    zr, zi = s[:LANE, :BRANCH_W], s[LANE:, :BRANCH_W]
    hr, hi = s[:LANE, BRANCH_W:], s[LANE:, BRANCH_W:]
    prod = jnp.concatenate([zr * hr - zi * hi, zr * hi + zi * hr], axis=0)
    p = _dot3((gt_ref[0, 0], gt_ref[1, 0]), _split_bf16(prod))
    o_ref[:, 0] = p.reshape(2, LANE, BRANCH_W)


def _dft_stage_c(az, ah, g, gt):
    _, n1, _, c = az.shape
    slab = pl.BlockSpec((2, 1, LANE, c), lambda k: (0, k, 0, 0))
    tab = pl.BlockSpec((2, 1, 2 * LANE, 2 * LANE), lambda k: (0, k, 0, 0))
    return pl.pallas_call(
        _dft_c_kernel,
        grid=(n1,),
        in_specs=[slab, slab, tab, tab],
        out_specs=slab,
        out_shape=jax.ShapeDtypeStruct(az.shape, F32),
        compiler_params=_cparams(("parallel",)),
        name="hyena_dft_c",
    )(az, ah, g, gt)


def _hy_epilogue(y, z, x0, norm, skip):
    reps = y.shape[1] // BRANCH_W
    nsum = norm[:, :BRANCH_W] + norm[:, BRANCH_W:]
    inv = jnp.concatenate([1.0 / nsum] * reps, axis=1)
    sk = jnp.concatenate([skip] * reps, axis=1)
    return (x0 * (y * inv + z * sk)).astype(BF16)


def _dft_a_inv_kernel(f_ref, p_ref, z_ref, x0_ref, norm_ref, skip_ref, o_ref):
    y = _dot3((f_ref[0], f_ref[1]), _split_bf16(p_ref[...]))
    o_ref[...] = _hy_epilogue(y, z_ref[...], x0_ref[...], norm_ref[...], skip_ref[...])


def _dft_stage_a_inv(table, p, z, x0, norm, skip, tc=2048):
    rows, cols = p.shape
    zr = z.shape[0]
    col = lambda r: pl.BlockSpec((r, tc), lambda j: (0, j))
    return pl.pallas_call(
        _dft_a_inv_kernel,
        grid=(cols // tc,),
        in_specs=[pl.BlockSpec(table.shape, lambda j: (0, 0, 0)), col(rows), col(zr), col(zr),
                  pl.BlockSpec(norm.shape, lambda j: (0, 0)), pl.BlockSpec(skip.shape, lambda j: (0, 0))],
        out_specs=col(zr),
        out_shape=jax.ShapeDtypeStruct((zr, cols), BF16),
        compiler_params=_cparams(("parallel",)),
        name="hyena_dft_a_inv",
    )(table, p, z, x0, norm, skip)


def _dft_tables(n):
    big_l = 2 * n
    n1 = big_l // LANE
    n1h = n1 // 2
    two_pi = 2.0 * math.pi
    k1 = jnp.arange(n1, dtype=jnp.int32)
    th = two_pi * ((k1[:, None] * k1[None, :]) % n1).astype(F32) / n1
    c_full, s_full = jnp.cos(th), jnp.sin(th)
    ch, sh = c_full[:, :n1h], s_full[:, :n1h]
    fz = jnp.concatenate([jnp.concatenate([ch, sh], axis=1), jnp.concatenate([-sh, ch], axis=1)], axis=0)
    ff = jnp.concatenate([c_full, -s_full], axis=0)
    ct, st = ch.T / big_l, sh.T / big_l
    fi = jnp.concatenate([jnp.concatenate([ct, -st], axis=1), jnp.concatenate([st, ct], axis=1)], axis=0)
    s2 = jnp.arange(LANE, dtype=jnp.int32)
    k = k1[:, None, None] + n1 * s2[None, :, None]
    ph = two_pi * ((k * s2[None, None, :]) % big_l).astype(F32) / big_l
    gr, gi = jnp.cos(ph), -jnp.sin(ph)
    g = jnp.concatenate([jnp.concatenate([gr, -gi], axis=2), jnp.concatenate([gi, gr], axis=2)], axis=1)
    pair = lambda t: jnp.stack(_split_bf16(t))
    return pair(fz), pair(ff), pair(fi), pair(g), pair(jnp.swapaxes(g, 1, 2))


def _hy_small_kernel(batch, n, z_ref, x0_ref, h_ref, f_ref, fi_ref, norm_ref, skip_ref, o_ref):
    big_l = 2 * n
    f = f_ref[...]
    sz = _hdot(f[:, :n], jnp.concatenate([z_ref[b] for b in range(batch)], axis=1))
    sh = _hdot(f, jnp.concatenate([h_ref[0], h_ref[1]], axis=0))
    hr, hi = sh[:big_l], sh[big_l:]
    prods = []
    for b in range(batch):
        zr = sz[:big_l, b * BRANCH_W:(b + 1) * BRANCH_W]
        zi = sz[big_l:, b * BRANCH_W:(b + 1) * BRANCH_W]
        prods.append(jnp.concatenate([zr * hr - zi * hi, zr * hi + zi * hr], axis=0))
    y = _hdot(fi_ref[...], jnp.concatenate(prods, axis=1))
    for b in range(batch):
        o_ref[b] = _hy_epilogue(y[:, b * BRANCH_W:(b + 1) * BRANCH_W], z_ref[b], x0_ref[b],
                                norm_ref[...], skip_ref[...])


def _hy_small(z, x0, hfilt, norm, skip):
    batch, n, c = z.shape
    big_l = 2 * n
    kk = jnp.arange(big_l, dtype=jnp.int32)
    th = 2.0 * math.pi * ((kk[:, None] * kk[None, :]) % big_l).astype(F32) / big_l
    f = jnp.concatenate([jnp.cos(th), -jnp.sin(th)], axis=0)
    fi = jnp.concatenate([jnp.cos(th[:, :n]).T, -jnp.sin(th[:, :n]).T], axis=1) / big_l
    full = lambda a: pl.BlockSpec(a.shape, lambda i: (0,) * a.ndim)
    return pl.pallas_call(
        functools.partial(_hy_small_kernel, batch, n),
        grid=(1,),
        in_specs=[full(z), full(x0), full(hfilt), full(f), full(fi), full(norm), full(skip)],
        out_specs=full(z),
        out_shape=jax.ShapeDtypeStruct(z.shape, BF16),
        compiler_params=_cparams(("arbitrary",)),
        name="hyena_small",
    )(z, x0, hfilt, f, fi, norm, skip)


def _hy_features(n):
    tm = min(n, 512)
    bands = jnp.linspace(1e-4, HY_BANDS - 1, HY_BANDS, dtype=F32)

    def feats(pos):
        t = pos.astype(F32) / n
        ang = 2.0 * math.pi * t[:, None] * bands[None, :]
        feat = jnp.concatenate([t[:, None], jnp.cos(ang), jnp.sin(ang)], axis=-1)
        return jnp.pad(feat, ((0, 0), (0, LANE - HY_EMB))).reshape(n // tm, tm, LANE)

    r = jnp.arange(n, dtype=jnp.int32)
    return jnp.concatenate([feats(r), feats(n - r)], axis=1)


def _merge_kernel(hy_ref, mla_ref, dif_ref, hf_ref, hb_ref, mo_ref, gate_ref, wb_ref, mln_ref, o_ref):
    hsum = hf_ref[...] + hb_ref[...]
    mln = mln_ref[...]
    y_ml = jnp.concatenate(
        [_rms(hsum[:, h * HEAD_W:(h + 1) * HEAD_W]) * mln[:, h * HEAD_W:(h + 1) * HEAD_W]
         for h in range(N_HEADS)], axis=1)
    y_ml = _sigmoid(mo_ref[...]) * y_ml
    ys = (hy_ref[...], mla_ref[...], dif_ref[...], y_ml.astype(BF16))
    acc = None
    for i in range(4):
        gi = _sigmoid(gate_ref[:, i * D_MODEL:(i + 1) * D_MODEL].astype(F32))
        term = gi * jnp.dot(ys[i], wb_ref[i], preferred_element_type=F32)
        acc = term if acc is None else acc + term
    o_ref[...] = acc.astype(BF16)


def _merge(y_hy, y_mla, y_dif, h_f, h_b, u, gate, wb, mln, nrows, tm=256):
    row = pl.BlockSpec((tm, 512), lambda i: (i, 0))
    full = lambda a: pl.BlockSpec(a.shape, lambda i: (0,) * a.ndim)
    return pl.pallas_call(
        _merge_kernel,
        grid=(nrows // tm,),
        in_specs=[row, row, row, row, row,
                  pl.BlockSpec((tm, 512), lambda i: (i, CB_MO)),
                  pl.BlockSpec((tm, GATE_COLS), lambda i: (i, 0)),
                  full(wb), full(mln)],
        out_specs=pl.BlockSpec((tm, D_MODEL), lambda i: (i, 0)),
        out_shape=jax.ShapeDtypeStruct((nrows, D_MODEL), BF16),
        compiler_params=_cparams(("parallel",), 56),
        name="branch_merge",
    )(y_hy, y_mla, y_dif, h_f, h_b, u, gate, wb, mln)


def _out_kernel(m_ref, w_ref, x_ref, ga_ref, gain_ref, sc_ref, sh_ref, rw_ref, rb_ref,
                xo_ref, f_ref, idx_ref, gt_ref):
    y = jnp.dot(m_ref[...], w_ref[...], preferred_element_type=F32)
    x = x_ref[...] + ga_ref[0] * y
    xo_ref[...] = x
    f = _rms(x) * gain_ref[...] * (1.0 + sc_ref[0]) + sh_ref[0]
    tm = f.shape[0]
    for j in range(ROW_TILES):
        f_ref[pl.ds(j, tm, stride=ROW_TILES), :] = f[:, j * LANE:(j + 1) * LANE]
    logits = _bdot(f, rw_ref[...]) + rb_ref[...]
    lane = lax.broadcasted_iota(jnp.int32, logits.shape, 1)
    idx_out = jnp.zeros(logits.shape, jnp.int32)
    val_out = jnp.full(logits.shape, -jnp.inf, F32)
    work = logits
    for k in range(TOP_K):
        mx = jnp.max(work, axis=-1, keepdims=True)
        am = jnp.min(jnp.where(work == mx, lane, LANE), axis=-1, keepdims=True)
        idx_out = jnp.where(lane == k, am, idx_out)
        val_out = jnp.where(lane == k, mx, val_out)
        work = jnp.where(lane == am, -jnp.inf, work)
    e = jnp.exp(val_out - jnp.max(val_out, axis=-1, keepdims=True))
    gt_ref[...] = e / jnp.sum(e, axis=-1, keepdims=True)
    idx_ref[...] = idx_out


def _out_proj(merged, w_out, x_all, ga, gain, sc, sh, rw, rb, nrows, lat_rows, seq_rows, tm=256):
    grp = functools.partial(_group_of_tile, tm=tm, lat_rows=lat_rows, seq_rows=seq_rows)
    row = lambda w: pl.BlockSpec((tm, w), lambda i: (i, 0))
    full = lambda a: pl.BlockSpec(a.shape, lambda i: (0,) * a.ndim)
    mod = pl.BlockSpec((1, 1, D_MODEL), lambda i: (grp(i), 0, 0))
    return pl.pallas_call(
        _out_kernel,
        grid=(nrows // tm,),
        in_specs=[row(D_MODEL), full(w_out), row(D_MODEL), mod, full(gain), mod, mod, full(rw), full(rb)],
        out_specs=[row(D_MODEL), pl.BlockSpec((tm * ROW_TILES, LANE), lambda i: (i, 0)), row(LANE), row(LANE)],
        out_shape=[jax.ShapeDtypeStruct((nrows, D_MODEL), F32), jax.ShapeDtypeStruct((nrows * ROW_TILES, LANE), F32),
                   jax.ShapeDtypeStruct((nrows, LANE), jnp.int32), jax.ShapeDtypeStruct((nrows, LANE), F32)],
        compiler_params=_cparams(("parallel",)),
        name="out_proj_router",
    )(merged, w_out, x_all, ga, gain, sc, sh, rw, rb)


def _moe_kernel(blk_e_ref, used_ref, tok_ref, nxt1_ref, nxt_ref, f_hbm, wgu_ref, bgu_ref, wd_ref, bd_ref, o_ref, xbuf,
                sem):
    i = pl.program_id(0)
    last = pl.num_programs(0) - 1
    cur = i % MOE_BUFS
    ahead = (i + 2) % MOE_BUFS

    buf_rows = MOE_BLOCK * ROW_TILES

    def start_rows(idx_ref, s, r0, r1):
        for r in range(r0, r1):
            pltpu.make_async_copy(f_hbm.at[idx_ref[0, 0, r]],
                                  xbuf.at[pl.ds(s * buf_rows + r * ROW_TILES, ROW_TILES), :], sem.at[s]).start()

    def wait_rows(s):
        whole = xbuf.at[pl.ds(s * buf_rows, buf_rows), :]
        pltpu.make_async_copy(whole, whole, sem.at[s]).wait()

    @pl.when(i == 0)
    def _():
        start_rows(tok_ref, 0, 0, MOE_BLOCK)
        start_rows(nxt1_ref, 1, 0, MOE_BLOCK)

    wait_rows(cur)

    start_rows(nxt_ref, ahead, 0, MOE_BLOCK)

    @pl.when(i >= used_ref[0])
    def _():
        o_ref[...] = jnp.zeros_like(o_ref)

    @pl.when(i < used_ref[0])
    def _():
        base = cur * buf_rows
        x = jnp.concatenate([xbuf[pl.ds(base + j, MOE_BLOCK, stride=ROW_TILES), :].astype(BF16)
                             for j in range(ROW_TILES)], axis=1)
        gu = jnp.dot(x, wgu_ref[0], preferred_element_type=F32) + bgu_ref[0]
        g = jnp.minimum(gu[:, :D_EXPERT], SWIGLU_LIMIT)
        u = jnp.clip(gu[:, D_EXPERT:], -SWIGLU_LIMIT, SWIGLU_LIMIT)
        act = g * _sigmoid(SWIGLU_ALPHA * g) * (u + 1.0)
        o_ref[...] = _bdot(act, wd_ref[0]) + bd_ref[0]

    @pl.when(i == last)
    def _():
        wait_rows((i + 1) % MOE_BUFS)
        wait_rows(ahead)


def _moe_experts(f_all, blk_e, n_used, slot_tok, wgu, bgu, wd, bd):
    n_blocks = blk_e.shape[0]
    slots = n_blocks * MOE_BLOCK
    n_exp = wgu.shape[0]
    grid_spec = pltpu.PrefetchScalarGridSpec(
        num_scalar_prefetch=2,
        grid=(n_blocks,),
        in_specs=[pl.BlockSpec((1, 1, MOE_BLOCK), lambda i, be, nu: (i, 0, 0), memory_space=pltpu.SMEM),
                  pl.BlockSpec((1, 1, MOE_BLOCK), lambda i, be, nu: (jnp.minimum(i + 1, n_blocks - 1), 0, 0),
                               memory_space=pltpu.SMEM),
                  pl.BlockSpec((1, 1, MOE_BLOCK), lambda i, be, nu: (jnp.minimum(i + 2, n_blocks - 1), 0, 0),
                               memory_space=pltpu.SMEM),
                  pl.BlockSpec(memory_space=pl.ANY),
                  pl.BlockSpec((1, D_MODEL, 2 * D_EXPERT), lambda i, be, nu: (be[i], 0, 0)),
                  pl.BlockSpec((1, 1, 2 * D_EXPERT), lambda i, be, nu: (be[i], 0, 0)),
                  pl.BlockSpec((1, D_EXPERT, D_MODEL), lambda i, be, nu: (be[i], 0, 0)),
                  pl.BlockSpec((1, 1, D_MODEL), lambda i, be, nu: (be[i], 0, 0))],
        out_specs=pl.BlockSpec((MOE_BLOCK, D_MODEL), lambda i, be, nu: (i, 0)),
        scratch_shapes=[pltpu.VMEM((MOE_BUFS * MOE_BLOCK * ROW_TILES, LANE), F32),
                        pltpu.SemaphoreType.DMA((MOE_BUFS,))],
    )
    tok3 = slot_tok.reshape(n_blocks, 1, MOE_BLOCK)
    return pl.pallas_call(
        _moe_kernel,
        grid_spec=grid_spec,
        out_shape=jax.ShapeDtypeStruct((slots, D_MODEL), F32),
        compiler_params=_cparams(("arbitrary",), 56),
        name="moe_experts",
    )(blk_e, n_used, tok3, tok3, tok3, f_all,
      wgu, bgu.reshape(n_exp, 1, 2 * D_EXPERT), wd, bd.reshape(n_exp, 1, D_MODEL))


def _combine_kernel(final, dest_ref, nxt_ref, y_hbm, x_ref, tg_ref, gf_ref, fn_ref, o_ref, ybuf, sem):
    tm = x_ref.shape[0]
    i = pl.program_id(0)
    last = pl.num_programs(0) - 1
    cur = i % 2

    def start_rows(idx_ref, s):
        for r in range(tm):
            for k in range(TOP_K):
                pltpu.make_async_copy(y_hbm.at[pl.ds(idx_ref[0, 0, r * TOP_K + k], 1), :],
                                      ybuf.at[s, k, pl.ds(r, 1), :], sem.at[s]).start()

    def wait_rows(s):
        for k in range(TOP_K):
            pltpu.make_async_copy(y_hbm.at[pl.ds(0, tm), :], ybuf.at[s, k], sem.at[s]).wait()

    @pl.when(i == 0)
    def _():
        start_rows(dest_ref, 0)

    start_rows(nxt_ref, 1 - cur)
    wait_rows(cur)
    tg = tg_ref[...]
    moe = None
    for k in range(TOP_K):
        term = tg[:, k:k + 1] * ybuf[cur, k]
        moe = term if moe is None else moe + term
    x = x_ref[...] + gf_ref[0] * moe
    if final:
        x = _rms(x) * fn_ref[...]
    o_ref[...] = x

    @pl.when(i == last)
    def _():
        wait_rows(1 - cur)


def _moe_combine(y_slots, dest, top_gate, x_all, gf, final_gain, final, nrows, lat_rows, seq_rows, tm=128):
    grp = functools.partial(_group_of_tile, tm=tm, lat_rows=lat_rows, seq_rows=seq_rows)
    nt = nrows // tm
    grid_spec = pl.GridSpec(
        grid=(nt,),
        in_specs=[pl.BlockSpec((1, 1, tm * TOP_K), lambda i: (i, 0, 0), memory_space=pltpu.SMEM),
                  pl.BlockSpec((1, 1, tm * TOP_K), lambda i: (jnp.minimum(i + 1, nt - 1), 0, 0),
                               memory_space=pltpu.SMEM),
                  pl.BlockSpec(memory_space=pl.ANY),
                  pl.BlockSpec((tm, D_MODEL), lambda i: (i, 0)),
                  pl.BlockSpec((tm, LANE), lambda i: (i, 0)),
                  pl.BlockSpec((1, 1, D_MODEL), lambda i: (grp(i), 0, 0)),
                  pl.BlockSpec((1, D_MODEL), lambda i: (0, 0))],
        out_specs=pl.BlockSpec((tm, D_MODEL), lambda i: (i, 0)),
        scratch_shapes=[pltpu.VMEM((2, TOP_K, tm, D_MODEL), F32), pltpu.SemaphoreType.DMA((2,))],
    )
    dest3 = dest.reshape(nt, 1, tm * TOP_K)
    return pl.pallas_call(
        functools.partial(_combine_kernel, final),
        grid_spec=grid_spec,
        out_shape=jax.ShapeDtypeStruct((nrows, D_MODEL), F32),
        compiler_params=_cparams(("arbitrary",)),
        name="moe_combine",
    )(dest3, dest3, y_slots, x_all, top_gate, gf, final_gain)


def _expert_onehot(idx):
    lane = lax.broadcasted_iota(jnp.int32, idx.shape, 1)
    oh = jnp.zeros(idx.shape, F32)
    for k in range(TOP_K):
        oh = oh + jnp.where(lane == idx[:, k:k + 1], 1.0, 0.0)
    return oh


def _rank_kernel(idx_ref, rank_ref, cnt_ref, carry):
    @pl.when(pl.program_id(0) == 0)
    def _():
        carry[...] = jnp.zeros_like(carry)

    oh = _expert_onehot(idx_ref[...])
    tm = oh.shape[0]
    r = lax.broadcasted_iota(jnp.int32, (tm, tm), 0)
    c = lax.broadcasted_iota(jnp.int32, (tm, tm), 1)
    earlier = jnp.where(c < r, 1.0, 0.0).astype(BF16)
    rank_ref[...] = jnp.dot(earlier, oh.astype(BF16), preferred_element_type=F32) + carry[...]
    carry[...] += jnp.sum(oh, axis=0, keepdims=True)
    cnt_ref[...] = carry[...]


def _dest_kernel(idx_ref, rank_ref, pstart_ref, dest_ref):
    idx = idx_ref[...]
    slot = rank_ref[...] + pstart_ref[...]
    lane = lax.broadcasted_iota(jnp.int32, idx.shape, 1)
    out = jnp.zeros(idx.shape, jnp.int32)
    for k in range(TOP_K):
        dk = jnp.sum(jnp.where(lane == idx[:, k:k + 1], slot, 0.0), axis=-1, keepdims=True)
        out = jnp.where(lane == k, dk.astype(jnp.int32), out)
    dest_ref[...] = out


def _route(top_idx128, tm=512):
    n = top_idx128.shape[0]
    a = n * TOP_K
    row = pl.BlockSpec((tm, LANE), lambda i: (i, 0))
    one = pl.BlockSpec((1, LANE), lambda i: (0, 0))
    rank, cnt = pl.pallas_call(
        _rank_kernel,
        grid=(n // tm,),
        in_specs=[row],
        out_specs=[row, one],
        out_shape=[jax.ShapeDtypeStruct((n, LANE), F32), jax.ShapeDtypeStruct((1, LANE), F32)],
        scratch_shapes=[pltpu.VMEM((1, LANE), F32)],
        compiler_params=_cparams(("arbitrary",)),
        name="route_rank",
    )(top_idx128)
    counts = cnt[0, :N_EXPERTS].astype(jnp.int32)
    padded = (counts + MOE_BLOCK - 1) // MOE_BLOCK * MOE_BLOCK
    pend = jnp.cumsum(padded)
    pstart = pend - padded
    pstart128 = jnp.pad(pstart.astype(F32), (0, LANE - N_EXPERTS)).reshape(1, LANE)
    dest = pl.pallas_call(
        _dest_kernel,
        grid=(n // tm,),
        in_specs=[row, row, one],
        out_specs=row,
        out_shape=jax.ShapeDtypeStruct((n, LANE), jnp.int32),
        compiler_params=_cparams(("parallel",)),
        name="route_dest",
    )(top_idx128, rank, pstart128)[:, :TOP_K]
    n_blocks = (a + N_EXPERTS * (MOE_BLOCK - 1)) // MOE_BLOCK + 1
    slots = n_blocks * MOE_BLOCK
    tok = jnp.broadcast_to(jnp.arange(n, dtype=jnp.int32)[:, None], (n, TOP_K))
    slot_tok = jnp.zeros((slots,), jnp.int32).at[dest.reshape(-1)].set(tok.reshape(-1))
    first_slot = jnp.arange(n_blocks, dtype=jnp.int32) * MOE_BLOCK
    blk_e = jnp.minimum(jnp.sum((pend[None, :] <= first_slot[:, None]).astype(jnp.int32), axis=1), N_EXPERTS - 1)
    n_used = (pend[-1:] // MOE_BLOCK).astype(jnp.int32)
    return blk_e, n_used, slot_tok, dest.astype(jnp.int32)


def _rope_tables(n_lat_rows, seq_len, ctx_rows):
    t = jnp.arange(seq_len, dtype=jnp.int32)
    inv = ROPE_THETA ** (-jnp.arange(0, 32, 2, dtype=F32) / 32)

    def cs(p):
        ang = p.astype(F32)[:, None] * inv[None, :]
        return jnp.cos(ang), jnp.sin(ang)

    cr, sr = cs(t // GRID_W)
    cc, sc = cs(t % GRID_W)
    c64 = jnp.concatenate([cr, cr, cc, cc], axis=1)
    s64 = jnp.concatenate([-sr, sr, -sc, sc], axis=1)
    reps = n_lat_rows // seq_len

    def build(c_half2, s_half2):
        c = jnp.concatenate([c64, c_half2], axis=1)
        s = jnp.concatenate([s64, s_half2], axis=1)
        c = jnp.concatenate([c] * reps + [jnp.ones((ctx_rows, LANE), F32)], axis=0)
        s = jnp.concatenate([s] * reps + [jnp.zeros((ctx_rows, LANE), F32)], axis=0)
        return c, s

    cd, sd = build(c64, s64)
    cm, sm = build(jnp.ones_like(c64), jnp.zeros_like(s64))
    return cd, sd, cm, sm


W_IN_SEGMENTS = ((320, 0, 512), (832, 512, 512), (1344, 1024, 512), (1856, 1536, 512), (2384, 2048, 512),
                 (2896, 2560, 512), (3408, 3072, 512), (3920, 3584, 512), (4432, 4096, 1536),
                 (0, 5632, 256), (256, 5888, 64), (2368, 5952, 16))
W_IN_USED = 5968
W_IN_GATE0 = 5968


def _w_in_layout_kernel(w_ref, small_ref, gate_ref):
    for src, dst, width in W_IN_SEGMENTS:
        small_ref[:, dst:dst + width] = w_ref[0, :, src:src + width].astype(BF16)
    small_ref[:, W_IN_USED:] = jnp.zeros((small_ref.shape[0], SMALL_COLS - W_IN_USED), BF16)
    gate_ref[...] = w_ref[0, :, W_IN_GATE0:].astype(BF16)


def _layout_w_in(w, layer, tk=128):
    _, d, ncol = w.shape
    return pl.pallas_call(
        _w_in_layout_kernel,
        grid=(d // tk,),
        in_specs=[pl.BlockSpec((1, tk, ncol), lambda i: (layer, i, 0))],
        out_specs=[pl.BlockSpec((tk, SMALL_COLS), lambda i: (i, 0)), pl.BlockSpec((tk, GATE_COLS), lambda i: (i, 0))],
        out_shape=[jax.ShapeDtypeStruct((d, SMALL_COLS), BF16), jax.ShapeDtypeStruct((d, GATE_COLS), BF16)],
        compiler_params=_cparams(("parallel",)),
        name="w_in_layout",
    )(w)


def _layout_w_qb(w):
    w = w.reshape(w.shape[0], N_HEADS, MLA_NOPE + MLA_ROPE)
    w = jnp.pad(w, ((0, 0), (0, 0), (0, MLA_QK_PAD - MLA_NOPE - MLA_ROPE)))
    return w.reshape(w.shape[0], N_HEADS * MLA_QK_PAD).astype(BF16)


def _layout_w_kvb(w):
    w = w.reshape(w.shape[0], N_HEADS, MLA_NOPE + HEAD_W)
    return jnp.concatenate([w[:, :, :MLA_NOPE].reshape(w.shape[0], -1),
                            w[:, :, MLA_NOPE:].reshape(w.shape[0], -1)], axis=1).astype(BF16)


def kernel(x, c, ctx, c_ctx, ada_w, ada_b, norm_mix, norm_ffn, w_in, hy_conv, hy_conv_b, hy_w1, hy_b1, hy_w2, hy_b2, hy_w3, hy_freq, hy_decay, hy_skip, mla_q_norm, mla_w_qb, mla_kv_norm, mla_w_kvb, dif_lambda, dif_norm, ml_conv_q, ml_conv_k, ml_gate_b, ml_norm, w_branch, w_out, router_w, router_b, exp_w_gu, exp_b_gu, exp_w_down, exp_b_down, final_norm):
    batch, n, d = x.shape
    n_ctx = ctx.shape[1]
    depth = w_in.shape[0]
    ctx_rows = batch * n_ctx
    lat_rows = batch * n
    t_all = ctx_rows + lat_rows

    assert batch + 1 <= SUBLANE
    cvec = jnp.zeros((SUBLANE, d), F32).at[:batch].set(c).at[batch].set(c_ctx)
    mod = _modulation(cvec, ada_w, ada_b)
    x_all = jnp.concatenate([x.reshape(lat_rows, d), ctx.reshape(ctx_rows, d)], axis=0)
    tabs = _rope_tables(lat_rows, n, ctx_rows)
    assert batch == 2, "the Hyena transform carries the two batches as one complex signal"
    fz_tab, ff_tab, fi_tab, g_tab, gt_tab = _dft_tables(n)
    feat_lat, feat_ctx = _hy_features(n), _hy_features(n_ctx)
    zero_state = (jnp.zeros((2, batch, N_HEADS, HEAD_W, HEAD_W), F32), jnp.zeros((2, batch, N_HEADS, 1, HEAD_W), F32),
                  jnp.zeros((2, batch, N_HEADS, 1, HEAD_W), F32))
    n1h = n // LANE
    wgu_all = exp_w_gu.astype(BF16).reshape(depth * N_EXPERTS, d, 2 * D_EXPERT)
    wd_all = exp_w_down.astype(BF16).reshape(depth * N_EXPERTS, D_EXPERT, d)
    bgu_all = exp_b_gu.reshape(depth * N_EXPERTS, 2 * D_EXPERT)
    bd_all = exp_b_down.reshape(depth * N_EXPERTS, d)

    for l in range(depth):
        last = l == depth - 1
        lam_init = 0.8 - 0.6 * math.exp(-0.3 * l)
        m6 = mod[l].reshape(SUBLANE, 6, 1, d)
        sh_a, sc_a, g_a, sh_f, sc_f, g_f = (m6[:, i] for i in range(6))
        w_small, w_gate = _layout_w_in(w_in, l)
        gain_mix = norm_mix[l].reshape(1, d)

        n_tok = lat_rows if last else t_all
        h_in = _norm_mod(x_all, gain_mix, sc_a, sh_a, lat_rows, n)
        u = _in_proj(h_in, w_small, F32, t_all)
        gate = _in_proj(h_in, w_gate, BF16, n_tok)

        mq, mk, mv, dq, dk, dv = _attn_prep(
            u, tabs, _layout_w_qb(mla_w_qb[l]), _layout_w_kvb(mla_w_kvb[l]),
            mla_q_norm[l].reshape(1, -1), mla_kv_norm[l].reshape(1, -1))
        ml_q, ml_k, ml_g = _ml_prep(u, ml_conv_q[l], ml_conv_k[l], ml_gate_b[l].reshape(1, 16), n, lat_rows, n_ctx)
        gt_lat = jnp.swapaxes(ml_g[:lat_rows].reshape(batch, n, ML_GATE_W), 1, 2)
        gt_ctx = jnp.swapaxes(ml_g[lat_rows:].reshape(batch, n_ctx, ML_GATE_W), 1, 2)

        lam_p = dif_lambda[l]
        dgain = dif_norm[l].reshape(1, HEAD_W)
        segs = [(lat_rows, n_ctx), (0, n)]
        y_mla = _attention(mq, mk, mv, 1, 0, n, segs, batch, MLA_QK_PAD)
        y_dif = _attention(dq, dk, dv, 2, 0, n, segs, batch, 2 * DIF_QK, lam_p, dgain, lam_init)

        hcf, hcb, st_ctx = _mlstm(ml_q, ml_k, u, ml_g, gt_ctx, zero_state, not last, lat_rows, n_ctx, batch)
        hlf, hlb, _ = _mlstm(ml_q, ml_k, u, ml_g, gt_lat, st_ctx, True, 0, n, batch)

        hw1 = jnp.pad(hy_w1[l], ((0, LANE - HY_EMB), (0, 0)))
        filt_args = (hw1, hy_b1[l].reshape(1, -1), hy_w2[l], hy_b2[l].reshape(1, -1), hy_w3[l], hy_freq[l],
                     hy_decay[l].reshape(1, -1))
        skip = hy_skip[l].reshape(1, -1)
        hconv_b = hy_conv_b[l].reshape(1, -1)
        z, x0 = _hy_prep(u, hy_conv[l], hconv_b, 0, lat_rows, n, lat_rows, n_ctx)
        hfilt, hnorm = _hy_filter(n, feat_lat, *filt_args)
        filt_full = hfilt.reshape(2 * n, BRANCH_W)
        n1 = 2 * n1h
        cols = LANE * BRANCH_W
        zr = z.reshape(batch * n1h, cols)
        az = _dft_stage_a(fz_tab, zr)
        ah = _dft_stage_a(ff_tab, filt_full.reshape(n1, cols))
        pz = _dft_stage_c(az.reshape(2, n1, LANE, BRANCH_W), ah.reshape(2, n1, LANE, BRANCH_W), g_tab, gt_tab)
        y_hy = _dft_stage_a_inv(fi_tab, pz.reshape(2 * n1, cols), zr, x0.reshape(batch * n1h, cols), hnorm,
                                skip).reshape(lat_rows, BRANCH_W)

        h_f, h_b = hlf.reshape(lat_rows, -1), hlb.reshape(lat_rows, -1)
        w_o = w_out[l].astype(BF16)
        gain_ffn = norm_ffn[l].reshape(1, d)
        rw = jnp.pad(router_w[l], ((0, 0), (0, LANE - N_EXPERTS))).astype(BF16)
        rb = jnp.concatenate([router_b[l], jnp.full((LANE - N_EXPERTS,), -jnp.inf, F32)]).reshape(1, LANE)

        if not last:
            cseg = [(lat_rows, n_ctx)]
            yc_mla = _attention(mq, mk, mv, 1, lat_rows, n_ctx, cseg, batch, MLA_QK_PAD)
            yc_dif = _attention(dq, dk, dv, 2, lat_rows, n_ctx, cseg, batch, 2 * DIF_QK, lam_p, dgain, lam_init)
            zc, x0c = _hy_prep(u, hy_conv[l], hconv_b, lat_rows, ctx_rows, n, lat_rows, n_ctx)
            hfc, hnc = _hy_filter(n_ctx, feat_ctx, *filt_args)
            yc_hy = _hy_small(zc.reshape(batch, n_ctx, BRANCH_W), x0c.reshape(batch, n_ctx, BRANCH_W), hfc, hnc,
                              skip).reshape(ctx_rows, BRANCH_W)
            cat = lambda a, b: jnp.concatenate([a, b], axis=0)
            y_hy, y_mla, y_dif = cat(y_hy, yc_hy), cat(y_mla, yc_mla), cat(y_dif, yc_dif)
            h_f, h_b = cat(h_f, hcf.reshape(ctx_rows, -1)), cat(h_b, hcb.reshape(ctx_rows, -1))

        merged = _merge(y_hy, y_mla, y_dif, h_f, h_b, u, gate, w_branch[l].astype(BF16), ml_norm[l].reshape(1, -1),
                        n_tok)
        xo, f_all, top_idx, top_gate = _out_proj(merged, w_o, x_all, g_a, gain_ffn, sc_f, sh_f, rw, rb,
                                                 n_tok, lat_rows, n)
        blk_e, n_used, slot_tok, dest = _route(top_idx)
        y_slots = _moe_experts(f_all.reshape(n_tok, ROW_TILES, LANE), blk_e + l * N_EXPERTS, n_used, slot_tok,
                               wgu_all, bgu_all, wd_all, bd_all)
        fgain = final_norm.reshape(1, d)
        x_all = _moe_combine(y_slots, dest, top_gate, xo, g_f, fgain, last, n_tok, lat_rows, n)

    return x_all.reshape(batch, n, d)
```
